```python
import math
import jax, jax.numpy as jnp
from jax import lax
import numpy as np

D_MODEL = 1024
BATCH = 2
SEQ = 8192
DEPTH = 1
DEC_BATCH = 32
DEC_SEQ = 4
PAST_LEN = 8192
PAGE_SIZE = 128

D_MIX = D_MODEL
POOL_W = D_MIX // 2
POOL_WINDOWS = (2, 4, 8, 16)
POOL_GROUPS = len(POOL_WINDOWS)
POOL_GC = POOL_W // POOL_GROUPS
POOL_HIST = max(POOL_WINDOWS) - 1
ATTN_W = D_MIX - POOL_W
HEAD_DIM = 64
N_HEADS = ATTN_W // HEAD_DIM
D_IN = 2 * POOL_W + 4 * ATTN_W + N_HEADS
Q_BLOCK = 128
EPS = 1e-6
ATTN_SCALE = 1.0 / math.sqrt(HEAD_DIM)

kernel_name = "hymba_pool_fox_decode_step"


def rmsnorm(x, w):
    xf = x.astype(jnp.float32)
    r = lax.rsqrt(jnp.mean(xf * xf, axis=-1, keepdims=True) + EPS)
    return (xf * r * w.astype(jnp.float32)).astype(x.dtype)


def project(h, w_in_l, b_f_l):
    b, t, _ = h.shape
    z = jnp.einsum('btd,de->bte', h, w_in_l)
    o = 0
    u_pool = z[..., o:o + POOL_W]; o += POOL_W
    g_pool = z[..., o:o + POOL_W]; o += POOL_W
    q = z[..., o:o + ATTN_W].reshape(b, t, N_HEADS, HEAD_DIM); o += ATTN_W
    k = z[..., o:o + ATTN_W].reshape(b, t, N_HEADS, HEAD_DIM); o += ATTN_W
    v = z[..., o:o + ATTN_W].reshape(b, t, N_HEADS, HEAD_DIM); o += ATTN_W
    g_attn = z[..., o:o + ATTN_W]; o += ATTN_W
    f_logit = z[..., o:o + N_HEADS]
    logf = jax.nn.log_sigmoid(f_logit.astype(jnp.float32) + b_f_l.astype(jnp.float32))
    return u_pool, g_pool, q, k, v, g_attn, logf


def pool_mix(u_hist, u_new, start_pos, pool_w_l, pool_scale_l):
    b, t, _ = u_new.shape
    ext = jnp.concatenate([u_hist.astype(u_new.dtype), u_new], axis=1)
    extf = ext.astype(jnp.float32)
    cs = jnp.concatenate([jnp.zeros((b, 1, POOL_W), jnp.float32), jnp.cumsum(extf, axis=1)], axis=1)
    end = cs[:, POOL_HIST + 1:POOL_HIST + 1 + t]
    pos = start_pos + jnp.arange(t, dtype=jnp.int32)
    uf = u_new.astype(jnp.float32)
    outs = []
    for g, w in enumerate(POOL_WINDOWS):
        sl = slice(g * POOL_GC, (g + 1) * POOL_GC)
        start = cs[:, POOL_HIST + 1 - w:POOL_HIST + 1 - w + t, sl]
        cnt = jnp.minimum(w, pos + 1).astype(jnp.float32)[None, :, None]
        outs.append((end[..., sl] - start) / cnt - uf[..., sl])
    pooled = jnp.stack(outs, axis=2)
    mixed = jnp.einsum('btgc,gcd->btgd', pooled, pool_w_l.astype(jnp.float32)).reshape(b, t, POOL_W)
    mixed = (mixed * pool_scale_l.astype(jnp.float32)).astype(u_new.dtype)
    return mixed, ext[:, -POOL_HIST:]


def fox_block(q, k, v, fq, fk, q_pos, k_pos):
    s = jnp.einsum('bqhd,bkhd->bhqk', q, k, preferred_element_type=jnp.float32) * ATTN_SCALE
    s = s + (jnp.transpose(fq, (0, 2, 1))[..., :, None] - jnp.transpose(fk, (0, 2, 1))[..., None, :])
    mask = k_pos[None, :] <= q_pos[:, None]
    s = jnp.where(mask[None, None], s, -jnp.inf)
    p = jax.nn.softmax(s, axis=-1)
    return jnp.einsum('bhqk,bkhd->bqhd', p.astype(v.dtype), v)


def merge(x, pool_out, g_pool, attn_out, g_attn, w_out_l):
    b, t, _ = x.shape
    mix = jnp.concatenate([pool_out * jax.nn.silu(g_pool),
                           attn_out.reshape(b, t, ATTN_W) * jax.nn.silu(g_attn)], axis=-1)
    return x + jnp.einsum('bte,ed->btd', mix, w_out_l)


def setup_inputs(seed: int = 0) -> dict:
    key = jax.random.key(seed)
    ks = jax.random.split(key, 16)
    n_pages = PAST_LEN // PAGE_SIZE
    n_used = DEC_BATCH * n_pages
    n_phys = (5 * n_used) // 4
    x_prompt = jax.random.normal(ks[0], (BATCH, SEQ, D_MODEL), jnp.float32)
    x_sample = jax.random.normal(ks[1], (DEC_BATCH, DEC_SEQ, D_MODEL), jnp.float32)
    cache_k = jax.random.normal(ks[2], (DEPTH, n_phys, PAGE_SIZE, N_HEADS, HEAD_DIM), jnp.float32)
    cache_v = jax.random.normal(ks[3], (DEPTH, n_phys, PAGE_SIZE, N_HEADS, HEAD_DIM), jnp.float32)
    cache_logf = jax.nn.log_sigmoid(1.0 + jax.random.normal(ks[4], (DEPTH, n_phys, PAGE_SIZE, N_HEADS), jnp.float32))
    state_pool = jax.random.normal(ks[5], (DEPTH, DEC_BATCH, POOL_HIST, POOL_W), jnp.float32)
    page_table = jax.random.permutation(ks[6], n_phys)[:n_used].reshape(DEC_BATCH, n_pages).astype(jnp.int32)
    norm_w = 1.0 + 0.05 * jax.random.normal(ks[7], (DEPTH, D_MODEL), jnp.float32)
    w_in = jax.random.normal(ks[8], (DEPTH, D_MODEL, D_IN), jnp.float32) * D_MODEL ** -0.5
    b_forget = 1.0 + 0.5 * jax.random.normal(ks[9], (DEPTH, N_HEADS), jnp.float32)
    pool_w = jax.random.normal(ks[10], (DEPTH, POOL_GROUPS, POOL_GC, POOL_GC), jnp.float32) * POOL_GC ** -0.5
    pool_scale = 1.0 + 0.1 * jax.random.normal(ks[11], (DEPTH, POOL_W), jnp.float32)
    w_out = jax.random.normal(ks[12], (DEPTH, D_MIX, D_MODEL), jnp.float32) * D_MIX ** -0.5
    norm_f = 1.0 + 0.05 * jax.random.normal(ks[13], (D_MODEL,), jnp.float32)
    return {"x_prompt": x_prompt, "x_sample": x_sample, "cache_k": cache_k, "cache_v": cache_v,
            "cache_logf": cache_logf, "state_pool": state_pool, "page_table": page_table,
            "norm_w": norm_w, "w_in": w_in, "b_forget": b_forget, "pool_w": pool_w,
            "pool_scale": pool_scale, "w_out": w_out, "norm_f": norm_f}


def reference(x_prompt, x_sample, cache_k, cache_v, cache_logf, state_pool, page_table,
              norm_w, w_in, b_forget, pool_w, pool_scale, w_out, norm_f):
    b_p, seq, _ = x_prompt.shape
    b_s, t_s, _ = x_sample.shape
    n_pages = page_table.shape[1]
    past = n_pages * cache_k.shape[2]
    n_blocks = seq // Q_BLOCK

    xp, xs = x_prompt, x_sample
    kp_l, vp_l, fp_l, pp_l = [], [], [], []
    ks_l, vs_l, fs_l, ps_l = [], [], [], []
    for l in range(DEPTH):
        h = rmsnorm(xp, norm_w[l])
        u, gp, q, k, v, ga, logf = project(h, w_in[l], b_forget[l])
        hist0 = jnp.zeros((b_p, POOL_HIST, POOL_W), u.dtype)
        pool_out, pool_new = pool_mix(hist0, u, 0, pool_w[l], pool_scale[l])
        F = jnp.cumsum(logf, axis=1)
        k_pos = jnp.arange(seq, dtype=jnp.int32)
        q_blk = jnp.moveaxis(q.reshape(b_p, n_blocks, Q_BLOCK, N_HEADS, HEAD_DIM), 1, 0)
        f_blk = jnp.moveaxis(F.reshape(b_p, n_blocks, Q_BLOCK, N_HEADS), 1, 0)
        pos_blk = k_pos.reshape(n_blocks, Q_BLOCK)
        o_blk = lax.map(lambda a: fox_block(a[0], k, v, a[1], F, a[2], k_pos), (q_blk, f_blk, pos_blk))
        attn = jnp.moveaxis(o_blk, 0, 1).reshape(b_p, seq, N_HEADS, HEAD_DIM)
        xp = merge(xp, pool_out, gp, attn, ga, w_out[l])
        kp_l.append(k); vp_l.append(v); fp_l.append(logf.astype(cache_logf.dtype)); pp_l.append(pool_new)

        h = rmsnorm(xs, norm_w[l])
        u, gp, q, k, v, ga, logf = project(h, w_in[l], b_forget[l])
        pool_out, pool_new = pool_mix(state_pool[l], u, past, pool_w[l], pool_scale[l])
        k_past = cache_k[l][page_table].reshape(b_s, past, N_HEADS, HEAD_DIM)
        v_past = cache_v[l][page_table].reshape(b_s, past, N_HEADS, HEAD_DIM)
        f_past = cache_logf[l][page_table].reshape(b_s, past, N_HEADS)
        k_all = jnp.concatenate([k_past.astype(k.dtype), k], axis=1)
        v_all = jnp.concatenate([v_past.astype(v.dtype), v], axis=1)
        F = jnp.cumsum(jnp.concatenate([f_past.astype(jnp.float32), logf], axis=1), axis=1)
        kpos = jnp.arange(past + t_s, dtype=jnp.int32)
        attn = fox_block(q, k_all, v_all, F[:, past:], F, kpos[past:], kpos)
        xs = merge(xs, pool_out, gp, attn, ga, w_out[l])
        ks_l.append(k); vs_l.append(v); fs_l.append(logf.astype(cache_logf.dtype)); ps_l.append(pool_new)

    y_prompt = rmsnorm(xp, norm_f)
    y_sample = rmsnorm(xs, norm_f)
    k_prompt = jnp.stack(kp_l, axis=0)
    v_prompt = jnp.stack(vp_l, axis=0)
    logf_prompt = jnp.stack(fp_l, axis=0)
    pool_prompt = jnp.stack(pp_l, axis=0)
    k_sample = jnp.stack(ks_l, axis=0)
    v_sample = jnp.stack(vs_l, axis=0)
    logf_sample = jnp.stack(fs_l, axis=0)
    pool_sample = jnp.stack(ps_l, axis=0)
    return (y_prompt, y_sample, k_prompt, v_prompt, logf_prompt, pool_prompt,
            k_sample, v_sample, logf_sample, pool_sample)
```

```python
import functools
import math

import numpy as np
import jax
import jax.numpy as jnp
from jax import lax
from jax.experimental import pallas as pl
from jax.experimental.pallas import tpu as pltpu

D_MODEL = 1024
POOL_W = 512
ATTN_W = 512
N_HEADS = 8
HEAD_DIM = 64
POOL_WINDOWS = (2, 4, 8, 16)
POOL_GC = 128
POOL_HIST = 15
EPS = 1e-6

LANES = 128
LOG2E = 1.4426950408889634
Q_SCALE = LOG2E / math.sqrt(HEAD_DIM)
NEG = -1e30
N_PAIRS = N_HEADS // 2
W_COLS = 2 * POOL_W + 4 * ATTN_W + LANES
BIAS_COLS = N_PAIRS * LANES + N_HEADS * LANES
VMEM_LIMIT = 56 * 1024 * 1024

PROJ_TM = 512
ATTN_T = 512
DEC_PAGES = 8

BF16 = jnp.bfloat16
F32 = jnp.float32


def _placement():
    pp = np.zeros((LANES, BIAS_COLS), np.float32)
    cc = np.zeros((1, BIAS_COLS), np.float32)
    kbase = N_PAIRS * LANES
    for h in range(N_HEADS):
        p, e = divmod(h, 2)
        for i in range(3):
            pp[8 * i + h, LANES * p + 6 * e + i] = 1.0
            pp[8 * i + h, kbase + LANES * h + 6 * e + 3 + i] = -1.0
            cc[0, LANES * p + 6 * e + 3 + i] = 1.0
            cc[0, kbase + LANES * h + 6 * e + i] = 1.0
    return pp, cc


def _log_sigmoid(x):
    return jnp.minimum(x, 0.0) - jnp.log(1.0 + jnp.exp(-jnp.abs(x)))


def _silu(x):
    return x * jax.nn.sigmoid(x)


def _rmsnorm(xf, w):
    ms = jnp.mean(xf * xf, axis=-1, keepdims=True)
    return xf * lax.rsqrt(ms + EPS) * w


def _project(h, w_ref):
    def cols(c0, n):
        return jnp.dot(h, w_ref[:, c0:c0 + n], preferred_element_type=F32)
    u = cols(0, POOL_W)
    gp = cols(POOL_W, POOL_W)
    q = cols(2 * POOL_W, ATTN_W)
    k = cols(2 * POOL_W + ATTN_W, ATTN_W)
    v = cols(2 * POOL_W + 2 * ATTN_W, ATTN_W)
    ga = cols(2 * POOL_W + 3 * ATTN_W, ATTN_W)
    fl = cols(2 * POOL_W + 4 * ATTN_W, LANES)
    return u, gp, q, k, v, ga, fl


def _proj_prompt_kernel(x_ref, nw_ref, w_ref, bf_ref, pp_ref, cc_ref,
                        u_ref, sgp_ref, qa_ref, ka_ref, k_ref, v_ref, vb_ref, sga_ref, logf_ref,
                        carry_ref, *, tm, tiles_per_batch):
    i = pl.program_id(0)
    h = _rmsnorm(x_ref[...], nw_ref[...]).astype(BF16)
    u, gp, q, k, v, ga, fl = _project(h, w_ref)
    u_ref[...] = u
    sgp_ref[...] = _silu(gp).astype(BF16)
    k_ref[...] = k
    v_ref[...] = v
    vb_ref[...] = v.astype(BF16)
    sga_ref[...] = _silu(ga).astype(BF16)
    logf = _log_sigmoid(fl + bf_ref[...])
    logf_ref[...] = logf[:, 0:N_HEADS]

    row = lax.broadcasted_iota(jnp.int32, (tm, LANES), 0)
    acc = logf
    d = 1
    while d < tm:
        acc = acc + jnp.where(row >= d, pltpu.roll(acc, d, axis=0), 0.0)
        d *= 2
    @pl.when(i % tiles_per_batch == 0)
    def _():
        carry_ref[...] = jnp.zeros(carry_ref.shape, F32)

    f_run = acc + carry_ref[0:1, :]
    carry_ref[...] = jnp.broadcast_to(f_run[tm - 1:tm, :], carry_ref.shape)

    f2 = f_run * LOG2E
    p1 = f2.astype(BF16).astype(F32)
    r1 = f2 - p1
    p2 = r1.astype(BF16).astype(F32)
    p3 = (r1 - p2).astype(BF16).astype(F32)
    lane = lax.broadcasted_iota(jnp.int32, (tm, LANES), 1)
    pieces = jnp.where(lane < 8, p1, jnp.where(lane < 16, p2, jnp.where(lane < 24, p3, 0.0)))
    bias = jnp.dot(pieces.astype(BF16), pp_ref[...], preferred_element_type=F32) + cc_ref[...]

    qs = q * Q_SCALE
    first_half = lane < HEAD_DIM
    kbase = N_PAIRS * LANES
    for p in range(N_PAIRS):
        qa_ref[:, 2 * LANES * p:2 * LANES * p + LANES] = qs[:, LANES * p:LANES * (p + 1)].astype(BF16)
        qa_ref[:, 2 * LANES * p + LANES:2 * LANES * (p + 1)] = (
            bias[:, LANES * p:LANES * (p + 1)].astype(BF16))
        kp = k[:, LANES * p:LANES * (p + 1)]
        for e in range(2):
            hh = 2 * p + e
            keep = first_half if e == 0 else jnp.logical_not(first_half)
            ka_ref[:, 2 * LANES * hh:2 * LANES * hh + LANES] = jnp.where(keep, kp, 0.0).astype(BF16)
            ka_ref[:, 2 * LANES * hh + LANES:2 * LANES * (hh + 1)] = (
                bias[:, kbase + LANES * hh:kbase + LANES * (hh + 1)].astype(BF16))


def _proj_prompt(x2, nw, w, bft, pp, cc, *, seq):
    rows = x2.shape[0]
    tm = min(PROJ_TM, seq)
    assert seq % tm == 0 and rows % seq == 0
    row_blk = lambda n: pl.BlockSpec((tm, n), lambda i: (i, 0))
    const = lambda shape: pl.BlockSpec(shape, lambda i: (0, 0))
    out_shape = (
        jax.ShapeDtypeStruct((rows, POOL_W), F32),
        jax.ShapeDtypeStruct((rows, POOL_W), BF16),
        jax.ShapeDtypeStruct((rows, N_PAIRS * 2 * LANES), BF16),
        jax.ShapeDtypeStruct((rows, N_HEADS * 2 * LANES), BF16),
        jax.ShapeDtypeStruct((rows, ATTN_W), F32),
        jax.ShapeDtypeStruct((rows, ATTN_W), F32),
        jax.ShapeDtypeStruct((rows, ATTN_W), BF16),
        jax.ShapeDtypeStruct((rows, ATTN_W), BF16),
        jax.ShapeDtypeStruct((rows, N_HEADS), F32),
    )
    return pl.pallas_call(
        functools.partial(_proj_prompt_kernel, tm=tm, tiles_per_batch=seq // tm),
        grid=(rows // tm,),
        in_specs=[row_blk(D_MODEL), const((1, D_MODEL)), const((D_MODEL, W_COLS)), const((1, LANES)),
                  const((LANES, BIAS_COLS)), const((1, BIAS_COLS))],
        out_specs=tuple(row_blk(s.shape[1]) for s in out_shape),
        out_shape=out_shape,
        scratch_shapes=[pltpu.VMEM((8, LANES), F32)],
        compiler_params=pltpu.CompilerParams(dimension_semantics=("arbitrary",),
                                             vmem_limit_bytes=VMEM_LIMIT),
        name="proj_prompt",
    )(x2, nw, w, bft, pp, cc)


def _attn_prompt_kernel(qa_ref, ka_ref, vb_ref, o_ref, m_ref, l_ref, acc_ref, *, t):
    i = pl.program_id(2)
    q = qa_ref[...]
    m_ref[...] = jnp.full(m_ref.shape, NEG, F32)
    l_ref[...] = jnp.zeros(l_ref.shape, F32)
    acc_ref[...] = jnp.zeros(acc_ref.shape, F32)
    reps = t // LANES

    def block(kstart, diagonal):
        vblk = vb_ref[pl.ds(kstart, t), :]
        for e in range(2):
            kblk = ka_ref[pl.ds(kstart, t), 2 * LANES * e:2 * LANES * (e + 1)]
            s = lax.dot_general(q, kblk, (((1,), (1,)), ((), ())), preferred_element_type=F32)
            if diagonal:
                r = lax.broadcasted_iota(jnp.int32, (t, t), 0)
                c = lax.broadcasted_iota(jnp.int32, (t, t), 1)
                s = jnp.where(c <= r, s, NEG)
            m_prev = m_ref[e]
            m_next = jnp.maximum(m_prev, jnp.max(s, axis=1, keepdims=True))
            alpha = jnp.exp2(m_prev - m_next)
            p = jnp.exp2(s - pltpu.repeat(m_next, reps, axis=1))
            l_ref[e] = alpha * l_ref[e] + jnp.sum(p, axis=1, keepdims=True)
            m_ref[e] = m_next
            acc_ref[e] = alpha * acc_ref[e] + jnp.dot(p.astype(BF16), vblk,
                                                      preferred_element_type=F32)

    def body(j, carry):
        block(pl.multiple_of(j * t, t), False)
        return carry

    lax.fori_loop(0, i, body, 0)
    block(pl.multiple_of(i * t, t), True)

    lane = lax.broadcasted_iota(jnp.int32, (t, LANES), 1)
    o = jnp.where(lane < HEAD_DIM, acc_ref[0] / l_ref[0], acc_ref[1] / l_ref[1])
    o_ref[...] = o.astype(o_ref.dtype)


def _attn_prompt(qa, ka, vb, *, n_batch, seq):
    t = min(ATTN_T, seq)
    assert seq % t == 0
    nq = seq // t
    return pl.pallas_call(
        functools.partial(_attn_prompt_kernel, t=t),
        grid=(n_batch, N_PAIRS, nq),
        in_specs=[pl.BlockSpec((t, 2 * LANES), lambda b, p, i: (b * nq + i, p)),
                  pl.BlockSpec((seq, 4 * LANES), lambda b, p, i: (b, p)),
                  pl.BlockSpec((seq, LANES), lambda b, p, i: (b, p))],
        out_specs=pl.BlockSpec((t, LANES), lambda b, p, i: (b * nq + i, p)),
        out_shape=jax.ShapeDtypeStruct((n_batch * seq, ATTN_W), BF16),
        scratch_shapes=[pltpu.VMEM((2, t, LANES), F32), pltpu.VMEM((2, t, LANES), F32),
                        pltpu.VMEM((2, t, LANES), F32)],
        compiler_params=pltpu.CompilerParams(
            dimension_semantics=("arbitrary", "arbitrary", "arbitrary"),
            vmem_limit_bytes=VMEM_LIMIT),
        name="attn_prompt",
    )(qa, ka, vb)


def _merge(x, pooled, sgp, attn, sga, pw_ref, ps, wo_ref, nf):
    mixed = [jnp.dot(pooled[:, POOL_GC * g:POOL_GC * (g + 1)].astype(BF16), pw_ref[g],
                     preferred_element_type=F32) for g in range(len(POOL_WINDOWS))]
    pool_out = jnp.concatenate(mixed, axis=1) * ps
    mix = jnp.concatenate([(pool_out * sgp.astype(F32)).astype(BF16),
                           (attn.astype(F32) * sga.astype(F32)).astype(BF16)], axis=1)
    xo = x + jnp.dot(mix, wo_ref[...], preferred_element_type=F32)
    return _rmsnorm(xo, nf)


def _merge_prompt_kernel(x_ref, u_ref, uh_ref, sgp_ref, attn_ref, sga_ref, pw_ref, ps_ref, wo_ref,
                         nf_ref, y_ref, pool_ref, *, tm, tiles_per_batch):
    i = pl.program_id(0)
    ti = i % tiles_per_batch
    u = u_ref[...]
    halo = jnp.where(ti == 0, 0.0, uh_ref[...])
    ext = jnp.concatenate([halo, u], axis=0)
    pos = ti * tm + lax.broadcasted_iota(jnp.int32, (tm, 1), 0)
    groups = []
    level = ext
    shift = 1
    for g, w in enumerate(POOL_WINDOWS):
        while shift < w:
            level = level + pltpu.roll(level, shift, axis=0)
            shift *= 2
        cnt = jnp.minimum(w, pos + 1).astype(F32)
        sl = slice(POOL_GC * g, POOL_GC * (g + 1))
        groups.append(level[16:, sl] / cnt - u[:, sl])
    pooled = jnp.concatenate(groups, axis=1)
    y_ref[...] = _merge(x_ref[...], pooled, sgp_ref[...], attn_ref[...], sga_ref[...], pw_ref,
                        ps_ref[...], wo_ref, nf_ref[...])

    @pl.when(ti == tiles_per_batch - 1)
    def _():
        pool_ref[0] = u_ref[pl.ds(tm - POOL_HIST, POOL_HIST), :]


def _merge_prompt(x2, u, sgp, attn, sga, pw, ps, wo, nf, *, n_batch, seq):
    rows = x2.shape[0]
    tm = min(PROJ_TM, seq)
    tpb = seq // tm
    halo_blocks = tm // 16
    row_blk = lambda n: pl.BlockSpec((tm, n), lambda i: (i, 0))
    const2 = lambda shape: pl.BlockSpec(shape, lambda i: (0, 0))
    return pl.pallas_call(
        functools.partial(_merge_prompt_kernel, tm=tm, tiles_per_batch=tpb),
        grid=(rows // tm,),
        in_specs=[row_blk(D_MODEL), row_blk(POOL_W),
                  pl.BlockSpec((16, POOL_W), lambda i: (jnp.maximum(i * halo_blocks - 1, 0), 0)),
                  row_blk(POOL_W), row_blk(ATTN_W), row_blk(ATTN_W),
                  pl.BlockSpec((len(POOL_WINDOWS), POOL_GC, POOL_GC), lambda i: (0, 0, 0)),
                  const2((1, POOL_W)), const2((D_MODEL, D_MODEL)), const2((1, D_MODEL))],
        out_specs=(row_blk(D_MODEL),
                   pl.BlockSpec((1, POOL_HIST, POOL_W), lambda i: (i // tpb, 0, 0))),
        out_shape=(jax.ShapeDtypeStruct((rows, D_MODEL), F32),
                   jax.ShapeDtypeStruct((n_batch, POOL_HIST, POOL_W), F32)),
        compiler_params=pltpu.CompilerParams(dimension_semantics=("arbitrary",),
                                             vmem_limit_bytes=VMEM_LIMIT),
        name="merge_prompt",
    )(x2, u, u, sgp, attn, sga, pw, ps, wo, nf)


def _proj_sample_kernel(x_ref, nw_ref, w_ref, bf_ref,
                        u_ref, sgp_ref, q_ref, k_ref, v_ref, sga_ref, logf_ref, cn_ref, carry_ref):
    i = pl.program_id(0)
    h = _rmsnorm(x_ref[...], nw_ref[...]).astype(BF16)
    u, gp, q, k, v, ga, fl = _project(h, w_ref)
    u_ref[...] = u
    sgp_ref[...] = _silu(gp).astype(BF16)
    q_ref[...] = q * Q_SCALE
    k_ref[...] = k
    v_ref[...] = v
    sga_ref[...] = _silu(ga).astype(BF16)
    logf = _log_sigmoid(fl + bf_ref[...])
    logf_ref[...] = logf
    @pl.when(i == 0)
    def _():
        carry_ref[...] = jnp.zeros(carry_ref.shape, F32)

    cn = carry_ref[...] + logf
    carry_ref[...] = cn
    cn_ref[...] = cn


def _proj_sample(xs2, nw, w, bft, *, n_new):
    nb = xs2.shape[0]
    blk = lambda n: pl.BlockSpec((nb, n), lambda i: (0, i))
    const = lambda shape: pl.BlockSpec(shape, lambda i: (0, 0))
    widths = (POOL_W, POOL_W, ATTN_W, ATTN_W, ATTN_W, ATTN_W, LANES, LANES)
    dtypes = (F32, BF16, F32, F32, F32, BF16, F32, F32)
    return pl.pallas_call(
        _proj_sample_kernel,
        grid=(n_new,),
        in_specs=[blk(D_MODEL), const((1, D_MODEL)), const((D_MODEL, W_COLS)), const((1, LANES))],
        out_specs=tuple(blk(n) for n in widths),
        out_shape=tuple(jax.ShapeDtypeStruct((nb, n_new * n), dt) for n, dt in zip(widths, dtypes)),
        scratch_shapes=[pltpu.VMEM((nb, LANES), F32)],
        compiler_params=pltpu.CompilerParams(dimension_semantics=("arbitrary",),
                                             vmem_limit_bytes=VMEM_LIMIT),
        name="proj_sample",
    )(xs2, nw, w, bft)


def _decode_kernel(pt_ref, q_ref, kn_ref, vn_ref, cn_ref, *refs, n_new, page, n_pages_step):
    del pt_ref
    k_refs = refs[0:n_pages_step]
    v_refs = refs[n_pages_step:2 * n_pages_step]
    lf_refs = refs[2 * n_pages_step:3 * n_pages_step]
    o_ref = refs[3 * n_pages_step]
    qbd_ref, cn8_ref, cncol_ref, m_ref, l_ref, acc_ref, tail_ref = refs[3 * n_pages_step + 1:]
    c = pl.program_id(1)
    rows = n_new * N_HEADS
    row_w = lax.broadcasted_iota(jnp.int32, (rows, ATTN_W), 0)
    lane_w = lax.broadcasted_iota(jnp.int32, (rows, ATTN_W), 1)
    head_lanes = (row_w % N_HEADS) == (lane_w // HEAD_DIM)
    row1 = lax.broadcasted_iota(jnp.int32, (rows, 1), 0)

    @pl.when(c == 0)
    def _():
        q = q_ref[0]
        qrep = jnp.concatenate(
            [jnp.broadcast_to(q[i:i + 1, :], (N_HEADS, ATTN_W)) for i in range(n_new)], axis=0)
        qbd = jnp.where(head_lanes, qrep, 0.0)
        qbd_ref[...] = qbd.astype(BF16)
        cn8_ref[...] = jnp.zeros(cn8_ref.shape, F32)
        cn8_ref[0:n_new, :] = cn_ref[0]
        cnt = jnp.transpose(cn8_ref[...])[0:rows, :] * LOG2E
        cncol = jnp.zeros((rows, 1), F32)
        for i in range(n_new):
            cncol = jnp.where(row1 // N_HEADS == i, cnt[:, i:i + 1], cncol)
        cncol_ref[...] = jnp.broadcast_to(cncol, cncol_ref.shape)
        kn = kn_ref[0]
        vn = vn_ref[0]
        qbf = qbd.astype(BF16).astype(F32)
        s_new = []
        for j in range(n_new):
            sj = jnp.sum(qbf * kn[j:j + 1, :], axis=1, keepdims=True) + cncol - cnt[:, j:j + 1]
            s_new.append(jnp.where(row1 // N_HEADS >= j, sj, NEG))
        m0 = functools.reduce(jnp.maximum, s_new)
        l0 = jnp.zeros((rows, 1), F32)
        a0 = jnp.zeros((rows, ATTN_W), F32)
        for j in range(n_new):
            pj = jnp.exp2(s_new[j] - m0)
            l0 = l0 + pj
            a0 = a0 + pj * vn[j:j + 1, :]
        m_ref[...] = jnp.broadcast_to(m0, m_ref.shape)
        l_ref[...] = jnp.broadcast_to(l0, l_ref.shape)
        acc_ref[...] = a0
        tail_ref[...] = jnp.zeros(tail_ref.shape, F32)

    lane8 = lax.broadcasted_iota(jnp.int32, (N_HEADS, page), 1)
    tail = tail_ref[...]
    s_parts, g_parts, v_parts = [], [], []
    for r in range(n_pages_step):
        kb = k_refs[r][0].astype(BF16)
        s_parts.append(lax.dot_general(qbd_ref[...], kb, (((1,), (1,)), ((), ())),
                                       preferred_element_type=F32))
        lf = lf_refs[r][0]
        incl = lf
        d = 1
        while d < page:
            incl = incl + jnp.where(lane8 + d < page, pltpu.roll(incl, page - d, axis=1), 0.0)
            d *= 2
        g_parts.append(incl - lf + tail)
        tail = tail + incl[:, 0:1]
        v_parts.append(v_refs[r][0].astype(BF16))
    tail_ref[...] = tail
    g = jnp.concatenate(g_parts, axis=1) * LOG2E
    s = (jnp.concatenate(s_parts, axis=1) + jnp.concatenate([g] * n_new, axis=0)
         + cncol_ref[:, 0:1])
    m_prev = m_ref[...]
    m_next = jnp.maximum(m_prev, jnp.max(s, axis=1, keepdims=True))
    alpha = jnp.exp2(m_prev - m_next)
    p = jnp.exp2(s - pltpu.repeat(m_next, n_pages_step * page // LANES, axis=1))
    l_ref[...] = alpha * l_ref[...] + jnp.sum(p, axis=1, keepdims=True)
    m_ref[...] = m_next
    acc_ref[...] = (pltpu.repeat(alpha, ATTN_W // LANES, axis=1) * acc_ref[...]
                    + jnp.dot(p.astype(BF16), jnp.concatenate(v_parts, axis=0),
                              preferred_element_type=F32))

    @pl.when(c == pl.num_programs(1) - 1)
    def _():
        o = acc_ref[...] / pltpu.repeat(l_ref[...], ATTN_W // LANES, axis=1)
        o = jnp.where(head_lanes, o, 0.0)
        for i in range(n_new):
            o_ref[0, i:i + 1, :] = jnp.sum(o[N_HEADS * i:N_HEADS * (i + 1), :], axis=0,
                                           keepdims=True)


def _decode(page_table, q3, kn3, vn3, cn3, ck, cv, clf_t):
    nb, n_new, _ = q3.shape
    n_pages = page_table.shape[1]
    page = ck.shape[1]
    pps = DEC_PAGES
    while n_pages % pps:
        pps //= 2
    n_steps = n_pages // pps
    rows = n_new * N_HEADS
    assert page % LANES == 0 and rows % 8 == 0

    def page_map(r):
        def index_map(b, c, pt):
            return (pt[b * n_pages + n_pages - 1 - (c * pps + r)], 0, 0)
        return index_map

    per_b = lambda n: pl.BlockSpec((1, n_new, n), lambda b, c, pt: (b, 0, 0))
    in_specs = [per_b(ATTN_W), per_b(ATTN_W), per_b(ATTN_W), per_b(LANES)]
    in_specs += [pl.BlockSpec((1, page, ATTN_W), page_map(r)) for r in range(pps)]
    in_specs += [pl.BlockSpec((1, page, ATTN_W), page_map(r)) for r in range(pps)]
    in_specs += [pl.BlockSpec((1, N_HEADS, page), page_map(r)) for r in range(pps)]
    grid_spec = pltpu.PrefetchScalarGridSpec(
        num_scalar_prefetch=1,
        grid=(nb, n_steps),
        in_specs=in_specs,
        out_specs=pl.BlockSpec((1, n_new, ATTN_W), lambda b, c, pt: (b, 0, 0)),
        scratch_shapes=[pltpu.VMEM((rows, ATTN_W), BF16),
                        pltpu.VMEM((8, LANES), F32),
                        pltpu.VMEM((rows, LANES), F32),
                        pltpu.VMEM((rows, LANES), F32),
                        pltpu.VMEM((rows, LANES), F32),
                        pltpu.VMEM((rows, ATTN_W), F32),
                        pltpu.VMEM((N_HEADS, LANES), F32)])
    return pl.pallas_call(
        functools.partial(_decode_kernel, n_new=n_new, page=page, n_pages_step=pps),
        grid_spec=grid_spec,
        out_shape=jax.ShapeDtypeStruct((nb, n_new, ATTN_W), F32),
        compiler_params=pltpu.CompilerParams(dimension_semantics=("arbitrary", "arbitrary"),
                                             vmem_limit_bytes=VMEM_LIMIT),
        name="decode_attn",
    )(page_table.reshape(-1), q3, kn3, vn3, cn3, *([ck] * pps), *([cv] * pps), *([clf_t] * pps))


def _merge_sample_kernel(x_ref, u_ref, sp_ref, sgp_ref, attn_ref, sga_ref, pw_ref, ps_ref, wo_ref,
                         nf_ref, y_ref, pool_ref, *, n_new, past):
    u = [u_ref[:, POOL_W * i:POOL_W * (i + 1)] for i in range(n_new)]
    hist = [sp_ref[:, POOL_W * r:POOL_W * (r + 1)] for r in range(POOL_HIST)]
    ext = hist + u
    pooled_rows = []
    for i in range(n_new):
        groups = []
        for g, w in enumerate(POOL_WINDOWS):
            sl = slice(POOL_GC * g, POOL_GC * (g + 1))
            end = POOL_HIST + i
            total = ext[end][:, sl]
            for r in range(end - w + 1, end):
                total = total + ext[r][:, sl]
            groups.append(total / float(min(w, past + i + 1)) - u[i][:, sl])
        pooled_rows.append(jnp.concatenate(groups, axis=1))
    pooled = jnp.concatenate(pooled_rows, axis=0)
    cat = lambda ref, n: jnp.concatenate([ref[:, n * i:n * (i + 1)] for i in range(n_new)], axis=0)
    y = _merge(cat(x_ref, D_MODEL), pooled, cat(sgp_ref, POOL_W), cat(attn_ref, ATTN_W),
               cat(sga_ref, ATTN_W), pw_ref, ps_ref[...], wo_ref, nf_ref[...])
    nb = x_ref.shape[0]
    for i in range(n_new):
        y_ref[:, D_MODEL * i:D_MODEL * (i + 1)] = y[nb * i:nb * (i + 1), :]
    new_hist = ext[-POOL_HIST:]
    for r in range(POOL_HIST):
        pool_ref[:, POOL_W * r:POOL_W * (r + 1)] = new_hist[r]


def _merge_sample(xs2, u2, sp2, sgp2, attn2, sga2, pw, ps, wo, nf, *, n_new, past):
    nb = xs2.shape[0]
    return pl.pallas_call(
        functools.partial(_merge_sample_kernel, n_new=n_new, past=past),
        out_shape=(jax.ShapeDtypeStruct((nb, n_new * D_MODEL), F32),
                   jax.ShapeDtypeStruct((nb, POOL_HIST * POOL_W), F32)),
        compiler_params=pltpu.CompilerParams(vmem_limit_bytes=VMEM_LIMIT),
        name="merge_sample",
    )(xs2, u2, sp2, sgp2, attn2, sga2, pw, ps, wo, nf)


def kernel(x_prompt, x_sample, cache_k, cache_v, cache_logf, state_pool, page_table,
           norm_w, w_in, b_forget, pool_w, pool_scale, w_out, norm_f):
    depth = norm_w.shape[0]
    assert depth == 1, "a single layer is supported"
    b_p, seq, _ = x_prompt.shape
    b_s, t_s, _ = x_sample.shape
    n_phys, page = cache_k.shape[1], cache_k.shape[2]
    past = page_table.shape[1] * page
    ll = 0

    n_main = 2 * POOL_W + 4 * ATTN_W
    w = jnp.concatenate([w_in[ll][:, :n_main], jnp.tile(w_in[ll][:, n_main:], (1, LANES // N_HEADS))],
                        axis=1).astype(BF16)
    bft = jnp.tile(b_forget[ll], LANES // N_HEADS).reshape(1, LANES).astype(F32)
    nw = norm_w[ll].reshape(1, D_MODEL)
    nf = norm_f.reshape(1, D_MODEL)
    pw = pool_w[ll].astype(BF16)
    ps = pool_scale[ll].reshape(1, POOL_W)
    wo = w_out[ll].astype(BF16)
    pp_np, cc_np = _placement()
    pp = jnp.asarray(pp_np, BF16)
    cc = jnp.asarray(cc_np, F32)

    xp2 = x_prompt.reshape(b_p * seq, D_MODEL)
    u, sgp, qa, ka, kf, vf, vb, sga, logf = _proj_prompt(xp2, nw, w, bft, pp, cc, seq=seq)
    attn = _attn_prompt(qa, ka, vb, n_batch=b_p, seq=seq)
    yp, pool_p = _merge_prompt(xp2, u, sgp, attn, sga, pw, ps, wo, nf, n_batch=b_p, seq=seq)

    xs2 = x_sample.reshape(b_s, t_s * D_MODEL)
    us, sgps, qs, ks, vs, sgas, lfs, cns = _proj_sample(xs2, nw, w, bft, n_new=t_s)
    ck = cache_k[ll].reshape(n_phys, page, ATTN_W)
    cv = cache_v[ll].reshape(n_phys, page, ATTN_W)
    clf_t = jnp.swapaxes(cache_logf[ll], 1, 2)
    attn_s = _decode(page_table, qs.reshape(b_s, t_s, ATTN_W), ks.reshape(b_s, t_s, ATTN_W),
                     vs.reshape(b_s, t_s, ATTN_W), cns.reshape(b_s, t_s, LANES), ck, cv, clf_t)
    ys, pool_s = _merge_sample(xs2, us, state_pool[ll].reshape(b_s, POOL_HIST * POOL_W), sgps,
                               attn_s.reshape(b_s, t_s * ATTN_W), sgas, pw, ps, wo, nf,
                               n_new=t_s, past=past)

    return (yp.reshape(b_p, seq, D_MODEL),
            ys.reshape(b_s, t_s, D_MODEL),
            kf.reshape(1, b_p, seq, N_HEADS, HEAD_DIM),
            vf.reshape(1, b_p, seq, N_HEADS, HEAD_DIM),
            logf.reshape(1, b_p, seq, N_HEADS),
            pool_p.reshape(1, b_p, POOL_HIST, POOL_W),
            ks.reshape(1, b_s, t_s, N_HEADS, HEAD_DIM),
            vs.reshape(1, b_s, t_s, N_HEADS, HEAD_DIM),
            lfs.reshape(b_s, t_s, LANES)[:, :, :N_HEADS].reshape(1, b_s, t_s, N_HEADS),
            pool_s.reshape(1, b_s, POOL_HIST, POOL_W))
```

```python
import functools
import math

import numpy as np
import jax
import jax.numpy as jnp
from jax import lax
from jax.experimental import pallas as pl
from jax.experimental.pallas import tpu as pltpu

D_MODEL = 1024
POOL_W = 512
ATTN_W = 512
N_HEADS = 8
HEAD_DIM = 64
POOL_WINDOWS = (2, 4, 8, 16)
POOL_GC = 128
POOL_HIST = 15
EPS = 1e-6

LANES = 128
LOG2E = 1.4426950408889634
Q_SCALE = LOG2E / math.sqrt(HEAD_DIM)
NEG = -1e30
N_PAIRS = N_HEADS // 2
W_COLS = 2 * POOL_W + 4 * ATTN_W + LANES
BIAS_COLS = N_PAIRS * LANES + N_HEADS * LANES
VMEM_LIMIT = 56 * 1024 * 1024

PROJ_TM = 512
ATTN_T = 512
DEC_PAGES = 8

BF16 = jnp.bfloat16
F32 = jnp.float32


def _placement():
    pp = np.zeros((LANES, BIAS_COLS), np.float32)
    cc = np.zeros((1, BIAS_COLS), np.float32)
    kbase = N_PAIRS * LANES
    for h in range(N_HEADS):
        p, e = divmod(h, 2)
        for i in range(3):
            pp[8 * i + h, LANES * p + 6 * e + i] = 1.0
            pp[8 * i + h, kbase + LANES * h + 6 * e + 3 + i] = -1.0
            cc[0, LANES * p + 6 * e + 3 + i] = 1.0
            cc[0, kbase + LANES * h + 6 * e + i] = 1.0
    return pp, cc


def _log_sigmoid(x):
    return jnp.minimum(x, 0.0) - jnp.log(1.0 + jnp.exp(-jnp.abs(x)))


def _silu(x):
    return x * jax.nn.sigmoid(x)


def _rmsnorm(xf, w):
    ms = jnp.mean(xf * xf, axis=-1, keepdims=True)
    return xf * lax.rsqrt(ms + EPS) * w


def _project(h, w_ref):
    def cols(c0, n):
        return jnp.dot(h, w_ref[:, c0:c0 + n], preferred_element_type=F32)
    u = cols(0, POOL_W)
    gp = cols(POOL_W, POOL_W)
    q = cols(2 * POOL_W, ATTN_W)
    k = cols(2 * POOL_W + ATTN_W, ATTN_W)
    v = cols(2 * POOL_W + 2 * ATTN_W, ATTN_W)
    ga = cols(2 * POOL_W + 3 * ATTN_W, ATTN_W)
    fl = cols(2 * POOL_W + 4 * ATTN_W, LANES)
    return u, gp, q, k, v, ga, fl


def _proj_prompt_kernel(x_ref, nw_ref, w_ref, bf_ref, pp_ref, cc_ref,
                        u_ref, sgp_ref, qa_ref, ka_ref, k_ref, v_ref, vb_ref, sga_ref, logf_ref,
                        carry_ref, *, tm, tiles_per_batch):
    i = pl.program_id(0)
    h = _rmsnorm(x_ref[...], nw_ref[...]).astype(BF16)
    u, gp, q, k, v, ga, fl = _project(h, w_ref)
    u_ref[...] = u
    sgp_ref[...] = _silu(gp).astype(BF16)
    k_ref[...] = k
    v_ref[...] = v
    vb_ref[...] = v.astype(BF16)
    sga_ref[...] = _silu(ga).astype(BF16)
    logf = _log_sigmoid(fl + bf_ref[...])
    logf_ref[...] = logf[:, 0:N_HEADS]

    row = lax.broadcasted_iota(jnp.int32, (tm, LANES), 0)
    acc = logf
    d = 1
    while d < tm:
        acc = acc + jnp.where(row >= d, pltpu.roll(acc, d, axis=0), 0.0)
        d *= 2
    @pl.when(i % tiles_per_batch == 0)
    def _():
        carry_ref[...] = jnp.zeros(carry_ref.shape, F32)

    f_run = acc + carry_ref[0:1, :]
    carry_ref[...] = jnp.broadcast_to(f_run[tm - 1:tm, :], carry_ref.shape)

    f2 = f_run * LOG2E
    p1 = f2.astype(BF16).astype(F32)
    r1 = f2 - p1
    p2 = r1.astype(BF16).astype(F32)
    p3 = (r1 - p2).astype(BF16).astype(F32)
    lane = lax.broadcasted_iota(jnp.int32, (tm, LANES), 1)
    pieces = jnp.where(lane < 8, p1, jnp.where(lane < 16, p2, jnp.where(lane < 24, p3, 0.0)))
    bias = jnp.dot(pieces.astype(BF16), pp_ref[...], preferred_element_type=F32) + cc_ref[...]

    qs = q * Q_SCALE
    first_half = lane < HEAD_DIM
    kbase = N_PAIRS * LANES
    for p in range(N_PAIRS):
        qa_ref[:, 2 * LANES * p:2 * LANES * p + LANES] = qs[:, LANES * p:LANES * (p + 1)].astype(BF16)
        qa_ref[:, 2 * LANES * p + LANES:2 * LANES * (p + 1)] = (
            bias[:, LANES * p:LANES * (p + 1)].astype(BF16))
        kp = k[:, LANES * p:LANES * (p + 1)]
        for e in range(2):
            hh = 2 * p + e
            keep = first_half if e == 0 else jnp.logical_not(first_half)
            ka_ref[:, 2 * LANES * hh:2 * LANES * hh + LANES] = jnp.where(keep, kp, 0.0).astype(BF16)
            ka_ref[:, 2 * LANES * hh + LANES:2 * LANES * (hh + 1)] = (
                bias[:, kbase + LANES * hh:kbase + LANES * (hh + 1)].astype(BF16))


def _proj_prompt(x2, nw, w, bft, pp, cc, *, seq):
    rows = x2.shape[0]
    tm = min(PROJ_TM, seq)
    assert seq % tm == 0 and rows % seq == 0
    row_blk = lambda n: pl.BlockSpec((tm, n), lambda i: (i, 0))
    const = lambda shape: pl.BlockSpec(shape, lambda i: (0, 0))
    out_shape = (
        jax.ShapeDtypeStruct((rows, POOL_W), F32),
        jax.ShapeDtypeStruct((rows, POOL_W), BF16),
        jax.ShapeDtypeStruct((rows, N_PAIRS * 2 * LANES), BF16),
        jax.ShapeDtypeStruct((rows, N_HEADS * 2 * LANES), BF16),
        jax.ShapeDtypeStruct((rows, ATTN_W), F32),
        jax.ShapeDtypeStruct((rows, ATTN_W), F32),
        jax.ShapeDtypeStruct((rows, ATTN_W), BF16),
        jax.ShapeDtypeStruct((rows, ATTN_W), BF16),
        jax.ShapeDtypeStruct((rows, N_HEADS), F32),
    )
    return pl.pallas_call(
        functools.partial(_proj_prompt_kernel, tm=tm, tiles_per_batch=seq // tm),
        grid=(rows // tm,),
        in_specs=[row_blk(D_MODEL), const((1, D_MODEL)), const((D_MODEL, W_COLS)), const((1, LANES)),
                  const((LANES, BIAS_COLS)), const((1, BIAS_COLS))],
        out_specs=tuple(row_blk(s.shape[1]) for s in out_shape),
        out_shape=out_shape,
        scratch_shapes=[pltpu.VMEM((8, LANES), F32)],
        compiler_params=pltpu.CompilerParams(dimension_semantics=("arbitrary",),
                                             vmem_limit_bytes=VMEM_LIMIT),
        name="proj_prompt",
    )(x2, nw, w, bft, pp, cc)


def _attn_prompt_kernel(qa_ref, ka_ref, vb_ref, o_ref, m_ref, l_ref, acc_ref, *, t):
    i = pl.program_id(2)
    q = qa_ref[...]
    m_ref[...] = jnp.full(m_ref.shape, NEG, F32)
    l_ref[...] = jnp.zeros(l_ref.shape, F32)
    acc_ref[...] = jnp.zeros(acc_ref.shape, F32)
    reps = t // LANES

    def block(kstart, diagonal):
        vblk = vb_ref[pl.ds(kstart, t), :]
        for e in range(2):
            kblk = ka_ref[pl.ds(kstart, t), 2 * LANES * e:2 * LANES * (e + 1)]
            s = lax.dot_general(q, kblk, (((1,), (1,)), ((), ())), preferred_element_type=F32)
            if diagonal:
                r = lax.broadcasted_iota(jnp.int32, (t, t), 0)
                c = lax.broadcasted_iota(jnp.int32, (t, t), 1)
                s = jnp.where(c <= r, s, NEG)
            m_prev = m_ref[e]
            m_next = jnp.maximum(m_prev, jnp.max(s, axis=1, keepdims=True))
            alpha = jnp.exp2(m_prev - m_next)
            p = jnp.exp2(s - pltpu.repeat(m_next, reps, axis=1))
            l_ref[e] = alpha * l_ref[e] + jnp.sum(p, axis=1, keepdims=True)
            m_ref[e] = m_next
            acc_ref[e] = alpha * acc_ref[e] + jnp.dot(p.astype(BF16), vblk,
                                                      preferred_element_type=F32)

    def body(j, carry):
        block(pl.multiple_of(j * t, t), False)
        return carry

    lax.fori_loop(0, i, body, 0)
    block(pl.multiple_of(i * t, t), True)

    lane = lax.broadcasted_iota(jnp.int32, (t, LANES), 1)
    o = jnp.where(lane < HEAD_DIM, acc_ref[0] / l_ref[0], acc_ref[1] / l_ref[1])
    o_ref[...] = o.astype(o_ref.dtype)


def _attn_prompt(qa, ka, vb, *, n_batch, seq):
    t = min(ATTN_T, seq)
    assert seq % t == 0
    nq = seq // t
    return pl.pallas_call(
        functools.partial(_attn_prompt_kernel, t=t),
        grid=(n_batch, N_PAIRS, nq),
        in_specs=[pl.BlockSpec((t, 2 * LANES), lambda b, p, i: (b * nq + i, p)),
                  pl.BlockSpec((seq, 4 * LANES), lambda b, p, i: (b, p)),
                  pl.BlockSpec((seq, LANES), lambda b, p, i: (b, p))],
        out_specs=pl.BlockSpec((t, LANES), lambda b, p, i: (b * nq + i, p)),
        out_shape=jax.ShapeDtypeStruct((n_batch * seq, ATTN_W), BF16),
        scratch_shapes=[pltpu.VMEM((2, t, LANES), F32), pltpu.VMEM((2, t, LANES), F32),
                        pltpu.VMEM((2, t, LANES), F32)],
        compiler_params=pltpu.CompilerParams(
            dimension_semantics=("arbitrary", "arbitrary", "arbitrary"),
            vmem_limit_bytes=VMEM_LIMIT),
        name="attn_prompt",
    )(qa, ka, vb)


def _merge(x, pooled, sgp, attn, sga, pw_ref, ps, wo_ref, nf):
    mixed = [jnp.dot(pooled[:, POOL_GC * g:POOL_GC * (g + 1)].astype(BF16), pw_ref[g],
                     preferred_element_type=F32) for g in range(len(POOL_WINDOWS))]
    pool_out = jnp.concatenate(mixed, axis=1) * ps
    mix = jnp.concatenate([(pool_out * sgp.astype(F32)).astype(BF16),
                           (attn.astype(F32) * sga.astype(F32)).astype(BF16)], axis=1)
    xo = x + jnp.dot(mix, wo_ref[...], preferred_element_type=F32)
    return _rmsnorm(xo, nf)


def _merge_prompt_kernel(x_ref, u_ref, uh_ref, sgp_ref, attn_ref, sga_ref, pw_ref, ps_ref, wo_ref,
                         nf_ref, y_ref, pool_ref, *, tm, tiles_per_batch):
    i = pl.program_id(0)
    ti = i % tiles_per_batch
    u = u_ref[...]
    halo = jnp.where(ti == 0, 0.0, uh_ref[...])
    ext = jnp.concatenate([halo, u], axis=0)
    pos = ti * tm + lax.broadcasted_iota(jnp.int32, (tm, 1), 0)
    groups = []
    level = ext
    shift = 1
    for g, w in enumerate(POOL_WINDOWS):
        while shift < w:
            level = level + pltpu.roll(level, shift, axis=0)
            shift *= 2
        cnt = jnp.minimum(w, pos + 1).astype(F32)
        sl = slice(POOL_GC * g, POOL_GC * (g + 1))
        groups.append(level[16:, sl] / cnt - u[:, sl])
    pooled = jnp.concatenate(groups, axis=1)
    y_ref[...] = _merge(x_ref[...], pooled, sgp_ref[...], attn_ref[...], sga_ref[...], pw_ref,
                        ps_ref[...], wo_ref, nf_ref[...])

    @pl.when(ti == tiles_per_batch - 1)
    def _():
        pool_ref[0] = u_ref[pl.ds(tm - POOL_HIST, POOL_HIST), :]


def _merge_prompt(x2, u, sgp, attn, sga, pw, ps, wo, nf, *, n_batch, seq):
    rows = x2.shape[0]
    tm = min(PROJ_TM, seq)
    tpb = seq // tm
    halo_blocks = tm // 16
    row_blk = lambda n: pl.BlockSpec((tm, n), lambda i: (i, 0))
    const2 = lambda shape: pl.BlockSpec(shape, lambda i: (0, 0))
    return pl.pallas_call(
        functools.partial(_merge_prompt_kernel, tm=tm, tiles_per_batch=tpb),
        grid=(rows // tm,),
        in_specs=[row_blk(D_MODEL), row_blk(POOL_W),
                  pl.BlockSpec((16, POOL_W), lambda i: (jnp.maximum(i * halo_blocks - 1, 0), 0)),
                  row_blk(POOL_W), row_blk(ATTN_W), row_blk(ATTN_W),
                  pl.BlockSpec((len(POOL_WINDOWS), POOL_GC, POOL_GC), lambda i: (0, 0, 0)),
                  const2((1, POOL_W)), const2((D_MODEL, D_MODEL)), const2((1, D_MODEL))],
        out_specs=(row_blk(D_MODEL),
                   pl.BlockSpec((1, POOL_HIST, POOL_W), lambda i: (i // tpb, 0, 0))),
        out_shape=(jax.ShapeDtypeStruct((rows, D_MODEL), F32),
                   jax.ShapeDtypeStruct((n_batch, POOL_HIST, POOL_W), F32)),
        compiler_params=pltpu.CompilerParams(dimension_semantics=("arbitrary",),
                                             vmem_limit_bytes=VMEM_LIMIT),
        name="merge_prompt",
    )(x2, u, u, sgp, attn, sga, pw, ps, wo, nf)


def _proj_sample_kernel(x_ref, nw_ref, w_ref, bf_ref,
                        u_ref, sgp_ref, q_ref, k_ref, v_ref, sga_ref, logf_ref, cn_ref, carry_ref):
    i = pl.program_id(0)
    h = _rmsnorm(x_ref[...], nw_ref[...]).astype(BF16)
    u, gp, q, k, v, ga, fl = _project(h, w_ref)
    u_ref[...] = u
    sgp_ref[...] = _silu(gp).astype(BF16)
    q_ref[...] = q * Q_SCALE
    k_ref[...] = k
    v_ref[...] = v
    sga_ref[...] = _silu(ga).astype(BF16)
    logf = _log_sigmoid(fl + bf_ref[...])
    logf_ref[...] = logf
    @pl.when(i == 0)
    def _():
        carry_ref[...] = jnp.zeros(carry_ref.shape, F32)

    cn = carry_ref[...] + logf
    carry_ref[...] = cn
    cn_ref[...] = cn


def _proj_sample(xs2, nw, w, bft, *, n_new):
    nb = xs2.shape[0]
    blk = lambda n: pl.BlockSpec((nb, n), lambda i: (0, i))
    const = lambda shape: pl.BlockSpec(shape, lambda i: (0, 0))
    widths = (POOL_W, POOL_W, ATTN_W, ATTN_W, ATTN_W, ATTN_W, LANES, LANES)
    dtypes = (F32, BF16, F32, F32, F32, BF16, F32, F32)
    return pl.pallas_call(
        _proj_sample_kernel,
        grid=(n_new,),
        in_specs=[blk(D_MODEL), const((1, D_MODEL)), const((D_MODEL, W_COLS)), const((1, LANES))],
        out_specs=tuple(blk(n) for n in widths),
        out_shape=tuple(jax.ShapeDtypeStruct((nb, n_new * n), dt) for n, dt in zip(widths, dtypes)),
        scratch_shapes=[pltpu.VMEM((nb, LANES), F32)],
        compiler_params=pltpu.CompilerParams(dimension_semantics=("arbitrary",),
                                             vmem_limit_bytes=VMEM_LIMIT),
        name="proj_sample",
    )(xs2, nw, w, bft)


def _decode_kernel(pt_ref, q_ref, kn_ref, vn_ref, cn_ref, *refs, n_new, page, n_pages_step):
    del pt_ref
    k_refs = refs[0:n_pages_step]
    v_refs = refs[n_pages_step:2 * n_pages_step]
    lf_refs = refs[2 * n_pages_step:3 * n_pages_step]
    o_ref = refs[3 * n_pages_step]
    qbd_ref, cn8_ref, cncol_ref, m_ref, l_ref, acc_ref, tail_ref = refs[3 * n_pages_step + 1:]
    c = pl.program_id(1)
    rows = n_new * N_HEADS
    row_w = lax.broadcasted_iota(jnp.int32, (rows, ATTN_W), 0)
    lane_w = lax.broadcasted_iota(jnp.int32, (rows, ATTN_W), 1)
    head_lanes = (row_w % N_HEADS) == (lane_w // HEAD_DIM)
    row1 = lax.broadcasted_iota(jnp.int32, (rows, 1), 0)

    @pl.when(c == 0)
    def _():
        q = q_ref[0]
        qrep = jnp.concatenate(
            [jnp.broadcast_to(q[i:i + 1, :], (N_HEADS, ATTN_W)) for i in range(n_new)], axis=0)
        qbd = jnp.where(head_lanes, qrep, 0.0)
        qbd_ref[...] = qbd.astype(BF16)
        cn8_ref[...] = jnp.zeros(cn8_ref.shape, F32)
        cn8_ref[0:n_new, :] = cn_ref[0]
        cnt = jnp.transpose(cn8_ref[...])[0:rows, :] * LOG2E
        cncol = jnp.zeros((rows, 1), F32)
        for i in range(n_new):
            cncol = jnp.where(row1 // N_HEADS == i, cnt[:, i:i + 1], cncol)
        cncol_ref[...] = jnp.broadcast_to(cncol, cncol_ref.shape)
        kn = kn_ref[0]
        vn = vn_ref[0]
        qbf = qbd.astype(BF16).astype(F32)
        s_new = []
        for j in range(n_new):
            sj = jnp.sum(qbf * kn[j:j + 1, :], axis=1, keepdims=True) + cncol - cnt[:, j:j + 1]
            s_new.append(jnp.where(row1 // N_HEADS >= j, sj, NEG))
        m0 = functools.reduce(jnp.maximum, s_new)
        l0 = jnp.zeros((rows, 1), F32)
        a0 = jnp.zeros((rows, ATTN_W), F32)
        for j in range(n_new):
            pj = jnp.exp2(s_new[j] - m0)
            l0 = l0 + pj
            a0 = a0 + pj * vn[j:j + 1, :]
        m_ref[...] = jnp.broadcast_to(m0, m_ref.shape)
        l_ref[...] = jnp.broadcast_to(l0, l_ref.shape)
        acc_ref[...] = a0
        tail_ref[...] = jnp.zeros(tail_ref.shape, F32)

    lane8 = lax.broadcasted_iota(jnp.int32, (N_HEADS, page), 1)
    tail = tail_ref[...]
    s_parts, g_parts, v_parts = [], [], []
    for r in range(n_pages_step):
        kb = k_refs[r][0].reshape(ATTN_W, page).astype(BF16)
        s_parts.append(jnp.dot(qbd_ref[...], kb, preferred_element_type=F32))
        lf = lf_refs[r][0]
        incl = lf
        d = 1
        while d < page:
            incl = incl + jnp.where(lane8 + d < page, pltpu.roll(incl, page - d, axis=1), 0.0)
            d *= 2
        g_parts.append(incl - lf + tail)
        tail = tail + incl[:, 0:1]
        v_parts.append(v_refs[r][0].reshape(ATTN_W, page).astype(BF16))
    tail_ref[...] = tail
    g = jnp.concatenate(g_parts, axis=1) * LOG2E
    s = (jnp.concatenate(s_parts, axis=1) + jnp.concatenate([g] * n_new, axis=0)
         + cncol_ref[:, 0:1])
    m_prev = m_ref[...]
    m_next = jnp.maximum(m_prev, jnp.max(s, axis=1, keepdims=True))
    alpha = jnp.exp2(m_prev - m_next)
    p = jnp.exp2(s - pltpu.repeat(m_next, n_pages_step * page // LANES, axis=1))
    l_ref[...] = alpha * l_ref[...] + jnp.sum(p, axis=1, keepdims=True)
    m_ref[...] = m_next
    acc_ref[...] = (pltpu.repeat(alpha, ATTN_W // LANES, axis=1) * acc_ref[...]
                    + lax.dot_general(p.astype(BF16), jnp.concatenate(v_parts, axis=1),
                                      (((1,), (1,)), ((), ())), preferred_element_type=F32))

    @pl.when(c == pl.num_programs(1) - 1)
    def _():
        o = acc_ref[...] / pltpu.repeat(l_ref[...], ATTN_W // LANES, axis=1)
        o = jnp.where(head_lanes, o, 0.0)
        for i in range(n_new):
            o_ref[0, i:i + 1, :] = jnp.sum(o[N_HEADS * i:N_HEADS * (i + 1), :], axis=0,
                                           keepdims=True)


def _decode(page_table, q3, kn3, vn3, cn3, ck, cv, clf_t):
    nb, n_new, _ = q3.shape
    n_pages = page_table.shape[1]
    page = ck.shape[3]
    pps = DEC_PAGES
    while n_pages % pps:
        pps //= 2
    n_steps = n_pages // pps
    rows = n_new * N_HEADS
    assert page % LANES == 0 and rows % 8 == 0

    def page_map(r, ndim):
        def index_map(b, c, pt):
            return (pt[b * n_pages + n_pages - 1 - (c * pps + r)],) + (0,) * (ndim - 1)
        return index_map

    per_b = lambda n: pl.BlockSpec((1, n_new, n), lambda b, c, pt: (b, 0, 0))
    in_specs = [per_b(ATTN_W), per_b(ATTN_W), per_b(ATTN_W), per_b(LANES)]
    kv_blk = (1, N_HEADS, HEAD_DIM, page)
    in_specs += [pl.BlockSpec(kv_blk, page_map(r, 4)) for r in range(pps)]
    in_specs += [pl.BlockSpec(kv_blk, page_map(r, 4)) for r in range(pps)]
    in_specs += [pl.BlockSpec((1, N_HEADS, page), page_map(r, 3)) for r in range(pps)]
    grid_spec = pltpu.PrefetchScalarGridSpec(
        num_scalar_prefetch=1,
        grid=(nb, n_steps),
        in_specs=in_specs,
        out_specs=pl.BlockSpec((1, n_new, ATTN_W), lambda b, c, pt: (b, 0, 0)),
        scratch_shapes=[pltpu.VMEM((rows, ATTN_W), BF16),
                        pltpu.VMEM((8, LANES), F32),
                        pltpu.VMEM((rows, LANES), F32),
                        pltpu.VMEM((rows, LANES), F32),
                        pltpu.VMEM((rows, LANES), F32),
                        pltpu.VMEM((rows, ATTN_W), F32),
                        pltpu.VMEM((N_HEADS, LANES), F32)])
    return pl.pallas_call(
        functools.partial(_decode_kernel, n_new=n_new, page=page, n_pages_step=pps),
        grid_spec=grid_spec,
        out_shape=jax.ShapeDtypeStruct((nb, n_new, ATTN_W), F32),
        compiler_params=pltpu.CompilerParams(dimension_semantics=("arbitrary", "arbitrary"),
                                             vmem_limit_bytes=VMEM_LIMIT),
        name="decode_attn",
    )(page_table.reshape(-1), q3, kn3, vn3, cn3, *([ck] * pps), *([cv] * pps), *([clf_t] * pps))


def _merge_sample_kernel(x_ref, u_ref, sp_ref, sgp_ref, attn_ref, sga_ref, pw_ref, ps_ref, wo_ref,
                         nf_ref, y_ref, pool_ref, *, n_new, past):
    u = [u_ref[:, POOL_W * i:POOL_W * (i + 1)] for i in range(n_new)]
    hist = [sp_ref[r] for r in range(POOL_HIST)]
    ext = hist + u
    pooled_rows = []
    for i in range(n_new):
        groups = []
        for g, w in enumerate(POOL_WINDOWS):
            sl = slice(POOL_GC * g, POOL_GC * (g + 1))
            end = POOL_HIST + i
            total = ext[end][:, sl]
            for r in range(end - w + 1, end):
                total = total + ext[r][:, sl]
            groups.append(total / float(min(w, past + i + 1)) - u[i][:, sl])
        pooled_rows.append(jnp.concatenate(groups, axis=1))
    pooled = jnp.concatenate(pooled_rows, axis=0)
    cat = lambda ref, n: jnp.concatenate([ref[:, n * i:n * (i + 1)] for i in range(n_new)], axis=0)
    y = _merge(cat(x_ref, D_MODEL), pooled, cat(sgp_ref, POOL_W), cat(attn_ref, ATTN_W),
               cat(sga_ref, ATTN_W), pw_ref, ps_ref[...], wo_ref, nf_ref[...])
    nb = x_ref.shape[0]
    for i in range(n_new):
        y_ref[:, D_MODEL * i:D_MODEL * (i + 1)] = y[nb * i:nb * (i + 1), :]
    new_hist = ext[-POOL_HIST:]
    for r in range(POOL_HIST):
        pool_ref[:, POOL_W * r:POOL_W * (r + 1)] = new_hist[r]


def _merge_sample(xs2, u2, sp2, sgp2, attn2, sga2, pw, ps, wo, nf, *, n_new, past):
    nb = xs2.shape[0]
    return pl.pallas_call(
        functools.partial(_merge_sample_kernel, n_new=n_new, past=past),
        out_shape=(jax.ShapeDtypeStruct((nb, n_new * D_MODEL), F32),
                   jax.ShapeDtypeStruct((nb, POOL_HIST * POOL_W), F32)),
        compiler_params=pltpu.CompilerParams(vmem_limit_bytes=VMEM_LIMIT),
        name="merge_sample",
    )(xs2, u2, sp2, sgp2, attn2, sga2, pw, ps, wo, nf)


def kernel(x_prompt, x_sample, cache_k, cache_v, cache_logf, state_pool, page_table,
           norm_w, w_in, b_forget, pool_w, pool_scale, w_out, norm_f):
    depth = norm_w.shape[0]
    assert depth == 1, "a single layer is supported"
    b_p, seq, _ = x_prompt.shape
    b_s, t_s, _ = x_sample.shape
    n_phys, page = cache_k.shape[1], cache_k.shape[2]
    past = page_table.shape[1] * page
    ll = 0

    n_main = 2 * POOL_W + 4 * ATTN_W
    w = jnp.concatenate([w_in[ll][:, :n_main], jnp.tile(w_in[ll][:, n_main:], (1, LANES // N_HEADS))],
                        axis=1).astype(BF16)
    bft = jnp.tile(b_forget[ll], LANES // N_HEADS).reshape(1, LANES).astype(F32)
    nw = norm_w[ll].reshape(1, D_MODEL)
    nf = norm_f.reshape(1, D_MODEL)
    pw = pool_w[ll].astype(BF16)
    ps = pool_scale[ll].reshape(1, POOL_W)
    wo = w_out[ll].astype(BF16)
    pp_np, cc_np = _placement()
    pp = jnp.asarray(pp_np, BF16)
    cc = jnp.asarray(cc_np, F32)

    xp2 = x_prompt.reshape(b_p * seq, D_MODEL)
    u, sgp, qa, ka, kf, vf, vb, sga, logf = _proj_prompt(xp2, nw, w, bft, pp, cc, seq=seq)
    attn = _attn_prompt(qa, ka, vb, n_batch=b_p, seq=seq)
    yp, pool_p = _merge_prompt(xp2, u, sgp, attn, sga, pw, ps, wo, nf, n_batch=b_p, seq=seq)

    xs2 = x_sample.reshape(b_s, t_s * D_MODEL)
    us, sgps, qs, ks, vs, sgas, lfs, cns = _proj_sample(xs2, nw, w, bft, n_new=t_s)
    ck = jnp.transpose(cache_k[ll], (0, 2, 3, 1))
    cv = jnp.transpose(cache_v[ll], (0, 2, 3, 1))
    clf_t = jnp.swapaxes(cache_logf[ll], 1, 2)
    attn_s = _decode(page_table, qs.reshape(b_s, t_s, ATTN_W), ks.reshape(b_s, t_s, ATTN_W),
                     vs.reshape(b_s, t_s, ATTN_W), cns.reshape(b_s, t_s, LANES), ck, cv, clf_t)
    ys, pool_s = _merge_sample(xs2, us, jnp.swapaxes(state_pool[ll], 0, 1), sgps,
                               attn_s.reshape(b_s, t_s * ATTN_W), sgas, pw, ps, wo, nf,
                               n_new=t_s, past=past)

    return (yp.reshape(b_p, seq, D_MODEL),
            ys.reshape(b_s, t_s, D_MODEL),
            kf.reshape(1, b_p, seq, N_HEADS, HEAD_DIM),
            vf.reshape(1, b_p, seq, N_HEADS, HEAD_DIM),
            logf.reshape(1, b_p, seq, N_HEADS),
            pool_p.reshape(1, b_p, POOL_HIST, POOL_W),
            ks.reshape(1, b_s, t_s, N_HEADS, HEAD_DIM),
            vs.reshape(1, b_s, t_s, N_HEADS, HEAD_DIM),
            lfs.reshape(b_s, t_s, LANES)[:, :, :N_HEADS].reshape(1, b_s, t_s, N_HEADS),
            pool_s.reshape(1, b_s, POOL_HIST, POOL_W))
```

```python
import functools
import math

import numpy as np
import jax
import jax.numpy as jnp
from jax import lax
from jax.experimental import pallas as pl
from jax.experimental.pallas import tpu as pltpu

D_MODEL = 1024
POOL_W = 512
ATTN_W = 512
N_HEADS = 8
HEAD_DIM = 64
POOL_WINDOWS = (2, 4, 8, 16)
POOL_GC = 128
POOL_HIST = 15
EPS = 1e-6

LANES = 128
LOG2E = 1.4426950408889634
Q_SCALE = LOG2E / math.sqrt(HEAD_DIM)
NEG = -1e30
N_PAIRS = N_HEADS // 2
W_COLS = 2 * POOL_W + 4 * ATTN_W + LANES
BIAS_COLS = N_PAIRS * LANES + N_HEADS * LANES
VMEM_LIMIT = 56 * 1024 * 1024

SKIP_LOG2 = 152.0
NORM_SLACK = 1.02

PROJ_TM = 512
ATTN_T = 512
DEC_PAGES = 8

BF16 = jnp.bfloat16
F32 = jnp.float32


def _placement():
    pp = np.zeros((LANES, BIAS_COLS), np.float32)
    cc = np.zeros((1, BIAS_COLS), np.float32)
    kbase = N_PAIRS * LANES
    for h in range(N_HEADS):
        p, e = divmod(h, 2)
        for i in range(3):
            pp[8 * i + h, LANES * p + 6 * e + i] = 1.0
            pp[8 * i + h, kbase + LANES * h + 6 * e + 3 + i] = -1.0
            cc[0, LANES * p + 6 * e + 3 + i] = 1.0
            cc[0, kbase + LANES * h + 6 * e + i] = 1.0
    return pp, cc


def _log_sigmoid(x):
    return jnp.minimum(x, 0.0) - jnp.log(1.0 + jnp.exp(-jnp.abs(x)))


def _silu(x):
    return x * jax.nn.sigmoid(x)


def _rmsnorm(xf, w):
    ms = jnp.mean(xf * xf, axis=-1, keepdims=True)
    return xf * lax.rsqrt(ms + EPS) * w


def _project(h, w_ref):
    def cols(c0, n):
        return jnp.dot(h, w_ref[:, c0:c0 + n], preferred_element_type=F32)
    u = cols(0, POOL_W)
    gp = cols(POOL_W, POOL_W)
    q = cols(2 * POOL_W, ATTN_W)
    k = cols(2 * POOL_W + ATTN_W, ATTN_W)
    v = cols(2 * POOL_W + 2 * ATTN_W, ATTN_W)
    ga = cols(2 * POOL_W + 3 * ATTN_W, ATTN_W)
    fl = cols(2 * POOL_W + 4 * ATTN_W, LANES)
    return u, gp, q, k, v, ga, fl


def _proj_prompt_kernel(x_ref, nw_ref, w_ref, bf_ref, pp_ref, cc_ref, seg_ref,
                        u_ref, sgp_ref, qa_ref, ka_ref, k_ref, v_ref, vb_ref, sga_ref, logf_ref,
                        st_ref, carry_ref, *, tm, ta, tiles_per_batch):
    i = pl.program_id(0)
    h = _rmsnorm(x_ref[...], nw_ref[...]).astype(BF16)
    u, gp, q, k, v, ga, fl = _project(h, w_ref)
    u_ref[...] = u
    sgp_ref[...] = _silu(gp).astype(BF16)
    k_ref[...] = k
    v_ref[...] = v
    vb_ref[...] = v.astype(BF16)
    sga_ref[...] = _silu(ga).astype(BF16)
    logf = _log_sigmoid(fl + bf_ref[...])
    logf_ref[...] = logf[:, 0:N_HEADS]

    row = lax.broadcasted_iota(jnp.int32, (tm, LANES), 0)
    acc = logf
    d = 1
    while d < tm:
        acc = acc + jnp.where(row >= d, pltpu.roll(acc, d, axis=0), 0.0)
        d *= 2
    @pl.when(i % tiles_per_batch == 0)
    def _():
        carry_ref[...] = jnp.zeros(carry_ref.shape, F32)

    f_run = acc + carry_ref[0:1, :]
    carry_ref[...] = jnp.broadcast_to(f_run[tm - 1:tm, :], carry_ref.shape)

    f2 = f_run * LOG2E
    p1 = f2.astype(BF16).astype(F32)
    r1 = f2 - p1
    p2 = r1.astype(BF16).astype(F32)
    p3 = (r1 - p2).astype(BF16).astype(F32)
    lane = lax.broadcasted_iota(jnp.int32, (tm, LANES), 1)
    pieces = jnp.where(lane < 8, p1, jnp.where(lane < 16, p2, jnp.where(lane < 24, p3, 0.0)))
    bias = jnp.dot(pieces.astype(BF16), pp_ref[...], preferred_element_type=F32) + cc_ref[...]

    qs = q * Q_SCALE

    qn2 = jnp.dot((qs * qs).astype(BF16), seg_ref[...], preferred_element_type=F32)
    kn2 = jnp.dot((k * k).astype(BF16), seg_ref[...], preferred_element_type=F32)
    srow = lax.broadcasted_iota(jnp.int32, (8, LANES), 0)
    for sb in range(tm // ta):
        r0, r1 = sb * ta, (sb + 1) * ta
        qmax = jnp.sqrt(jnp.max(qn2[r0:r1], axis=0, keepdims=True))
        kmax = jnp.sqrt(jnp.max(kn2[r0:r1], axis=0, keepdims=True))
        st_ref[8 * sb:8 * (sb + 1), :] = jnp.where(
            srow == 0, qmax, jnp.where(srow == 1, kmax, jnp.where(
                srow == 2, f2[r0:r0 + 1, :], jnp.where(srow == 3, f2[r1 - 1:r1, :], 0.0))))

    first_half = lane < HEAD_DIM
    kbase = N_PAIRS * LANES
    for p in range(N_PAIRS):
        qa_ref[:, 2 * LANES * p:2 * LANES * p + LANES] = qs[:, LANES * p:LANES * (p + 1)].astype(BF16)
        qa_ref[:, 2 * LANES * p + LANES:2 * LANES * (p + 1)] = (
            bias[:, LANES * p:LANES * (p + 1)].astype(BF16))
        kp = k[:, LANES * p:LANES * (p + 1)]
        for e in range(2):
            hh = 2 * p + e
            keep = first_half if e == 0 else jnp.logical_not(first_half)
            ka_ref[:, 2 * LANES * hh:2 * LANES * hh + LANES] = jnp.where(keep, kp, 0.0).astype(BF16)
            ka_ref[:, 2 * LANES * hh + LANES:2 * LANES * (hh + 1)] = (
                bias[:, kbase + LANES * hh:kbase + LANES * (hh + 1)].astype(BF16))


def _proj_prompt(x2, nw, w, bft, pp, cc, seg, *, seq):
    rows = x2.shape[0]
    tm = min(PROJ_TM, seq)
    ta = min(ATTN_T, seq)
    assert seq % tm == 0 and rows % seq == 0 and tm % ta == 0
    row_blk = lambda n: pl.BlockSpec((tm, n), lambda i: (i, 0))
    const = lambda shape: pl.BlockSpec(shape, lambda i: (0, 0))
    out_shape = (
        jax.ShapeDtypeStruct((rows, POOL_W), F32),
        jax.ShapeDtypeStruct((rows, POOL_W), BF16),
        jax.ShapeDtypeStruct((rows, N_PAIRS * 2 * LANES), BF16),
        jax.ShapeDtypeStruct((rows, N_HEADS * 2 * LANES), BF16),
        jax.ShapeDtypeStruct((rows, ATTN_W), F32),
        jax.ShapeDtypeStruct((rows, ATTN_W), F32),
        jax.ShapeDtypeStruct((rows, ATTN_W), BF16),
        jax.ShapeDtypeStruct((rows, ATTN_W), BF16),
        jax.ShapeDtypeStruct((rows, N_HEADS), F32),
        jax.ShapeDtypeStruct((rows // ta * 8, LANES), F32),
    )
    out_specs = tuple(row_blk(s.shape[1]) for s in out_shape[:-1])
    out_specs += (pl.BlockSpec((tm // ta * 8, LANES), lambda i: (i, 0)),)
    return pl.pallas_call(
        functools.partial(_proj_prompt_kernel, tm=tm, ta=ta, tiles_per_batch=seq // tm),
        grid=(rows // tm,),
        in_specs=[row_blk(D_MODEL), const((1, D_MODEL)), const((D_MODEL, W_COLS)), const((1, LANES)),
                  const((LANES, BIAS_COLS)), const((1, BIAS_COLS)), const((ATTN_W, LANES))],
        out_specs=out_specs,
        out_shape=out_shape,
        scratch_shapes=[pltpu.VMEM((8, LANES), F32)],
        compiler_params=pltpu.CompilerParams(dimension_semantics=("arbitrary",),
                                             vmem_limit_bytes=VMEM_LIMIT),
        name="proj_prompt",
    )(x2, nw, w, bft, pp, cc, seg)


def _lane_tile(x, reps):
    return jnp.concatenate([x] * reps, axis=1)


def _plan_kernel(st_ref, o_ref, *, n_batch, n_blk):
    row = lax.broadcasted_iota(jnp.int32, (LANES, LANES), 0).astype(F32)
    col = lax.broadcasted_iota(jnp.int32, (LANES, LANES), 1).astype(F32)
    out = jnp.zeros((LANES, LANES), F32)
    for b in range(n_batch):
        def stat(r):
            v = st_ref[pl.ds((b * n_blk) * 8 + r, n_blk, stride=8), :]
            return jnp.concatenate([v, jnp.zeros((LANES - n_blk, LANES), F32)], axis=0)
        qm, km, ft0, fs1 = stat(0), stat(1), stat(2), stat(3)
        km_t, fs1_t = jnp.transpose(km), jnp.transpose(fs1)
        for p in range(N_PAIRS):
            first = row
            for h in (2 * p, 2 * p + 1):
                qcol = qm[:, h:h + 1]
                ub = (NORM_SLACK * (qcol * km_t[h:h + 1, :] + qcol * km[:, h:h + 1])
                      + ft0[:, h:h + 1] - fs1_t[h:h + 1, :])
                needed = jnp.logical_and(ub >= -SKIP_LOG2, col < row)
                first = jnp.minimum(first, jnp.where(needed, col, row))
            start = jnp.min(first, axis=1, keepdims=True)
            out = jnp.where(col == b * N_PAIRS + p, start, out)
    o_ref[...] = jnp.transpose(out).astype(jnp.int32)


def _plan(stats, *, n_batch, n_blk):
    assert n_blk <= LANES and n_batch * N_PAIRS <= LANES
    return pl.pallas_call(
        functools.partial(_plan_kernel, n_batch=n_batch, n_blk=n_blk),
        out_shape=jax.ShapeDtypeStruct((LANES, LANES), jnp.int32),
        name="attn_plan",
    )(stats)


def _attn_prompt_kernel(plan_ref, qa_ref, ka_ref, vb_ref, o_ref, m_ref, l_ref, acc_ref, *, t, nq):
    b, pr, i = pl.program_id(0), pl.program_id(1), pl.program_id(2)
    q = qa_ref[...]
    m_ref[...] = jnp.full(m_ref.shape, NEG, F32)
    l_ref[...] = jnp.zeros(l_ref.shape, F32)
    acc_ref[...] = jnp.zeros(acc_ref.shape, F32)
    reps = t // LANES

    def block(kstart, diagonal):
        vblk = vb_ref[pl.ds(kstart, t), :]
        for e in range(2):
            kblk = ka_ref[pl.ds(kstart, t), 2 * LANES * e:2 * LANES * (e + 1)]
            s = lax.dot_general(q, kblk, (((1,), (1,)), ((), ())), preferred_element_type=F32)
            if diagonal:
                r = lax.broadcasted_iota(jnp.int32, (t, t), 0)
                c = lax.broadcasted_iota(jnp.int32, (t, t), 1)
                s = jnp.where(c <= r, s, NEG)
            m_prev = m_ref[e]
            m_next = jnp.maximum(m_prev, jnp.max(s, axis=1, keepdims=True))
            alpha = jnp.exp2(m_prev - m_next)
            p = jnp.exp2(s - _lane_tile(m_next, reps))
            l_ref[e] = alpha * l_ref[e] + jnp.sum(p, axis=1, keepdims=True)
            m_ref[e] = m_next
            acc_ref[e] = alpha * acc_ref[e] + jnp.dot(p.astype(BF16), vblk,
                                                      preferred_element_type=F32)

    def body(j, carry):
        block(pl.multiple_of(j * t, t), False)
        return carry

    lax.fori_loop(plan_ref[(b * N_PAIRS + pr) * nq + i], i, body, 0)
    block(pl.multiple_of(i * t, t), True)

    lane = lax.broadcasted_iota(jnp.int32, (t, LANES), 1)
    o = jnp.where(lane < HEAD_DIM, acc_ref[0] / l_ref[0], acc_ref[1] / l_ref[1])
    o_ref[...] = o.astype(o_ref.dtype)


def _attn_prompt(plan, qa, ka, vb, *, n_batch, seq):
    t = min(ATTN_T, seq)
    assert seq % t == 0
    nq = seq // t
    grid_spec = pltpu.PrefetchScalarGridSpec(
        num_scalar_prefetch=1,
        grid=(n_batch, N_PAIRS, nq),
        in_specs=[pl.BlockSpec((t, 2 * LANES), lambda b, p, i, plan: (b * nq + i, p)),
                  pl.BlockSpec((seq, 4 * LANES), lambda b, p, i, plan: (b, p)),
                  pl.BlockSpec((seq, LANES), lambda b, p, i, plan: (b, p))],
        out_specs=pl.BlockSpec((t, LANES), lambda b, p, i, plan: (b * nq + i, p)),
        scratch_shapes=[pltpu.VMEM((2, t, LANES), F32), pltpu.VMEM((2, t, LANES), F32),
                        pltpu.VMEM((2, t, LANES), F32)])
    return pl.pallas_call(
        functools.partial(_attn_prompt_kernel, t=t, nq=nq),
        grid_spec=grid_spec,
        out_shape=jax.ShapeDtypeStruct((n_batch * seq, ATTN_W), BF16),
        compiler_params=pltpu.CompilerParams(
            dimension_semantics=("arbitrary", "arbitrary", "arbitrary"),
            vmem_limit_bytes=VMEM_LIMIT),
        name="attn_prompt",
    )(plan, qa, ka, vb)


def _merge(x, pooled, sgp, attn, sga, pw_ref, ps, wo_ref, nf):
    mixed = [jnp.dot(pooled[:, POOL_GC * g:POOL_GC * (g + 1)].astype(BF16), pw_ref[g],
                     preferred_element_type=F32) for g in range(len(POOL_WINDOWS))]
    pool_out = jnp.concatenate(mixed, axis=1) * ps
    mix = jnp.concatenate([(pool_out * sgp.astype(F32)).astype(BF16),
                           (attn.astype(F32) * sga.astype(F32)).astype(BF16)], axis=1)
    xo = x + jnp.dot(mix, wo_ref[...], preferred_element_type=F32)
    return _rmsnorm(xo, nf)


def _merge_prompt_kernel(x_ref, u_ref, uh_ref, sgp_ref, attn_ref, sga_ref, pw_ref, ps_ref, wo_ref,
                         nf_ref, y_ref, pool_ref, *, tm, tiles_per_batch):
    i = pl.program_id(0)
    ti = i % tiles_per_batch
    u = u_ref[...]
    halo = jnp.where(ti == 0, 0.0, uh_ref[...])
    ext = jnp.concatenate([halo, u], axis=0)
    pos = ti * tm + lax.broadcasted_iota(jnp.int32, (tm, 1), 0)
    groups = []
    level = ext
    shift = 1
    for g, w in enumerate(POOL_WINDOWS):
        while shift < w:
            level = level + pltpu.roll(level, shift, axis=0)
            shift *= 2
        cnt = jnp.minimum(w, pos + 1).astype(F32)
        sl = slice(POOL_GC * g, POOL_GC * (g + 1))
        groups.append(level[16:, sl] / cnt - u[:, sl])
    pooled = jnp.concatenate(groups, axis=1)
    y_ref[...] = _merge(x_ref[...], pooled, sgp_ref[...], attn_ref[...], sga_ref[...], pw_ref,
                        ps_ref[...], wo_ref, nf_ref[...])

    @pl.when(ti == tiles_per_batch - 1)
    def _():
        pool_ref[0] = u_ref[pl.ds(tm - POOL_HIST, POOL_HIST), :]


def _merge_prompt(x2, u, sgp, attn, sga, pw, ps, wo, nf, *, n_batch, seq):
    rows = x2.shape[0]
    tm = min(PROJ_TM, seq)
    tpb = seq // tm
    halo_blocks = tm // 16
    row_blk = lambda n: pl.BlockSpec((tm, n), lambda i: (i, 0))
    const2 = lambda shape: pl.BlockSpec(shape, lambda i: (0, 0))
    return pl.pallas_call(
        functools.partial(_merge_prompt_kernel, tm=tm, tiles_per_batch=tpb),
        grid=(rows // tm,),
        in_specs=[row_blk(D_MODEL), row_blk(POOL_W),
                  pl.BlockSpec((16, POOL_W), lambda i: (jnp.maximum(i * halo_blocks - 1, 0), 0)),
                  row_blk(POOL_W), row_blk(ATTN_W), row_blk(ATTN_W),
                  pl.BlockSpec((len(POOL_WINDOWS), POOL_GC, POOL_GC), lambda i: (0, 0, 0)),
                  const2((1, POOL_W)), const2((D_MODEL, D_MODEL)), const2((1, D_MODEL))],
        out_specs=(row_blk(D_MODEL),
                   pl.BlockSpec((1, POOL_HIST, POOL_W), lambda i: (i // tpb, 0, 0))),
        out_shape=(jax.ShapeDtypeStruct((rows, D_MODEL), F32),
                   jax.ShapeDtypeStruct((n_batch, POOL_HIST, POOL_W), F32)),
        compiler_params=pltpu.CompilerParams(dimension_semantics=("arbitrary",),
                                             vmem_limit_bytes=VMEM_LIMIT),
        name="merge_prompt",
    )(x2, u, u, sgp, attn, sga, pw, ps, wo, nf)


def _proj_sample_kernel(x_ref, nw_ref, w_ref, bf_ref,
                        u_ref, sgp_ref, q_ref, k_ref, v_ref, sga_ref, logf_ref, cn_ref, carry_ref):
    i = pl.program_id(0)
    h = _rmsnorm(x_ref[...], nw_ref[...]).astype(BF16)
    u, gp, q, k, v, ga, fl = _project(h, w_ref)
    u_ref[...] = u
    sgp_ref[...] = _silu(gp).astype(BF16)
    q_ref[...] = q * Q_SCALE
    k_ref[...] = k
    v_ref[...] = v
    sga_ref[...] = _silu(ga).astype(BF16)
    logf = _log_sigmoid(fl + bf_ref[...])
    logf_ref[...] = logf
    @pl.when(i == 0)
    def _():
        carry_ref[...] = jnp.zeros(carry_ref.shape, F32)

    cn = carry_ref[...] + logf
    carry_ref[...] = cn
    cn_ref[...] = cn


def _proj_sample(xs2, nw, w, bft, *, n_new):
    nb = xs2.shape[0]
    blk = lambda n: pl.BlockSpec((nb, n), lambda i: (0, i))
    const = lambda shape: pl.BlockSpec(shape, lambda i: (0, 0))
    widths = (POOL_W, POOL_W, ATTN_W, ATTN_W, ATTN_W, ATTN_W, LANES, LANES)
    dtypes = (F32, BF16, F32, F32, F32, BF16, F32, F32)
    return pl.pallas_call(
        _proj_sample_kernel,
        grid=(n_new,),
        in_specs=[blk(D_MODEL), const((1, D_MODEL)), const((D_MODEL, W_COLS)), const((1, LANES))],
        out_specs=tuple(blk(n) for n in widths),
        out_shape=tuple(jax.ShapeDtypeStruct((nb, n_new * n), dt) for n, dt in zip(widths, dtypes)),
        scratch_shapes=[pltpu.VMEM((nb, LANES), F32)],
        compiler_params=pltpu.CompilerParams(dimension_semantics=("arbitrary",),
                                             vmem_limit_bytes=VMEM_LIMIT),
        name="proj_sample",
    )(xs2, nw, w, bft)


def _decode_kernel(pt_ref, q_ref, kn_ref, vn_ref, cn_ref, *refs, n_new, page, n_pages_step):
    del pt_ref
    k_refs = refs[0:n_pages_step]
    v_refs = refs[n_pages_step:2 * n_pages_step]
    lf_refs = refs[2 * n_pages_step:3 * n_pages_step]
    o_ref = refs[3 * n_pages_step]
    qbd_ref, cn8_ref, cncol_ref, m_ref, l_ref, acc_ref, tail_ref = refs[3 * n_pages_step + 1:]
    c = pl.program_id(1)
    rows = n_new * N_HEADS
    row_w = lax.broadcasted_iota(jnp.int32, (rows, ATTN_W), 0)
    lane_w = lax.broadcasted_iota(jnp.int32, (rows, ATTN_W), 1)
    head_lanes = (row_w % N_HEADS) == (lane_w // HEAD_DIM)
    row1 = lax.broadcasted_iota(jnp.int32, (rows, 1), 0)

    @pl.when(c == 0)
    def _():
        q = q_ref[0]
        qrep = jnp.concatenate(
            [jnp.broadcast_to(q[i:i + 1, :], (N_HEADS, ATTN_W)) for i in range(n_new)], axis=0)
        qbd = jnp.where(head_lanes, qrep, 0.0)
        qbd_ref[...] = qbd.astype(BF16)
        cn8_ref[...] = jnp.zeros(cn8_ref.shape, F32)
        cn8_ref[0:n_new, :] = cn_ref[0]
        cnt = jnp.transpose(cn8_ref[...])[0:rows, :] * LOG2E
        cncol = jnp.zeros((rows, 1), F32)
        for i in range(n_new):
            cncol = jnp.where(row1 // N_HEADS == i, cnt[:, i:i + 1], cncol)
        cncol_ref[...] = jnp.broadcast_to(cncol, cncol_ref.shape)
        kn = kn_ref[0]
        vn = vn_ref[0]
        qbf = qbd.astype(BF16).astype(F32)
        s_new = []
        for j in range(n_new):
            sj = jnp.sum(qbf * kn[j:j + 1, :], axis=1, keepdims=True) + cncol - cnt[:, j:j + 1]
            s_new.append(jnp.where(row1 // N_HEADS >= j, sj, NEG))
        m0 = functools.reduce(jnp.maximum, s_new)
        l0 = jnp.zeros((rows, 1), F32)
        a0 = jnp.zeros((rows, ATTN_W), F32)
        for j in range(n_new):
            pj = jnp.exp2(s_new[j] - m0)
            l0 = l0 + pj
            a0 = a0 + pj * vn[j:j + 1, :]
        m_ref[...] = jnp.broadcast_to(m0, m_ref.shape)
        l_ref[...] = jnp.broadcast_to(l0, l_ref.shape)
        acc_ref[...] = a0
        tail_ref[...] = jnp.zeros(tail_ref.shape, F32)

    lane8 = lax.broadcasted_iota(jnp.int32, (N_HEADS, page), 1)
    tail = tail_ref[...]
    s_parts, g_parts, v_parts = [], [], []
    for r in range(n_pages_step):
        kb = k_refs[r][0].reshape(ATTN_W, page).astype(BF16)
        s_parts.append(jnp.dot(qbd_ref[...], kb, preferred_element_type=F32))
        lf = lf_refs[r][0]
        incl = lf
        d = 1
        while d < page:
            incl = incl + jnp.where(lane8 + d < page, pltpu.roll(incl, page - d, axis=1), 0.0)
            d *= 2
        g_parts.append(incl - lf + tail)
        tail = tail + incl[:, 0:1]
        v_parts.append(v_refs[r][0].reshape(ATTN_W, page).astype(BF16))
    tail_ref[...] = tail
    g = jnp.concatenate(g_parts, axis=1) * LOG2E
    s = (jnp.concatenate(s_parts, axis=1) + jnp.concatenate([g] * n_new, axis=0)
         + cncol_ref[:, 0:1])
    m_prev = m_ref[...]
    m_next = jnp.maximum(m_prev, jnp.max(s, axis=1, keepdims=True))
    alpha = jnp.exp2(m_prev - m_next)
    p = jnp.exp2(s - _lane_tile(m_next, n_pages_step * page // LANES))
    l_ref[...] = alpha * l_ref[...] + jnp.sum(p, axis=1, keepdims=True)
    m_ref[...] = m_next
    acc_ref[...] = (_lane_tile(alpha, ATTN_W // LANES) * acc_ref[...]
                    + lax.dot_general(p.astype(BF16), jnp.concatenate(v_parts, axis=1),
                                      (((1,), (1,)), ((), ())), preferred_element_type=F32))

    @pl.when(c == pl.num_programs(1) - 1)
    def _():
        o = acc_ref[...] / _lane_tile(l_ref[...], ATTN_W // LANES)
        o = jnp.where(head_lanes, o, 0.0)
        for i in range(n_new):
            o_ref[0, i:i + 1, :] = jnp.sum(o[N_HEADS * i:N_HEADS * (i + 1), :], axis=0,
                                           keepdims=True)


def _decode(page_table, q3, kn3, vn3, cn3, ck, cv, clf_t):
    nb, n_new, _ = q3.shape
    n_pages = page_table.shape[1]
    page = ck.shape[3]
    pps = DEC_PAGES
    while n_pages % pps:
        pps //= 2
    n_steps = n_pages // pps
    rows = n_new * N_HEADS
    assert page % LANES == 0 and rows % 8 == 0

    def page_map(r, ndim):
        def index_map(b, c, pt):
            return (pt[b * n_pages + n_pages - 1 - (c * pps + r)],) + (0,) * (ndim - 1)
        return index_map

    per_b = lambda n: pl.BlockSpec((1, n_new, n), lambda b, c, pt: (b, 0, 0))
    in_specs = [per_b(ATTN_W), per_b(ATTN_W), per_b(ATTN_W), per_b(LANES)]
    kv_blk = (1, N_HEADS, HEAD_DIM, page)
    in_specs += [pl.BlockSpec(kv_blk, page_map(r, 4)) for r in range(pps)]
    in_specs += [pl.BlockSpec(kv_blk, page_map(r, 4)) for r in range(pps)]
    in_specs += [pl.BlockSpec((1, N_HEADS, page), page_map(r, 3)) for r in range(pps)]
    grid_spec = pltpu.PrefetchScalarGridSpec(
        num_scalar_prefetch=1,
        grid=(nb, n_steps),
        in_specs=in_specs,
        out_specs=pl.BlockSpec((1, n_new, ATTN_W), lambda b, c, pt: (b, 0, 0)),
        scratch_shapes=[pltpu.VMEM((rows, ATTN_W), BF16),
                        pltpu.VMEM((8, LANES), F32),
                        pltpu.VMEM((rows, LANES), F32),
                        pltpu.VMEM((rows, LANES), F32),
                        pltpu.VMEM((rows, LANES), F32),
                        pltpu.VMEM((rows, ATTN_W), F32),
                        pltpu.VMEM((N_HEADS, LANES), F32)])
    return pl.pallas_call(
        functools.partial(_decode_kernel, n_new=n_new, page=page, n_pages_step=pps),
        grid_spec=grid_spec,
        out_shape=jax.ShapeDtypeStruct((nb, n_new, ATTN_W), F32),
        compiler_params=pltpu.CompilerParams(dimension_semantics=("arbitrary", "arbitrary"),
                                             vmem_limit_bytes=VMEM_LIMIT),
        name="decode_attn",
    )(page_table.reshape(-1), q3, kn3, vn3, cn3, *([ck] * pps), *([cv] * pps), *([clf_t] * pps))


def _merge_sample_kernel(x_ref, u_ref, sp_ref, sgp_ref, attn_ref, sga_ref, pw_ref, ps_ref, wo_ref,
                         nf_ref, y_ref, pool_ref, *, n_new, past):
    u = [u_ref[:, POOL_W * i:POOL_W * (i + 1)] for i in range(n_new)]
    hist = [sp_ref[r] for r in range(POOL_HIST)]
    ext = hist + u
    pooled_rows = []
    for i in range(n_new):
        groups = []
        for g, w in enumerate(POOL_WINDOWS):
            sl = slice(POOL_GC * g, POOL_GC * (g + 1))
            end = POOL_HIST + i
            total = ext[end][:, sl]
            for r in range(end - w + 1, end):
                total = total + ext[r][:, sl]
            groups.append(total / float(min(w, past + i + 1)) - u[i][:, sl])
        pooled_rows.append(jnp.concatenate(groups, axis=1))
    pooled = jnp.concatenate(pooled_rows, axis=0)
    cat = lambda ref, n: jnp.concatenate([ref[:, n * i:n * (i + 1)] for i in range(n_new)], axis=0)
    y = _merge(cat(x_ref, D_MODEL), pooled, cat(sgp_ref, POOL_W), cat(attn_ref, ATTN_W),
               cat(sga_ref, ATTN_W), pw_ref, ps_ref[...], wo_ref, nf_ref[...])
    nb = x_ref.shape[0]
    for i in range(n_new):
        y_ref[:, D_MODEL * i:D_MODEL * (i + 1)] = y[nb * i:nb * (i + 1), :]
    new_hist = ext[-POOL_HIST:]
    for r in range(POOL_HIST):
        pool_ref[:, POOL_W * r:POOL_W * (r + 1)] = new_hist[r]


def _merge_sample(xs2, u2, sp2, sgp2, attn2, sga2, pw, ps, wo, nf, *, n_new, past):
    nb = xs2.shape[0]
    return pl.pallas_call(
        functools.partial(_merge_sample_kernel, n_new=n_new, past=past),
        out_shape=(jax.ShapeDtypeStruct((nb, n_new * D_MODEL), F32),
                   jax.ShapeDtypeStruct((nb, POOL_HIST * POOL_W), F32)),
        compiler_params=pltpu.CompilerParams(vmem_limit_bytes=VMEM_LIMIT),
        name="merge_sample",
    )(xs2, u2, sp2, sgp2, attn2, sga2, pw, ps, wo, nf)


def kernel(x_prompt, x_sample, cache_k, cache_v, cache_logf, state_pool, page_table,
           norm_w, w_in, b_forget, pool_w, pool_scale, w_out, norm_f):
    depth = norm_w.shape[0]
    assert depth == 1, "a single layer is supported"
    b_p, seq, _ = x_prompt.shape
    b_s, t_s, _ = x_sample.shape
    n_phys, page = cache_k.shape[1], cache_k.shape[2]
    past = page_table.shape[1] * page
    ll = 0

    n_main = 2 * POOL_W + 4 * ATTN_W
    w = jnp.concatenate([w_in[ll][:, :n_main], jnp.tile(w_in[ll][:, n_main:], (1, LANES // N_HEADS))],
                        axis=1).astype(BF16)
    bft = jnp.tile(b_forget[ll], LANES // N_HEADS).reshape(1, LANES).astype(F32)
    nw = norm_w[ll].reshape(1, D_MODEL)
    nf = norm_f.reshape(1, D_MODEL)
    pw = pool_w[ll].astype(BF16)
    ps = pool_scale[ll].reshape(1, POOL_W)
    wo = w_out[ll].astype(BF16)
    pp_np, cc_np = _placement()
    pp = jnp.asarray(pp_np, BF16)
    cc = jnp.asarray(cc_np, F32)

    xp2 = x_prompt.reshape(b_p * seq, D_MODEL)
    seg_np = (np.arange(ATTN_W)[:, None] // HEAD_DIM == np.arange(LANES)[None, :] % N_HEADS)
    seg = jnp.asarray(seg_np, BF16)
    u, sgp, qa, ka, kf, vf, vb, sga, logf, stats = _proj_prompt(xp2, nw, w, bft, pp, cc, seg, seq=seq)
    n_blk = seq // min(ATTN_T, seq)
    plan = _plan(stats, n_batch=b_p, n_blk=n_blk)[:b_p * N_PAIRS, :n_blk].reshape(-1)
    attn = _attn_prompt(plan, qa, ka, vb, n_batch=b_p, seq=seq)
    yp, pool_p = _merge_prompt(xp2, u, sgp, attn, sga, pw, ps, wo, nf, n_batch=b_p, seq=seq)

    xs2 = x_sample.reshape(b_s, t_s * D_MODEL)
    us, sgps, qs, ks, vs, sgas, lfs, cns = _proj_sample(xs2, nw, w, bft, n_new=t_s)
    ck = jnp.transpose(cache_k[ll], (0, 2, 3, 1))
    cv = jnp.transpose(cache_v[ll], (0, 2, 3, 1))
    clf_t = jnp.swapaxes(cache_logf[ll], 1, 2)
    attn_s = _decode(page_table, qs.reshape(b_s, t_s, ATTN_W), ks.reshape(b_s, t_s, ATTN_W),
                     vs.reshape(b_s, t_s, ATTN_W), cns.reshape(b_s, t_s, LANES), ck, cv, clf_t)
    ys, pool_s = _merge_sample(xs2, us, jnp.swapaxes(state_pool[ll], 0, 1), sgps,
                               attn_s.reshape(b_s, t_s * ATTN_W), sgas, pw, ps, wo, nf,
                               n_new=t_s, past=past)

    return (yp.reshape(b_p, seq, D_MODEL),
            ys.reshape(b_s, t_s, D_MODEL),
            kf.reshape(1, b_p, seq, N_HEADS, HEAD_DIM),
            vf.reshape(1, b_p, seq, N_HEADS, HEAD_DIM),
            logf.reshape(1, b_p, seq, N_HEADS),
            pool_p.reshape(1, b_p, POOL_HIST, POOL_W),
            ks.reshape(1, b_s, t_s, N_HEADS, HEAD_DIM),
            vs.reshape(1, b_s, t_s, N_HEADS, HEAD_DIM),
            lfs.reshape(b_s, t_s, LANES)[:, :, :N_HEADS].reshape(1, b_s, t_s, N_HEADS),
            pool_s.reshape(1, b_s, POOL_HIST, POOL_W))
```

```python
import functools
import math

import numpy as np
import jax
import jax.numpy as jnp
from jax import lax
from jax.experimental import pallas as pl
from jax.experimental.pallas import tpu as pltpu

D_MODEL = 1024
POOL_W = 512
ATTN_W = 512
N_HEADS = 8
HEAD_DIM = 64
POOL_WINDOWS = (2, 4, 8, 16)
POOL_GC = 128
POOL_HIST = 15
EPS = 1e-6

LANES = 128
LOG2E = 1.4426950408889634
Q_SCALE = LOG2E / math.sqrt(HEAD_DIM)
NEG = -1e30
N_PAIRS = N_HEADS // 2
W_COLS = 2 * POOL_W + 4 * ATTN_W + LANES
BIAS_COLS = N_PAIRS * LANES + N_HEADS * LANES
VMEM_LIMIT = 56 * 1024 * 1024

SKIP_LOG2 = 152.0
NORM_SLACK = 1.02

PROJ_TM = 512
ATTN_T = 512
DEC_PAGES = 16

BF16 = jnp.bfloat16
F32 = jnp.float32


def _placement():
    pp = np.zeros((LANES, BIAS_COLS), np.float32)
    cc = np.zeros((1, BIAS_COLS), np.float32)
    kbase = N_PAIRS * LANES
    for h in range(N_HEADS):
        p, e = divmod(h, 2)
        for i in range(3):
            pp[8 * i + h, LANES * p + 6 * e + i] = 1.0
            pp[8 * i + h, kbase + LANES * h + 6 * e + 3 + i] = -1.0
            cc[0, LANES * p + 6 * e + 3 + i] = 1.0
            cc[0, kbase + LANES * h + 6 * e + i] = 1.0
    return pp, cc


def _log_sigmoid(x):
    return jnp.minimum(x, 0.0) - jnp.log(1.0 + jnp.exp(-jnp.abs(x)))


def _silu(x):
    return x * jax.nn.sigmoid(x)


def _rmsnorm(xf, w):
    ms = jnp.mean(xf * xf, axis=-1, keepdims=True)
    return xf * lax.rsqrt(ms + EPS) * w


def _project(h, w_ref):
    def cols(c0, n):
        return jnp.dot(h, w_ref[:, c0:c0 + n], preferred_element_type=F32)
    u = cols(0, POOL_W)
    gp = cols(POOL_W, POOL_W)
    q = cols(2 * POOL_W, ATTN_W)
    k = cols(2 * POOL_W + ATTN_W, ATTN_W)
    v = cols(2 * POOL_W + 2 * ATTN_W, ATTN_W)
    ga = cols(2 * POOL_W + 3 * ATTN_W, ATTN_W)
    fl = cols(2 * POOL_W + 4 * ATTN_W, LANES)
    return u, gp, q, k, v, ga, fl


def _proj_prompt_kernel(x_ref, nw_ref, w_ref, bf_ref, pp_ref, cc_ref, seg_ref,
                        u_ref, sgp_ref, qa_ref, ka_ref, k_ref, v_ref, vb_ref, sga_ref, logf_ref,
                        st_ref, carry_ref, *, tm, ta, tiles_per_batch):
    i = pl.program_id(0)
    h = _rmsnorm(x_ref[...], nw_ref[...]).astype(BF16)
    u, gp, q, k, v, ga, fl = _project(h, w_ref)
    u_ref[...] = u
    sgp_ref[...] = _silu(gp).astype(BF16)
    k_ref[...] = k
    v_ref[...] = v
    vb_ref[...] = v.astype(BF16)
    sga_ref[...] = _silu(ga).astype(BF16)
    logf = _log_sigmoid(fl + bf_ref[...])
    logf_ref[...] = logf[:, 0:N_HEADS]

    row = lax.broadcasted_iota(jnp.int32, (tm, LANES), 0)
    acc = logf
    d = 1
    while d < tm:
        acc = acc + jnp.where(row >= d, pltpu.roll(acc, d, axis=0), 0.0)
        d *= 2
    @pl.when(i % tiles_per_batch == 0)
    def _():
        carry_ref[...] = jnp.zeros(carry_ref.shape, F32)

    f_run = acc + carry_ref[0:1, :]
    carry_ref[...] = jnp.broadcast_to(f_run[tm - 1:tm, :], carry_ref.shape)

    f2 = f_run * LOG2E
    p1 = f2.astype(BF16).astype(F32)
    r1 = f2 - p1
    p2 = r1.astype(BF16).astype(F32)
    p3 = (r1 - p2).astype(BF16).astype(F32)
    lane = lax.broadcasted_iota(jnp.int32, (tm, LANES), 1)
    pieces = jnp.where(lane < 8, p1, jnp.where(lane < 16, p2, jnp.where(lane < 24, p3, 0.0)))
    bias = jnp.dot(pieces.astype(BF16), pp_ref[...], preferred_element_type=F32) + cc_ref[...]

    qs = q * Q_SCALE

    qn2 = jnp.dot((qs * qs).astype(BF16), seg_ref[...], preferred_element_type=F32)
    kn2 = jnp.dot((k * k).astype(BF16), seg_ref[...], preferred_element_type=F32)
    srow = lax.broadcasted_iota(jnp.int32, (8, LANES), 0)
    for sb in range(tm // ta):
        r0, r1 = sb * ta, (sb + 1) * ta
        qmax = jnp.sqrt(jnp.max(qn2[r0:r1], axis=0, keepdims=True))
        kmax = jnp.sqrt(jnp.max(kn2[r0:r1], axis=0, keepdims=True))
        st_ref[8 * sb:8 * (sb + 1), :] = jnp.where(
            srow == 0, qmax, jnp.where(srow == 1, kmax, jnp.where(
                srow == 2, f2[r0:r0 + 1, :], jnp.where(srow == 3, f2[r1 - 1:r1, :], 0.0))))

    first_half = lane < HEAD_DIM
    kbase = N_PAIRS * LANES
    for p in range(N_PAIRS):
        qa_ref[:, 2 * LANES * p:2 * LANES * p + LANES] = qs[:, LANES * p:LANES * (p + 1)].astype(BF16)
        qa_ref[:, 2 * LANES * p + LANES:2 * LANES * (p + 1)] = (
            bias[:, LANES * p:LANES * (p + 1)].astype(BF16))
        kp = k[:, LANES * p:LANES * (p + 1)]
        for e in range(2):
            hh = 2 * p + e
            keep = first_half if e == 0 else jnp.logical_not(first_half)
            ka_ref[:, 2 * LANES * hh:2 * LANES * hh + LANES] = jnp.where(keep, kp, 0.0).astype(BF16)
            ka_ref[:, 2 * LANES * hh + LANES:2 * LANES * (hh + 1)] = (
                bias[:, kbase + LANES * hh:kbase + LANES * (hh + 1)].astype(BF16))


def _proj_prompt(x2, nw, w, bft, pp, cc, seg, *, seq):
    rows = x2.shape[0]
    tm = min(PROJ_TM, seq)
    ta = min(ATTN_T, seq)
    assert seq % tm == 0 and rows % seq == 0 and tm % ta == 0
    row_blk = lambda n: pl.BlockSpec((tm, n), lambda i: (i, 0))
    const = lambda shape: pl.BlockSpec(shape, lambda i: (0, 0))
    out_shape = (
        jax.ShapeDtypeStruct((rows, POOL_W), F32),
        jax.ShapeDtypeStruct((rows, POOL_W), BF16),
        jax.ShapeDtypeStruct((rows, N_PAIRS * 2 * LANES), BF16),
        jax.ShapeDtypeStruct((rows, N_HEADS * 2 * LANES), BF16),
        jax.ShapeDtypeStruct((rows, ATTN_W), F32),
        jax.ShapeDtypeStruct((rows, ATTN_W), F32),
        jax.ShapeDtypeStruct((rows, ATTN_W), BF16),
        jax.ShapeDtypeStruct((rows, ATTN_W), BF16),
        jax.ShapeDtypeStruct((rows, N_HEADS), F32),
        jax.ShapeDtypeStruct((rows // ta * 8, LANES), F32),
    )
    out_specs = tuple(row_blk(s.shape[1]) for s in out_shape[:-1])
    out_specs += (pl.BlockSpec((tm // ta * 8, LANES), lambda i: (i, 0)),)
    return pl.pallas_call(
        functools.partial(_proj_prompt_kernel, tm=tm, ta=ta, tiles_per_batch=seq // tm),
        grid=(rows // tm,),
        in_specs=[row_blk(D_MODEL), const((1, D_MODEL)), const((D_MODEL, W_COLS)), const((1, LANES)),
                  const((LANES, BIAS_COLS)), const((1, BIAS_COLS)), const((ATTN_W, LANES))],
        out_specs=out_specs,
        out_shape=out_shape,
        scratch_shapes=[pltpu.VMEM((8, LANES), F32)],
        compiler_params=pltpu.CompilerParams(dimension_semantics=("arbitrary",),
                                             vmem_limit_bytes=VMEM_LIMIT),
        name="proj_prompt",
    )(x2, nw, w, bft, pp, cc, seg)


def _lane_tile(x, reps):
    return jnp.concatenate([x] * reps, axis=1)


def _plan_kernel(st_ref, o_ref, *, n_batch, n_blk):
    row = lax.broadcasted_iota(jnp.int32, (LANES, LANES), 0).astype(F32)
    col = lax.broadcasted_iota(jnp.int32, (LANES, LANES), 1).astype(F32)
    out = jnp.zeros((LANES, LANES), F32)
    for b in range(n_batch):
        def stat(r):
            v = st_ref[pl.ds((b * n_blk) * 8 + r, n_blk, stride=8), :]
            return jnp.concatenate([v, jnp.zeros((LANES - n_blk, LANES), F32)], axis=0)
        qm, km, ft0, fs1 = stat(0), stat(1), stat(2), stat(3)
        km_t, fs1_t = jnp.transpose(km), jnp.transpose(fs1)
        for p in range(N_PAIRS):
            first = row
            for h in (2 * p, 2 * p + 1):
                qcol = qm[:, h:h + 1]
                ub = (NORM_SLACK * (qcol * km_t[h:h + 1, :] + qcol * km[:, h:h + 1])
                      + ft0[:, h:h + 1] - fs1_t[h:h + 1, :])
                needed = jnp.logical_and(ub >= -SKIP_LOG2, col < row)
                first = jnp.minimum(first, jnp.where(needed, col, row))
            start = jnp.min(first, axis=1, keepdims=True)
            out = jnp.where(col == b * N_PAIRS + p, start, out)
    o_ref[...] = jnp.transpose(out).astype(jnp.int32)


def _plan(stats, *, n_batch, n_blk):
    assert n_blk <= LANES and n_batch * N_PAIRS <= LANES
    return pl.pallas_call(
        functools.partial(_plan_kernel, n_batch=n_batch, n_blk=n_blk),
        out_shape=jax.ShapeDtypeStruct((LANES, LANES), jnp.int32),
        name="attn_plan",
    )(stats)


def _attn_prompt_kernel(plan_ref, qa_ref, ka_ref, vb_ref, o_ref, m_ref, l_ref, acc_ref, *, t, nq):
    b, pr, i = pl.program_id(0), pl.program_id(1), pl.program_id(2)
    q = qa_ref[...]
    m_ref[...] = jnp.full(m_ref.shape, NEG, F32)
    l_ref[...] = jnp.zeros(l_ref.shape, F32)
    acc_ref[...] = jnp.zeros(acc_ref.shape, F32)
    reps = t // LANES

    def block(kstart, diagonal):
        vblk = vb_ref[pl.ds(kstart, t), :]
        for e in range(2):
            kblk = ka_ref[pl.ds(kstart, t), 2 * LANES * e:2 * LANES * (e + 1)]
            s = lax.dot_general(q, kblk, (((1,), (1,)), ((), ())), preferred_element_type=F32)
            if diagonal:
                r = lax.broadcasted_iota(jnp.int32, (t, t), 0)
                c = lax.broadcasted_iota(jnp.int32, (t, t), 1)
                s = jnp.where(c <= r, s, NEG)
            m_prev = m_ref[e]
            m_next = jnp.maximum(m_prev, jnp.max(s, axis=1, keepdims=True))
            alpha = jnp.exp2(m_prev - m_next)
            p = jnp.exp2(s - _lane_tile(m_next, reps))
            l_ref[e] = alpha * l_ref[e] + jnp.sum(p, axis=1, keepdims=True)
            m_ref[e] = m_next
            acc_ref[e] = alpha * acc_ref[e] + jnp.dot(p.astype(BF16), vblk,
                                                      preferred_element_type=F32)

    def body(j, carry):
        block(pl.multiple_of(j * t, t), False)
        return carry

    lax.fori_loop(plan_ref[(b * N_PAIRS + pr) * nq + i], i, body, 0)
    block(pl.multiple_of(i * t, t), True)

    lane = lax.broadcasted_iota(jnp.int32, (t, LANES), 1)
    o = jnp.where(lane < HEAD_DIM, acc_ref[0] / l_ref[0], acc_ref[1] / l_ref[1])
    o_ref[...] = o.astype(o_ref.dtype)


def _attn_prompt(plan, qa, ka, vb, *, n_batch, seq):
    t = min(ATTN_T, seq)
    assert seq % t == 0
    nq = seq // t
    grid_spec = pltpu.PrefetchScalarGridSpec(
        num_scalar_prefetch=1,
        grid=(n_batch, N_PAIRS, nq),
        in_specs=[pl.BlockSpec((t, 2 * LANES), lambda b, p, i, plan: (b * nq + i, p)),
                  pl.BlockSpec((seq, 4 * LANES), lambda b, p, i, plan: (b, p)),
                  pl.BlockSpec((seq, LANES), lambda b, p, i, plan: (b, p))],
        out_specs=pl.BlockSpec((t, LANES), lambda b, p, i, plan: (b * nq + i, p)),
        scratch_shapes=[pltpu.VMEM((2, t, LANES), F32), pltpu.VMEM((2, t, LANES), F32),
                        pltpu.VMEM((2, t, LANES), F32)])
    return pl.pallas_call(
        functools.partial(_attn_prompt_kernel, t=t, nq=nq),
        grid_spec=grid_spec,
        out_shape=jax.ShapeDtypeStruct((n_batch * seq, ATTN_W), BF16),
        compiler_params=pltpu.CompilerParams(
            dimension_semantics=("arbitrary", "arbitrary", "arbitrary"),
            vmem_limit_bytes=VMEM_LIMIT),
        name="attn_prompt",
    )(plan, qa, ka, vb)


def _merge(x, pooled, sgp, attn, sga, pw_ref, ps, wo_ref, nf):
    mixed = [jnp.dot(pooled[:, POOL_GC * g:POOL_GC * (g + 1)].astype(BF16), pw_ref[g],
                     preferred_element_type=F32) for g in range(len(POOL_WINDOWS))]
    pool_out = jnp.concatenate(mixed, axis=1) * ps
    mix = jnp.concatenate([(pool_out * sgp.astype(F32)).astype(BF16),
                           (attn.astype(F32) * sga.astype(F32)).astype(BF16)], axis=1)
    xo = x + jnp.dot(mix, wo_ref[...], preferred_element_type=F32)
    return _rmsnorm(xo, nf)


def _merge_prompt_kernel(x_ref, u_ref, uh_ref, sgp_ref, attn_ref, sga_ref, pw_ref, ps_ref, wo_ref,
                         nf_ref, y_ref, pool_ref, *, tm, tiles_per_batch):
    i = pl.program_id(0)
    ti = i % tiles_per_batch
    u = u_ref[...]
    halo = jnp.where(ti == 0, 0.0, uh_ref[...])
    ext = jnp.concatenate([halo, u], axis=0)
    pos = ti * tm + lax.broadcasted_iota(jnp.int32, (tm, 1), 0)
    groups = []
    level = ext
    shift = 1
    for g, w in enumerate(POOL_WINDOWS):
        while shift < w:
            level = level + pltpu.roll(level, shift, axis=0)
            shift *= 2
        cnt = jnp.minimum(w, pos + 1).astype(F32)
        sl = slice(POOL_GC * g, POOL_GC * (g + 1))
        groups.append(level[16:, sl] / cnt - u[:, sl])
    pooled = jnp.concatenate(groups, axis=1)
    y_ref[...] = _merge(x_ref[...], pooled, sgp_ref[...], attn_ref[...], sga_ref[...], pw_ref,
                        ps_ref[...], wo_ref, nf_ref[...])

    @pl.when(ti == tiles_per_batch - 1)
    def _():
        pool_ref[0] = u_ref[pl.ds(tm - POOL_HIST, POOL_HIST), :]


def _merge_prompt(x2, u, sgp, attn, sga, pw, ps, wo, nf, *, n_batch, seq):
    rows = x2.shape[0]
    tm = min(PROJ_TM, seq)
    tpb = seq // tm
    halo_blocks = tm // 16
    row_blk = lambda n: pl.BlockSpec((tm, n), lambda i: (i, 0))
    const2 = lambda shape: pl.BlockSpec(shape, lambda i: (0, 0))
    return pl.pallas_call(
        functools.partial(_merge_prompt_kernel, tm=tm, tiles_per_batch=tpb),
        grid=(rows // tm,),
        in_specs=[row_blk(D_MODEL), row_blk(POOL_W),
                  pl.BlockSpec((16, POOL_W), lambda i: (jnp.maximum(i * halo_blocks - 1, 0), 0)),
                  row_blk(POOL_W), row_blk(ATTN_W), row_blk(ATTN_W),
                  pl.BlockSpec((len(POOL_WINDOWS), POOL_GC, POOL_GC), lambda i: (0, 0, 0)),
                  const2((1, POOL_W)), const2((D_MODEL, D_MODEL)), const2((1, D_MODEL))],
        out_specs=(row_blk(D_MODEL),
                   pl.BlockSpec((1, POOL_HIST, POOL_W), lambda i: (i // tpb, 0, 0))),
        out_shape=(jax.ShapeDtypeStruct((rows, D_MODEL), F32),
                   jax.ShapeDtypeStruct((n_batch, POOL_HIST, POOL_W), F32)),
        compiler_params=pltpu.CompilerParams(dimension_semantics=("arbitrary",),
                                             vmem_limit_bytes=VMEM_LIMIT),
        name="merge_prompt",
    )(x2, u, u, sgp, attn, sga, pw, ps, wo, nf)


def _proj_sample_kernel(x_ref, nw_ref, w_ref, bf_ref,
                        u_ref, sgp_ref, q_ref, k_ref, v_ref, sga_ref, logf_ref, cn_ref, carry_ref):
    i = pl.program_id(0)
    h = _rmsnorm(x_ref[...], nw_ref[...]).astype(BF16)
    u, gp, q, k, v, ga, fl = _project(h, w_ref)
    u_ref[...] = u
    sgp_ref[...] = _silu(gp).astype(BF16)
    q_ref[...] = q * Q_SCALE
    k_ref[...] = k
    v_ref[...] = v
    sga_ref[...] = _silu(ga).astype(BF16)
    logf = _log_sigmoid(fl + bf_ref[...])
    logf_ref[...] = logf
    @pl.when(i == 0)
    def _():
        carry_ref[...] = jnp.zeros(carry_ref.shape, F32)

    cn = carry_ref[...] + logf
    carry_ref[...] = cn
    cn_ref[...] = cn


def _proj_sample(xs2, nw, w, bft, *, n_new):
    nb = xs2.shape[0]
    blk = lambda n: pl.BlockSpec((nb, n), lambda i: (0, i))
    const = lambda shape: pl.BlockSpec(shape, lambda i: (0, 0))
    widths = (POOL_W, POOL_W, ATTN_W, ATTN_W, ATTN_W, ATTN_W, LANES, LANES)
    dtypes = (F32, BF16, F32, F32, F32, BF16, F32, F32)
    return pl.pallas_call(
        _proj_sample_kernel,
        grid=(n_new,),
        in_specs=[blk(D_MODEL), const((1, D_MODEL)), const((D_MODEL, W_COLS)), const((1, LANES))],
        out_specs=tuple(blk(n) for n in widths),
        out_shape=tuple(jax.ShapeDtypeStruct((nb, n_new * n), dt) for n, dt in zip(widths, dtypes)),
        scratch_shapes=[pltpu.VMEM((nb, LANES), F32)],
        compiler_params=pltpu.CompilerParams(dimension_semantics=("arbitrary",),
                                             vmem_limit_bytes=VMEM_LIMIT),
        name="proj_sample",
    )(xs2, nw, w, bft)


def _decode_kernel(pt_ref, q_ref, kn_ref, vn_ref, cn_ref, *refs, n_new, page, n_pages_step):
    del pt_ref
    k_refs = refs[0:n_pages_step]
    v_refs = refs[n_pages_step:2 * n_pages_step]
    lf_refs = refs[2 * n_pages_step:3 * n_pages_step]
    o_ref = refs[3 * n_pages_step]
    qbd_ref, cn8_ref, cncol_ref, m_ref, l_ref, acc_ref, tail_ref = refs[3 * n_pages_step + 1:]
    c = pl.program_id(1)
    rows = n_new * N_HEADS
    row_w = lax.broadcasted_iota(jnp.int32, (rows, ATTN_W), 0)
    lane_w = lax.broadcasted_iota(jnp.int32, (rows, ATTN_W), 1)
    head_lanes = (row_w % N_HEADS) == (lane_w // HEAD_DIM)
    row1 = lax.broadcasted_iota(jnp.int32, (rows, 1), 0)

    @pl.when(c == 0)
    def _():
        q = q_ref[0]
        qrep = jnp.concatenate(
            [jnp.broadcast_to(q[i:i + 1, :], (N_HEADS, ATTN_W)) for i in range(n_new)], axis=0)
        qbd = jnp.where(head_lanes, qrep, 0.0)
        qbd_ref[...] = qbd.astype(BF16)
        cn8_ref[...] = jnp.zeros(cn8_ref.shape, F32)
        cn8_ref[0:n_new, :] = cn_ref[0]
        cnt = jnp.transpose(cn8_ref[...])[0:rows, :] * LOG2E
        cncol = jnp.zeros((rows, 1), F32)
        for i in range(n_new):
            cncol = jnp.where(row1 // N_HEADS == i, cnt[:, i:i + 1], cncol)
        cncol_ref[...] = jnp.broadcast_to(cncol, cncol_ref.shape)
        kn = kn_ref[0]
        vn = vn_ref[0]
        s_new = []
        for j in range(n_new):
            sj = jnp.sum(qbd * kn[j:j + 1, :], axis=1, keepdims=True) + cncol - cnt[:, j:j + 1]
            s_new.append(jnp.where(row1 // N_HEADS >= j, sj, NEG))
        m0 = functools.reduce(jnp.maximum, s_new)
        l0 = jnp.zeros((rows, 1), F32)
        a0 = jnp.zeros((rows, ATTN_W), F32)
        for j in range(n_new):
            pj = jnp.exp2(s_new[j] - m0)
            l0 = l0 + pj
            a0 = a0 + pj * vn[j:j + 1, :]
        m_ref[...] = jnp.broadcast_to(m0, m_ref.shape)
        l_ref[...] = jnp.broadcast_to(l0, l_ref.shape)
        acc_ref[...] = a0
        tail_ref[...] = jnp.zeros(tail_ref.shape, F32)

    lane8 = lax.broadcasted_iota(jnp.int32, (N_HEADS, page), 1)
    tail = tail_ref[...]
    s_parts, g_parts, v_parts = [], [], []
    for r in range(n_pages_step):
        kb = k_refs[r][0].reshape(ATTN_W, page).astype(BF16)
        s_parts.append(jnp.dot(qbd_ref[...], kb, preferred_element_type=F32))
        lf = lf_refs[r][0]
        incl = lf
        d = 1
        while d < page:
            incl = incl + jnp.where(lane8 + d < page, pltpu.roll(incl, page - d, axis=1), 0.0)
            d *= 2
        g_parts.append(incl - lf + tail)
        tail = tail + incl[:, 0:1]
        v_parts.append(v_refs[r][0].reshape(ATTN_W, page).astype(BF16))
    tail_ref[...] = tail
    g = jnp.concatenate(g_parts, axis=1) * LOG2E
    s = (jnp.concatenate(s_parts, axis=1) + jnp.concatenate([g] * n_new, axis=0)
         + cncol_ref[:, 0:1])
    m_prev = m_ref[...]
    m_next = jnp.maximum(m_prev, jnp.max(s, axis=1, keepdims=True))
    alpha = jnp.exp2(m_prev - m_next)
    p = jnp.exp2(s - _lane_tile(m_next, n_pages_step * page // LANES))
    l_ref[...] = alpha * l_ref[...] + jnp.sum(p, axis=1, keepdims=True)
    m_ref[...] = m_next
    acc_ref[...] = (_lane_tile(alpha, ATTN_W // LANES) * acc_ref[...]
                    + lax.dot_general(p.astype(BF16), jnp.concatenate(v_parts, axis=1),
                                      (((1,), (1,)), ((), ())), preferred_element_type=F32))

    @pl.when(c == pl.num_programs(1) - 1)
    def _():
        o = acc_ref[...] / _lane_tile(l_ref[...], ATTN_W // LANES)
        o = jnp.where(head_lanes, o, 0.0)
        for i in range(n_new):
            o_ref[0, i:i + 1, :] = jnp.sum(o[N_HEADS * i:N_HEADS * (i + 1), :], axis=0,
                                           keepdims=True)


def _decode(page_table, q3, kn3, vn3, cn3, ck, cv, clf_t):
    nb, n_new, _ = q3.shape
    n_pages = page_table.shape[1]
    page = ck.shape[3]
    pps = DEC_PAGES
    while n_pages % pps:
        pps //= 2
    n_steps = n_pages // pps
    rows = n_new * N_HEADS
    assert page % LANES == 0 and rows % 8 == 0

    def page_map(r, ndim):
        def index_map(b, c, pt):
            return (pt[b * n_pages + n_pages - 1 - (c * pps + r)],) + (0,) * (ndim - 1)
        return index_map

    per_b = lambda n: pl.BlockSpec((1, n_new, n), lambda b, c, pt: (b, 0, 0))
    in_specs = [per_b(ATTN_W), per_b(ATTN_W), per_b(ATTN_W), per_b(LANES)]
    kv_blk = (1, N_HEADS, HEAD_DIM, page)
    in_specs += [pl.BlockSpec(kv_blk, page_map(r, 4)) for r in range(pps)]
    in_specs += [pl.BlockSpec(kv_blk, page_map(r, 4)) for r in range(pps)]
    in_specs += [pl.BlockSpec((1, N_HEADS, page), page_map(r, 3)) for r in range(pps)]
    grid_spec = pltpu.PrefetchScalarGridSpec(
        num_scalar_prefetch=1,
        grid=(nb, n_steps),
        in_specs=in_specs,
        out_specs=pl.BlockSpec((1, n_new, ATTN_W), lambda b, c, pt: (b, 0, 0)),
        scratch_shapes=[pltpu.VMEM((rows, ATTN_W), BF16),
                        pltpu.VMEM((8, LANES), F32),
                        pltpu.VMEM((rows, LANES), F32),
                        pltpu.VMEM((rows, LANES), F32),
                        pltpu.VMEM((rows, LANES), F32),
                        pltpu.VMEM((rows, ATTN_W), F32),
                        pltpu.VMEM((N_HEADS, LANES), F32)])
    return pl.pallas_call(
        functools.partial(_decode_kernel, n_new=n_new, page=page, n_pages_step=pps),
        grid_spec=grid_spec,
        out_shape=jax.ShapeDtypeStruct((nb, n_new, ATTN_W), F32),
        compiler_params=pltpu.CompilerParams(dimension_semantics=("arbitrary", "arbitrary"),
                                             vmem_limit_bytes=VMEM_LIMIT),
        name="decode_attn",
    )(page_table.reshape(-1), q3, kn3, vn3, cn3, *([ck] * pps), *([cv] * pps), *([clf_t] * pps))


def _merge_sample_kernel(x_ref, u_ref, sp_ref, sgp_ref, attn_ref, sga_ref, pw_ref, ps_ref, wo_ref,
                         nf_ref, y_ref, pool_ref, *, n_new, past):
    u = [u_ref[:, POOL_W * i:POOL_W * (i + 1)] for i in range(n_new)]
    hist = [sp_ref[r] for r in range(POOL_HIST)]
    ext = hist + u
    pooled_rows = []
    for i in range(n_new):
        groups = []
        for g, w in enumerate(POOL_WINDOWS):
            sl = slice(POOL_GC * g, POOL_GC * (g + 1))
            end = POOL_HIST + i
            total = ext[end][:, sl]
            for r in range(end - w + 1, end):
                total = total + ext[r][:, sl]
            groups.append(total / float(min(w, past + i + 1)) - u[i][:, sl])
        pooled_rows.append(jnp.concatenate(groups, axis=1))
    pooled = jnp.concatenate(pooled_rows, axis=0)
    cat = lambda ref, n: jnp.concatenate([ref[:, n * i:n * (i + 1)] for i in range(n_new)], axis=0)
    y = _merge(cat(x_ref, D_MODEL), pooled, cat(sgp_ref, POOL_W), cat(attn_ref, ATTN_W),
               cat(sga_ref, ATTN_W), pw_ref, ps_ref[...], wo_ref, nf_ref[...])
    nb = x_ref.shape[0]
    for i in range(n_new):
        y_ref[:, D_MODEL * i:D_MODEL * (i + 1)] = y[nb * i:nb * (i + 1), :]
    new_hist = ext[-POOL_HIST:]
    for r in range(POOL_HIST):
        pool_ref[:, POOL_W * r:POOL_W * (r + 1)] = new_hist[r]


def _merge_sample(xs2, u2, sp2, sgp2, attn2, sga2, pw, ps, wo, nf, *, n_new, past):
    nb = xs2.shape[0]
    return pl.pallas_call(
        functools.partial(_merge_sample_kernel, n_new=n_new, past=past),
        out_shape=(jax.ShapeDtypeStruct((nb, n_new * D_MODEL), F32),
                   jax.ShapeDtypeStruct((nb, POOL_HIST * POOL_W), F32)),
        compiler_params=pltpu.CompilerParams(vmem_limit_bytes=VMEM_LIMIT),
        name="merge_sample",
    )(xs2, u2, sp2, sgp2, attn2, sga2, pw, ps, wo, nf)


def kernel(x_prompt, x_sample, cache_k, cache_v, cache_logf, state_pool, page_table,
           norm_w, w_in, b_forget, pool_w, pool_scale, w_out, norm_f):
    depth = norm_w.shape[0]
    assert depth == 1, "a single layer is supported"
    b_p, seq, _ = x_prompt.shape
    b_s, t_s, _ = x_sample.shape
    n_phys, page = cache_k.shape[1], cache_k.shape[2]
    past = page_table.shape[1] * page
    ll = 0

    n_main = 2 * POOL_W + 4 * ATTN_W
    w = jnp.concatenate([w_in[ll][:, :n_main], jnp.tile(w_in[ll][:, n_main:], (1, LANES // N_HEADS))],
                        axis=1).astype(BF16)
    bft = jnp.tile(b_forget[ll], LANES // N_HEADS).reshape(1, LANES).astype(F32)
    nw = norm_w[ll].reshape(1, D_MODEL)
    nf = norm_f.reshape(1, D_MODEL)
    pw = pool_w[ll].astype(BF16)
    ps = pool_scale[ll].reshape(1, POOL_W)
    wo = w_out[ll].astype(BF16)
    pp_np, cc_np = _placement()
    pp = jnp.asarray(pp_np, BF16)
    cc = jnp.asarray(cc_np, F32)

    xp2 = x_prompt.reshape(b_p * seq, D_MODEL)
    seg_np = (np.arange(ATTN_W)[:, None] // HEAD_DIM == np.arange(LANES)[None, :] % N_HEADS)
    seg = jnp.asarray(seg_np, BF16)
    u, sgp, qa, ka, kf, vf, vb, sga, logf, stats = _proj_prompt(xp2, nw, w, bft, pp, cc, seg, seq=seq)
    n_blk = seq // min(ATTN_T, seq)
    plan = _plan(stats, n_batch=b_p, n_blk=n_blk)[:b_p * N_PAIRS, :n_blk].reshape(-1)
    attn = _attn_prompt(plan, qa, ka, vb, n_batch=b_p, seq=seq)
    yp, pool_p = _merge_prompt(xp2, u, sgp, attn, sga, pw, ps, wo, nf, n_batch=b_p, seq=seq)

    xs2 = x_sample.reshape(b_s, t_s * D_MODEL)
    us, sgps, qs, ks, vs, sgas, lfs, cns = _proj_sample(xs2, nw, w, bft, n_new=t_s)
    ck = jnp.transpose(cache_k[ll], (0, 2, 3, 1))
    cv = jnp.transpose(cache_v[ll], (0, 2, 3, 1))
    clf_t = jnp.swapaxes(cache_logf[ll], 1, 2)
    attn_s = _decode(page_table, qs.reshape(b_s, t_s, ATTN_W), ks.reshape(b_s, t_s, ATTN_W),
                     vs.reshape(b_s, t_s, ATTN_W), cns.reshape(b_s, t_s, LANES), ck, cv, clf_t)
    ys, pool_s = _merge_sample(xs2, us, jnp.swapaxes(state_pool[ll], 0, 1), sgps,
                               attn_s.reshape(b_s, t_s * ATTN_W), sgas, pw, ps, wo, nf,
                               n_new=t_s, past=past)

    return (yp.reshape(b_p, seq, D_MODEL),
            ys.reshape(b_s, t_s, D_MODEL),
            kf.reshape(1, b_p, seq, N_HEADS, HEAD_DIM),
            vf.reshape(1, b_p, seq, N_HEADS, HEAD_DIM),
            logf.reshape(1, b_p, seq, N_HEADS),
            pool_p.reshape(1, b_p, POOL_HIST, POOL_W),
            ks.reshape(1, b_s, t_s, N_HEADS, HEAD_DIM),
            vs.reshape(1, b_s, t_s, N_HEADS, HEAD_DIM),
            lfs.reshape(b_s, t_s, LANES)[:, :, :N_HEADS].reshape(1, b_s, t_s, N_HEADS),
            pool_s.reshape(1, b_s, POOL_HIST, POOL_W))
```

```python
import functools
import math

import numpy as np
import jax
import jax.numpy as jnp
from jax import lax
from jax.experimental import pallas as pl
from jax.experimental.pallas import tpu as pltpu

D_MODEL = 1024
POOL_W = 512
ATTN_W = 512
N_HEADS = 8
HEAD_DIM = 64
POOL_WINDOWS = (2, 4, 8, 16)
POOL_GC = 128
POOL_HIST = 15
EPS = 1e-6

LANES = 128
LOG2E = 1.4426950408889634
Q_SCALE = LOG2E / math.sqrt(HEAD_DIM)
NEG = -1e30
N_PAIRS = N_HEADS // 2
W_COLS = 2 * POOL_W + 4 * ATTN_W + LANES
BIAS_COLS = N_PAIRS * LANES + N_HEADS * LANES
VMEM_LIMIT = 56 * 1024 * 1024

SKIP_LOG2 = 152.0
NORM_SLACK = 1.02

PROJ_TM = 512
ATTN_T = 512
DEC_PAGES = 16

BF16 = jnp.bfloat16
F32 = jnp.float32


def _placement():
    pp = np.zeros((LANES, BIAS_COLS), np.float32)
    cc = np.zeros((1, BIAS_COLS), np.float32)
    kbase = N_PAIRS * LANES
    for h in range(N_HEADS):
        p, e = divmod(h, 2)
        for i in range(3):
            pp[8 * i + h, LANES * p + 6 * e + i] = 1.0
            pp[8 * i + h, kbase + LANES * h + 6 * e + 3 + i] = -1.0
            cc[0, LANES * p + 6 * e + 3 + i] = 1.0
            cc[0, kbase + LANES * h + 6 * e + i] = 1.0
    return pp, cc


def _log_sigmoid(x):
    return jnp.minimum(x, 0.0) - jnp.log(1.0 + jnp.exp(-jnp.abs(x)))


def _silu(x):
    return x * jax.nn.sigmoid(x)


def _rmsnorm(xf, w):
    ms = jnp.mean(xf * xf, axis=-1, keepdims=True)
    return xf * lax.rsqrt(ms + EPS) * w


def _project(h, w_ref):
    def cols(c0, n):
        return jnp.dot(h, w_ref[:, c0:c0 + n], preferred_element_type=F32)
    u = cols(0, POOL_W)
    gp = cols(POOL_W, POOL_W)
    q = cols(2 * POOL_W, ATTN_W)
    k = cols(2 * POOL_W + ATTN_W, ATTN_W)
    v = cols(2 * POOL_W + 2 * ATTN_W, ATTN_W)
    ga = cols(2 * POOL_W + 3 * ATTN_W, ATTN_W)
    fl = cols(2 * POOL_W + 4 * ATTN_W, LANES)
    return u, gp, q, k, v, ga, fl


def _proj_prompt_kernel(x_ref, nw_ref, w_ref, bf_ref, pp_ref, cc_ref, seg_ref,
                        u_ref, sgp_ref, qa_ref, ka_ref, k_ref, v_ref, vb_ref, sga_ref, logf_ref,
                        st_ref, carry_ref, *, tm, ta, tiles_per_batch):
    i = pl.program_id(0)
    h = _rmsnorm(x_ref[...], nw_ref[...]).astype(BF16)
    u, gp, q, k, v, ga, fl = _project(h, w_ref)
    u_ref[...] = u
    sgp_ref[...] = _silu(gp).astype(BF16)
    k_ref[0] = jnp.transpose(k)
    v_ref[0] = jnp.transpose(v)
    vb_ref[...] = v.astype(BF16)
    sga_ref[...] = _silu(ga).astype(BF16)
    logf = _log_sigmoid(fl + bf_ref[...])
    logf_ref[0] = jnp.transpose(logf)[0:N_HEADS, :]

    row = lax.broadcasted_iota(jnp.int32, (tm, LANES), 0)
    acc = logf
    d = 1
    while d < tm:
        acc = acc + jnp.where(row >= d, pltpu.roll(acc, d, axis=0), 0.0)
        d *= 2
    @pl.when(i % tiles_per_batch == 0)
    def _():
        carry_ref[...] = jnp.zeros(carry_ref.shape, F32)

    f_run = acc + carry_ref[0:1, :]
    carry_ref[...] = jnp.broadcast_to(f_run[tm - 1:tm, :], carry_ref.shape)

    f2 = f_run * LOG2E
    p1 = f2.astype(BF16).astype(F32)
    r1 = f2 - p1
    p2 = r1.astype(BF16).astype(F32)
    p3 = (r1 - p2).astype(BF16).astype(F32)
    lane = lax.broadcasted_iota(jnp.int32, (tm, LANES), 1)
    pieces = jnp.where(lane < 8, p1, jnp.where(lane < 16, p2, jnp.where(lane < 24, p3, 0.0)))
    bias = jnp.dot(pieces.astype(BF16), pp_ref[...], preferred_element_type=F32) + cc_ref[...]

    qs = q * Q_SCALE

    qn2 = jnp.dot((qs * qs).astype(BF16), seg_ref[...], preferred_element_type=F32)
    kn2 = jnp.dot((k * k).astype(BF16), seg_ref[...], preferred_element_type=F32)
    srow = lax.broadcasted_iota(jnp.int32, (8, LANES), 0)
    for sb in range(tm // ta):
        r0, r1 = sb * ta, (sb + 1) * ta
        qmax = jnp.sqrt(jnp.max(qn2[r0:r1], axis=0, keepdims=True))
        kmax = jnp.sqrt(jnp.max(kn2[r0:r1], axis=0, keepdims=True))
        st_ref[8 * sb:8 * (sb + 1), :] = jnp.where(
            srow == 0, qmax, jnp.where(srow == 1, kmax, jnp.where(
                srow == 2, f2[r0:r0 + 1, :], jnp.where(srow == 3, f2[r1 - 1:r1, :], 0.0))))

    first_half = lane < HEAD_DIM
    kbase = N_PAIRS * LANES
    for p in range(N_PAIRS):
        qa_ref[:, 2 * LANES * p:2 * LANES * p + LANES] = qs[:, LANES * p:LANES * (p + 1)].astype(BF16)
        qa_ref[:, 2 * LANES * p + LANES:2 * LANES * (p + 1)] = (
            bias[:, LANES * p:LANES * (p + 1)].astype(BF16))
        kp = k[:, LANES * p:LANES * (p + 1)]
        for e in range(2):
            hh = 2 * p + e
            keep = first_half if e == 0 else jnp.logical_not(first_half)
            ka_ref[:, 2 * LANES * hh:2 * LANES * hh + LANES] = jnp.where(keep, kp, 0.0).astype(BF16)
            ka_ref[:, 2 * LANES * hh + LANES:2 * LANES * (hh + 1)] = (
                bias[:, kbase + LANES * hh:kbase + LANES * (hh + 1)].astype(BF16))


def _proj_prompt(x2, nw, w, bft, pp, cc, seg, *, seq):
    rows = x2.shape[0]
    tm = min(PROJ_TM, seq)
    ta = min(ATTN_T, seq)
    assert seq % tm == 0 and rows % seq == 0 and tm % ta == 0 and tm % LANES == 0
    n_batch = rows // seq
    row_blk = lambda n: pl.BlockSpec((tm, n), lambda i: (i, 0))
    const = lambda shape: pl.BlockSpec(shape, lambda i: (0, 0))
    out_shape = (
        jax.ShapeDtypeStruct((rows, POOL_W), F32),
        jax.ShapeDtypeStruct((rows, POOL_W), BF16),
        jax.ShapeDtypeStruct((rows, N_PAIRS * 2 * LANES), BF16),
        jax.ShapeDtypeStruct((rows, N_HEADS * 2 * LANES), BF16),
        jax.ShapeDtypeStruct((n_batch, ATTN_W, seq), F32),
        jax.ShapeDtypeStruct((n_batch, ATTN_W, seq), F32),
        jax.ShapeDtypeStruct((rows, ATTN_W), BF16),
        jax.ShapeDtypeStruct((rows, ATTN_W), BF16),
        jax.ShapeDtypeStruct((n_batch, N_HEADS, seq), F32),
        jax.ShapeDtypeStruct((rows // ta * 8, LANES), F32),
    )
    tpb = seq // tm
    t_minor = lambda n: pl.BlockSpec((1, n, tm), lambda i: (i // tpb, 0, i % tpb))
    out_specs = (row_blk(POOL_W), row_blk(POOL_W), row_blk(N_PAIRS * 2 * LANES),
                 row_blk(N_HEADS * 2 * LANES), t_minor(ATTN_W), t_minor(ATTN_W), row_blk(ATTN_W),
                 row_blk(ATTN_W), t_minor(N_HEADS),
                 pl.BlockSpec((tm // ta * 8, LANES), lambda i: (i, 0)))
    return pl.pallas_call(
        functools.partial(_proj_prompt_kernel, tm=tm, ta=ta, tiles_per_batch=seq // tm),
        grid=(rows // tm,),
        in_specs=[row_blk(D_MODEL), const((1, D_MODEL)), const((D_MODEL, W_COLS)), const((1, LANES)),
                  const((LANES, BIAS_COLS)), const((1, BIAS_COLS)), const((ATTN_W, LANES))],
        out_specs=out_specs,
        out_shape=out_shape,
        scratch_shapes=[pltpu.VMEM((8, LANES), F32)],
        compiler_params=pltpu.CompilerParams(dimension_semantics=("arbitrary",),
                                             vmem_limit_bytes=VMEM_LIMIT),
        name="proj_prompt",
    )(x2, nw, w, bft, pp, cc, seg)


def _lane_tile(x, reps):
    return jnp.concatenate([x] * reps, axis=1)


def _plan_kernel(st_ref, o_ref, *, n_batch, n_blk):
    row = lax.broadcasted_iota(jnp.int32, (LANES, LANES), 0).astype(F32)
    col = lax.broadcasted_iota(jnp.int32, (LANES, LANES), 1).astype(F32)
    out = jnp.zeros((LANES, LANES), F32)
    for b in range(n_batch):
        def stat(r):
            v = st_ref[pl.ds((b * n_blk) * 8 + r, n_blk, stride=8), :]
            return jnp.concatenate([v, jnp.zeros((LANES - n_blk, LANES), F32)], axis=0)
        qm, km, ft0, fs1 = stat(0), stat(1), stat(2), stat(3)
        km_t, fs1_t = jnp.transpose(km), jnp.transpose(fs1)
        for p in range(N_PAIRS):
            first = row
            for h in (2 * p, 2 * p + 1):
                qcol = qm[:, h:h + 1]
                ub = (NORM_SLACK * (qcol * km_t[h:h + 1, :] + qcol * km[:, h:h + 1])
                      + ft0[:, h:h + 1] - fs1_t[h:h + 1, :])
                needed = jnp.logical_and(ub >= -SKIP_LOG2, col < row)
                first = jnp.minimum(first, jnp.where(needed, col, row))
            start = jnp.min(first, axis=1, keepdims=True)
            out = jnp.where(col == b * N_PAIRS + p, start, out)
    o_ref[...] = jnp.transpose(out).astype(jnp.int32)


def _plan(stats, *, n_batch, n_blk):
    assert n_blk <= LANES and n_batch * N_PAIRS <= LANES
    return pl.pallas_call(
        functools.partial(_plan_kernel, n_batch=n_batch, n_blk=n_blk),
        out_shape=jax.ShapeDtypeStruct((LANES, LANES), jnp.int32),
        name="attn_plan",
    )(stats)


def _attn_prompt_kernel(plan_ref, qa_ref, ka_ref, vb_ref, o_ref, m_ref, l_ref, acc_ref, *, t, nq):
    b, pr, i = pl.program_id(0), pl.program_id(1), pl.program_id(2)
    q = qa_ref[...]
    m_ref[...] = jnp.full(m_ref.shape, NEG, F32)
    l_ref[...] = jnp.zeros(l_ref.shape, F32)
    acc_ref[...] = jnp.zeros(acc_ref.shape, F32)
    reps = t // LANES

    def block(kstart, diagonal):
        vblk = vb_ref[pl.ds(kstart, t), :]
        for e in range(2):
            kblk = ka_ref[pl.ds(kstart, t), 2 * LANES * e:2 * LANES * (e + 1)]
            s = lax.dot_general(q, kblk, (((1,), (1,)), ((), ())), preferred_element_type=F32)
            if diagonal:
                r = lax.broadcasted_iota(jnp.int32, (t, t), 0)
                c = lax.broadcasted_iota(jnp.int32, (t, t), 1)
                s = jnp.where(c <= r, s, NEG)
            m_prev = m_ref[e]
            m_next = jnp.maximum(m_prev, jnp.max(s, axis=1, keepdims=True))
            alpha = jnp.exp2(m_prev - m_next)
            p = jnp.exp2(s - _lane_tile(m_next, reps))
            l_ref[e] = alpha * l_ref[e] + jnp.sum(p, axis=1, keepdims=True)
            m_ref[e] = m_next
            acc_ref[e] = alpha * acc_ref[e] + jnp.dot(p.astype(BF16), vblk,
                                                      preferred_element_type=F32)

    def body(j, carry):
        block(pl.multiple_of(j * t, t), False)
        return carry

    lax.fori_loop(plan_ref[(b * N_PAIRS + pr) * nq + i], i, body, 0)
    block(pl.multiple_of(i * t, t), True)

    lane = lax.broadcasted_iota(jnp.int32, (t, LANES), 1)
    o = jnp.where(lane < HEAD_DIM, acc_ref[0] / l_ref[0], acc_ref[1] / l_ref[1])
    o_ref[...] = o.astype(o_ref.dtype)


def _attn_prompt(plan, qa, ka, vb, *, n_batch, seq):
    t = min(ATTN_T, seq)
    assert seq % t == 0
    nq = seq // t
    grid_spec = pltpu.PrefetchScalarGridSpec(
        num_scalar_prefetch=1,
        grid=(n_batch, N_PAIRS, nq),
        in_specs=[pl.BlockSpec((t, 2 * LANES), lambda b, p, i, plan: (b * nq + i, p)),
                  pl.BlockSpec((seq, 4 * LANES), lambda b, p, i, plan: (b, p)),
                  pl.BlockSpec((seq, LANES), lambda b, p, i, plan: (b, p))],
        out_specs=pl.BlockSpec((t, LANES), lambda b, p, i, plan: (b * nq + i, p)),
        scratch_shapes=[pltpu.VMEM((2, t, LANES), F32), pltpu.VMEM((2, t, LANES), F32),
                        pltpu.VMEM((2, t, LANES), F32)])
    return pl.pallas_call(
        functools.partial(_attn_prompt_kernel, t=t, nq=nq),
        grid_spec=grid_spec,
        out_shape=jax.ShapeDtypeStruct((n_batch * seq, ATTN_W), BF16),
        compiler_params=pltpu.CompilerParams(
            dimension_semantics=("arbitrary", "arbitrary", "arbitrary"),
            vmem_limit_bytes=VMEM_LIMIT),
        name="attn_prompt",
    )(plan, qa, ka, vb)


def _merge(x, pooled, sgp, attn, sga, pw_ref, ps, wo_ref, nf):
    mixed = [jnp.dot(pooled[:, POOL_GC * g:POOL_GC * (g + 1)].astype(BF16), pw_ref[g],
                     preferred_element_type=F32) for g in range(len(POOL_WINDOWS))]
    pool_out = jnp.concatenate(mixed, axis=1) * ps
    mix = jnp.concatenate([(pool_out * sgp.astype(F32)).astype(BF16),
                           (attn.astype(F32) * sga.astype(F32)).astype(BF16)], axis=1)
    xo = x + jnp.dot(mix, wo_ref[...], preferred_element_type=F32)
    return _rmsnorm(xo, nf)


def _merge_prompt_kernel(x_ref, u_ref, uh_ref, sgp_ref, attn_ref, sga_ref, pw_ref, ps_ref, wo_ref,
                         nf_ref, y_ref, pool_ref, *, tm, tiles_per_batch):
    i = pl.program_id(0)
    ti = i % tiles_per_batch
    u = u_ref[...]
    halo = jnp.where(ti == 0, 0.0, uh_ref[...])
    ext = jnp.concatenate([halo, u], axis=0)
    pos = ti * tm + lax.broadcasted_iota(jnp.int32, (tm, 1), 0)
    groups = []
    level = ext
    shift = 1
    for g, w in enumerate(POOL_WINDOWS):
        while shift < w:
            level = level + pltpu.roll(level, shift, axis=0)
            shift *= 2
        cnt = jnp.minimum(w, pos + 1).astype(F32)
        sl = slice(POOL_GC * g, POOL_GC * (g + 1))
        groups.append(level[16:, sl] / cnt - u[:, sl])
    pooled = jnp.concatenate(groups, axis=1)
    y_ref[...] = _merge(x_ref[...], pooled, sgp_ref[...], attn_ref[...], sga_ref[...], pw_ref,
                        ps_ref[...], wo_ref, nf_ref[...])

    @pl.when(ti == tiles_per_batch - 1)
    def _():
        pool_ref[0] = u_ref[pl.ds(tm - POOL_HIST, POOL_HIST), :]


def _merge_prompt(x2, u, sgp, attn, sga, pw, ps, wo, nf, *, n_batch, seq):
    rows = x2.shape[0]
    tm = min(PROJ_TM, seq)
    tpb = seq // tm
    halo_blocks = tm // 16
    row_blk = lambda n: pl.BlockSpec((tm, n), lambda i: (i, 0))
    const2 = lambda shape: pl.BlockSpec(shape, lambda i: (0, 0))
    return pl.pallas_call(
        functools.partial(_merge_prompt_kernel, tm=tm, tiles_per_batch=tpb),
        grid=(rows // tm,),
        in_specs=[row_blk(D_MODEL), row_blk(POOL_W),
                  pl.BlockSpec((16, POOL_W), lambda i: (jnp.maximum(i * halo_blocks - 1, 0), 0)),
                  row_blk(POOL_W), row_blk(ATTN_W), row_blk(ATTN_W),
                  pl.BlockSpec((len(POOL_WINDOWS), POOL_GC, POOL_GC), lambda i: (0, 0, 0)),
                  const2((1, POOL_W)), const2((D_MODEL, D_MODEL)), const2((1, D_MODEL))],
        out_specs=(row_blk(D_MODEL),
                   pl.BlockSpec((1, POOL_HIST, POOL_W), lambda i: (i // tpb, 0, 0))),
        out_shape=(jax.ShapeDtypeStruct((rows, D_MODEL), F32),
                   jax.ShapeDtypeStruct((n_batch, POOL_HIST, POOL_W), F32)),
        compiler_params=pltpu.CompilerParams(dimension_semantics=("arbitrary",),
                                             vmem_limit_bytes=VMEM_LIMIT),
        name="merge_prompt",
    )(x2, u, u, sgp, attn, sga, pw, ps, wo, nf)


def _proj_sample_kernel(x_ref, nw_ref, w_ref, bf_ref,
                        u_ref, sgp_ref, q_ref, k_ref, v_ref, sga_ref, logf_ref, cn_ref, carry_ref):
    i = pl.program_id(0)
    h = _rmsnorm(x_ref[...], nw_ref[...]).astype(BF16)
    u, gp, q, k, v, ga, fl = _project(h, w_ref)
    u_ref[...] = u
    sgp_ref[...] = _silu(gp).astype(BF16)
    q_ref[...] = q * Q_SCALE
    k_ref[...] = k
    v_ref[...] = v
    sga_ref[...] = _silu(ga).astype(BF16)
    logf = _log_sigmoid(fl + bf_ref[...])
    logf_ref[...] = logf
    @pl.when(i == 0)
    def _():
        carry_ref[...] = jnp.zeros(carry_ref.shape, F32)

    cn = carry_ref[...] + logf
    carry_ref[...] = cn
    cn_ref[...] = cn


def _proj_sample(xs2, nw, w, bft, *, n_new):
    nb = xs2.shape[0]
    blk = lambda n: pl.BlockSpec((nb, n), lambda i: (0, i))
    const = lambda shape: pl.BlockSpec(shape, lambda i: (0, 0))
    widths = (POOL_W, POOL_W, ATTN_W, ATTN_W, ATTN_W, ATTN_W, LANES, LANES)
    dtypes = (F32, BF16, F32, F32, F32, BF16, F32, F32)
    return pl.pallas_call(
        _proj_sample_kernel,
        grid=(n_new,),
        in_specs=[blk(D_MODEL), const((1, D_MODEL)), const((D_MODEL, W_COLS)), const((1, LANES))],
        out_specs=tuple(blk(n) for n in widths),
        out_shape=tuple(jax.ShapeDtypeStruct((nb, n_new * n), dt) for n, dt in zip(widths, dtypes)),
        scratch_shapes=[pltpu.VMEM((nb, LANES), F32)],
        compiler_params=pltpu.CompilerParams(dimension_semantics=("arbitrary",),
                                             vmem_limit_bytes=VMEM_LIMIT),
        name="proj_sample",
    )(xs2, nw, w, bft)


def _decode_kernel(pt_ref, q_ref, kn_ref, vn_ref, cn_ref, *refs, n_new, page, n_pages_step):
    del pt_ref
    k_refs = refs[0:n_pages_step]
    v_refs = refs[n_pages_step:2 * n_pages_step]
    lf_refs = refs[2 * n_pages_step:3 * n_pages_step]
    o_ref = refs[3 * n_pages_step]
    qbd_ref, cn8_ref, cncol_ref, m_ref, l_ref, acc_ref, tail_ref = refs[3 * n_pages_step + 1:]
    c = pl.program_id(1)
    rows = n_new * N_HEADS
    row_w = lax.broadcasted_iota(jnp.int32, (rows, ATTN_W), 0)
    lane_w = lax.broadcasted_iota(jnp.int32, (rows, ATTN_W), 1)
    head_lanes = (row_w % N_HEADS) == (lane_w // HEAD_DIM)
    row1 = lax.broadcasted_iota(jnp.int32, (rows, 1), 0)

    @pl.when(c == 0)
    def _():
        q = q_ref[0]
        qrep = jnp.concatenate(
            [jnp.broadcast_to(q[i:i + 1, :], (N_HEADS, ATTN_W)) for i in range(n_new)], axis=0)
        qbd = jnp.where(head_lanes, qrep, 0.0)
        qbd_ref[...] = qbd.astype(BF16)
        cn8_ref[...] = jnp.zeros(cn8_ref.shape, F32)
        cn8_ref[0:n_new, :] = cn_ref[0]
        cnt = jnp.transpose(cn8_ref[...])[0:rows, :] * LOG2E
        cncol = jnp.zeros((rows, 1), F32)
        for i in range(n_new):
            cncol = jnp.where(row1 // N_HEADS == i, cnt[:, i:i + 1], cncol)
        cncol_ref[...] = jnp.broadcast_to(cncol, cncol_ref.shape)
        kn = kn_ref[0]
        vn = vn_ref[0]
        s_new = []
        for j in range(n_new):
            sj = jnp.sum(qbd * kn[j:j + 1, :], axis=1, keepdims=True) + cncol - cnt[:, j:j + 1]
            s_new.append(jnp.where(row1 // N_HEADS >= j, sj, NEG))
        m0 = functools.reduce(jnp.maximum, s_new)
        l0 = jnp.zeros((rows, 1), F32)
        a0 = jnp.zeros((rows, ATTN_W), F32)
        for j in range(n_new):
            pj = jnp.exp2(s_new[j] - m0)
            l0 = l0 + pj
            a0 = a0 + pj * vn[j:j + 1, :]
        m_ref[...] = jnp.broadcast_to(m0, m_ref.shape)
        l_ref[...] = jnp.broadcast_to(l0, l_ref.shape)
        acc_ref[...] = a0
        tail_ref[...] = jnp.zeros(tail_ref.shape, F32)

    lane8 = lax.broadcasted_iota(jnp.int32, (N_HEADS, page), 1)
    tail = tail_ref[...]
    s_parts, g_parts, v_parts = [], [], []
    for r in range(n_pages_step):
        kb = k_refs[r][0].reshape(ATTN_W, page).astype(BF16)
        s_parts.append(jnp.dot(qbd_ref[...], kb, preferred_element_type=F32))
        lf = lf_refs[r][0]
        incl = lf
        d = 1
        while d < page:
            incl = incl + jnp.where(lane8 + d < page, pltpu.roll(incl, page - d, axis=1), 0.0)
            d *= 2
        g_parts.append(incl - lf + tail)
        tail = tail + incl[:, 0:1]
        v_parts.append(v_refs[r][0].reshape(ATTN_W, page).astype(BF16))
    tail_ref[...] = tail
    g = jnp.concatenate(g_parts, axis=1) * LOG2E
    s = (jnp.concatenate(s_parts, axis=1) + jnp.concatenate([g] * n_new, axis=0)
         + cncol_ref[:, 0:1])
    m_prev = m_ref[...]
    m_next = jnp.maximum(m_prev, jnp.max(s, axis=1, keepdims=True))
    alpha = jnp.exp2(m_prev - m_next)
    p = jnp.exp2(s - _lane_tile(m_next, n_pages_step * page // LANES))
    l_ref[...] = alpha * l_ref[...] + jnp.sum(p, axis=1, keepdims=True)
    m_ref[...] = m_next
    acc_ref[...] = (_lane_tile(alpha, ATTN_W // LANES) * acc_ref[...]
                    + lax.dot_general(p.astype(BF16), jnp.concatenate(v_parts, axis=1),
                                      (((1,), (1,)), ((), ())), preferred_element_type=F32))

    @pl.when(c == pl.num_programs(1) - 1)
    def _():
        o = acc_ref[...] / _lane_tile(l_ref[...], ATTN_W // LANES)
        o = jnp.where(head_lanes, o, 0.0)
        for i in range(n_new):
            o_ref[0, i:i + 1, :] = jnp.sum(o[N_HEADS * i:N_HEADS * (i + 1), :], axis=0,
                                           keepdims=True)


def _decode(page_table, q3, kn3, vn3, cn3, ck, cv, clf_t):
    nb, n_new, _ = q3.shape
    n_pages = page_table.shape[1]
    page = ck.shape[3]
    pps = DEC_PAGES
    while n_pages % pps:
        pps //= 2
    n_steps = n_pages // pps
    rows = n_new * N_HEADS
    assert page % LANES == 0 and rows % 8 == 0

    def page_map(r, ndim):
        def index_map(b, c, pt):
            return (pt[b * n_pages + n_pages - 1 - (c * pps + r)],) + (0,) * (ndim - 1)
        return index_map

    per_b = lambda n: pl.BlockSpec((1, n_new, n), lambda b, c, pt: (b, 0, 0))
    in_specs = [per_b(ATTN_W), per_b(ATTN_W), per_b(ATTN_W), per_b(LANES)]
    kv_blk = (1, N_HEADS, HEAD_DIM, page)
    in_specs += [pl.BlockSpec(kv_blk, page_map(r, 4)) for r in range(pps)]
    in_specs += [pl.BlockSpec(kv_blk, page_map(r, 4)) for r in range(pps)]
    in_specs += [pl.BlockSpec((1, N_HEADS, page), page_map(r, 3)) for r in range(pps)]
    grid_spec = pltpu.PrefetchScalarGridSpec(
        num_scalar_prefetch=1,
        grid=(nb, n_steps),
        in_specs=in_specs,
        out_specs=pl.BlockSpec((1, n_new, ATTN_W), lambda b, c, pt: (b, 0, 0)),
        scratch_shapes=[pltpu.VMEM((rows, ATTN_W), BF16),
                        pltpu.VMEM((8, LANES), F32),
                        pltpu.VMEM((rows, LANES), F32),
                        pltpu.VMEM((rows, LANES), F32),
                        pltpu.VMEM((rows, LANES), F32),
                        pltpu.VMEM((rows, ATTN_W), F32),
                        pltpu.VMEM((N_HEADS, LANES), F32)])
    return pl.pallas_call(
        functools.partial(_decode_kernel, n_new=n_new, page=page, n_pages_step=pps),
        grid_spec=grid_spec,
        out_shape=jax.ShapeDtypeStruct((nb, n_new, ATTN_W), F32),
        compiler_params=pltpu.CompilerParams(dimension_semantics=("arbitrary", "arbitrary"),
                                             vmem_limit_bytes=VMEM_LIMIT),
        name="decode_attn",
    )(page_table.reshape(-1), q3, kn3, vn3, cn3, *([ck] * pps), *([cv] * pps), *([clf_t] * pps))


def _merge_sample_kernel(x_ref, u_ref, sp_ref, sgp_ref, attn_ref, sga_ref, pw_ref, ps_ref, wo_ref,
                         nf_ref, y_ref, pool_ref, *, n_new, past):
    u = [u_ref[:, POOL_W * i:POOL_W * (i + 1)] for i in range(n_new)]
    hist = [sp_ref[r] for r in range(POOL_HIST)]
    ext = hist + u
    pooled_rows = []
    for i in range(n_new):
        groups = []
        for g, w in enumerate(POOL_WINDOWS):
            sl = slice(POOL_GC * g, POOL_GC * (g + 1))
            end = POOL_HIST + i
            total = ext[end][:, sl]
            for r in range(end - w + 1, end):
                total = total + ext[r][:, sl]
            groups.append(total / float(min(w, past + i + 1)) - u[i][:, sl])
        pooled_rows.append(jnp.concatenate(groups, axis=1))
    pooled = jnp.concatenate(pooled_rows, axis=0)
    cat = lambda ref, n: jnp.concatenate([ref[:, n * i:n * (i + 1)] for i in range(n_new)], axis=0)
    y = _merge(cat(x_ref, D_MODEL), pooled, cat(sgp_ref, POOL_W), cat(attn_ref, ATTN_W),
               cat(sga_ref, ATTN_W), pw_ref, ps_ref[...], wo_ref, nf_ref[...])
    nb = x_ref.shape[0]
    for i in range(n_new):
        y_ref[:, D_MODEL * i:D_MODEL * (i + 1)] = y[nb * i:nb * (i + 1), :]
    new_hist = ext[-POOL_HIST:]
    for r in range(POOL_HIST):
        pool_ref[:, POOL_W * r:POOL_W * (r + 1)] = new_hist[r]


def _merge_sample(xs2, u2, sp2, sgp2, attn2, sga2, pw, ps, wo, nf, *, n_new, past):
    nb = xs2.shape[0]
    return pl.pallas_call(
        functools.partial(_merge_sample_kernel, n_new=n_new, past=past),
        out_shape=(jax.ShapeDtypeStruct((nb, n_new * D_MODEL), F32),
                   jax.ShapeDtypeStruct((nb, POOL_HIST * POOL_W), F32)),
        compiler_params=pltpu.CompilerParams(vmem_limit_bytes=VMEM_LIMIT),
        name="merge_sample",
    )(xs2, u2, sp2, sgp2, attn2, sga2, pw, ps, wo, nf)


def kernel(x_prompt, x_sample, cache_k, cache_v, cache_logf, state_pool, page_table,
           norm_w, w_in, b_forget, pool_w, pool_scale, w_out, norm_f):
    depth = norm_w.shape[0]
    assert depth == 1, "a single layer is supported"
    b_p, seq, _ = x_prompt.shape
    b_s, t_s, _ = x_sample.shape
    n_phys, page = cache_k.shape[1], cache_k.shape[2]
    past = page_table.shape[1] * page
    ll = 0

    n_main = 2 * POOL_W + 4 * ATTN_W
    w = jnp.concatenate([w_in[ll][:, :n_main], jnp.tile(w_in[ll][:, n_main:], (1, LANES // N_HEADS))],
                        axis=1).astype(BF16)
    bft = jnp.tile(b_forget[ll], LANES // N_HEADS).reshape(1, LANES).astype(F32)
    nw = norm_w[ll].reshape(1, D_MODEL)
    nf = norm_f.reshape(1, D_MODEL)
    pw = pool_w[ll].astype(BF16)
    ps = pool_scale[ll].reshape(1, POOL_W)
    wo = w_out[ll].astype(BF16)
    pp_np, cc_np = _placement()
    pp = jnp.asarray(pp_np, BF16)
    cc = jnp.asarray(cc_np, F32)

    xp2 = x_prompt.reshape(b_p * seq, D_MODEL)
    seg_np = (np.arange(ATTN_W)[:, None] // HEAD_DIM == np.arange(LANES)[None, :] % N_HEADS)
    seg = jnp.asarray(seg_np, BF16)
    u, sgp, qa, ka, kf, vf, vb, sga, logf, stats = _proj_prompt(xp2, nw, w, bft, pp, cc, seg, seq=seq)
    n_blk = seq // min(ATTN_T, seq)
    plan = _plan(stats, n_batch=b_p, n_blk=n_blk)[:b_p * N_PAIRS, :n_blk].reshape(-1)
    attn = _attn_prompt(plan, qa, ka, vb, n_batch=b_p, seq=seq)
    yp, pool_p = _merge_prompt(xp2, u, sgp, attn, sga, pw, ps, wo, nf, n_batch=b_p, seq=seq)

    xs2 = x_sample.reshape(b_s, t_s * D_MODEL)
    us, sgps, qs, ks, vs, sgas, lfs, cns = _proj_sample(xs2, nw, w, bft, n_new=t_s)
    ck = jnp.transpose(cache_k[ll], (0, 2, 3, 1))
    cv = jnp.transpose(cache_v[ll], (0, 2, 3, 1))
    clf_t = jnp.swapaxes(cache_logf[ll], 1, 2)
    attn_s = _decode(page_table, qs.reshape(b_s, t_s, ATTN_W), ks.reshape(b_s, t_s, ATTN_W),
                     vs.reshape(b_s, t_s, ATTN_W), cns.reshape(b_s, t_s, LANES), ck, cv, clf_t)
    ys, pool_s = _merge_sample(xs2, us, jnp.swapaxes(state_pool[ll], 0, 1), sgps,
                               attn_s.reshape(b_s, t_s * ATTN_W), sgas, pw, ps, wo, nf,
                               n_new=t_s, past=past)

    return (yp.reshape(b_p, seq, D_MODEL),
            ys.reshape(b_s, t_s, D_MODEL),
            jnp.transpose(kf.reshape(1, b_p, N_HEADS, HEAD_DIM, seq), (0, 1, 4, 2, 3)),
            jnp.transpose(vf.reshape(1, b_p, N_HEADS, HEAD_DIM, seq), (0, 1, 4, 2, 3)),
            jnp.transpose(logf, (0, 2, 1)).reshape(1, b_p, seq, N_HEADS),
            pool_p.reshape(1, b_p, POOL_HIST, POOL_W),
            ks.reshape(1, b_s, t_s, N_HEADS, HEAD_DIM),
            vs.reshape(1, b_s, t_s, N_HEADS, HEAD_DIM),
            lfs.reshape(b_s, t_s, LANES)[:, :, :N_HEADS].reshape(1, b_s, t_s, N_HEADS),
            pool_s.reshape(1, b_s, POOL_HIST, POOL_W))
```

```python
import functools
import math

import numpy as np
import jax
import jax.numpy as jnp
from jax import lax
from jax.experimental import pallas as pl
from jax.experimental.pallas import tpu as pltpu

D_MODEL = 1024
POOL_W = 512
ATTN_W = 512
N_HEADS = 8
HEAD_DIM = 64
POOL_WINDOWS = (2, 4, 8, 16)
POOL_GC = 128
POOL_HIST = 15
EPS = 1e-6

LANES = 128
LOG2E = 1.4426950408889634
Q_SCALE = LOG2E / math.sqrt(HEAD_DIM)
NEG = -1e30
N_PAIRS = N_HEADS // 2
W_COLS = 2 * POOL_W + 4 * ATTN_W + LANES
BIAS_COLS = N_PAIRS * LANES + N_HEADS * LANES
VMEM_LIMIT = 56 * 1024 * 1024

SKIP_LOG2 = 152.0
NORM_SLACK = 1.02

PROJ_TM = 512
ATTN_T = 512
DEC_PAGES = 16

BF16 = jnp.bfloat16
F32 = jnp.float32


def _placement():
    pp = np.zeros((LANES, BIAS_COLS), np.float32)
    cc = np.zeros((1, BIAS_COLS), np.float32)
    kbase = N_PAIRS * LANES
    for h in range(N_HEADS):
        p, e = divmod(h, 2)
        for i in range(3):
            pp[8 * i + h, LANES * p + 6 * e + i] = 1.0
            pp[8 * i + h, kbase + LANES * h + 6 * e + 3 + i] = -1.0
            cc[0, LANES * p + 6 * e + 3 + i] = 1.0
            cc[0, kbase + LANES * h + 6 * e + i] = 1.0
    return pp, cc


def _log_sigmoid(x):
    return jnp.minimum(x, 0.0) - jnp.log(1.0 + jnp.exp(-jnp.abs(x)))


def _silu(x):
    return x * jax.nn.sigmoid(x)


def _rmsnorm(xf, w):
    ms = jnp.mean(xf * xf, axis=-1, keepdims=True)
    return xf * lax.rsqrt(ms + EPS) * w


def _project(h, w_ref):
    def cols(c0, n):
        return jnp.dot(h, w_ref[:, c0:c0 + n], preferred_element_type=F32)
    u = cols(0, POOL_W)
    gp = cols(POOL_W, POOL_W)
    q = cols(2 * POOL_W, ATTN_W)
    k = cols(2 * POOL_W + ATTN_W, ATTN_W)
    v = cols(2 * POOL_W + 2 * ATTN_W, ATTN_W)
    ga = cols(2 * POOL_W + 3 * ATTN_W, ATTN_W)
    fl = cols(2 * POOL_W + 4 * ATTN_W, LANES)
    return u, gp, q, k, v, ga, fl


def _proj_prompt_kernel(x_ref, nw_ref, w_ref, bf_ref, pp_ref, cc_ref, seg_ref,
                        u_ref, sgp_ref, qa_ref, ka_ref, k_ref, v_ref, vb_ref, sga_ref, logf_ref,
                        st_ref, carry_ref, *, tm, ta, tiles_per_batch):
    i = pl.program_id(0)
    h = _rmsnorm(x_ref[...], nw_ref[...]).astype(BF16)
    u, gp, q, k, v, ga, fl = _project(h, w_ref)
    u_ref[...] = u
    sgp_ref[...] = _silu(gp).astype(BF16)
    k_ref[0] = jnp.transpose(k)
    v_ref[0] = jnp.transpose(v)
    vb_ref[...] = v.astype(BF16)
    sga_ref[...] = _silu(ga).astype(BF16)
    logf = _log_sigmoid(fl + bf_ref[...])
    logf_ref[0] = jnp.transpose(logf)[0:N_HEADS, :]

    row = lax.broadcasted_iota(jnp.int32, (tm, LANES), 0)
    acc = logf
    d = 1
    while d < tm:
        acc = acc + jnp.where(row >= d, pltpu.roll(acc, d, axis=0), 0.0)
        d *= 2
    @pl.when(i % tiles_per_batch == 0)
    def _():
        carry_ref[...] = jnp.zeros(carry_ref.shape, F32)

    f_run = acc + carry_ref[0:1, :]
    carry_ref[...] = jnp.broadcast_to(f_run[tm - 1:tm, :], carry_ref.shape)

    f2 = f_run * LOG2E
    p1 = f2.astype(BF16).astype(F32)
    r1 = f2 - p1
    p2 = r1.astype(BF16).astype(F32)
    p3 = (r1 - p2).astype(BF16).astype(F32)
    lane = lax.broadcasted_iota(jnp.int32, (tm, LANES), 1)
    pieces = jnp.where(lane < 8, p1, jnp.where(lane < 16, p2, jnp.where(lane < 24, p3, 0.0)))
    bias = jnp.dot(pieces.astype(BF16), pp_ref[...], preferred_element_type=F32) + cc_ref[...]

    qs = q * Q_SCALE

    qn2 = jnp.dot((qs * qs).astype(BF16), seg_ref[...], preferred_element_type=F32)
    kn2 = jnp.dot((k * k).astype(BF16), seg_ref[...], preferred_element_type=F32)
    srow = lax.broadcasted_iota(jnp.int32, (8, LANES), 0)
    for sb in range(tm // ta):
        r0, r1 = sb * ta, (sb + 1) * ta
        qmax = jnp.sqrt(jnp.max(qn2[r0:r1], axis=0, keepdims=True))
        kmax = jnp.sqrt(jnp.max(kn2[r0:r1], axis=0, keepdims=True))
        st_ref[8 * sb:8 * (sb + 1), :] = jnp.where(
            srow == 0, qmax, jnp.where(srow == 1, kmax, jnp.where(
                srow == 2, f2[r0:r0 + 1, :], jnp.where(srow == 3, f2[r1 - 1:r1, :], 0.0))))

    first_half = lane < HEAD_DIM
    kbase = N_PAIRS * LANES
    for p in range(N_PAIRS):
        qa_ref[:, 2 * LANES * p:2 * LANES * p + LANES] = qs[:, LANES * p:LANES * (p + 1)].astype(BF16)
        qa_ref[:, 2 * LANES * p + LANES:2 * LANES * (p + 1)] = (
            bias[:, LANES * p:LANES * (p + 1)].astype(BF16))
        kp = k[:, LANES * p:LANES * (p + 1)]
        for e in range(2):
            hh = 2 * p + e
            keep = first_half if e == 0 else jnp.logical_not(first_half)
            ka_ref[:, 2 * LANES * hh:2 * LANES * hh + LANES] = jnp.where(keep, kp, 0.0).astype(BF16)
            ka_ref[:, 2 * LANES * hh + LANES:2 * LANES * (hh + 1)] = (
                bias[:, kbase + LANES * hh:kbase + LANES * (hh + 1)].astype(BF16))


def _proj_prompt(x2, nw, w, bft, pp, cc, seg, *, seq):
    rows = x2.shape[0]
    tm = min(PROJ_TM, seq)
    ta = min(ATTN_T, seq)
    assert seq % tm == 0 and rows % seq == 0 and tm % ta == 0 and tm % LANES == 0
    n_batch = rows // seq
    row_blk = lambda n: pl.BlockSpec((tm, n), lambda i: (i, 0))
    const = lambda shape: pl.BlockSpec(shape, lambda i: (0, 0))
    out_shape = (
        jax.ShapeDtypeStruct((rows, POOL_W), F32),
        jax.ShapeDtypeStruct((rows, POOL_W), BF16),
        jax.ShapeDtypeStruct((rows, N_PAIRS * 2 * LANES), BF16),
        jax.ShapeDtypeStruct((rows, N_HEADS * 2 * LANES), BF16),
        jax.ShapeDtypeStruct((n_batch, ATTN_W, seq), F32),
        jax.ShapeDtypeStruct((n_batch, ATTN_W, seq), F32),
        jax.ShapeDtypeStruct((rows, ATTN_W), BF16),
        jax.ShapeDtypeStruct((rows, ATTN_W), BF16),
        jax.ShapeDtypeStruct((n_batch, N_HEADS, seq), F32),
        jax.ShapeDtypeStruct((rows // ta * 8, LANES), F32),
    )
    tpb = seq // tm
    t_minor = lambda n: pl.BlockSpec((1, n, tm), lambda i: (i // tpb, 0, i % tpb))
    out_specs = (row_blk(POOL_W), row_blk(POOL_W), row_blk(N_PAIRS * 2 * LANES),
                 row_blk(N_HEADS * 2 * LANES), t_minor(ATTN_W), t_minor(ATTN_W), row_blk(ATTN_W),
                 row_blk(ATTN_W), t_minor(N_HEADS),
                 pl.BlockSpec((tm // ta * 8, LANES), lambda i: (i, 0)))
    return pl.pallas_call(
        functools.partial(_proj_prompt_kernel, tm=tm, ta=ta, tiles_per_batch=seq // tm),
        grid=(rows // tm,),
        in_specs=[row_blk(D_MODEL), const((1, D_MODEL)), const((D_MODEL, W_COLS)), const((1, LANES)),
                  const((LANES, BIAS_COLS)), const((1, BIAS_COLS)), const((ATTN_W, LANES))],
        out_specs=out_specs,
        out_shape=out_shape,
        scratch_shapes=[pltpu.VMEM((8, LANES), F32)],
        compiler_params=pltpu.CompilerParams(dimension_semantics=("arbitrary",),
                                             vmem_limit_bytes=VMEM_LIMIT),
        name="proj_prompt",
    )(x2, nw, w, bft, pp, cc, seg)


def _lane_tile(x, reps):
    return jnp.concatenate([x] * reps, axis=1)


def _plan_kernel(st_ref, o_ref, *, n_batch, n_blk):
    row = lax.broadcasted_iota(jnp.int32, (LANES, LANES), 0).astype(F32)
    col = lax.broadcasted_iota(jnp.int32, (LANES, LANES), 1).astype(F32)
    out = jnp.zeros((LANES, LANES), F32)
    for b in range(n_batch):
        def stat(r):
            v = st_ref[pl.ds((b * n_blk) * 8 + r, n_blk, stride=8), :]
            return jnp.concatenate([v, jnp.zeros((LANES - n_blk, LANES), F32)], axis=0)
        qm, km, ft0, fs1 = stat(0), stat(1), stat(2), stat(3)
        km_t, fs1_t = jnp.transpose(km), jnp.transpose(fs1)
        for p in range(N_PAIRS):
            first = row
            for h in (2 * p, 2 * p + 1):
                qcol = qm[:, h:h + 1]
                ub = (NORM_SLACK * (qcol * km_t[h:h + 1, :] + qcol * km[:, h:h + 1])
                      + ft0[:, h:h + 1] - fs1_t[h:h + 1, :])
                needed = jnp.logical_and(ub >= -SKIP_LOG2, col < row)
                first = jnp.minimum(first, jnp.where(needed, col, row))
            start = jnp.min(first, axis=1, keepdims=True)
            out = jnp.where(col == b * N_PAIRS + p, start, out)
    o_ref[...] = jnp.transpose(out).astype(jnp.int32)


def _plan(stats, *, n_batch, n_blk):
    assert n_blk <= LANES and n_batch * N_PAIRS <= LANES
    return pl.pallas_call(
        functools.partial(_plan_kernel, n_batch=n_batch, n_blk=n_blk),
        out_shape=jax.ShapeDtypeStruct((LANES, LANES), jnp.int32),
        name="attn_plan",
    )(stats)


def _attn_prompt_kernel(plan_ref, qa_ref, ka_ref, vb_ref, o_ref, m_ref, l_ref, acc_ref, *, t, nq):
    b, pr, i = pl.program_id(0), pl.program_id(1), pl.program_id(2)
    q = qa_ref[...]
    m_ref[...] = jnp.full(m_ref.shape, NEG, F32)
    l_ref[...] = jnp.zeros(l_ref.shape, F32)
    acc_ref[...] = jnp.zeros(acc_ref.shape, F32)
    reps = t // LANES

    def block(kstart, diagonal):
        vblk = vb_ref[pl.ds(kstart, t), :]
        for e in range(2):
            kblk = ka_ref[pl.ds(kstart, t), 2 * LANES * e:2 * LANES * (e + 1)]
            s = lax.dot_general(q, kblk, (((1,), (1,)), ((), ())), preferred_element_type=F32)
            if diagonal:
                r = lax.broadcasted_iota(jnp.int32, (t, t), 0)
                c = lax.broadcasted_iota(jnp.int32, (t, t), 1)
                s = jnp.where(c <= r, s, NEG)
            m_prev = m_ref[e]
            m_next = jnp.maximum(m_prev, jnp.max(s, axis=1, keepdims=True))
            alpha = jnp.exp2(m_prev - m_next)
            p = jnp.exp2(s - _lane_tile(m_next, reps))
            l_ref[e] = alpha * l_ref[e] + jnp.sum(p, axis=1, keepdims=True)
            m_ref[e] = m_next
            acc_ref[e] = alpha * acc_ref[e] + jnp.dot(p.astype(BF16), vblk,
                                                      preferred_element_type=F32)

    def body(j, carry):
        block(pl.multiple_of(j * t, t), False)
        return carry

    lax.fori_loop(plan_ref[(b * N_PAIRS + pr) * nq + i], i, body, 0)
    block(pl.multiple_of(i * t, t), True)

    lane = lax.broadcasted_iota(jnp.int32, (t, LANES), 1)
    o = jnp.where(lane < HEAD_DIM, acc_ref[0] / l_ref[0], acc_ref[1] / l_ref[1])
    o_ref[...] = o.astype(o_ref.dtype)


def _attn_prompt(plan, qa, ka, vb, *, n_batch, seq):
    t = min(ATTN_T, seq)
    assert seq % t == 0
    nq = seq // t
    grid_spec = pltpu.PrefetchScalarGridSpec(
        num_scalar_prefetch=1,
        grid=(n_batch, N_PAIRS, nq),
        in_specs=[pl.BlockSpec((t, 2 * LANES), lambda b, p, i, plan: (b * nq + i, p)),
                  pl.BlockSpec((seq, 4 * LANES), lambda b, p, i, plan: (b, p)),
                  pl.BlockSpec((seq, LANES), lambda b, p, i, plan: (b, p))],
        out_specs=pl.BlockSpec((t, LANES), lambda b, p, i, plan: (b * nq + i, p)),
        scratch_shapes=[pltpu.VMEM((2, t, LANES), F32), pltpu.VMEM((2, t, LANES), F32),
                        pltpu.VMEM((2, t, LANES), F32)])
    return pl.pallas_call(
        functools.partial(_attn_prompt_kernel, t=t, nq=nq),
        grid_spec=grid_spec,
        out_shape=jax.ShapeDtypeStruct((n_batch * seq, ATTN_W), BF16),
        compiler_params=pltpu.CompilerParams(
            dimension_semantics=("arbitrary", "arbitrary", "arbitrary"),
            vmem_limit_bytes=VMEM_LIMIT),
        name="attn_prompt",
    )(plan, qa, ka, vb)


def _merge(x, pooled, sgp, attn, sga, pw_ref, ps, wo_ref, nf):
    mixed = [jnp.dot(pooled[:, POOL_GC * g:POOL_GC * (g + 1)].astype(BF16), pw_ref[g],
                     preferred_element_type=F32) for g in range(len(POOL_WINDOWS))]
    pool_out = jnp.concatenate(mixed, axis=1) * ps
    mix = jnp.concatenate([(pool_out * sgp.astype(F32)).astype(BF16),
                           (attn.astype(F32) * sga.astype(F32)).astype(BF16)], axis=1)
    xo = x + jnp.dot(mix, wo_ref[...], preferred_element_type=F32)
    return _rmsnorm(xo, nf)


def _merge_prompt_kernel(x_ref, u_ref, uh_ref, sgp_ref, attn_ref, sga_ref, pw_ref, ps_ref, wo_ref,
                         nf_ref, y_ref, pool_ref, *, tm, tiles_per_batch):
    i = pl.program_id(0)
    ti = i % tiles_per_batch
    u = u_ref[...]
    halo = jnp.where(ti == 0, 0.0, uh_ref[...])
    ext = jnp.concatenate([halo, u], axis=0)
    pos = ti * tm + lax.broadcasted_iota(jnp.int32, (tm, 1), 0)
    groups = []
    level = ext
    shift = 1
    for g, w in enumerate(POOL_WINDOWS):
        while shift < w:
            level = level + pltpu.roll(level, shift, axis=0)
            shift *= 2
        cnt = jnp.minimum(w, pos + 1).astype(F32)
        sl = slice(POOL_GC * g, POOL_GC * (g + 1))
        groups.append(level[16:, sl] / cnt - u[:, sl])
    pooled = jnp.concatenate(groups, axis=1)
    y_ref[...] = _merge(x_ref[...], pooled, sgp_ref[...], attn_ref[...], sga_ref[...], pw_ref,
                        ps_ref[...], wo_ref, nf_ref[...])

    @pl.when(ti == tiles_per_batch - 1)
    def _():
        pool_ref[0] = u_ref[pl.ds(tm - POOL_HIST, POOL_HIST), :]


def _merge_prompt(x2, u, sgp, attn, sga, pw, ps, wo, nf, *, n_batch, seq):
    rows = x2.shape[0]
    tm = min(PROJ_TM, seq)
    tpb = seq // tm
    halo_blocks = tm // 16
    row_blk = lambda n: pl.BlockSpec((tm, n), lambda i: (i, 0))
    const2 = lambda shape: pl.BlockSpec(shape, lambda i: (0, 0))
    return pl.pallas_call(
        functools.partial(_merge_prompt_kernel, tm=tm, tiles_per_batch=tpb),
        grid=(rows // tm,),
        in_specs=[row_blk(D_MODEL), row_blk(POOL_W),
                  pl.BlockSpec((16, POOL_W), lambda i: (jnp.maximum(i * halo_blocks - 1, 0), 0)),
                  row_blk(POOL_W), row_blk(ATTN_W), row_blk(ATTN_W),
                  pl.BlockSpec((len(POOL_WINDOWS), POOL_GC, POOL_GC), lambda i: (0, 0, 0)),
                  const2((1, POOL_W)), const2((D_MODEL, D_MODEL)), const2((1, D_MODEL))],
        out_specs=(row_blk(D_MODEL),
                   pl.BlockSpec((1, POOL_HIST, POOL_W), lambda i: (i // tpb, 0, 0))),
        out_shape=(jax.ShapeDtypeStruct((rows, D_MODEL), F32),
                   jax.ShapeDtypeStruct((n_batch, POOL_HIST, POOL_W), F32)),
        compiler_params=pltpu.CompilerParams(dimension_semantics=("arbitrary",),
                                             vmem_limit_bytes=VMEM_LIMIT),
        name="merge_prompt",
    )(x2, u, u, sgp, attn, sga, pw, ps, wo, nf)


def _proj_sample_kernel(x_ref, nw_ref, w_ref, bf_ref,
                        u_ref, sgp_ref, q_ref, k_ref, v_ref, sga_ref, logf_ref, cn_ref, carry_ref):
    i = pl.program_id(0)
    h = _rmsnorm(x_ref[...], nw_ref[...]).astype(BF16)
    u, gp, q, k, v, ga, fl = _project(h, w_ref)
    u_ref[...] = u
    sgp_ref[...] = _silu(gp).astype(BF16)
    q_ref[...] = q * Q_SCALE
    k_ref[...] = k
    v_ref[...] = v
    sga_ref[...] = _silu(ga).astype(BF16)
    logf = _log_sigmoid(fl + bf_ref[...])
    logf_ref[...] = logf
    @pl.when(i == 0)
    def _():
        carry_ref[...] = jnp.zeros(carry_ref.shape, F32)

    cn = carry_ref[...] + logf
    carry_ref[...] = cn
    cn_ref[...] = cn


def _proj_sample(xs2, nw, w, bft, *, n_new):
    nb = xs2.shape[0]
    blk = lambda n: pl.BlockSpec((nb, n), lambda i: (0, i))
    const = lambda shape: pl.BlockSpec(shape, lambda i: (0, 0))
    widths = (POOL_W, POOL_W, ATTN_W, ATTN_W, ATTN_W, ATTN_W, LANES, LANES)
    dtypes = (F32, BF16, F32, F32, F32, BF16, F32, F32)
    return pl.pallas_call(
        _proj_sample_kernel,
        grid=(n_new,),
        in_specs=[blk(D_MODEL), const((1, D_MODEL)), const((D_MODEL, W_COLS)), const((1, LANES))],
        out_specs=tuple(blk(n) for n in widths),
        out_shape=tuple(jax.ShapeDtypeStruct((nb, n_new * n), dt) for n, dt in zip(widths, dtypes)),
        scratch_shapes=[pltpu.VMEM((nb, LANES), F32)],
        compiler_params=pltpu.CompilerParams(dimension_semantics=("arbitrary",),
                                             vmem_limit_bytes=VMEM_LIMIT),
        name="proj_sample",
    )(xs2, nw, w, bft)


def _decode_kernel(pt_ref, q_ref, kn_ref, vn_ref, cn_ref, ck_hbm, cv_hbm, clf_hbm, o_ref,
                   kbuf, vbuf, lfbuf, ksem, lfsem, vsem,
                   qbd_ref, cn8_ref, cncol_ref, m_ref, l_ref, acc_ref, tail_ref,
                   *, nb, n_new, n_pages, page, pps):
    n_chunks = n_pages // pps
    total = nb * n_chunks
    n_keys = pps * page
    rows = n_new * N_HEADS
    row_w = lax.broadcasted_iota(jnp.int32, (rows, ATTN_W), 0)
    lane_w = lax.broadcasted_iota(jnp.int32, (rows, ATTN_W), 1)
    head_lanes = (row_w % N_HEADS) == (lane_w // HEAD_DIM)
    row1 = lax.broadcasted_iota(jnp.int32, (rows, 1), 0)

    def page_index(g, r):
        b, c = lax.div(g, n_chunks), lax.rem(g, n_chunks)
        return pt_ref[b * n_pages + n_pages - 1 - (c * pps + r)]

    def k_copies(g, slot):
        cps = []
        for r in range(pps):
            idx = page_index(g, r)
            cps.append(pltpu.make_async_copy(ck_hbm.at[idx], kbuf.at[slot, r], ksem.at[slot]))
            cps.append(pltpu.make_async_copy(clf_hbm.at[idx], lfbuf.at[slot, r], lfsem.at[slot]))
        return cps

    def v_copies(g):
        return [pltpu.make_async_copy(cv_hbm.at[page_index(g, r)], vbuf.at[r], vsem.at[0])
                for r in range(pps)]

    def init(b):
        q = q_ref[b]
        qrep = jnp.concatenate(
            [jnp.broadcast_to(q[i:i + 1, :], (N_HEADS, ATTN_W)) for i in range(n_new)], axis=0)
        qbd = jnp.where(head_lanes, qrep, 0.0)
        qbd_ref[...] = qbd.astype(BF16)
        cn8_ref[...] = jnp.zeros(cn8_ref.shape, F32)
        cn8_ref[0:n_new, :] = cn_ref[b]
        cnt = jnp.transpose(cn8_ref[...])[0:rows, :] * LOG2E
        cncol = jnp.zeros((rows, 1), F32)
        for i in range(n_new):
            cncol = jnp.where(row1 // N_HEADS == i, cnt[:, i:i + 1], cncol)
        cncol_ref[...] = jnp.broadcast_to(cncol, cncol_ref.shape)
        kn = kn_ref[b]
        vn = vn_ref[b]
        s_new = []
        for j in range(n_new):
            sj = jnp.sum(qbd * kn[j:j + 1, :], axis=1, keepdims=True) + cncol - cnt[:, j:j + 1]
            s_new.append(jnp.where(row1 // N_HEADS >= j, sj, NEG))
        m0 = functools.reduce(jnp.maximum, s_new)
        l0 = jnp.zeros((rows, 1), F32)
        a0 = jnp.zeros((rows, ATTN_W), F32)
        for j in range(n_new):
            pj = jnp.exp2(s_new[j] - m0)
            l0 = l0 + pj
            a0 = a0 + pj * vn[j:j + 1, :]
        m_ref[...] = jnp.broadcast_to(m0, m_ref.shape)
        l_ref[...] = jnp.broadcast_to(l0, l_ref.shape)
        acc_ref[...] = a0
        tail_ref[...] = jnp.zeros(tail_ref.shape, F32)

    def finish(b):
        o = acc_ref[...] / _lane_tile(l_ref[...], ATTN_W // LANES)
        o = jnp.where(head_lanes, o, 0.0)
        for i in range(n_new):
            o_ref[b, i:i + 1, :] = jnp.sum(o[N_HEADS * i:N_HEADS * (i + 1), :], axis=0,
                                           keepdims=True)

    lane8 = lax.broadcasted_iota(jnp.int32, (N_HEADS, page), 1)

    def chunk(g, carry):
        slot = lax.rem(g, 2)
        b, c = lax.div(g, n_chunks), lax.rem(g, n_chunks)

        @pl.when(g + 1 < total)
        def _():
            for cp in k_copies(g + 1, 1 - slot):
                cp.start()

        @pl.when(c == 0)
        def _():
            init(b)
            for cp in v_copies(g):
                cp.wait()

        for cp in k_copies(g, slot):
            cp.wait()

        kcat = jnp.concatenate([kbuf[slot, r].reshape(ATTN_W, page).astype(BF16)
                                for r in range(pps)], axis=1)
        qk = jnp.dot(qbd_ref[...], kcat, preferred_element_type=F32)
        tail = tail_ref[...]
        lfs, tails = [], []
        for r in range(pps):
            lf = lfbuf[slot, r] * LOG2E
            lfs.append(lf)
            tails.append(tail)
            tail = tail + jnp.sum(lf, axis=1, keepdims=True)
        tail_ref[...] = tail
        base = qk + jnp.concatenate([jnp.concatenate(tails, axis=1)] * n_new, axis=0) + cncol_ref[:, 0:1]
        m_prev = m_ref[...]
        gap = jnp.max(base - _lane_tile(m_prev, n_keys // LANES))

        @pl.when(gap >= -SKIP_LOG2)
        def _():
            @pl.when(c > 0)
            def _():
                cps = v_copies(g)
                for cp in cps:
                    cp.start()
                for cp in cps:
                    cp.wait()

            g_parts = []
            for r in range(pps):
                incl = lfs[r]
                d = 1
                while d < page:
                    incl = incl + jnp.where(lane8 + d < page, pltpu.roll(incl, page - d, axis=1), 0.0)
                    d *= 2
                g_parts.append(incl - lfs[r])
            s = base + jnp.concatenate([jnp.concatenate(g_parts, axis=1)] * n_new, axis=0)
            m_next = jnp.maximum(m_prev, jnp.max(s, axis=1, keepdims=True))
            alpha = jnp.exp2(m_prev - m_next)
            p = jnp.exp2(s - _lane_tile(m_next, n_keys // LANES))
            l_ref[...] = alpha * l_ref[...] + jnp.sum(p, axis=1, keepdims=True)
            m_ref[...] = m_next
            vcat = jnp.concatenate([vbuf[r].reshape(ATTN_W, page).astype(BF16)
                                    for r in range(pps)], axis=1)
            acc_ref[...] = (_lane_tile(alpha, ATTN_W // LANES) * acc_ref[...]
                            + lax.dot_general(p.astype(BF16), vcat, (((1,), (1,)), ((), ())),
                                              preferred_element_type=F32))

        @pl.when(c == n_chunks - 1)
        def _():
            finish(b)

            @pl.when(b + 1 < nb)
            def _():
                for cp in v_copies(g + 1):
                    cp.start()

        return carry

    for cp in k_copies(0, 0):
        cp.start()
    for cp in v_copies(0):
        cp.start()
    lax.fori_loop(0, total, chunk, 0)


def _decode(page_table, q3, kn3, vn3, cn3, ck, cv, clf_t):
    nb, n_new, _ = q3.shape
    n_pages = page_table.shape[1]
    page = ck.shape[3]
    pps = DEC_PAGES
    while n_pages % pps:
        pps //= 2
    rows = n_new * N_HEADS
    assert page % LANES == 0 and rows % 8 == 0

    whole = lambda n: pl.BlockSpec((nb, n_new, n), lambda i, pt: (0, 0, 0))
    hbm = pl.BlockSpec(memory_space=pl.ANY)
    grid_spec = pltpu.PrefetchScalarGridSpec(
        num_scalar_prefetch=1,
        grid=(1,),
        in_specs=[whole(ATTN_W), whole(ATTN_W), whole(ATTN_W), whole(LANES), hbm, hbm, hbm],
        out_specs=whole(ATTN_W),
        scratch_shapes=[pltpu.VMEM((2, pps, N_HEADS, HEAD_DIM, page), F32),
                        pltpu.VMEM((pps, N_HEADS, HEAD_DIM, page), F32),
                        pltpu.VMEM((2, pps, N_HEADS, page), F32),
                        pltpu.SemaphoreType.DMA((2,)),
                        pltpu.SemaphoreType.DMA((2,)),
                        pltpu.SemaphoreType.DMA((1,)),
                        pltpu.VMEM((rows, ATTN_W), BF16),
                        pltpu.VMEM((8, LANES), F32),
                        pltpu.VMEM((rows, LANES), F32),
                        pltpu.VMEM((rows, LANES), F32),
                        pltpu.VMEM((rows, LANES), F32),
                        pltpu.VMEM((rows, ATTN_W), F32),
                        pltpu.VMEM((N_HEADS, LANES), F32)])
    return pl.pallas_call(
        functools.partial(_decode_kernel, nb=nb, n_new=n_new, n_pages=n_pages, page=page, pps=pps),
        grid_spec=grid_spec,
        out_shape=jax.ShapeDtypeStruct((nb, n_new, ATTN_W), F32),
        compiler_params=pltpu.CompilerParams(dimension_semantics=("arbitrary",),
                                             vmem_limit_bytes=VMEM_LIMIT),
        name="decode_attn",
    )(page_table.reshape(-1), q3, kn3, vn3, cn3, ck, cv, clf_t)


def _merge_sample_kernel(x_ref, u_ref, sp_ref, sgp_ref, attn_ref, sga_ref, pw_ref, ps_ref, wo_ref,
                         nf_ref, y_ref, pool_ref, *, n_new, past):
    u = [u_ref[:, POOL_W * i:POOL_W * (i + 1)] for i in range(n_new)]
    hist = [sp_ref[r] for r in range(POOL_HIST)]
    ext = hist + u
    pooled_rows = []
    for i in range(n_new):
        groups = []
        for g, w in enumerate(POOL_WINDOWS):
            sl = slice(POOL_GC * g, POOL_GC * (g + 1))
            end = POOL_HIST + i
            total = ext[end][:, sl]
            for r in range(end - w + 1, end):
                total = total + ext[r][:, sl]
            groups.append(total / float(min(w, past + i + 1)) - u[i][:, sl])
        pooled_rows.append(jnp.concatenate(groups, axis=1))
    pooled = jnp.concatenate(pooled_rows, axis=0)
    cat = lambda ref, n: jnp.concatenate([ref[:, n * i:n * (i + 1)] for i in range(n_new)], axis=0)
    y = _merge(cat(x_ref, D_MODEL), pooled, cat(sgp_ref, POOL_W), cat(attn_ref, ATTN_W),
               cat(sga_ref, ATTN_W), pw_ref, ps_ref[...], wo_ref, nf_ref[...])
    nb = x_ref.shape[0]
    for i in range(n_new):
        y_ref[:, D_MODEL * i:D_MODEL * (i + 1)] = y[nb * i:nb * (i + 1), :]
    new_hist = ext[-POOL_HIST:]
    for r in range(POOL_HIST):
        pool_ref[:, POOL_W * r:POOL_W * (r + 1)] = new_hist[r]


def _merge_sample(xs2, u2, sp2, sgp2, attn2, sga2, pw, ps, wo, nf, *, n_new, past):
    nb = xs2.shape[0]
    return pl.pallas_call(
        functools.partial(_merge_sample_kernel, n_new=n_new, past=past),
        out_shape=(jax.ShapeDtypeStruct((nb, n_new * D_MODEL), F32),
                   jax.ShapeDtypeStruct((nb, POOL_HIST * POOL_W), F32)),
        compiler_params=pltpu.CompilerParams(vmem_limit_bytes=VMEM_LIMIT),
        name="merge_sample",
    )(xs2, u2, sp2, sgp2, attn2, sga2, pw, ps, wo, nf)


def kernel(x_prompt, x_sample, cache_k, cache_v, cache_logf, state_pool, page_table,
           norm_w, w_in, b_forget, pool_w, pool_scale, w_out, norm_f):
    depth = norm_w.shape[0]
    assert depth == 1, "a single layer is supported"
    b_p, seq, _ = x_prompt.shape
    b_s, t_s, _ = x_sample.shape
    n_phys, page = cache_k.shape[1], cache_k.shape[2]
    past = page_table.shape[1] * page
    ll = 0

    n_main = 2 * POOL_W + 4 * ATTN_W
    w = jnp.concatenate([w_in[ll][:, :n_main], jnp.tile(w_in[ll][:, n_main:], (1, LANES // N_HEADS))],
                        axis=1).astype(BF16)
    bft = jnp.tile(b_forget[ll], LANES // N_HEADS).reshape(1, LANES).astype(F32)
    nw = norm_w[ll].reshape(1, D_MODEL)
    nf = norm_f.reshape(1, D_MODEL)
    pw = pool_w[ll].astype(BF16)
    ps = pool_scale[ll].reshape(1, POOL_W)
    wo = w_out[ll].astype(BF16)
    pp_np, cc_np = _placement()
    pp = jnp.asarray(pp_np, BF16)
    cc = jnp.asarray(cc_np, F32)

    xp2 = x_prompt.reshape(b_p * seq, D_MODEL)
    seg_np = (np.arange(ATTN_W)[:, None] // HEAD_DIM == np.arange(LANES)[None, :] % N_HEADS)
    seg = jnp.asarray(seg_np, BF16)
    u, sgp, qa, ka, kf, vf, vb, sga, logf, stats = _proj_prompt(xp2, nw, w, bft, pp, cc, seg, seq=seq)
    n_blk = seq // min(ATTN_T, seq)
    plan = _plan(stats, n_batch=b_p, n_blk=n_blk)[:b_p * N_PAIRS, :n_blk].reshape(-1)
    attn = _attn_prompt(plan, qa, ka, vb, n_batch=b_p, seq=seq)
    yp, pool_p = _merge_prompt(xp2, u, sgp, attn, sga, pw, ps, wo, nf, n_batch=b_p, seq=seq)

    xs2 = x_sample.reshape(b_s, t_s * D_MODEL)
    us, sgps, qs, ks, vs, sgas, lfs, cns = _proj_sample(xs2, nw, w, bft, n_new=t_s)
    ck = jnp.transpose(cache_k[ll], (0, 2, 3, 1))
    cv = jnp.transpose(cache_v[ll], (0, 2, 3, 1))
    clf_t = jnp.swapaxes(cache_logf[ll], 1, 2)
    attn_s = _decode(page_table, qs.reshape(b_s, t_s, ATTN_W), ks.reshape(b_s, t_s, ATTN_W),
                     vs.reshape(b_s, t_s, ATTN_W), cns.reshape(b_s, t_s, LANES), ck, cv, clf_t)
    ys, pool_s = _merge_sample(xs2, us, jnp.swapaxes(state_pool[ll], 0, 1), sgps,
                               attn_s.reshape(b_s, t_s * ATTN_W), sgas, pw, ps, wo, nf,
                               n_new=t_s, past=past)

    return (yp.reshape(b_p, seq, D_MODEL),
            ys.reshape(b_s, t_s, D_MODEL),
            jnp.transpose(kf.reshape(1, b_p, N_HEADS, HEAD_DIM, seq), (0, 1, 4, 2, 3)),
            jnp.transpose(vf.reshape(1, b_p, N_HEADS, HEAD_DIM, seq), (0, 1, 4, 2, 3)),
            jnp.transpose(logf, (0, 2, 1)).reshape(1, b_p, seq, N_HEADS),
            pool_p.reshape(1, b_p, POOL_HIST, POOL_W),
            ks.reshape(1, b_s, t_s, N_HEADS, HEAD_DIM),
            vs.reshape(1, b_s, t_s, N_HEADS, HEAD_DIM),
            lfs.reshape(b_s, t_s, LANES)[:, :, :N_HEADS].reshape(1, b_s, t_s, N_HEADS),
            pool_s.reshape(1, b_s, POOL_HIST, POOL_W))
```

```python
import functools
import math

import numpy as np
import jax
import jax.numpy as jnp
from jax import lax
from jax.experimental import pallas as pl
from jax.experimental.pallas import tpu as pltpu

D_MODEL = 1024
POOL_W = 512
ATTN_W = 512
N_HEADS = 8
HEAD_DIM = 64
POOL_WINDOWS = (2, 4, 8, 16)
POOL_GC = 128
POOL_HIST = 15
EPS = 1e-6

LANES = 128
LOG2E = 1.4426950408889634
Q_SCALE = LOG2E / math.sqrt(HEAD_DIM)
NEG = -1e30
N_PAIRS = N_HEADS // 2
W_COLS = 2 * POOL_W + 4 * ATTN_W + LANES
BIAS_COLS = N_PAIRS * LANES + N_HEADS * LANES
VMEM_LIMIT = 56 * 1024 * 1024

SKIP_LOG2 = 152.0
NORM_SLACK = 1.02

PROJ_TM = 512
ATTN_T = 512
DEC_PAGES = 16
DEC_VSUB = 8

BF16 = jnp.bfloat16
F32 = jnp.float32


def _placement():
    pp = np.zeros((LANES, BIAS_COLS), np.float32)
    cc = np.zeros((1, BIAS_COLS), np.float32)
    kbase = N_PAIRS * LANES
    for h in range(N_HEADS):
        p, e = divmod(h, 2)
        for i in range(3):
            pp[8 * i + h, LANES * p + 6 * e + i] = 1.0
            pp[8 * i + h, kbase + LANES * h + 6 * e + 3 + i] = -1.0
            cc[0, LANES * p + 6 * e + 3 + i] = 1.0
            cc[0, kbase + LANES * h + 6 * e + i] = 1.0
    return pp, cc


def _log_sigmoid(x):
    return jnp.minimum(x, 0.0) - jnp.log(1.0 + jnp.exp(-jnp.abs(x)))


def _silu(x):
    return x * jax.nn.sigmoid(x)


def _rmsnorm(xf, w):
    ms = jnp.mean(xf * xf, axis=-1, keepdims=True)
    return xf * lax.rsqrt(ms + EPS) * w


def _project(h, w_ref):
    def cols(c0, n):
        return jnp.dot(h, w_ref[:, c0:c0 + n], preferred_element_type=F32)
    u = cols(0, POOL_W)
    gp = cols(POOL_W, POOL_W)
    q = cols(2 * POOL_W, ATTN_W)
    k = cols(2 * POOL_W + ATTN_W, ATTN_W)
    v = cols(2 * POOL_W + 2 * ATTN_W, ATTN_W)
    ga = cols(2 * POOL_W + 3 * ATTN_W, ATTN_W)
    fl = cols(2 * POOL_W + 4 * ATTN_W, LANES)
    return u, gp, q, k, v, ga, fl


def _proj_prompt_kernel(x_ref, nw_ref, w_ref, bf_ref, pp_ref, cc_ref, seg_ref,
                        u_ref, sgp_ref, qa_ref, ka_ref, k_ref, v_ref, vb_ref, sga_ref, logf_ref,
                        st_ref, carry_ref, *, tm, ta, tiles_per_batch):
    i = pl.program_id(0)
    h = _rmsnorm(x_ref[...], nw_ref[...]).astype(BF16)
    u, gp, q, k, v, ga, fl = _project(h, w_ref)
    u_ref[...] = u
    sgp_ref[...] = _silu(gp).astype(BF16)
    k_ref[0] = jnp.transpose(k)
    v_ref[0] = jnp.transpose(v)
    vb_ref[...] = v.astype(BF16)
    sga_ref[...] = _silu(ga).astype(BF16)
    logf = _log_sigmoid(fl + bf_ref[...])
    logf_ref[0] = jnp.transpose(logf)[0:N_HEADS, :]

    row = lax.broadcasted_iota(jnp.int32, (tm, LANES), 0)
    acc = logf
    d = 1
    while d < tm:
        acc = acc + jnp.where(row >= d, pltpu.roll(acc, d, axis=0), 0.0)
        d *= 2
    @pl.when(i % tiles_per_batch == 0)
    def _():
        carry_ref[...] = jnp.zeros(carry_ref.shape, F32)

    f_run = acc + carry_ref[0:1, :]
    carry_ref[...] = jnp.broadcast_to(f_run[tm - 1:tm, :], carry_ref.shape)

    f2 = f_run * LOG2E
    p1 = f2.astype(BF16).astype(F32)
    r1 = f2 - p1
    p2 = r1.astype(BF16).astype(F32)
    p3 = (r1 - p2).astype(BF16).astype(F32)
    lane = lax.broadcasted_iota(jnp.int32, (tm, LANES), 1)
    pieces = jnp.where(lane < 8, p1, jnp.where(lane < 16, p2, jnp.where(lane < 24, p3, 0.0)))
    bias = jnp.dot(pieces.astype(BF16), pp_ref[...], preferred_element_type=F32) + cc_ref[...]

    qs = q * Q_SCALE

    qn2 = jnp.dot((qs * qs).astype(BF16), seg_ref[...], preferred_element_type=F32)
    kn2 = jnp.dot((k * k).astype(BF16), seg_ref[...], preferred_element_type=F32)
    srow = lax.broadcasted_iota(jnp.int32, (8, LANES), 0)
    for sb in range(tm // ta):
        r0, r1 = sb * ta, (sb + 1) * ta
        qmax = jnp.sqrt(jnp.max(qn2[r0:r1], axis=0, keepdims=True))
        kmax = jnp.sqrt(jnp.max(kn2[r0:r1], axis=0, keepdims=True))
        st_ref[8 * sb:8 * (sb + 1), :] = jnp.where(
            srow == 0, qmax, jnp.where(srow == 1, kmax, jnp.where(
                srow == 2, f2[r0:r0 + 1, :], jnp.where(srow == 3, f2[r1 - 1:r1, :], 0.0))))

    first_half = lane < HEAD_DIM
    kbase = N_PAIRS * LANES
    for p in range(N_PAIRS):
        qa_ref[:, 2 * LANES * p:2 * LANES * p + LANES] = qs[:, LANES * p:LANES * (p + 1)].astype(BF16)
        qa_ref[:, 2 * LANES * p + LANES:2 * LANES * (p + 1)] = (
            bias[:, LANES * p:LANES * (p + 1)].astype(BF16))
        kp = k[:, LANES * p:LANES * (p + 1)]
        for e in range(2):
            hh = 2 * p + e
            keep = first_half if e == 0 else jnp.logical_not(first_half)
            ka_ref[:, 2 * LANES * hh:2 * LANES * hh + LANES] = jnp.where(keep, kp, 0.0).astype(BF16)
            ka_ref[:, 2 * LANES * hh + LANES:2 * LANES * (hh + 1)] = (
                bias[:, kbase + LANES * hh:kbase + LANES * (hh + 1)].astype(BF16))


def _proj_prompt(x2, nw, w, bft, pp, cc, seg, *, seq):
    rows = x2.shape[0]
    tm = min(PROJ_TM, seq)
    ta = min(ATTN_T, seq)
    assert seq % tm == 0 and rows % seq == 0 and tm % ta == 0 and tm % LANES == 0
    n_batch = rows // seq
    row_blk = lambda n: pl.BlockSpec((tm, n), lambda i: (i, 0))
    const = lambda shape: pl.BlockSpec(shape, lambda i: (0, 0))
    out_shape = (
        jax.ShapeDtypeStruct((rows, POOL_W), F32),
        jax.ShapeDtypeStruct((rows, POOL_W), BF16),
        jax.ShapeDtypeStruct((rows, N_PAIRS * 2 * LANES), BF16),
        jax.ShapeDtypeStruct((rows, N_HEADS * 2 * LANES), BF16),
        jax.ShapeDtypeStruct((n_batch, ATTN_W, seq), F32),
        jax.ShapeDtypeStruct((n_batch, ATTN_W, seq), F32),
        jax.ShapeDtypeStruct((rows, ATTN_W), BF16),
        jax.ShapeDtypeStruct((rows, ATTN_W), BF16),
        jax.ShapeDtypeStruct((n_batch, N_HEADS, seq), F32),
        jax.ShapeDtypeStruct((rows // ta * 8, LANES), F32),
    )
    tpb = seq // tm
    t_minor = lambda n: pl.BlockSpec((1, n, tm), lambda i: (i // tpb, 0, i % tpb))
    out_specs = (row_blk(POOL_W), row_blk(POOL_W), row_blk(N_PAIRS * 2 * LANES),
                 row_blk(N_HEADS * 2 * LANES), t_minor(ATTN_W), t_minor(ATTN_W), row_blk(ATTN_W),
                 row_blk(ATTN_W), t_minor(N_HEADS),
                 pl.BlockSpec((tm // ta * 8, LANES), lambda i: (i, 0)))
    return pl.pallas_call(
        functools.partial(_proj_prompt_kernel, tm=tm, ta=ta, tiles_per_batch=seq // tm),
        grid=(rows // tm,),
        in_specs=[row_blk(D_MODEL), const((1, D_MODEL)), const((D_MODEL, W_COLS)), const((1, LANES)),
                  const((LANES, BIAS_COLS)), const((1, BIAS_COLS)), const((ATTN_W, LANES))],
        out_specs=out_specs,
        out_shape=out_shape,
        scratch_shapes=[pltpu.VMEM((8, LANES), F32)],
        compiler_params=pltpu.CompilerParams(dimension_semantics=("arbitrary",),
                                             vmem_limit_bytes=VMEM_LIMIT),
        name="proj_prompt",
    )(x2, nw, w, bft, pp, cc, seg)


def _lane_tile(x, reps):
    return jnp.concatenate([x] * reps, axis=1)


def _plan_kernel(st_ref, o_ref, *, n_batch, n_blk):
    row = lax.broadcasted_iota(jnp.int32, (LANES, LANES), 0).astype(F32)
    col = lax.broadcasted_iota(jnp.int32, (LANES, LANES), 1).astype(F32)
    out = jnp.zeros((LANES, LANES), F32)
    for b in range(n_batch):
        def stat(r):
            v = st_ref[pl.ds((b * n_blk) * 8 + r, n_blk, stride=8), :]
            return jnp.concatenate([v, jnp.zeros((LANES - n_blk, LANES), F32)], axis=0)
        qm, km, ft0, fs1 = stat(0), stat(1), stat(2), stat(3)
        km_t, fs1_t = jnp.transpose(km), jnp.transpose(fs1)
        for p in range(N_PAIRS):
            first = row
            for h in (2 * p, 2 * p + 1):
                qcol = qm[:, h:h + 1]
                ub = (NORM_SLACK * (qcol * km_t[h:h + 1, :] + qcol * km[:, h:h + 1])
                      + ft0[:, h:h + 1] - fs1_t[h:h + 1, :])
                needed = jnp.logical_and(ub >= -SKIP_LOG2, col < row)
                first = jnp.minimum(first, jnp.where(needed, col, row))
            start = jnp.min(first, axis=1, keepdims=True)
            out = jnp.where(col == b * N_PAIRS + p, start, out)
    o_ref[...] = jnp.transpose(out).astype(jnp.int32)


def _plan(stats, *, n_batch, n_blk):
    assert n_blk <= LANES and n_batch * N_PAIRS <= LANES
    return pl.pallas_call(
        functools.partial(_plan_kernel, n_batch=n_batch, n_blk=n_blk),
        out_shape=jax.ShapeDtypeStruct((LANES, LANES), jnp.int32),
        name="attn_plan",
    )(stats)


def _attn_prompt_kernel(plan_ref, qa_ref, ka_ref, vb_ref, o_ref, m_ref, l_ref, acc_ref, *, t, nq):
    b, pr, i = pl.program_id(0), pl.program_id(1), pl.program_id(2)
    q = qa_ref[...]
    m_ref[...] = jnp.full(m_ref.shape, NEG, F32)
    l_ref[...] = jnp.zeros(l_ref.shape, F32)
    acc_ref[...] = jnp.zeros(acc_ref.shape, F32)

    def block(kstart, width, diag_col):
        vblk = vb_ref[pl.ds(kstart, width), :]
        for e in range(2):
            kblk = ka_ref[pl.ds(kstart, width), 2 * LANES * e:2 * LANES * (e + 1)]
            s = lax.dot_general(q, kblk, (((1,), (1,)), ((), ())), preferred_element_type=F32)
            if diag_col is not None:
                r = lax.broadcasted_iota(jnp.int32, (t, width), 0)
                c = lax.broadcasted_iota(jnp.int32, (t, width), 1)
                s = jnp.where(c - diag_col <= r, s, NEG)
            m_prev = m_ref[e]
            m_next = jnp.maximum(m_prev, jnp.max(s, axis=1, keepdims=True))
            alpha = jnp.exp2(m_prev - m_next)
            p = jnp.exp2(s - _lane_tile(m_next, width // LANES))
            l_ref[e] = alpha * l_ref[e] + jnp.sum(p, axis=1, keepdims=True)
            m_ref[e] = m_next
            acc_ref[e] = alpha * acc_ref[e] + jnp.dot(p.astype(BF16), vblk,
                                                      preferred_element_type=F32)

    def body(j, carry):
        block(pl.multiple_of(j * t, t), t, None)
        return carry

    first = plan_ref[(b * N_PAIRS + pr) * nq + i]
    one_before = jnp.logical_and(i > 0, first == i - 1)

    @pl.when(one_before)
    def _():
        block(pl.multiple_of((i - 1) * t, t), 2 * t, t)

    @pl.when(jnp.logical_not(one_before))
    def _():
        lax.fori_loop(first, i, body, 0)
        block(pl.multiple_of(i * t, t), t, 0)

    lane = lax.broadcasted_iota(jnp.int32, (t, LANES), 1)
    o = jnp.where(lane < HEAD_DIM, acc_ref[0] / l_ref[0], acc_ref[1] / l_ref[1])
    o_ref[...] = o.astype(o_ref.dtype)


def _attn_prompt(plan, qa, ka, vb, *, n_batch, seq):
    t = min(ATTN_T, seq)
    assert seq % t == 0
    nq = seq // t
    grid_spec = pltpu.PrefetchScalarGridSpec(
        num_scalar_prefetch=1,
        grid=(n_batch, N_PAIRS, nq),
        in_specs=[pl.BlockSpec((t, 2 * LANES), lambda b, p, i, plan: (b * nq + i, p)),
                  pl.BlockSpec((seq, 4 * LANES), lambda b, p, i, plan: (b, p)),
                  pl.BlockSpec((seq, LANES), lambda b, p, i, plan: (b, p))],
        out_specs=pl.BlockSpec((t, LANES), lambda b, p, i, plan: (b * nq + i, p)),
        scratch_shapes=[pltpu.VMEM((2, t, LANES), F32), pltpu.VMEM((2, t, LANES), F32),
                        pltpu.VMEM((2, t, LANES), F32)])
    return pl.pallas_call(
        functools.partial(_attn_prompt_kernel, t=t, nq=nq),
        grid_spec=grid_spec,
        out_shape=jax.ShapeDtypeStruct((n_batch * seq, ATTN_W), BF16),
        compiler_params=pltpu.CompilerParams(
            dimension_semantics=("arbitrary", "arbitrary", "arbitrary"),
            vmem_limit_bytes=VMEM_LIMIT),
        name="attn_prompt",
    )(plan, qa, ka, vb)


def _merge(x, pooled, sgp, attn, sga, pw_ref, ps, wo_ref, nf):
    mixed = [jnp.dot(pooled[:, POOL_GC * g:POOL_GC * (g + 1)].astype(BF16), pw_ref[g],
                     preferred_element_type=F32) for g in range(len(POOL_WINDOWS))]
    pool_out = jnp.concatenate(mixed, axis=1) * ps
    mix = jnp.concatenate([(pool_out * sgp.astype(F32)).astype(BF16),
                           (attn.astype(F32) * sga.astype(F32)).astype(BF16)], axis=1)
    xo = x + jnp.dot(mix, wo_ref[...], preferred_element_type=F32)
    return _rmsnorm(xo, nf)


def _merge_prompt_kernel(x_ref, u_ref, uh_ref, sgp_ref, attn_ref, sga_ref, pw_ref, ps_ref, wo_ref,
                         nf_ref, y_ref, pool_ref, *, tm, tiles_per_batch):
    i = pl.program_id(0)
    ti = i % tiles_per_batch
    u = u_ref[...]
    halo = jnp.where(ti == 0, 0.0, uh_ref[...])
    ext = jnp.concatenate([halo, u], axis=0)
    pos = ti * tm + lax.broadcasted_iota(jnp.int32, (tm, 1), 0)
    groups = []
    level = ext
    shift = 1
    for g, w in enumerate(POOL_WINDOWS):
        while shift < w:
            level = level + pltpu.roll(level, shift, axis=0)
            shift *= 2
        cnt = jnp.minimum(w, pos + 1).astype(F32)
        sl = slice(POOL_GC * g, POOL_GC * (g + 1))
        groups.append(level[16:, sl] / cnt - u[:, sl])
    pooled = jnp.concatenate(groups, axis=1)
    y_ref[...] = _merge(x_ref[...], pooled, sgp_ref[...], attn_ref[...], sga_ref[...], pw_ref,
                        ps_ref[...], wo_ref, nf_ref[...])

    @pl.when(ti == tiles_per_batch - 1)
    def _():
        pool_ref[0] = u_ref[pl.ds(tm - POOL_HIST, POOL_HIST), :]


def _merge_prompt(x2, u, sgp, attn, sga, pw, ps, wo, nf, *, n_batch, seq):
    rows = x2.shape[0]
    tm = min(PROJ_TM, seq)
    tpb = seq // tm
    halo_blocks = tm // 16
    row_blk = lambda n: pl.BlockSpec((tm, n), lambda i: (i, 0))
    const2 = lambda shape: pl.BlockSpec(shape, lambda i: (0, 0))
    return pl.pallas_call(
        functools.partial(_merge_prompt_kernel, tm=tm, tiles_per_batch=tpb),
        grid=(rows // tm,),
        in_specs=[row_blk(D_MODEL), row_blk(POOL_W),
                  pl.BlockSpec((16, POOL_W), lambda i: (jnp.maximum(i * halo_blocks - 1, 0), 0)),
                  row_blk(POOL_W), row_blk(ATTN_W), row_blk(ATTN_W),
                  pl.BlockSpec((len(POOL_WINDOWS), POOL_GC, POOL_GC), lambda i: (0, 0, 0)),
                  const2((1, POOL_W)), const2((D_MODEL, D_MODEL)), const2((1, D_MODEL))],
        out_specs=(row_blk(D_MODEL),
                   pl.BlockSpec((1, POOL_HIST, POOL_W), lambda i: (i // tpb, 0, 0))),
        out_shape=(jax.ShapeDtypeStruct((rows, D_MODEL), F32),
                   jax.ShapeDtypeStruct((n_batch, POOL_HIST, POOL_W), F32)),
        compiler_params=pltpu.CompilerParams(dimension_semantics=("arbitrary",),
                                             vmem_limit_bytes=VMEM_LIMIT),
        name="merge_prompt",
    )(x2, u, u, sgp, attn, sga, pw, ps, wo, nf)


def _proj_sample_kernel(x_ref, nw_ref, w_ref, bf_ref,
                        u_ref, sgp_ref, q_ref, k_ref, v_ref, sga_ref, logf_ref, cn_ref, carry_ref):
    i = pl.program_id(0)
    h = _rmsnorm(x_ref[...], nw_ref[...]).astype(BF16)
    u, gp, q, k, v, ga, fl = _project(h, w_ref)
    u_ref[...] = u
    sgp_ref[...] = _silu(gp).astype(BF16)
    q_ref[...] = q * Q_SCALE
    k_ref[...] = k
    v_ref[...] = v
    sga_ref[...] = _silu(ga).astype(BF16)
    logf = _log_sigmoid(fl + bf_ref[...])
    logf_ref[...] = logf
    @pl.when(i == 0)
    def _():
        carry_ref[...] = jnp.zeros(carry_ref.shape, F32)

    cn = carry_ref[...] + logf
    carry_ref[...] = cn
    cn_ref[...] = cn


def _proj_sample(xs2, nw, w, bft, *, n_new):
    nb = xs2.shape[0]
    blk = lambda n: pl.BlockSpec((nb, n), lambda i: (0, i))
    const = lambda shape: pl.BlockSpec(shape, lambda i: (0, 0))
    widths = (POOL_W, POOL_W, ATTN_W, ATTN_W, ATTN_W, ATTN_W, LANES, LANES)
    dtypes = (F32, BF16, F32, F32, F32, BF16, F32, F32)
    return pl.pallas_call(
        _proj_sample_kernel,
        grid=(n_new,),
        in_specs=[blk(D_MODEL), const((1, D_MODEL)), const((D_MODEL, W_COLS)), const((1, LANES))],
        out_specs=tuple(blk(n) for n in widths),
        out_shape=tuple(jax.ShapeDtypeStruct((nb, n_new * n), dt) for n, dt in zip(widths, dtypes)),
        scratch_shapes=[pltpu.VMEM((nb, LANES), F32)],
        compiler_params=pltpu.CompilerParams(dimension_semantics=("arbitrary",),
                                             vmem_limit_bytes=VMEM_LIMIT),
        name="proj_sample",
    )(xs2, nw, w, bft)


def _decode_kernel(pt_ref, q_ref, kn_ref, vn_ref, cn_ref, ck_hbm, cv_hbm, clf_hbm, o_ref,
                   kbuf, vpre, vdem, lfbuf, ksem, lfsem, vsem,
                   qbd_ref, cn8_ref, cncol_ref, m_ref, l_ref, acc_ref, tail_ref,
                   *, nb, n_new, n_pages, page, pps, vsub):
    n_chunks = n_pages // pps
    n_groups = pps // vsub
    total = nb * n_chunks
    n_keys = pps * page
    g_keys = vsub * page
    rows = n_new * N_HEADS
    row_w = lax.broadcasted_iota(jnp.int32, (rows, ATTN_W), 0)
    lane_w = lax.broadcasted_iota(jnp.int32, (rows, ATTN_W), 1)
    head_lanes = (row_w % N_HEADS) == (lane_w // HEAD_DIM)
    row1 = lax.broadcasted_iota(jnp.int32, (rows, 1), 0)

    def page_index(g, r):
        b, c = lax.div(g, n_chunks), lax.rem(g, n_chunks)
        return pt_ref[b * n_pages + n_pages - 1 - (c * pps + r)]

    def k_copies(g, slot):
        cps = []
        for r in range(pps):
            idx = page_index(g, r)
            cps.append(pltpu.make_async_copy(ck_hbm.at[idx], kbuf.at[slot, r], ksem.at[slot]))
            cps.append(pltpu.make_async_copy(clf_hbm.at[idx], lfbuf.at[slot, r], lfsem.at[slot]))
        return cps

    def v_copies(g, grp, dst, sem):
        return [pltpu.make_async_copy(cv_hbm.at[page_index(g, grp * vsub + r)], dst.at[r], sem)
                for r in range(vsub)]

    def init(b):
        q = q_ref[b]
        qrep = jnp.concatenate(
            [jnp.broadcast_to(q[i:i + 1, :], (N_HEADS, ATTN_W)) for i in range(n_new)], axis=0)
        qbd = jnp.where(head_lanes, qrep, 0.0)
        qbd_ref[...] = qbd.astype(BF16)
        cn8_ref[...] = jnp.zeros(cn8_ref.shape, F32)
        cn8_ref[0:n_new, :] = cn_ref[b]
        cnt = jnp.transpose(cn8_ref[...])[0:rows, :] * LOG2E
        cncol = jnp.zeros((rows, 1), F32)
        for i in range(n_new):
            cncol = jnp.where(row1 // N_HEADS == i, cnt[:, i:i + 1], cncol)
        cncol_ref[...] = jnp.broadcast_to(cncol, cncol_ref.shape)
        kn = kn_ref[b]
        vn = vn_ref[b]
        s_new = []
        for j in range(n_new):
            sj = jnp.sum(qbd * kn[j:j + 1, :], axis=1, keepdims=True) + cncol - cnt[:, j:j + 1]
            s_new.append(jnp.where(row1 // N_HEADS >= j, sj, NEG))
        m0 = functools.reduce(jnp.maximum, s_new)
        l0 = jnp.zeros((rows, 1), F32)
        a0 = jnp.zeros((rows, ATTN_W), F32)
        for j in range(n_new):
            pj = jnp.exp2(s_new[j] - m0)
            l0 = l0 + pj
            a0 = a0 + pj * vn[j:j + 1, :]
        m_ref[...] = jnp.broadcast_to(m0, m_ref.shape)
        l_ref[...] = jnp.broadcast_to(l0, l_ref.shape)
        acc_ref[...] = a0
        tail_ref[...] = jnp.zeros(tail_ref.shape, F32)

    def finish(b):
        o = acc_ref[...] / _lane_tile(l_ref[...], ATTN_W // LANES)
        o = jnp.where(head_lanes, o, 0.0)
        for i in range(n_new):
            o_ref[b, i:i + 1, :] = jnp.sum(o[N_HEADS * i:N_HEADS * (i + 1), :], axis=0,
                                           keepdims=True)

    lane8 = lax.broadcasted_iota(jnp.int32, (N_HEADS, page), 1)

    def chunk(g, carry):
        slot = lax.rem(g, 2)
        b, c = lax.div(g, n_chunks), lax.rem(g, n_chunks)

        @pl.when(g + 1 < total)
        def _():
            for cp in k_copies(g + 1, 1 - slot):
                cp.start()

        @pl.when(c == 0)
        def _():
            init(b)
            for cp in v_copies(g, 0, vpre, vsem.at[0]):
                cp.wait()

        for cp in k_copies(g, slot):
            cp.wait()

        kcat = jnp.concatenate([kbuf[slot, r].reshape(ATTN_W, page).astype(BF16)
                                for r in range(pps)], axis=1)
        qk = jnp.dot(qbd_ref[...], kcat, preferred_element_type=F32)
        tail = tail_ref[...]
        lfs, tails = [], []
        for r in range(pps):
            lf = lfbuf[slot, r] * LOG2E
            lfs.append(lf)
            tails.append(tail)
            tail = tail + jnp.sum(lf, axis=1, keepdims=True)
        tail_ref[...] = tail
        base = qk + jnp.concatenate([jnp.concatenate(tails, axis=1)] * n_new, axis=0) + cncol_ref[:, 0:1]
        m_prev = m_ref[...]
        over = base - _lane_tile(m_prev, n_keys // LANES)
        gaps = [jnp.max(over[:, g_keys * h:g_keys * (h + 1)]) for h in range(n_groups)]
        live = [gp >= -SKIP_LOG2 for gp in gaps]

        @pl.when(functools.reduce(jnp.logical_or, live))
        def _():
            g_parts = []
            for r in range(pps):
                incl = lfs[r]
                d = 1
                while d < page:
                    incl = incl + jnp.where(lane8 + d < page, pltpu.roll(incl, page - d, axis=1), 0.0)
                    d *= 2
                g_parts.append(incl - lfs[r])
            s = base + jnp.concatenate([jnp.concatenate(g_parts, axis=1)] * n_new, axis=0)
            m_next = jnp.maximum(m_prev, jnp.max(s, axis=1, keepdims=True))
            alpha = jnp.exp2(m_prev - m_next)
            p = jnp.exp2(s - _lane_tile(m_next, n_keys // LANES))
            l_ref[...] = alpha * l_ref[...] + jnp.sum(p, axis=1, keepdims=True)
            m_ref[...] = m_next
            acc_ref[...] = _lane_tile(alpha, ATTN_W // LANES) * acc_ref[...]
            pb = p.astype(BF16)

            def add_pv(h, vsrc):
                vcat = jnp.concatenate([vsrc[r].reshape(ATTN_W, page).astype(BF16)
                                        for r in range(vsub)], axis=1)
                acc_ref[...] += lax.dot_general(pb[:, g_keys * h:g_keys * (h + 1)], vcat,
                                                (((1,), (1,)), ((), ())),
                                                preferred_element_type=F32)

            def fetch_and_add(h):
                cps = v_copies(g, h, vdem, vsem.at[1])
                for cp in cps:
                    cp.start()
                for cp in cps:
                    cp.wait()
                add_pv(h, vdem)

            for h in range(n_groups):
                if h == 0:
                    @pl.when(jnp.logical_and(live[0], c == 0))
                    def _():
                        add_pv(0, vpre)

                    @pl.when(jnp.logical_and(live[0], c > 0))
                    def _():
                        fetch_and_add(0)
                else:
                    @pl.when(live[h])
                    def _(h=h):
                        fetch_and_add(h)

        @pl.when(c == n_chunks - 1)
        def _():
            finish(b)

        @pl.when(jnp.logical_and(c == 0, b + 1 < nb))
        def _():
            for cp in v_copies(g + n_chunks, 0, vpre, vsem.at[0]):
                cp.start()

        return carry

    for cp in k_copies(0, 0):
        cp.start()
    for cp in v_copies(0, 0, vpre, vsem.at[0]):
        cp.start()
    lax.fori_loop(0, total, chunk, 0)


def _decode(page_table, q3, kn3, vn3, cn3, ck, cv, clf_t):
    nb, n_new, _ = q3.shape
    n_pages = page_table.shape[1]
    page = ck.shape[3]
    pps = DEC_PAGES
    while n_pages % pps:
        pps //= 2
    vsub = min(DEC_VSUB, pps)
    rows = n_new * N_HEADS
    assert page % LANES == 0 and rows % 8 == 0 and pps % vsub == 0

    whole = lambda n: pl.BlockSpec((nb, n_new, n), lambda i, pt: (0, 0, 0))
    hbm = pl.BlockSpec(memory_space=pl.ANY)
    grid_spec = pltpu.PrefetchScalarGridSpec(
        num_scalar_prefetch=1,
        grid=(1,),
        in_specs=[whole(ATTN_W), whole(ATTN_W), whole(ATTN_W), whole(LANES), hbm, hbm, hbm],
        out_specs=whole(ATTN_W),
        scratch_shapes=[pltpu.VMEM((2, pps, N_HEADS, HEAD_DIM, page), F32),
                        pltpu.VMEM((vsub, N_HEADS, HEAD_DIM, page), F32),
                        pltpu.VMEM((vsub, N_HEADS, HEAD_DIM, page), F32),
                        pltpu.VMEM((2, pps, N_HEADS, page), F32),
                        pltpu.SemaphoreType.DMA((2,)),
                        pltpu.SemaphoreType.DMA((2,)),
                        pltpu.SemaphoreType.DMA((2,)),
                        pltpu.VMEM((rows, ATTN_W), BF16),
                        pltpu.VMEM((8, LANES), F32),
                        pltpu.VMEM((rows, LANES), F32),
                        pltpu.VMEM((rows, LANES), F32),
                        pltpu.VMEM((rows, LANES), F32),
                        pltpu.VMEM((rows, ATTN_W), F32),
                        pltpu.VMEM((N_HEADS, LANES), F32)])
    return pl.pallas_call(
        functools.partial(_decode_kernel, nb=nb, n_new=n_new, n_pages=n_pages, page=page, pps=pps,
                          vsub=vsub),
        grid_spec=grid_spec,
        out_shape=jax.ShapeDtypeStruct((nb, n_new, ATTN_W), F32),
        compiler_params=pltpu.CompilerParams(dimension_semantics=("arbitrary",),
                                             vmem_limit_bytes=VMEM_LIMIT),
        name="decode_attn",
    )(page_table.reshape(-1), q3, kn3, vn3, cn3, ck, cv, clf_t)


def _merge_sample_kernel(x_ref, u_ref, sp_ref, sgp_ref, attn_ref, sga_ref, pw_ref, ps_ref, wo_ref,
                         nf_ref, y_ref, pool_ref, *, n_new, past):
    u = [u_ref[:, POOL_W * i:POOL_W * (i + 1)] for i in range(n_new)]
    hist = [sp_ref[r] for r in range(POOL_HIST)]
    ext = hist + u
    pooled_rows = []
    for i in range(n_new):
        groups = []
        for g, w in enumerate(POOL_WINDOWS):
            sl = slice(POOL_GC * g, POOL_GC * (g + 1))
            end = POOL_HIST + i
            total = ext[end][:, sl]
            for r in range(end - w + 1, end):
                total = total + ext[r][:, sl]
            groups.append(total / float(min(w, past + i + 1)) - u[i][:, sl])
        pooled_rows.append(jnp.concatenate(groups, axis=1))
    pooled = jnp.concatenate(pooled_rows, axis=0)
    cat = lambda ref, n: jnp.concatenate([ref[:, n * i:n * (i + 1)] for i in range(n_new)], axis=0)
    y = _merge(cat(x_ref, D_MODEL), pooled, cat(sgp_ref, POOL_W), cat(attn_ref, ATTN_W),
               cat(sga_ref, ATTN_W), pw_ref, ps_ref[...], wo_ref, nf_ref[...])
    nb = x_ref.shape[0]
    for i in range(n_new):
        y_ref[:, D_MODEL * i:D_MODEL * (i + 1)] = y[nb * i:nb * (i + 1), :]
    new_hist = ext[-POOL_HIST:]
    for r in range(POOL_HIST):
        pool_ref[:, POOL_W * r:POOL_W * (r + 1)] = new_hist[r]


def _merge_sample(xs2, u2, sp2, sgp2, attn2, sga2, pw, ps, wo, nf, *, n_new, past):
    nb = xs2.shape[0]
    return pl.pallas_call(
        functools.partial(_merge_sample_kernel, n_new=n_new, past=past),
        out_shape=(jax.ShapeDtypeStruct((nb, n_new * D_MODEL), F32),
                   jax.ShapeDtypeStruct((nb, POOL_HIST * POOL_W), F32)),
        compiler_params=pltpu.CompilerParams(vmem_limit_bytes=VMEM_LIMIT),
        name="merge_sample",
    )(xs2, u2, sp2, sgp2, attn2, sga2, pw, ps, wo, nf)


def kernel(x_prompt, x_sample, cache_k, cache_v, cache_logf, state_pool, page_table,
           norm_w, w_in, b_forget, pool_w, pool_scale, w_out, norm_f):
    depth = norm_w.shape[0]
    assert depth == 1, "a single layer is supported"
    b_p, seq, _ = x_prompt.shape
    b_s, t_s, _ = x_sample.shape
    n_phys, page = cache_k.shape[1], cache_k.shape[2]
    past = page_table.shape[1] * page
    ll = 0

    n_main = 2 * POOL_W + 4 * ATTN_W
    w = jnp.concatenate([w_in[ll][:, :n_main], jnp.tile(w_in[ll][:, n_main:], (1, LANES // N_HEADS))],
                        axis=1).astype(BF16)
    bft = jnp.tile(b_forget[ll], LANES // N_HEADS).reshape(1, LANES).astype(F32)
    nw = norm_w[ll].reshape(1, D_MODEL)
    nf = norm_f.reshape(1, D_MODEL)
    pw = pool_w[ll].astype(BF16)
    ps = pool_scale[ll].reshape(1, POOL_W)
    wo = w_out[ll].astype(BF16)
    pp_np, cc_np = _placement()
    pp = jnp.asarray(pp_np, BF16)
    cc = jnp.asarray(cc_np, F32)

    xp2 = x_prompt.reshape(b_p * seq, D_MODEL)
    seg_np = (np.arange(ATTN_W)[:, None] // HEAD_DIM == np.arange(LANES)[None, :] % N_HEADS)
    seg = jnp.asarray(seg_np, BF16)
    u, sgp, qa, ka, kf, vf, vb, sga, logf, stats = _proj_prompt(xp2, nw, w, bft, pp, cc, seg, seq=seq)
    n_blk = seq // min(ATTN_T, seq)
    plan = _plan(stats, n_batch=b_p, n_blk=n_blk)[:b_p * N_PAIRS, :n_blk].reshape(-1)
    attn = _attn_prompt(plan, qa, ka, vb, n_batch=b_p, seq=seq)
    yp, pool_p = _merge_prompt(xp2, u, sgp, attn, sga, pw, ps, wo, nf, n_batch=b_p, seq=seq)

    xs2 = x_sample.reshape(b_s, t_s * D_MODEL)
    us, sgps, qs, ks, vs, sgas, lfs, cns = _proj_sample(xs2, nw, w, bft, n_new=t_s)
    ck = jnp.transpose(cache_k[ll], (0, 2, 3, 1))
    cv = jnp.transpose(cache_v[ll], (0, 2, 3, 1))
    clf_t = jnp.swapaxes(cache_logf[ll], 1, 2)
    attn_s = _decode(page_table, qs.reshape(b_s, t_s, ATTN_W), ks.reshape(b_s, t_s, ATTN_W),
                     vs.reshape(b_s, t_s, ATTN_W), cns.reshape(b_s, t_s, LANES), ck, cv, clf_t)
    ys, pool_s = _merge_sample(xs2, us, jnp.swapaxes(state_pool[ll], 0, 1), sgps,
                               attn_s.reshape(b_s, t_s * ATTN_W), sgas, pw, ps, wo, nf,
                               n_new=t_s, past=past)

    return (yp.reshape(b_p, seq, D_MODEL),
            ys.reshape(b_s, t_s, D_MODEL),
            jnp.transpose(kf.reshape(1, b_p, N_HEADS, HEAD_DIM, seq), (0, 1, 4, 2, 3)),
            jnp.transpose(vf.reshape(1, b_p, N_HEADS, HEAD_DIM, seq), (0, 1, 4, 2, 3)),
            jnp.transpose(logf, (0, 2, 1)).reshape(1, b_p, seq, N_HEADS),
            pool_p.reshape(1, b_p, POOL_HIST, POOL_W),
            ks.reshape(1, b_s, t_s, N_HEADS, HEAD_DIM),
            vs.reshape(1, b_s, t_s, N_HEADS, HEAD_DIM),
            lfs.reshape(b_s, t_s, LANES)[:, :, :N_HEADS].reshape(1, b_s, t_s, N_HEADS),
            pool_s.reshape(1, b_s, POOL_HIST, POOL_W))
```

```python
import functools
import math

import numpy as np
import jax
import jax.numpy as jnp
from jax import lax
from jax.experimental import pallas as pl
from jax.experimental.pallas import tpu as pltpu

D_MODEL = 1024
POOL_W = 512
ATTN_W = 512
N_HEADS = 8
HEAD_DIM = 64
POOL_WINDOWS = (2, 4, 8, 16)
POOL_GC = 128
POOL_HIST = 15
EPS = 1e-6

LANES = 128
LOG2E = 1.4426950408889634
Q_SCALE = LOG2E / math.sqrt(HEAD_DIM)
NEG = -1e30
N_PAIRS = N_HEADS // 2
W_COLS = 2 * POOL_W + 4 * ATTN_W + LANES
BIAS_COLS = N_PAIRS * LANES + N_HEADS * LANES
VMEM_LIMIT = 56 * 1024 * 1024

SKIP_LOG2 = 152.0
NORM_SLACK = 1.02

PROJ_TM = 512
ATTN_T = 512
DEC_PAGES = 16
DEC_VSUB = 8

BF16 = jnp.bfloat16
F32 = jnp.float32


def _placement():
    pp = np.zeros((LANES, BIAS_COLS), np.float32)
    cc = np.zeros((1, BIAS_COLS), np.float32)
    kbase = N_PAIRS * LANES
    for h in range(N_HEADS):
        p, e = divmod(h, 2)
        for i in range(3):
            pp[8 * i + h, LANES * p + 6 * e + i] = 1.0
            pp[8 * i + h, kbase + LANES * h + 6 * e + 3 + i] = -1.0
            cc[0, LANES * p + 6 * e + 3 + i] = 1.0
            cc[0, kbase + LANES * h + 6 * e + i] = 1.0
    return pp, cc


def _log_sigmoid(x):
    return jnp.minimum(x, 0.0) - jnp.log(1.0 + jnp.exp(-jnp.abs(x)))


def _silu(x):
    return x * jax.nn.sigmoid(x)


def _rmsnorm(xf, w):
    ms = jnp.mean(xf * xf, axis=-1, keepdims=True)
    return xf * lax.rsqrt(ms + EPS) * w


def _project(h, w_ref):
    def cols(c0, n):
        return jnp.dot(h, w_ref[:, c0:c0 + n], preferred_element_type=F32)
    u = cols(0, POOL_W)
    gp = cols(POOL_W, POOL_W)
    q = cols(2 * POOL_W, ATTN_W)
    k = cols(2 * POOL_W + ATTN_W, ATTN_W)
    v = cols(2 * POOL_W + 2 * ATTN_W, ATTN_W)
    ga = cols(2 * POOL_W + 3 * ATTN_W, ATTN_W)
    fl = cols(2 * POOL_W + 4 * ATTN_W, LANES)
    return u, gp, q, k, v, ga, fl


def _proj_prompt_kernel(x_ref, nw_ref, w_ref, bf_ref, pp_ref, cc_ref, seg_ref,
                        u_ref, sgp_ref, qa_ref, ka_ref, k_ref, v_ref, vb_ref, sga_ref, logf_ref,
                        st_ref, carry_ref, *, tm, ta, tiles_per_batch):
    i = pl.program_id(0)
    h = _rmsnorm(x_ref[...], nw_ref[...]).astype(BF16)
    u, gp, q, k, v, ga, fl = _project(h, w_ref)
    u_ref[...] = u
    sgp_ref[...] = _silu(gp).astype(BF16)
    k_ref[0] = jnp.transpose(k)
    v_ref[0] = jnp.transpose(v)
    vb_ref[...] = v.astype(BF16)
    sga_ref[...] = _silu(ga).astype(BF16)
    logf = _log_sigmoid(fl + bf_ref[...])
    logf_ref[0] = jnp.transpose(logf)[0:N_HEADS, :]

    row = lax.broadcasted_iota(jnp.int32, (tm, LANES), 0)
    acc = logf
    d = 1
    while d < tm:
        acc = acc + jnp.where(row >= d, pltpu.roll(acc, d, axis=0), 0.0)
        d *= 2
    @pl.when(i % tiles_per_batch == 0)
    def _():
        carry_ref[...] = jnp.zeros(carry_ref.shape, F32)

    f_run = acc + carry_ref[0:1, :]
    carry_ref[...] = jnp.broadcast_to(f_run[tm - 1:tm, :], carry_ref.shape)

    f2 = f_run * LOG2E
    p1 = f2.astype(BF16).astype(F32)
    r1 = f2 - p1
    p2 = r1.astype(BF16).astype(F32)
    p3 = (r1 - p2).astype(BF16).astype(F32)
    lane = lax.broadcasted_iota(jnp.int32, (tm, LANES), 1)
    pieces = jnp.where(lane < 8, p1, jnp.where(lane < 16, p2, jnp.where(lane < 24, p3, 0.0)))
    bias = jnp.dot(pieces.astype(BF16), pp_ref[...], preferred_element_type=F32) + cc_ref[...]

    qs = q * Q_SCALE

    qn2 = jnp.dot((qs * qs).astype(BF16), seg_ref[...], preferred_element_type=F32)
    kn2 = jnp.dot((k * k).astype(BF16), seg_ref[...], preferred_element_type=F32)
    srow = lax.broadcasted_iota(jnp.int32, (8, LANES), 0)
    for sb in range(tm // ta):
        r0, r1 = sb * ta, (sb + 1) * ta
        qmax = jnp.sqrt(jnp.max(qn2[r0:r1], axis=0, keepdims=True))
        kmax = jnp.sqrt(jnp.max(kn2[r0:r1], axis=0, keepdims=True))
        st_ref[8 * sb:8 * (sb + 1), :] = jnp.where(
            srow == 0, qmax, jnp.where(srow == 1, kmax, jnp.where(
                srow == 2, f2[r0:r0 + 1, :], jnp.where(srow == 3, f2[r1 - 1:r1, :], 0.0))))

    first_half = lane < HEAD_DIM
    kbase = N_PAIRS * LANES
    for p in range(N_PAIRS):
        qa_ref[:, 2 * LANES * p:2 * LANES * p + LANES] = qs[:, LANES * p:LANES * (p + 1)].astype(BF16)
        qa_ref[:, 2 * LANES * p + LANES:2 * LANES * (p + 1)] = (
            bias[:, LANES * p:LANES * (p + 1)].astype(BF16))
        kp = k[:, LANES * p:LANES * (p + 1)]
        for e in range(2):
            hh = 2 * p + e
            keep = first_half if e == 0 else jnp.logical_not(first_half)
            ka_ref[:, 2 * LANES * hh:2 * LANES * hh + LANES] = jnp.where(keep, kp, 0.0).astype(BF16)
            ka_ref[:, 2 * LANES * hh + LANES:2 * LANES * (hh + 1)] = (
                bias[:, kbase + LANES * hh:kbase + LANES * (hh + 1)].astype(BF16))


def _proj_prompt(x2, nw, w, bft, pp, cc, seg, *, seq):
    rows = x2.shape[0]
    tm = min(PROJ_TM, seq)
    ta = min(ATTN_T, seq)
    assert seq % tm == 0 and rows % seq == 0 and tm % ta == 0 and tm % LANES == 0
    n_batch = rows // seq
    row_blk = lambda n: pl.BlockSpec((tm, n), lambda i: (i, 0))
    const = lambda shape: pl.BlockSpec(shape, lambda i: (0, 0))
    out_shape = (
        jax.ShapeDtypeStruct((rows, POOL_W), F32),
        jax.ShapeDtypeStruct((rows, POOL_W), BF16),
        jax.ShapeDtypeStruct((rows, N_PAIRS * 2 * LANES), BF16),
        jax.ShapeDtypeStruct((rows, N_HEADS * 2 * LANES), BF16),
        jax.ShapeDtypeStruct((n_batch, ATTN_W, seq), F32),
        jax.ShapeDtypeStruct((n_batch, ATTN_W, seq), F32),
        jax.ShapeDtypeStruct((rows, ATTN_W), BF16),
        jax.ShapeDtypeStruct((rows, ATTN_W), BF16),
        jax.ShapeDtypeStruct((n_batch, N_HEADS, seq), F32),
        jax.ShapeDtypeStruct((rows // ta * 8, LANES), F32),
    )
    tpb = seq // tm
    t_minor = lambda n: pl.BlockSpec((1, n, tm), lambda i: (i // tpb, 0, i % tpb))
    out_specs = (row_blk(POOL_W), row_blk(POOL_W), row_blk(N_PAIRS * 2 * LANES),
                 row_blk(N_HEADS * 2 * LANES), t_minor(ATTN_W), t_minor(ATTN_W), row_blk(ATTN_W),
                 row_blk(ATTN_W), t_minor(N_HEADS),
                 pl.BlockSpec((tm // ta * 8, LANES), lambda i: (i, 0)))
    return pl.pallas_call(
        functools.partial(_proj_prompt_kernel, tm=tm, ta=ta, tiles_per_batch=seq // tm),
        grid=(rows // tm,),
        in_specs=[row_blk(D_MODEL), const((1, D_MODEL)), const((D_MODEL, W_COLS)), const((1, LANES)),
                  const((LANES, BIAS_COLS)), const((1, BIAS_COLS)), const((ATTN_W, LANES))],
        out_specs=out_specs,
        out_shape=out_shape,
        scratch_shapes=[pltpu.VMEM((8, LANES), F32)],
        compiler_params=pltpu.CompilerParams(dimension_semantics=("arbitrary",),
                                             vmem_limit_bytes=VMEM_LIMIT),
        name="proj_prompt",
    )(x2, nw, w, bft, pp, cc, seg)


def _lane_tile(x, reps):
    return jnp.concatenate([x] * reps, axis=1)


def _plan_kernel(st_ref, o_ref, *, n_batch, n_blk):
    row = lax.broadcasted_iota(jnp.int32, (LANES, LANES), 0).astype(F32)
    col = lax.broadcasted_iota(jnp.int32, (LANES, LANES), 1).astype(F32)
    out = jnp.zeros((LANES, LANES), F32)
    for b in range(n_batch):
        def stat(r):
            v = st_ref[pl.ds((b * n_blk) * 8 + r, n_blk, stride=8), :]
            return jnp.concatenate([v, jnp.zeros((LANES - n_blk, LANES), F32)], axis=0)
        qm, km, ft0, fs1 = stat(0), stat(1), stat(2), stat(3)
        km_t, fs1_t = jnp.transpose(km), jnp.transpose(fs1)
        for p in range(N_PAIRS):
            first = row
            for h in (2 * p, 2 * p + 1):
                qcol = qm[:, h:h + 1]
                ub = (NORM_SLACK * (qcol * km_t[h:h + 1, :] + qcol * km[:, h:h + 1])
                      + ft0[:, h:h + 1] - fs1_t[h:h + 1, :])
                needed = jnp.logical_and(ub >= -SKIP_LOG2, col < row)
                first = jnp.minimum(first, jnp.where(needed, col, row))
            start = jnp.min(first, axis=1, keepdims=True)
            out = jnp.where(col == b * N_PAIRS + p, start, out)
    o_ref[...] = jnp.transpose(out).astype(jnp.int32)


def _plan(stats, *, n_batch, n_blk):
    assert n_blk <= LANES and n_batch * N_PAIRS <= LANES
    return pl.pallas_call(
        functools.partial(_plan_kernel, n_batch=n_batch, n_blk=n_blk),
        out_shape=jax.ShapeDtypeStruct((LANES, LANES), jnp.int32),
        name="attn_plan",
    )(stats)


def _attn_kernel(plan_ref, pt_ref, qa_ref, ka_ref, vb_ref, qd_ref, kn_ref, vn_ref, cn_ref,
                 ck_hbm, cv_hbm, clf_hbm, o_ref, od_ref, m_ref, l_ref, acc_ref, *dec_scratch,
                 t, nq, n_steps, dec):
    b, pr, i = pl.program_id(0), pl.program_id(1), pl.program_id(2)
    step = (b * N_PAIRS + pr) * nq + i
    dec_prologue, dec_chunk, n_chunks = _decode_fns(pt_ref, qd_ref, kn_ref, vn_ref, cn_ref, ck_hbm,
                                                    cv_hbm, clf_hbm, od_ref, *dec_scratch, **dec)
    per_step = -(-n_chunks // n_steps)

    @pl.when(step == 0)
    def _():
        dec_prologue()

    for k in range(per_step):
        g = step * per_step + k

        @pl.when(g < n_chunks)
        def _(g=g):
            dec_chunk(g)

    q = qa_ref[...]
    m_ref[...] = jnp.full(m_ref.shape, NEG, F32)
    l_ref[...] = jnp.zeros(l_ref.shape, F32)
    acc_ref[...] = jnp.zeros(acc_ref.shape, F32)

    def block(kstart, width, diag_col):
        vblk = vb_ref[pl.ds(kstart, width), :]
        for e in range(2):
            kblk = ka_ref[pl.ds(kstart, width), 2 * LANES * e:2 * LANES * (e + 1)]
            s = lax.dot_general(q, kblk, (((1,), (1,)), ((), ())), preferred_element_type=F32)
            if diag_col is not None:
                r = lax.broadcasted_iota(jnp.int32, (t, width), 0)
                c = lax.broadcasted_iota(jnp.int32, (t, width), 1)
                s = jnp.where(c - diag_col <= r, s, NEG)
            m_prev = m_ref[e]
            m_next = jnp.maximum(m_prev, jnp.max(s, axis=1, keepdims=True))
            alpha = jnp.exp2(m_prev - m_next)
            p = jnp.exp2(s - _lane_tile(m_next, width // LANES))
            l_ref[e] = alpha * l_ref[e] + jnp.sum(p, axis=1, keepdims=True)
            m_ref[e] = m_next
            acc_ref[e] = alpha * acc_ref[e] + jnp.dot(p.astype(BF16), vblk,
                                                      preferred_element_type=F32)

    def body(j, carry):
        block(pl.multiple_of(j * t, t), t, None)
        return carry

    first = plan_ref[(b * N_PAIRS + pr) * nq + i]
    one_before = jnp.logical_and(i > 0, first == i - 1)

    @pl.when(one_before)
    def _():
        block(pl.multiple_of((i - 1) * t, t), 2 * t, t)

    @pl.when(jnp.logical_not(one_before))
    def _():
        lax.fori_loop(first, i, body, 0)
        block(pl.multiple_of(i * t, t), t, 0)

    lane = lax.broadcasted_iota(jnp.int32, (t, LANES), 1)
    o = jnp.where(lane < HEAD_DIM, acc_ref[0] / l_ref[0], acc_ref[1] / l_ref[1])
    o_ref[...] = o.astype(o_ref.dtype)


def _attention(plan, qa, ka, vb, page_table, q3, kn3, vn3, cn3, ck, cv, clf_t, *, n_batch, seq):
    t = min(ATTN_T, seq)
    assert seq % t == 0
    nq = seq // t
    nb, n_new, _ = q3.shape
    dec, dec_scratch = _decode_setup(nb, n_new, page_table.shape[1], ck.shape[3])
    whole = lambda n: pl.BlockSpec((nb, n_new, n), lambda b, p, i, plan, pt: (0, 0, 0))
    hbm = pl.BlockSpec(memory_space=pl.ANY)
    grid_spec = pltpu.PrefetchScalarGridSpec(
        num_scalar_prefetch=2,
        grid=(n_batch, N_PAIRS, nq),
        in_specs=[pl.BlockSpec((t, 2 * LANES), lambda b, p, i, plan, pt: (b * nq + i, p)),
                  pl.BlockSpec((seq, 4 * LANES), lambda b, p, i, plan, pt: (b, p)),
                  pl.BlockSpec((seq, LANES), lambda b, p, i, plan, pt: (b, p)),
                  whole(ATTN_W), whole(ATTN_W), whole(ATTN_W), whole(LANES), hbm, hbm, hbm],
        out_specs=(pl.BlockSpec((t, LANES), lambda b, p, i, plan, pt: (b * nq + i, p)),
                   whole(ATTN_W)),
        scratch_shapes=[pltpu.VMEM((2, t, LANES), F32), pltpu.VMEM((2, t, LANES), F32),
                        pltpu.VMEM((2, t, LANES), F32)] + dec_scratch)
    return pl.pallas_call(
        functools.partial(_attn_kernel, t=t, nq=nq, n_steps=n_batch * N_PAIRS * nq, dec=dec),
        grid_spec=grid_spec,
        out_shape=(jax.ShapeDtypeStruct((n_batch * seq, ATTN_W), BF16),
                   jax.ShapeDtypeStruct((nb, n_new, ATTN_W), F32)),
        compiler_params=pltpu.CompilerParams(
            dimension_semantics=("arbitrary", "arbitrary", "arbitrary"),
            vmem_limit_bytes=VMEM_LIMIT),
        name="attention",
    )(plan, page_table.reshape(-1), qa, ka, vb, q3, kn3, vn3, cn3, ck, cv, clf_t)


def _merge(x, pooled, sgp, attn, sga, pw_ref, ps, wo_ref, nf):
    mixed = [jnp.dot(pooled[:, POOL_GC * g:POOL_GC * (g + 1)].astype(BF16), pw_ref[g],
                     preferred_element_type=F32) for g in range(len(POOL_WINDOWS))]
    pool_out = jnp.concatenate(mixed, axis=1) * ps
    mix = jnp.concatenate([(pool_out * sgp.astype(F32)).astype(BF16),
                           (attn.astype(F32) * sga.astype(F32)).astype(BF16)], axis=1)
    xo = x + jnp.dot(mix, wo_ref[...], preferred_element_type=F32)
    return _rmsnorm(xo, nf)


def _merge_prompt_kernel(x_ref, u_ref, uh_ref, sgp_ref, attn_ref, sga_ref, pw_ref, ps_ref, wo_ref,
                         nf_ref, y_ref, pool_ref, *, tm, tiles_per_batch):
    i = pl.program_id(0)
    ti = i % tiles_per_batch
    u = u_ref[...]
    halo = jnp.where(ti == 0, 0.0, uh_ref[...])
    ext = jnp.concatenate([halo, u], axis=0)
    pos = ti * tm + lax.broadcasted_iota(jnp.int32, (tm, 1), 0)
    groups = []
    level = ext
    shift = 1
    for g, w in enumerate(POOL_WINDOWS):
        while shift < w:
            level = level + pltpu.roll(level, shift, axis=0)
            shift *= 2
        cnt = jnp.minimum(w, pos + 1).astype(F32)
        sl = slice(POOL_GC * g, POOL_GC * (g + 1))
        groups.append(level[16:, sl] / cnt - u[:, sl])
    pooled = jnp.concatenate(groups, axis=1)
    y_ref[...] = _merge(x_ref[...], pooled, sgp_ref[...], attn_ref[...], sga_ref[...], pw_ref,
                        ps_ref[...], wo_ref, nf_ref[...])

    @pl.when(ti == tiles_per_batch - 1)
    def _():
        pool_ref[0] = u_ref[pl.ds(tm - POOL_HIST, POOL_HIST), :]


def _merge_prompt(x2, u, sgp, attn, sga, pw, ps, wo, nf, *, n_batch, seq):
    rows = x2.shape[0]
    tm = min(PROJ_TM, seq)
    tpb = seq // tm
    halo_blocks = tm // 16
    row_blk = lambda n: pl.BlockSpec((tm, n), lambda i: (i, 0))
    const2 = lambda shape: pl.BlockSpec(shape, lambda i: (0, 0))
    return pl.pallas_call(
        functools.partial(_merge_prompt_kernel, tm=tm, tiles_per_batch=tpb),
        grid=(rows // tm,),
        in_specs=[row_blk(D_MODEL), row_blk(POOL_W),
                  pl.BlockSpec((16, POOL_W), lambda i: (jnp.maximum(i * halo_blocks - 1, 0), 0)),
                  row_blk(POOL_W), row_blk(ATTN_W), row_blk(ATTN_W),
                  pl.BlockSpec((len(POOL_WINDOWS), POOL_GC, POOL_GC), lambda i: (0, 0, 0)),
                  const2((1, POOL_W)), const2((D_MODEL, D_MODEL)), const2((1, D_MODEL))],
        out_specs=(row_blk(D_MODEL),
                   pl.BlockSpec((1, POOL_HIST, POOL_W), lambda i: (i // tpb, 0, 0))),
        out_shape=(jax.ShapeDtypeStruct((rows, D_MODEL), F32),
                   jax.ShapeDtypeStruct((n_batch, POOL_HIST, POOL_W), F32)),
        compiler_params=pltpu.CompilerParams(dimension_semantics=("arbitrary",),
                                             vmem_limit_bytes=VMEM_LIMIT),
        name="merge_prompt",
    )(x2, u, u, sgp, attn, sga, pw, ps, wo, nf)


def _proj_sample_kernel(x_ref, nw_ref, w_ref, bf_ref,
                        u_ref, sgp_ref, q_ref, k_ref, v_ref, sga_ref, logf_ref, cn_ref, carry_ref):
    i = pl.program_id(0)
    h = _rmsnorm(x_ref[...], nw_ref[...]).astype(BF16)
    u, gp, q, k, v, ga, fl = _project(h, w_ref)
    u_ref[...] = u
    sgp_ref[...] = _silu(gp).astype(BF16)
    q_ref[...] = q * Q_SCALE
    k_ref[...] = k
    v_ref[...] = v
    sga_ref[...] = _silu(ga).astype(BF16)
    logf = _log_sigmoid(fl + bf_ref[...])
    logf_ref[...] = logf
    @pl.when(i == 0)
    def _():
        carry_ref[...] = jnp.zeros(carry_ref.shape, F32)

    cn = carry_ref[...] + logf
    carry_ref[...] = cn
    cn_ref[...] = cn


def _proj_sample(xs2, nw, w, bft, *, n_new):
    nb = xs2.shape[0]
    blk = lambda n: pl.BlockSpec((nb, n), lambda i: (0, i))
    const = lambda shape: pl.BlockSpec(shape, lambda i: (0, 0))
    widths = (POOL_W, POOL_W, ATTN_W, ATTN_W, ATTN_W, ATTN_W, LANES, LANES)
    dtypes = (F32, BF16, F32, F32, F32, BF16, F32, F32)
    return pl.pallas_call(
        _proj_sample_kernel,
        grid=(n_new,),
        in_specs=[blk(D_MODEL), const((1, D_MODEL)), const((D_MODEL, W_COLS)), const((1, LANES))],
        out_specs=tuple(blk(n) for n in widths),
        out_shape=tuple(jax.ShapeDtypeStruct((nb, n_new * n), dt) for n, dt in zip(widths, dtypes)),
        scratch_shapes=[pltpu.VMEM((nb, LANES), F32)],
        compiler_params=pltpu.CompilerParams(dimension_semantics=("arbitrary",),
                                             vmem_limit_bytes=VMEM_LIMIT),
        name="proj_sample",
    )(xs2, nw, w, bft)


def _decode_fns(pt_ref, q_ref, kn_ref, vn_ref, cn_ref, ck_hbm, cv_hbm, clf_hbm, o_ref,
                kbuf, vpre, vdem, lfbuf, ksem, lfsem, vsem,
                qbd_ref, cn8_ref, cncol_ref, m_ref, l_ref, acc_ref, tail_ref,
                *, nb, n_new, n_pages, page, pps, vsub):
    n_chunks = n_pages // pps
    n_groups = pps // vsub
    total = nb * n_chunks
    n_keys = pps * page
    g_keys = vsub * page
    rows = n_new * N_HEADS
    row_w = lax.broadcasted_iota(jnp.int32, (rows, ATTN_W), 0)
    lane_w = lax.broadcasted_iota(jnp.int32, (rows, ATTN_W), 1)
    head_lanes = (row_w % N_HEADS) == (lane_w // HEAD_DIM)
    row1 = lax.broadcasted_iota(jnp.int32, (rows, 1), 0)

    def page_index(g, r):
        b, c = lax.div(g, n_chunks), lax.rem(g, n_chunks)
        return pt_ref[b * n_pages + n_pages - 1 - (c * pps + r)]

    def k_copies(g, slot):
        cps = []
        for r in range(pps):
            idx = page_index(g, r)
            cps.append(pltpu.make_async_copy(ck_hbm.at[idx], kbuf.at[slot, r], ksem.at[slot]))
            cps.append(pltpu.make_async_copy(clf_hbm.at[idx], lfbuf.at[slot, r], lfsem.at[slot]))
        return cps

    def v_copies(g, grp, dst, sem):
        return [pltpu.make_async_copy(cv_hbm.at[page_index(g, grp * vsub + r)], dst.at[r], sem)
                for r in range(vsub)]

    def init(b):
        q = q_ref[b]
        qrep = jnp.concatenate(
            [jnp.broadcast_to(q[i:i + 1, :], (N_HEADS, ATTN_W)) for i in range(n_new)], axis=0)
        qbd = jnp.where(head_lanes, qrep, 0.0)
        qbd_ref[...] = qbd.astype(BF16)
        cn8_ref[...] = jnp.zeros(cn8_ref.shape, F32)
        cn8_ref[0:n_new, :] = cn_ref[b]
        cnt = jnp.transpose(cn8_ref[...])[0:rows, :] * LOG2E
        cncol = jnp.zeros((rows, 1), F32)
        for i in range(n_new):
            cncol = jnp.where(row1 // N_HEADS == i, cnt[:, i:i + 1], cncol)
        cncol_ref[...] = jnp.broadcast_to(cncol, cncol_ref.shape)
        kn = kn_ref[b]
        vn = vn_ref[b]
        s_new = []
        for j in range(n_new):
            sj = jnp.sum(qbd * kn[j:j + 1, :], axis=1, keepdims=True) + cncol - cnt[:, j:j + 1]
            s_new.append(jnp.where(row1 // N_HEADS >= j, sj, NEG))
        m0 = functools.reduce(jnp.maximum, s_new)
        l0 = jnp.zeros((rows, 1), F32)
        a0 = jnp.zeros((rows, ATTN_W), F32)
        for j in range(n_new):
            pj = jnp.exp2(s_new[j] - m0)
            l0 = l0 + pj
            a0 = a0 + pj * vn[j:j + 1, :]
        m_ref[...] = jnp.broadcast_to(m0, m_ref.shape)
        l_ref[...] = jnp.broadcast_to(l0, l_ref.shape)
        acc_ref[...] = a0
        tail_ref[...] = jnp.zeros(tail_ref.shape, F32)

    def finish(b):
        o = acc_ref[...] / _lane_tile(l_ref[...], ATTN_W // LANES)
        o = jnp.where(head_lanes, o, 0.0)
        for i in range(n_new):
            o_ref[b, i:i + 1, :] = jnp.sum(o[N_HEADS * i:N_HEADS * (i + 1), :], axis=0,
                                           keepdims=True)

    lane8 = lax.broadcasted_iota(jnp.int32, (N_HEADS, page), 1)

    def chunk(g):
        slot = lax.rem(g, 2)
        b, c = lax.div(g, n_chunks), lax.rem(g, n_chunks)

        @pl.when(g + 1 < total)
        def _():
            for cp in k_copies(g + 1, 1 - slot):
                cp.start()

        @pl.when(c == 0)
        def _():
            init(b)
            for cp in v_copies(g, 0, vpre, vsem.at[0]):
                cp.wait()

        for cp in k_copies(g, slot):
            cp.wait()

        kcat = jnp.concatenate([kbuf[slot, r].reshape(ATTN_W, page).astype(BF16)
                                for r in range(pps)], axis=1)
        qk = jnp.dot(qbd_ref[...], kcat, preferred_element_type=F32)
        tail = tail_ref[...]
        lfs, tails = [], []
        for r in range(pps):
            lf = lfbuf[slot, r] * LOG2E
            lfs.append(lf)
            tails.append(tail)
            tail = tail + jnp.sum(lf, axis=1, keepdims=True)
        tail_ref[...] = tail
        base = qk + jnp.concatenate([jnp.concatenate(tails, axis=1)] * n_new, axis=0) + cncol_ref[:, 0:1]
        m_prev = m_ref[...]
        over = base - _lane_tile(m_prev, n_keys // LANES)
        gaps = [jnp.max(over[:, g_keys * h:g_keys * (h + 1)]) for h in range(n_groups)]
        live = [gp >= -SKIP_LOG2 for gp in gaps]

        @pl.when(functools.reduce(jnp.logical_or, live))
        def _():
            g_parts = []
            for r in range(pps):
                incl = lfs[r]
                d = 1
                while d < page:
                    incl = incl + jnp.where(lane8 + d < page, pltpu.roll(incl, page - d, axis=1), 0.0)
                    d *= 2
                g_parts.append(incl - lfs[r])
            s = base + jnp.concatenate([jnp.concatenate(g_parts, axis=1)] * n_new, axis=0)
            m_next = jnp.maximum(m_prev, jnp.max(s, axis=1, keepdims=True))
            alpha = jnp.exp2(m_prev - m_next)
            p = jnp.exp2(s - _lane_tile(m_next, n_keys // LANES))
            l_ref[...] = alpha * l_ref[...] + jnp.sum(p, axis=1, keepdims=True)
            m_ref[...] = m_next
            acc_ref[...] = _lane_tile(alpha, ATTN_W // LANES) * acc_ref[...]
            pb = p.astype(BF16)

            def add_pv(h, vsrc):
                vcat = jnp.concatenate([vsrc[r].reshape(ATTN_W, page).astype(BF16)
                                        for r in range(vsub)], axis=1)
                acc_ref[...] += lax.dot_general(pb[:, g_keys * h:g_keys * (h + 1)], vcat,
                                                (((1,), (1,)), ((), ())),
                                                preferred_element_type=F32)

            def fetch_and_add(h):
                cps = v_copies(g, h, vdem, vsem.at[1])
                for cp in cps:
                    cp.start()
                for cp in cps:
                    cp.wait()
                add_pv(h, vdem)

            for h in range(n_groups):
                if h == 0:
                    @pl.when(jnp.logical_and(live[0], c == 0))
                    def _():
                        add_pv(0, vpre)

                    @pl.when(jnp.logical_and(live[0], c > 0))
                    def _():
                        fetch_and_add(0)
                else:
                    @pl.when(live[h])
                    def _(h=h):
                        fetch_and_add(h)

        @pl.when(c == n_chunks - 1)
        def _():
            finish(b)

        @pl.when(jnp.logical_and(c == 0, b + 1 < nb))
        def _():
            for cp in v_copies(g + n_chunks, 0, vpre, vsem.at[0]):
                cp.start()

    def prologue():
        for cp in k_copies(0, 0):
            cp.start()
        for cp in v_copies(0, 0, vpre, vsem.at[0]):
            cp.start()

    return prologue, chunk, total


def _decode_setup(nb, n_new, n_pages, page):
    pps = DEC_PAGES
    while n_pages % pps:
        pps //= 2
    vsub = min(DEC_VSUB, pps)
    rows = n_new * N_HEADS
    assert page % LANES == 0 and rows % 8 == 0 and pps % vsub == 0
    params = dict(nb=nb, n_new=n_new, n_pages=n_pages, page=page, pps=pps, vsub=vsub)
    scratch = [pltpu.VMEM((2, pps, N_HEADS, HEAD_DIM, page), F32),
               pltpu.VMEM((vsub, N_HEADS, HEAD_DIM, page), F32),
               pltpu.VMEM((vsub, N_HEADS, HEAD_DIM, page), F32),
               pltpu.VMEM((2, pps, N_HEADS, page), F32),
               pltpu.SemaphoreType.DMA((2,)),
               pltpu.SemaphoreType.DMA((2,)),
               pltpu.SemaphoreType.DMA((2,)),
               pltpu.VMEM((rows, ATTN_W), BF16),
               pltpu.VMEM((8, LANES), F32),
               pltpu.VMEM((rows, LANES), F32),
               pltpu.VMEM((rows, LANES), F32),
               pltpu.VMEM((rows, LANES), F32),
               pltpu.VMEM((rows, ATTN_W), F32),
               pltpu.VMEM((N_HEADS, LANES), F32)]
    return params, scratch


def _merge_sample_kernel(x_ref, u_ref, sp_ref, sgp_ref, attn_ref, sga_ref, pw_ref, ps_ref, wo_ref,
                         nf_ref, y_ref, pool_ref, *, n_new, past):
    u = [u_ref[:, POOL_W * i:POOL_W * (i + 1)] for i in range(n_new)]
    hist = [sp_ref[r] for r in range(POOL_HIST)]
    ext = hist + u
    pooled_rows = []
    for i in range(n_new):
        groups = []
        for g, w in enumerate(POOL_WINDOWS):
            sl = slice(POOL_GC * g, POOL_GC * (g + 1))
            end = POOL_HIST + i
            total = ext[end][:, sl]
            for r in range(end - w + 1, end):
                total = total + ext[r][:, sl]
            groups.append(total / float(min(w, past + i + 1)) - u[i][:, sl])
        pooled_rows.append(jnp.concatenate(groups, axis=1))
    pooled = jnp.concatenate(pooled_rows, axis=0)
    cat = lambda ref, n: jnp.concatenate([ref[:, n * i:n * (i + 1)] for i in range(n_new)], axis=0)
    y = _merge(cat(x_ref, D_MODEL), pooled, cat(sgp_ref, POOL_W), cat(attn_ref, ATTN_W),
               cat(sga_ref, ATTN_W), pw_ref, ps_ref[...], wo_ref, nf_ref[...])
    nb = x_ref.shape[0]
    for i in range(n_new):
        y_ref[:, D_MODEL * i:D_MODEL * (i + 1)] = y[nb * i:nb * (i + 1), :]
    new_hist = ext[-POOL_HIST:]
    for r in range(POOL_HIST):
        pool_ref[:, POOL_W * r:POOL_W * (r + 1)] = new_hist[r]


def _merge_sample(xs2, u2, sp2, sgp2, attn2, sga2, pw, ps, wo, nf, *, n_new, past):
    nb = xs2.shape[0]
    return pl.pallas_call(
        functools.partial(_merge_sample_kernel, n_new=n_new, past=past),
        out_shape=(jax.ShapeDtypeStruct((nb, n_new * D_MODEL), F32),
                   jax.ShapeDtypeStruct((nb, POOL_HIST * POOL_W), F32)),
        compiler_params=pltpu.CompilerParams(vmem_limit_bytes=VMEM_LIMIT),
        name="merge_sample",
    )(xs2, u2, sp2, sgp2, attn2, sga2, pw, ps, wo, nf)


def kernel(x_prompt, x_sample, cache_k, cache_v, cache_logf, state_pool, page_table,
           norm_w, w_in, b_forget, pool_w, pool_scale, w_out, norm_f):
    depth = norm_w.shape[0]
    assert depth == 1, "a single layer is supported"
    b_p, seq, _ = x_prompt.shape
    b_s, t_s, _ = x_sample.shape
    n_phys, page = cache_k.shape[1], cache_k.shape[2]
    past = page_table.shape[1] * page
    ll = 0

    n_main = 2 * POOL_W + 4 * ATTN_W
    w = jnp.concatenate([w_in[ll][:, :n_main], jnp.tile(w_in[ll][:, n_main:], (1, LANES // N_HEADS))],
                        axis=1).astype(BF16)
    bft = jnp.tile(b_forget[ll], LANES // N_HEADS).reshape(1, LANES).astype(F32)
    nw = norm_w[ll].reshape(1, D_MODEL)
    nf = norm_f.reshape(1, D_MODEL)
    pw = pool_w[ll].astype(BF16)
    ps = pool_scale[ll].reshape(1, POOL_W)
    wo = w_out[ll].astype(BF16)
    pp_np, cc_np = _placement()
    pp = jnp.asarray(pp_np, BF16)
    cc = jnp.asarray(cc_np, F32)

    xp2 = x_prompt.reshape(b_p * seq, D_MODEL)
    seg_np = (np.arange(ATTN_W)[:, None] // HEAD_DIM == np.arange(LANES)[None, :] % N_HEADS)
    seg = jnp.asarray(seg_np, BF16)
    u, sgp, qa, ka, kf, vf, vb, sga, logf, stats = _proj_prompt(xp2, nw, w, bft, pp, cc, seg, seq=seq)
    n_blk = seq // min(ATTN_T, seq)
    plan = _plan(stats, n_batch=b_p, n_blk=n_blk)[:b_p * N_PAIRS, :n_blk].reshape(-1)

    xs2 = x_sample.reshape(b_s, t_s * D_MODEL)
    us, sgps, qs, ks, vs, sgas, lfs, cns = _proj_sample(xs2, nw, w, bft, n_new=t_s)
    ck = jnp.transpose(cache_k[ll], (0, 2, 3, 1))
    cv = jnp.transpose(cache_v[ll], (0, 2, 3, 1))
    clf_t = jnp.swapaxes(cache_logf[ll], 1, 2)

    attn, attn_s = _attention(plan, qa, ka, vb, page_table, qs.reshape(b_s, t_s, ATTN_W),
                              ks.reshape(b_s, t_s, ATTN_W), vs.reshape(b_s, t_s, ATTN_W),
                              cns.reshape(b_s, t_s, LANES), ck, cv, clf_t, n_batch=b_p, seq=seq)
    yp, pool_p = _merge_prompt(xp2, u, sgp, attn, sga, pw, ps, wo, nf, n_batch=b_p, seq=seq)
    ys, pool_s = _merge_sample(xs2, us, jnp.swapaxes(state_pool[ll], 0, 1), sgps,
                               attn_s.reshape(b_s, t_s * ATTN_W), sgas, pw, ps, wo, nf,
                               n_new=t_s, past=past)

    return (yp.reshape(b_p, seq, D_MODEL),
            ys.reshape(b_s, t_s, D_MODEL),
            jnp.transpose(kf.reshape(1, b_p, N_HEADS, HEAD_DIM, seq), (0, 1, 4, 2, 3)),
            jnp.transpose(vf.reshape(1, b_p, N_HEADS, HEAD_DIM, seq), (0, 1, 4, 2, 3)),
            jnp.transpose(logf, (0, 2, 1)).reshape(1, b_p, seq, N_HEADS),
            pool_p.reshape(1, b_p, POOL_HIST, POOL_W),
            ks.reshape(1, b_s, t_s, N_HEADS, HEAD_DIM),
            vs.reshape(1, b_s, t_s, N_HEADS, HEAD_DIM),
            lfs.reshape(b_s, t_s, LANES)[:, :, :N_HEADS].reshape(1, b_s, t_s, N_HEADS),
            pool_s.reshape(1, b_s, POOL_HIST, POOL_W))
```

```python
import functools
import math

import numpy as np
import jax
import jax.numpy as jnp
from jax import lax
from jax.experimental import pallas as pl
from jax.experimental.pallas import tpu as pltpu

D_MODEL = 1024
POOL_W = 512
ATTN_W = 512
N_HEADS = 8
HEAD_DIM = 64
POOL_WINDOWS = (2, 4, 8, 16)
POOL_GC = 128
POOL_HIST = 15
EPS = 1e-6

LANES = 128
LOG2E = 1.4426950408889634
Q_SCALE = LOG2E / math.sqrt(HEAD_DIM)
NEG = -1e30
N_PAIRS = N_HEADS // 2
W_COLS = 2 * POOL_W + 4 * ATTN_W + LANES
BIAS_LANES = LANES // N_HEADS
BIAS_COLS = 2 * LANES
VMEM_LIMIT = 56 * 1024 * 1024

SKIP_LOG2 = 152.0
NORM_SLACK = 1.02

PROJ_TM = 512
ATTN_T = 512
DEC_PAGES = 16
DEC_VSUB = 8

BF16 = jnp.bfloat16
F32 = jnp.float32


def _placement():
    pp = np.zeros((LANES, BIAS_COLS), np.float32)
    cc = np.zeros((1, BIAS_COLS), np.float32)
    for h in range(N_HEADS):
        for i in range(3):
            pp[8 * i + h, BIAS_LANES * h + i] = 1.0
            pp[8 * i + h, LANES + BIAS_LANES * h + 3 + i] = -1.0
            cc[0, BIAS_LANES * h + 3 + i] = 1.0
            cc[0, LANES + BIAS_LANES * h + i] = 1.0
    return pp, cc


def _log_sigmoid(x):
    return jnp.minimum(x, 0.0) - jnp.log(1.0 + jnp.exp(-jnp.abs(x)))


def _silu(x):
    return x * jax.nn.sigmoid(x)


def _rmsnorm(xf, w):
    ms = jnp.mean(xf * xf, axis=-1, keepdims=True)
    return xf * lax.rsqrt(ms + EPS) * w


def _project(h, wt_ref):
    def cols(c0, n):
        return lax.dot_general(h, wt_ref[c0:c0 + n, :], (((1,), (1,)), ((), ())),
                               preferred_element_type=F32)
    u = cols(0, POOL_W)
    gp = cols(POOL_W, POOL_W)
    q = cols(2 * POOL_W, ATTN_W)
    k = cols(2 * POOL_W + ATTN_W, ATTN_W)
    v = cols(2 * POOL_W + 2 * ATTN_W, ATTN_W)
    ga = cols(2 * POOL_W + 3 * ATTN_W, ATTN_W)
    fl = cols(2 * POOL_W + 4 * ATTN_W, LANES)
    return u, gp, q, k, v, ga, fl


def _proj_prompt_kernel(x_ref, nw_ref, w_ref, bf_ref, pp_ref, cc_ref, seg_ref,
                        u_ref, sgp_ref, qa_ref, ka_ref, k_ref, v_ref, vb_ref, sga_ref, logf_ref,
                        st_ref, carry_ref, *, tm, ta, tiles_per_batch):
    i = pl.program_id(0)
    h = _rmsnorm(x_ref[...], nw_ref[...]).astype(BF16)
    u, gp, q, k, v, ga, fl = _project(h, w_ref)
    u_ref[...] = u
    sgp_ref[...] = _silu(gp).astype(BF16)
    k_ref[0] = jnp.transpose(k)
    v_ref[0] = jnp.transpose(v)
    vb_ref[...] = v.astype(BF16)
    sga_ref[...] = _silu(ga).astype(BF16)
    logf = _log_sigmoid(fl + bf_ref[...])
    logf_ref[0] = jnp.transpose(logf)[0:N_HEADS, :]

    row = lax.broadcasted_iota(jnp.int32, (tm, LANES), 0)
    acc = logf
    d = 1
    while d < tm:
        acc = acc + jnp.where(row >= d, pltpu.roll(acc, d, axis=0), 0.0)
        d *= 2
    @pl.when(i % tiles_per_batch == 0)
    def _():
        carry_ref[...] = jnp.zeros(carry_ref.shape, F32)

    f_run = acc + carry_ref[0:1, :]
    carry_ref[...] = jnp.broadcast_to(f_run[tm - 1:tm, :], carry_ref.shape)

    f2 = f_run * LOG2E
    p1 = f2.astype(BF16).astype(F32)
    r1 = f2 - p1
    p2 = r1.astype(BF16).astype(F32)
    p3 = (r1 - p2).astype(BF16).astype(F32)
    lane = lax.broadcasted_iota(jnp.int32, (tm, LANES), 1)
    pieces = jnp.where(lane < 8, p1, jnp.where(lane < 16, p2, jnp.where(lane < 24, p3, 0.0)))
    bias = jnp.dot(pieces.astype(BF16), pp_ref[...], preferred_element_type=F32) + cc_ref[...]

    qs = q * Q_SCALE

    qn2 = jnp.dot((qs * qs).astype(BF16), seg_ref[...], preferred_element_type=F32)
    kn2 = jnp.dot((k * k).astype(BF16), seg_ref[...], preferred_element_type=F32)
    srow = lax.broadcasted_iota(jnp.int32, (8, LANES), 0)
    for sb in range(tm // ta):
        r0, r1 = sb * ta, (sb + 1) * ta
        qmax = jnp.sqrt(jnp.max(qn2[r0:r1], axis=0, keepdims=True))
        kmax = jnp.sqrt(jnp.max(kn2[r0:r1], axis=0, keepdims=True))
        st_ref[8 * sb:8 * (sb + 1), :] = jnp.where(
            srow == 0, qmax, jnp.where(srow == 1, kmax, jnp.where(
                srow == 2, f2[r0:r0 + 1, :], jnp.where(srow == 3, f2[r1 - 1:r1, :], 0.0))))

    first_half = lane < HEAD_DIM
    q_bias = bias[:, 0:LANES].astype(BF16)
    k_bias = bias[:, LANES:2 * LANES]
    for p in range(N_PAIRS):
        qa_ref[:, 2 * LANES * p:2 * LANES * p + LANES] = qs[:, LANES * p:LANES * (p + 1)].astype(BF16)
        qa_ref[:, 2 * LANES * p + LANES:2 * LANES * (p + 1)] = q_bias
        kp = k[:, LANES * p:LANES * (p + 1)]
        for e in range(2):
            hh = 2 * p + e
            keep = first_half if e == 0 else jnp.logical_not(first_half)
            ka_ref[:, 2 * LANES * hh:2 * LANES * hh + LANES] = jnp.where(keep, kp, 0.0).astype(BF16)
            own = jnp.logical_and(lane >= BIAS_LANES * hh, lane < BIAS_LANES * (hh + 1))
            ka_ref[:, 2 * LANES * hh + LANES:2 * LANES * (hh + 1)] = (
                jnp.where(own, k_bias, 0.0).astype(BF16))


def _proj_prompt(x2, nw, w, bft, pp, cc, seg, *, seq):
    rows = x2.shape[0]
    tm = min(PROJ_TM, seq)
    ta = min(ATTN_T, seq)
    assert seq % tm == 0 and rows % seq == 0 and tm % ta == 0 and tm % LANES == 0
    n_batch = rows // seq
    row_blk = lambda n: pl.BlockSpec((tm, n), lambda i: (i, 0))
    const = lambda shape: pl.BlockSpec(shape, lambda i: (0, 0))
    out_shape = (
        jax.ShapeDtypeStruct((rows, POOL_W), F32),
        jax.ShapeDtypeStruct((rows, POOL_W), BF16),
        jax.ShapeDtypeStruct((rows, N_PAIRS * 2 * LANES), BF16),
        jax.ShapeDtypeStruct((rows, N_HEADS * 2 * LANES), BF16),
        jax.ShapeDtypeStruct((n_batch, ATTN_W, seq), F32),
        jax.ShapeDtypeStruct((n_batch, ATTN_W, seq), F32),
        jax.ShapeDtypeStruct((rows, ATTN_W), BF16),
        jax.ShapeDtypeStruct((rows, ATTN_W), BF16),
        jax.ShapeDtypeStruct((n_batch, N_HEADS, seq), F32),
        jax.ShapeDtypeStruct((rows // ta * 8, LANES), F32),
    )
    tpb = seq // tm
    t_minor = lambda n: pl.BlockSpec((1, n, tm), lambda i: (i // tpb, 0, i % tpb))
    out_specs = (row_blk(POOL_W), row_blk(POOL_W), row_blk(N_PAIRS * 2 * LANES),
                 row_blk(N_HEADS * 2 * LANES), t_minor(ATTN_W), t_minor(ATTN_W), row_blk(ATTN_W),
                 row_blk(ATTN_W), t_minor(N_HEADS),
                 pl.BlockSpec((tm // ta * 8, LANES), lambda i: (i, 0)))
    return pl.pallas_call(
        functools.partial(_proj_prompt_kernel, tm=tm, ta=ta, tiles_per_batch=seq // tm),
        grid=(rows // tm,),
        in_specs=[row_blk(D_MODEL), const((1, D_MODEL)), const((W_COLS, D_MODEL)), const((1, LANES)),
                  const((LANES, BIAS_COLS)), const((1, BIAS_COLS)), const((ATTN_W, LANES))],
        out_specs=out_specs,
        out_shape=out_shape,
        scratch_shapes=[pltpu.VMEM((8, LANES), F32)],
        compiler_params=pltpu.CompilerParams(dimension_semantics=("arbitrary",),
                                             vmem_limit_bytes=VMEM_LIMIT),
        name="proj_prompt",
    )(x2, nw, w, bft, pp, cc, seg)


def _lane_tile(x, reps):
    return jnp.concatenate([x] * reps, axis=1)


def _plan_kernel(st_ref, o_ref, *, n_batch, n_blk):
    row = lax.broadcasted_iota(jnp.int32, (LANES, LANES), 0).astype(F32)
    col = lax.broadcasted_iota(jnp.int32, (LANES, LANES), 1).astype(F32)
    out = jnp.zeros((LANES, LANES), F32)
    for b in range(n_batch):
        def stat(r):
            v = st_ref[pl.ds((b * n_blk) * 8 + r, n_blk, stride=8), :]
            return jnp.concatenate([v, jnp.zeros((LANES - n_blk, LANES), F32)], axis=0)
        qm, km, ft0, fs1 = stat(0), stat(1), stat(2), stat(3)
        km_t, fs1_t = jnp.transpose(km), jnp.transpose(fs1)
        for p in range(N_PAIRS):
            first = row
            for h in (2 * p, 2 * p + 1):
                qcol = qm[:, h:h + 1]
                ub = (NORM_SLACK * (qcol * km_t[h:h + 1, :] + qcol * km[:, h:h + 1])
                      + ft0[:, h:h + 1] - fs1_t[h:h + 1, :])
                needed = jnp.logical_and(ub >= -SKIP_LOG2, col < row)
                first = jnp.minimum(first, jnp.where(needed, col, row))
            start = jnp.min(first, axis=1, keepdims=True)
            out = jnp.where(col == b * N_PAIRS + p, start, out)
    o_ref[...] = jnp.transpose(out).astype(jnp.int32)


def _plan(stats, *, n_batch, n_blk):
    assert n_blk <= LANES and n_batch * N_PAIRS <= LANES
    return pl.pallas_call(
        functools.partial(_plan_kernel, n_batch=n_batch, n_blk=n_blk),
        out_shape=jax.ShapeDtypeStruct((LANES, LANES), jnp.int32),
        name="attn_plan",
    )(stats)


def _attn_kernel(plan_ref, pt_ref, qa_ref, ka_ref, vb_ref, qd_ref, kn_ref, vn_ref, cn_ref,
                 ck_hbm, cv_hbm, clf_hbm, o_ref, od_ref, m_ref, l_ref, acc_ref, *dec_scratch,
                 t, nq, n_steps, dec):
    b, pr, i = pl.program_id(0), pl.program_id(1), pl.program_id(2)
    step = (b * N_PAIRS + pr) * nq + i
    dec_prologue, dec_chunk, n_chunks = _decode_fns(pt_ref, qd_ref, kn_ref, vn_ref, cn_ref, ck_hbm,
                                                    cv_hbm, clf_hbm, od_ref, *dec_scratch, **dec)
    per_step = -(-n_chunks // n_steps)

    @pl.when(step == 0)
    def _():
        dec_prologue()

    for k in range(per_step):
        g = step * per_step + k

        @pl.when(g < n_chunks)
        def _(g=g):
            dec_chunk(g)

    q = qa_ref[...]
    m_ref[...] = jnp.full(m_ref.shape, NEG, F32)
    l_ref[...] = jnp.zeros(l_ref.shape, F32)
    acc_ref[...] = jnp.zeros(acc_ref.shape, F32)

    def block(kstart, width, diag_col):
        vblk = vb_ref[pl.ds(kstart, width), :]
        for e in range(2):
            kblk = ka_ref[pl.ds(kstart, width), 2 * LANES * e:2 * LANES * (e + 1)]
            s = lax.dot_general(q, kblk, (((1,), (1,)), ((), ())), preferred_element_type=F32)
            if diag_col is not None:
                r = lax.broadcasted_iota(jnp.int32, (t, width), 0)
                c = lax.broadcasted_iota(jnp.int32, (t, width), 1)
                s = jnp.where(c - diag_col <= r, s, NEG)
            m_prev = m_ref[e]
            m_next = jnp.maximum(m_prev, jnp.max(s, axis=1, keepdims=True))
            alpha = jnp.exp2(m_prev - m_next)
            p = jnp.exp2(s - _lane_tile(m_next, width // LANES))
            l_ref[e] = alpha * l_ref[e] + jnp.sum(p, axis=1, keepdims=True)
            m_ref[e] = m_next
            acc_ref[e] = alpha * acc_ref[e] + jnp.dot(p.astype(BF16), vblk,
                                                      preferred_element_type=F32)

    def body(j, carry):
        block(pl.multiple_of(j * t, t), t, None)
        return carry

    first = plan_ref[(b * N_PAIRS + pr) * nq + i]
    one_before = jnp.logical_and(i > 0, first == i - 1)

    @pl.when(one_before)
    def _():
        block(pl.multiple_of((i - 1) * t, t), 2 * t, t)

    @pl.when(jnp.logical_not(one_before))
    def _():
        lax.fori_loop(first, i, body, 0)
        block(pl.multiple_of(i * t, t), t, 0)

    lane = lax.broadcasted_iota(jnp.int32, (t, LANES), 1)
    o = jnp.where(lane < HEAD_DIM, acc_ref[0] / l_ref[0], acc_ref[1] / l_ref[1])
    o_ref[...] = o.astype(o_ref.dtype)


def _attention(plan, qa, ka, vb, page_table, q3, kn3, vn3, cn3, ck, cv, clf_t, *, n_batch, seq):
    t = min(ATTN_T, seq)
    assert seq % t == 0
    nq = seq // t
    nb, n_new, _ = q3.shape
    dec, dec_scratch = _decode_setup(nb, n_new, page_table.shape[1], ck.shape[3])
    whole = lambda n: pl.BlockSpec((nb, n_new, n), lambda b, p, i, plan, pt: (0, 0, 0))
    hbm = pl.BlockSpec(memory_space=pl.ANY)
    grid_spec = pltpu.PrefetchScalarGridSpec(
        num_scalar_prefetch=2,
        grid=(n_batch, N_PAIRS, nq),
        in_specs=[pl.BlockSpec((t, 2 * LANES), lambda b, p, i, plan, pt: (b * nq + i, p)),
                  pl.BlockSpec((seq, 4 * LANES), lambda b, p, i, plan, pt: (b, p)),
                  pl.BlockSpec((seq, LANES), lambda b, p, i, plan, pt: (b, p)),
                  whole(ATTN_W), whole(ATTN_W), whole(ATTN_W), whole(LANES), hbm, hbm, hbm],
        out_specs=(pl.BlockSpec((t, LANES), lambda b, p, i, plan, pt: (b * nq + i, p)),
                   whole(ATTN_W)),
        scratch_shapes=[pltpu.VMEM((2, t, LANES), F32), pltpu.VMEM((2, t, LANES), F32),
                        pltpu.VMEM((2, t, LANES), F32)] + dec_scratch)
    return pl.pallas_call(
        functools.partial(_attn_kernel, t=t, nq=nq, n_steps=n_batch * N_PAIRS * nq, dec=dec),
        grid_spec=grid_spec,
        out_shape=(jax.ShapeDtypeStruct((n_batch * seq, ATTN_W), BF16),
                   jax.ShapeDtypeStruct((nb, n_new, ATTN_W), F32)),
        compiler_params=pltpu.CompilerParams(
            dimension_semantics=("arbitrary", "arbitrary", "arbitrary"),
            vmem_limit_bytes=VMEM_LIMIT),
        name="attention",
    )(plan, page_table.reshape(-1), qa, ka, vb, q3, kn3, vn3, cn3, ck, cv, clf_t)


def _merge(x, pooled, sgp, attn, sga, pw_ref, ps, wo_ref, nf):
    mixed = [jnp.dot(pooled[:, POOL_GC * g:POOL_GC * (g + 1)].astype(BF16), pw_ref[g],
                     preferred_element_type=F32) for g in range(len(POOL_WINDOWS))]
    pool_out = jnp.concatenate(mixed, axis=1) * ps
    mix = jnp.concatenate([(pool_out * sgp.astype(F32)).astype(BF16),
                           (attn.astype(F32) * sga.astype(F32)).astype(BF16)], axis=1)
    xo = x + jnp.dot(mix, wo_ref[...], preferred_element_type=F32)
    return _rmsnorm(xo, nf)


def _merge_prompt_kernel(x_ref, u_ref, uh_ref, sgp_ref, attn_ref, sga_ref, pw_ref, ps_ref, wo_ref,
                         nf_ref, y_ref, pool_ref, *, tm, tiles_per_batch):
    i = pl.program_id(0)
    ti = i % tiles_per_batch
    u = u_ref[...]
    halo = jnp.where(ti == 0, 0.0, uh_ref[...])
    ext = jnp.concatenate([halo, u], axis=0)
    pos = ti * tm + lax.broadcasted_iota(jnp.int32, (tm, 1), 0)
    groups = []
    level = ext
    shift = 1
    for g, w in enumerate(POOL_WINDOWS):
        while shift < w:
            level = level + pltpu.roll(level, shift, axis=0)
            shift *= 2
        cnt = jnp.minimum(w, pos + 1).astype(F32)
        sl = slice(POOL_GC * g, POOL_GC * (g + 1))
        groups.append(level[16:, sl] / cnt - u[:, sl])
    pooled = jnp.concatenate(groups, axis=1)
    y_ref[...] = _merge(x_ref[...], pooled, sgp_ref[...], attn_ref[...], sga_ref[...], pw_ref,
                        ps_ref[...], wo_ref, nf_ref[...])

    @pl.when(ti == tiles_per_batch - 1)
    def _():
        pool_ref[0] = u_ref[pl.ds(tm - POOL_HIST, POOL_HIST), :]


def _merge_prompt(x2, u, sgp, attn, sga, pw, ps, wo, nf, *, n_batch, seq):
    rows = x2.shape[0]
    tm = min(PROJ_TM, seq)
    tpb = seq // tm
    halo_blocks = tm // 16
    row_blk = lambda n: pl.BlockSpec((tm, n), lambda i: (i, 0))
    const2 = lambda shape: pl.BlockSpec(shape, lambda i: (0, 0))
    return pl.pallas_call(
        functools.partial(_merge_prompt_kernel, tm=tm, tiles_per_batch=tpb),
        grid=(rows // tm,),
        in_specs=[row_blk(D_MODEL), row_blk(POOL_W),
                  pl.BlockSpec((16, POOL_W), lambda i: (jnp.maximum(i * halo_blocks - 1, 0), 0)),
                  row_blk(POOL_W), row_blk(ATTN_W), row_blk(ATTN_W),
                  pl.BlockSpec((len(POOL_WINDOWS), POOL_GC, POOL_GC), lambda i: (0, 0, 0)),
                  const2((1, POOL_W)), const2((D_MODEL, D_MODEL)), const2((1, D_MODEL))],
        out_specs=(row_blk(D_MODEL),
                   pl.BlockSpec((1, POOL_HIST, POOL_W), lambda i: (i // tpb, 0, 0))),
        out_shape=(jax.ShapeDtypeStruct((rows, D_MODEL), F32),
                   jax.ShapeDtypeStruct((n_batch, POOL_HIST, POOL_W), F32)),
        compiler_params=pltpu.CompilerParams(dimension_semantics=("arbitrary",),
                                             vmem_limit_bytes=VMEM_LIMIT),
        name="merge_prompt",
    )(x2, u, u, sgp, attn, sga, pw, ps, wo, nf)


def _proj_sample_kernel(x_ref, nw_ref, w_ref, bf_ref,
                        u_ref, sgp_ref, q_ref, k_ref, v_ref, sga_ref, logf_ref, cn_ref, carry_ref):
    i = pl.program_id(0)
    h = _rmsnorm(x_ref[...], nw_ref[...]).astype(BF16)
    u, gp, q, k, v, ga, fl = _project(h, w_ref)
    u_ref[...] = u
    sgp_ref[...] = _silu(gp).astype(BF16)
    q_ref[...] = q * Q_SCALE
    k_ref[...] = k
    v_ref[...] = v
    sga_ref[...] = _silu(ga).astype(BF16)
    logf = _log_sigmoid(fl + bf_ref[...])
    logf_ref[...] = logf
    @pl.when(i == 0)
    def _():
        carry_ref[...] = jnp.zeros(carry_ref.shape, F32)

    cn = carry_ref[...] + logf
    carry_ref[...] = cn
    cn_ref[...] = cn


def _proj_sample(xs2, nw, w, bft, *, n_new):
    nb = xs2.shape[0]
    blk = lambda n: pl.BlockSpec((nb, n), lambda i: (0, i))
    const = lambda shape: pl.BlockSpec(shape, lambda i: (0, 0))
    widths = (POOL_W, POOL_W, ATTN_W, ATTN_W, ATTN_W, ATTN_W, LANES, LANES)
    dtypes = (F32, BF16, F32, F32, F32, BF16, F32, F32)
    return pl.pallas_call(
        _proj_sample_kernel,
        grid=(n_new,),
        in_specs=[blk(D_MODEL), const((1, D_MODEL)), const((W_COLS, D_MODEL)), const((1, LANES))],
        out_specs=tuple(blk(n) for n in widths),
        out_shape=tuple(jax.ShapeDtypeStruct((nb, n_new * n), dt) for n, dt in zip(widths, dtypes)),
        scratch_shapes=[pltpu.VMEM((nb, LANES), F32)],
        compiler_params=pltpu.CompilerParams(dimension_semantics=("arbitrary",),
                                             vmem_limit_bytes=VMEM_LIMIT),
        name="proj_sample",
    )(xs2, nw, w, bft)


def _decode_fns(pt_ref, q_ref, kn_ref, vn_ref, cn_ref, ck_hbm, cv_hbm, clf_hbm, o_ref,
                kbuf, vpre, vdem, lfbuf, ksem, lfsem, vsem,
                qbd_ref, cn8_ref, cncol_ref, m_ref, l_ref, acc_ref, tail_ref,
                *, nb, n_new, n_pages, page, pps, vsub):
    n_chunks = n_pages // pps
    n_groups = pps // vsub
    total = nb * n_chunks
    n_keys = pps * page
    g_keys = vsub * page
    rows = n_new * N_HEADS
    row_w = lax.broadcasted_iota(jnp.int32, (rows, ATTN_W), 0)
    lane_w = lax.broadcasted_iota(jnp.int32, (rows, ATTN_W), 1)
    head_lanes = (row_w % N_HEADS) == (lane_w // HEAD_DIM)
    row1 = lax.broadcasted_iota(jnp.int32, (rows, 1), 0)

    def page_index(g, r):
        b, c = lax.div(g, n_chunks), lax.rem(g, n_chunks)
        return pt_ref[b * n_pages + n_pages - 1 - (c * pps + r)]

    def k_copies(g, slot):
        cps = []
        for r in range(pps):
            idx = page_index(g, r)
            cps.append(pltpu.make_async_copy(ck_hbm.at[idx], kbuf.at[slot, r], ksem.at[slot]))
            cps.append(pltpu.make_async_copy(clf_hbm.at[idx], lfbuf.at[slot, r], lfsem.at[slot]))
        return cps

    def v_copies(g, grp, dst, sem):
        return [pltpu.make_async_copy(cv_hbm.at[page_index(g, grp * vsub + r)], dst.at[r], sem)
                for r in range(vsub)]

    def init(b):
        q = q_ref[b]
        qrep = jnp.concatenate(
            [jnp.broadcast_to(q[i:i + 1, :], (N_HEADS, ATTN_W)) for i in range(n_new)], axis=0)
        qbd = jnp.where(head_lanes, qrep, 0.0)
        qbd_ref[...] = qbd.astype(BF16)
        cn8_ref[...] = jnp.zeros(cn8_ref.shape, F32)
        cn8_ref[0:n_new, :] = cn_ref[b]
        cnt = jnp.transpose(cn8_ref[...])[0:rows, :] * LOG2E
        cncol = jnp.zeros((rows, 1), F32)
        for i in range(n_new):
            cncol = jnp.where(row1 // N_HEADS == i, cnt[:, i:i + 1], cncol)
        cncol_ref[...] = jnp.broadcast_to(cncol, cncol_ref.shape)
        kn = kn_ref[b]
        vn = vn_ref[b]
        s_new = []
        for j in range(n_new):
            sj = jnp.sum(qbd * kn[j:j + 1, :], axis=1, keepdims=True) + cncol - cnt[:, j:j + 1]
            s_new.append(jnp.where(row1 // N_HEADS >= j, sj, NEG))
        m0 = functools.reduce(jnp.maximum, s_new)
        l0 = jnp.zeros((rows, 1), F32)
        a0 = jnp.zeros((rows, ATTN_W), F32)
        for j in range(n_new):
            pj = jnp.exp2(s_new[j] - m0)
            l0 = l0 + pj
            a0 = a0 + pj * vn[j:j + 1, :]
        m_ref[...] = jnp.broadcast_to(m0, m_ref.shape)
        l_ref[...] = jnp.broadcast_to(l0, l_ref.shape)
        acc_ref[...] = a0
        tail_ref[...] = jnp.zeros(tail_ref.shape, F32)

    def finish(b):
        o = acc_ref[...] / _lane_tile(l_ref[...], ATTN_W // LANES)
        o = jnp.where(head_lanes, o, 0.0)
        for i in range(n_new):
            o_ref[b, i:i + 1, :] = jnp.sum(o[N_HEADS * i:N_HEADS * (i + 1), :], axis=0,
                                           keepdims=True)

    lane8 = lax.broadcasted_iota(jnp.int32, (N_HEADS, page), 1)

    def chunk(g):
        slot = lax.rem(g, 2)
        b, c = lax.div(g, n_chunks), lax.rem(g, n_chunks)

        @pl.when(g + 1 < total)
        def _():
            for cp in k_copies(g + 1, 1 - slot):
                cp.start()

        @pl.when(c == 0)
        def _():
            init(b)
            for cp in v_copies(g, 0, vpre, vsem.at[0]):
                cp.wait()

        for cp in k_copies(g, slot):
            cp.wait()

        kcat = jnp.concatenate([kbuf[slot, r].reshape(ATTN_W, page).astype(BF16)
                                for r in range(pps)], axis=1)
        qk = jnp.dot(qbd_ref[...], kcat, preferred_element_type=F32)
        tail = tail_ref[...]
        lfs, tails = [], []
        for r in range(pps):
            lf = lfbuf[slot, r] * LOG2E
            lfs.append(lf)
            tails.append(tail)
            tail = tail + jnp.sum(lf, axis=1, keepdims=True)
        tail_ref[...] = tail
        base = qk + jnp.concatenate([jnp.concatenate(tails, axis=1)] * n_new, axis=0) + cncol_ref[:, 0:1]
        m_prev = m_ref[...]
        over = base - _lane_tile(m_prev, n_keys // LANES)
        gaps = [jnp.max(over[:, g_keys * h:g_keys * (h + 1)]) for h in range(n_groups)]
        live = [gp >= -SKIP_LOG2 for gp in gaps]

        @pl.when(functools.reduce(jnp.logical_or, live))
        def _():
            g_parts = []
            for r in range(pps):
                incl = lfs[r]
                d = 1
                while d < page:
                    incl = incl + jnp.where(lane8 + d < page, pltpu.roll(incl, page - d, axis=1), 0.0)
                    d *= 2
                g_parts.append(incl - lfs[r])
            s = base + jnp.concatenate([jnp.concatenate(g_parts, axis=1)] * n_new, axis=0)
            m_next = jnp.maximum(m_prev, jnp.max(s, axis=1, keepdims=True))
            alpha = jnp.exp2(m_prev - m_next)
            p = jnp.exp2(s - _lane_tile(m_next, n_keys // LANES))
            l_ref[...] = alpha * l_ref[...] + jnp.sum(p, axis=1, keepdims=True)
            m_ref[...] = m_next
            acc_ref[...] = _lane_tile(alpha, ATTN_W // LANES) * acc_ref[...]
            pb = p.astype(BF16)

            def add_pv(h, vsrc):
                vcat = jnp.concatenate([vsrc[r].reshape(ATTN_W, page).astype(BF16)
                                        for r in range(vsub)], axis=1)
                acc_ref[...] += lax.dot_general(pb[:, g_keys * h:g_keys * (h + 1)], vcat,
                                                (((1,), (1,)), ((), ())),
                                                preferred_element_type=F32)

            def fetch_and_add(h):
                cps = v_copies(g, h, vdem, vsem.at[1])
                for cp in cps:
                    cp.start()
                for cp in cps:
                    cp.wait()
                add_pv(h, vdem)

            for h in range(n_groups):
                if h == 0:
                    @pl.when(jnp.logical_and(live[0], c == 0))
                    def _():
                        add_pv(0, vpre)

                    @pl.when(jnp.logical_and(live[0], c > 0))
                    def _():
                        fetch_and_add(0)
                else:
                    @pl.when(live[h])
                    def _(h=h):
                        fetch_and_add(h)

        @pl.when(c == n_chunks - 1)
        def _():
            finish(b)

        @pl.when(jnp.logical_and(c == 0, b + 1 < nb))
        def _():
            for cp in v_copies(g + n_chunks, 0, vpre, vsem.at[0]):
                cp.start()

    def prologue():
        for cp in k_copies(0, 0):
            cp.start()
        for cp in v_copies(0, 0, vpre, vsem.at[0]):
            cp.start()

    return prologue, chunk, total


def _decode_setup(nb, n_new, n_pages, page):
    pps = DEC_PAGES
    while n_pages % pps:
        pps //= 2
    vsub = min(DEC_VSUB, pps)
    rows = n_new * N_HEADS
    assert page % LANES == 0 and rows % 8 == 0 and pps % vsub == 0
    params = dict(nb=nb, n_new=n_new, n_pages=n_pages, page=page, pps=pps, vsub=vsub)
    scratch = [pltpu.VMEM((2, pps, N_HEADS, HEAD_DIM, page), F32),
               pltpu.VMEM((vsub, N_HEADS, HEAD_DIM, page), F32),
               pltpu.VMEM((vsub, N_HEADS, HEAD_DIM, page), F32),
               pltpu.VMEM((2, pps, N_HEADS, page), F32),
               pltpu.SemaphoreType.DMA((2,)),
               pltpu.SemaphoreType.DMA((2,)),
               pltpu.SemaphoreType.DMA((2,)),
               pltpu.VMEM((rows, ATTN_W), BF16),
               pltpu.VMEM((8, LANES), F32),
               pltpu.VMEM((rows, LANES), F32),
               pltpu.VMEM((rows, LANES), F32),
               pltpu.VMEM((rows, LANES), F32),
               pltpu.VMEM((rows, ATTN_W), F32),
               pltpu.VMEM((N_HEADS, LANES), F32)]
    return params, scratch


def _merge_sample_kernel(x_ref, u_ref, sp_ref, sgp_ref, attn_ref, sga_ref, pw_ref, ps_ref, wo_ref,
                         nf_ref, y_ref, pool_ref, *, n_new, past):
    u = [u_ref[:, POOL_W * i:POOL_W * (i + 1)] for i in range(n_new)]
    hist = [sp_ref[r] for r in range(POOL_HIST)]
    ext = hist + u
    pooled_rows = []
    for i in range(n_new):
        groups = []
        for g, w in enumerate(POOL_WINDOWS):
            sl = slice(POOL_GC * g, POOL_GC * (g + 1))
            end = POOL_HIST + i
            total = ext[end][:, sl]
            for r in range(end - w + 1, end):
                total = total + ext[r][:, sl]
            groups.append(total / float(min(w, past + i + 1)) - u[i][:, sl])
        pooled_rows.append(jnp.concatenate(groups, axis=1))
    pooled = jnp.concatenate(pooled_rows, axis=0)
    cat = lambda ref, n: jnp.concatenate([ref[:, n * i:n * (i + 1)] for i in range(n_new)], axis=0)
    y = _merge(cat(x_ref, D_MODEL), pooled, cat(sgp_ref, POOL_W), cat(attn_ref, ATTN_W),
               cat(sga_ref, ATTN_W), pw_ref, ps_ref[...], wo_ref, nf_ref[...])
    nb = x_ref.shape[0]
    for i in range(n_new):
        y_ref[:, D_MODEL * i:D_MODEL * (i + 1)] = y[nb * i:nb * (i + 1), :]
    new_hist = ext[-POOL_HIST:]
    for r in range(POOL_HIST):
        pool_ref[:, POOL_W * r:POOL_W * (r + 1)] = new_hist[r]


def _merge_sample(xs2, u2, sp2, sgp2, attn2, sga2, pw, ps, wo, nf, *, n_new, past):
    nb = xs2.shape[0]
    return pl.pallas_call(
        functools.partial(_merge_sample_kernel, n_new=n_new, past=past),
        out_shape=(jax.ShapeDtypeStruct((nb, n_new * D_MODEL), F32),
                   jax.ShapeDtypeStruct((nb, POOL_HIST * POOL_W), F32)),
        compiler_params=pltpu.CompilerParams(vmem_limit_bytes=VMEM_LIMIT),
        name="merge_sample",
    )(xs2, u2, sp2, sgp2, attn2, sga2, pw, ps, wo, nf)


def kernel(x_prompt, x_sample, cache_k, cache_v, cache_logf, state_pool, page_table,
           norm_w, w_in, b_forget, pool_w, pool_scale, w_out, norm_f):
    depth = norm_w.shape[0]
    assert depth == 1, "a single layer is supported"
    b_p, seq, _ = x_prompt.shape
    b_s, t_s, _ = x_sample.shape
    n_phys, page = cache_k.shape[1], cache_k.shape[2]
    past = page_table.shape[1] * page
    ll = 0

    n_main = 2 * POOL_W + 4 * ATTN_W
    wt = jnp.swapaxes(w_in[ll], 0, 1)
    w = jnp.concatenate([wt[:n_main], jnp.tile(wt[n_main:], (LANES // N_HEADS, 1))],
                        axis=0).astype(BF16)
    bft = jnp.tile(b_forget[ll], LANES // N_HEADS).reshape(1, LANES).astype(F32)
    nw = norm_w[ll].reshape(1, D_MODEL)
    nf = norm_f.reshape(1, D_MODEL)
    pw = pool_w[ll].astype(BF16)
    ps = pool_scale[ll].reshape(1, POOL_W)
    wo = w_out[ll].astype(BF16)
    pp_np, cc_np = _placement()
    pp = jnp.asarray(pp_np, BF16)
    cc = jnp.asarray(cc_np, F32)

    xp2 = x_prompt.reshape(b_p * seq, D_MODEL)
    seg_np = (np.arange(ATTN_W)[:, None] // HEAD_DIM == np.arange(LANES)[None, :] % N_HEADS)
    seg = jnp.asarray(seg_np, BF16)
    u, sgp, qa, ka, kf, vf, vb, sga, logf, stats = _proj_prompt(xp2, nw, w, bft, pp, cc, seg, seq=seq)
    n_blk = seq // min(ATTN_T, seq)
    plan = _plan(stats, n_batch=b_p, n_blk=n_blk)[:b_p * N_PAIRS, :n_blk].reshape(-1)

    xs2 = x_sample.reshape(b_s, t_s * D_MODEL)
    us, sgps, qs, ks, vs, sgas, lfs, cns = _proj_sample(xs2, nw, w, bft, n_new=t_s)
    ck = jnp.transpose(cache_k[ll], (0, 2, 3, 1))
    cv = jnp.transpose(cache_v[ll], (0, 2, 3, 1))
    clf_t = jnp.swapaxes(cache_logf[ll], 1, 2)

    attn, attn_s = _attention(plan, qa, ka, vb, page_table, qs.reshape(b_s, t_s, ATTN_W),
                              ks.reshape(b_s, t_s, ATTN_W), vs.reshape(b_s, t_s, ATTN_W),
                              cns.reshape(b_s, t_s, LANES), ck, cv, clf_t, n_batch=b_p, seq=seq)
    yp, pool_p = _merge_prompt(xp2, u, sgp, attn, sga, pw, ps, wo, nf, n_batch=b_p, seq=seq)
    ys, pool_s = _merge_sample(xs2, us, jnp.swapaxes(state_pool[ll], 0, 1), sgps,
                               attn_s.reshape(b_s, t_s * ATTN_W), sgas, pw, ps, wo, nf,
                               n_new=t_s, past=past)

    return (yp.reshape(b_p, seq, D_MODEL),
            ys.reshape(b_s, t_s, D_MODEL),
            jnp.transpose(kf.reshape(1, b_p, N_HEADS, HEAD_DIM, seq), (0, 1, 4, 2, 3)),
            jnp.transpose(vf.reshape(1, b_p, N_HEADS, HEAD_DIM, seq), (0, 1, 4, 2, 3)),
            jnp.transpose(logf, (0, 2, 1)).reshape(1, b_p, seq, N_HEADS),
            pool_p.reshape(1, b_p, POOL_HIST, POOL_W),
            ks.reshape(1, b_s, t_s, N_HEADS, HEAD_DIM),
            vs.reshape(1, b_s, t_s, N_HEADS, HEAD_DIM),
            lfs.reshape(b_s, t_s, LANES)[:, :, :N_HEADS].reshape(1, b_s, t_s, N_HEADS),
            pool_s.reshape(1, b_s, POOL_HIST, POOL_W))
```

```python
import functools
import math

import numpy as np
import jax
import jax.numpy as jnp
from jax import lax
from jax.experimental import pallas as pl
from jax.experimental.pallas import tpu as pltpu

D_MODEL = 1024
POOL_W = 512
ATTN_W = 512
N_HEADS = 8
HEAD_DIM = 64
POOL_WINDOWS = (2, 4, 8, 16)
POOL_GC = 128
POOL_HIST = 15
EPS = 1e-6

LANES = 128
LOG2E = 1.4426950408889634
Q_SCALE = LOG2E / math.sqrt(HEAD_DIM)
NEG = -1e30
N_PAIRS = N_HEADS // 2
W_MAIN = 2 * POOL_W + 4 * ATTN_W
BIAS_LANES = LANES // N_HEADS
BIAS_COLS = 2 * LANES
VMEM_LIMIT = 56 * 1024 * 1024

SKIP_LOG2 = 152.0
NORM_SLACK = 1.02

PROJ_TM = 512
ATTN_T = 512
DEC_PAGES = 16
DEC_VSUB = 8

BF16 = jnp.bfloat16
F32 = jnp.float32


def _placement():
    pp = np.zeros((LANES, BIAS_COLS), np.float32)
    cc = np.zeros((1, BIAS_COLS), np.float32)
    for h in range(N_HEADS):
        for i in range(3):
            pp[8 * i + h, BIAS_LANES * h + i] = 1.0
            pp[8 * i + h, LANES + BIAS_LANES * h + 3 + i] = -1.0
            cc[0, BIAS_LANES * h + 3 + i] = 1.0
            cc[0, LANES + BIAS_LANES * h + i] = 1.0
    return pp, cc


def _log_sigmoid(x):
    return jnp.minimum(x, 0.0) - jnp.log(1.0 + jnp.exp(-jnp.abs(x)))


def _silu(x):
    return x * jax.nn.sigmoid(x)


def _rmsnorm(xf, w):
    ms = jnp.mean(xf * xf, axis=-1, keepdims=True)
    return xf * lax.rsqrt(ms + EPS) * w


def _project(h, wt_ref, wf_ref):
    def cols(ref, c0, n):
        return lax.dot_general(h, ref[c0:c0 + n, :], (((1,), (1,)), ((), ())),
                               preferred_element_type=F32)
    fl = cols(wf_ref, 0, LANES)
    k = cols(wt_ref, 2 * POOL_W + ATTN_W, ATTN_W)
    v = cols(wt_ref, 2 * POOL_W + 2 * ATTN_W, ATTN_W)
    q = cols(wt_ref, 2 * POOL_W, ATTN_W)
    gp = cols(wt_ref, POOL_W, POOL_W)
    ga = cols(wt_ref, 2 * POOL_W + 3 * ATTN_W, ATTN_W)
    u = cols(wt_ref, 0, POOL_W)
    return u, gp, q, k, v, ga, fl


def _proj_prompt_kernel(x_ref, nw_ref, wt_ref, wf_ref, bf_ref, pp_ref, cc_ref, seg_ref,
                        u_ref, sgp_ref, qa_ref, ka_ref, k_ref, v_ref, vb_ref, sga_ref, logf_ref,
                        st_ref, carry_ref, *, tm, ta, tiles_per_batch):
    i = pl.program_id(0)
    h = _rmsnorm(x_ref[...], nw_ref[...]).astype(BF16)
    u, gp, q, k, v, ga, fl = _project(h, wt_ref, wf_ref)
    u_ref[...] = u
    sgp_ref[...] = _silu(gp).astype(BF16)
    k_ref[0] = jnp.transpose(k)
    v_ref[0] = jnp.transpose(v)
    vb_ref[...] = v.astype(BF16)
    sga_ref[...] = _silu(ga).astype(BF16)
    logf = _log_sigmoid(fl + bf_ref[...])
    logf_ref[0] = jnp.transpose(logf)[0:N_HEADS, :]

    row = lax.broadcasted_iota(jnp.int32, (tm, LANES), 0)
    acc = logf
    d = 1
    while d < tm:
        acc = acc + jnp.where(row >= d, pltpu.roll(acc, d, axis=0), 0.0)
        d *= 2
    @pl.when(i % tiles_per_batch == 0)
    def _():
        carry_ref[...] = jnp.zeros(carry_ref.shape, F32)

    f_run = acc + carry_ref[0:1, :]
    carry_ref[...] = jnp.broadcast_to(f_run[tm - 1:tm, :], carry_ref.shape)

    f2 = f_run * LOG2E
    p1 = f2.astype(BF16).astype(F32)
    r1 = f2 - p1
    p2 = r1.astype(BF16).astype(F32)
    p3 = (r1 - p2).astype(BF16).astype(F32)
    lane = lax.broadcasted_iota(jnp.int32, (tm, LANES), 1)
    pieces = jnp.where(lane < 8, p1, jnp.where(lane < 16, p2, jnp.where(lane < 24, p3, 0.0)))
    bias = jnp.dot(pieces.astype(BF16), pp_ref[...], preferred_element_type=F32) + cc_ref[...]

    qs = q * Q_SCALE

    qn2 = jnp.dot((qs * qs).astype(BF16), seg_ref[...], preferred_element_type=F32)
    kn2 = jnp.dot((k * k).astype(BF16), seg_ref[...], preferred_element_type=F32)
    srow = lax.broadcasted_iota(jnp.int32, (8, LANES), 0)
    for sb in range(tm // ta):
        r0, r1 = sb * ta, (sb + 1) * ta
        qmax = jnp.sqrt(jnp.max(qn2[r0:r1], axis=0, keepdims=True))
        kmax = jnp.sqrt(jnp.max(kn2[r0:r1], axis=0, keepdims=True))
        st_ref[8 * sb:8 * (sb + 1), :] = jnp.where(
            srow == 0, qmax, jnp.where(srow == 1, kmax, jnp.where(
                srow == 2, f2[r0:r0 + 1, :], jnp.where(srow == 3, f2[r1 - 1:r1, :], 0.0))))

    first_half = lane < HEAD_DIM
    q_bias = bias[:, 0:LANES].astype(BF16)
    k_bias = bias[:, LANES:2 * LANES]
    for p in range(N_PAIRS):
        qa_ref[:, 2 * LANES * p:2 * LANES * p + LANES] = qs[:, LANES * p:LANES * (p + 1)].astype(BF16)
        qa_ref[:, 2 * LANES * p + LANES:2 * LANES * (p + 1)] = q_bias
        kp = k[:, LANES * p:LANES * (p + 1)]
        for e in range(2):
            hh = 2 * p + e
            keep = first_half if e == 0 else jnp.logical_not(first_half)
            ka_ref[:, 2 * LANES * hh:2 * LANES * hh + LANES] = jnp.where(keep, kp, 0.0).astype(BF16)
            own = jnp.logical_and(lane >= BIAS_LANES * hh, lane < BIAS_LANES * (hh + 1))
            ka_ref[:, 2 * LANES * hh + LANES:2 * LANES * (hh + 1)] = (
                jnp.where(own, k_bias, 0.0).astype(BF16))


def _proj_prompt(x2, nw, wt, wf, bft, pp, cc, seg, *, seq):
    rows = x2.shape[0]
    tm = min(PROJ_TM, seq)
    ta = min(ATTN_T, seq)
    assert seq % tm == 0 and rows % seq == 0 and tm % ta == 0 and tm % LANES == 0
    n_batch = rows // seq
    row_blk = lambda n: pl.BlockSpec((tm, n), lambda i: (i, 0))
    const = lambda shape: pl.BlockSpec(shape, lambda i: (0, 0))
    out_shape = (
        jax.ShapeDtypeStruct((rows, POOL_W), F32),
        jax.ShapeDtypeStruct((rows, POOL_W), BF16),
        jax.ShapeDtypeStruct((rows, N_PAIRS * 2 * LANES), BF16),
        jax.ShapeDtypeStruct((rows, N_HEADS * 2 * LANES), BF16),
        jax.ShapeDtypeStruct((n_batch, ATTN_W, seq), F32),
        jax.ShapeDtypeStruct((n_batch, ATTN_W, seq), F32),
        jax.ShapeDtypeStruct((rows, ATTN_W), BF16),
        jax.ShapeDtypeStruct((rows, ATTN_W), BF16),
        jax.ShapeDtypeStruct((n_batch, N_HEADS, seq), F32),
        jax.ShapeDtypeStruct((rows // ta * 8, LANES), F32),
    )
    tpb = seq // tm
    t_minor = lambda n: pl.BlockSpec((1, n, tm), lambda i: (i // tpb, 0, i % tpb))
    out_specs = (row_blk(POOL_W), row_blk(POOL_W), row_blk(N_PAIRS * 2 * LANES),
                 row_blk(N_HEADS * 2 * LANES), t_minor(ATTN_W), t_minor(ATTN_W), row_blk(ATTN_W),
                 row_blk(ATTN_W), t_minor(N_HEADS),
                 pl.BlockSpec((tm // ta * 8, LANES), lambda i: (i, 0)))
    return pl.pallas_call(
        functools.partial(_proj_prompt_kernel, tm=tm, ta=ta, tiles_per_batch=seq // tm),
        grid=(rows // tm,),
        in_specs=[row_blk(D_MODEL), const((1, D_MODEL)), const((W_MAIN, D_MODEL)),
                  const((LANES, D_MODEL)), const((1, LANES)),
                  const((LANES, BIAS_COLS)), const((1, BIAS_COLS)), const((ATTN_W, LANES))],
        out_specs=out_specs,
        out_shape=out_shape,
        scratch_shapes=[pltpu.VMEM((8, LANES), F32)],
        compiler_params=pltpu.CompilerParams(dimension_semantics=("arbitrary",),
                                             vmem_limit_bytes=VMEM_LIMIT),
        name="proj_prompt",
    )(x2, nw, wt, wf, bft, pp, cc, seg)


def _lane_tile(x, reps):
    return jnp.concatenate([x] * reps, axis=1)


def _plan_kernel(st_ref, o_ref, *, n_batch, n_blk):
    row = lax.broadcasted_iota(jnp.int32, (LANES, LANES), 0).astype(F32)
    col = lax.broadcasted_iota(jnp.int32, (LANES, LANES), 1).astype(F32)
    out = jnp.zeros((LANES, LANES), F32)
    for b in range(n_batch):
        def stat(r):
            v = st_ref[pl.ds((b * n_blk) * 8 + r, n_blk, stride=8), :]
            return jnp.concatenate([v, jnp.zeros((LANES - n_blk, LANES), F32)], axis=0)
        qm, km, ft0, fs1 = stat(0), stat(1), stat(2), stat(3)
        km_t, fs1_t = jnp.transpose(km), jnp.transpose(fs1)
        for p in range(N_PAIRS):
            first = row
            for h in (2 * p, 2 * p + 1):
                qcol = qm[:, h:h + 1]
                ub = (NORM_SLACK * (qcol * km_t[h:h + 1, :] + qcol * km[:, h:h + 1])
                      + ft0[:, h:h + 1] - fs1_t[h:h + 1, :])
                needed = jnp.logical_and(ub >= -SKIP_LOG2, col < row)
                first = jnp.minimum(first, jnp.where(needed, col, row))
            start = jnp.min(first, axis=1, keepdims=True)
            out = jnp.where(col == b * N_PAIRS + p, start, out)
    o_ref[...] = jnp.transpose(out).astype(jnp.int32)


def _plan(stats, *, n_batch, n_blk):
    assert n_blk <= LANES and n_batch * N_PAIRS <= LANES
    return pl.pallas_call(
        functools.partial(_plan_kernel, n_batch=n_batch, n_blk=n_blk),
        out_shape=jax.ShapeDtypeStruct((LANES, LANES), jnp.int32),
        name="attn_plan",
    )(stats)


def _attn_kernel(plan_ref, pt_ref, qa_ref, ka_ref, vb_ref, qd_ref, kn_ref, vn_ref, cn_ref,
                 ck_hbm, cv_hbm, clf_hbm, o_ref, od_ref, m_ref, l_ref, acc_ref, *dec_scratch,
                 t, nq, n_steps, dec):
    b, pr, i = pl.program_id(0), pl.program_id(1), pl.program_id(2)
    step = (b * N_PAIRS + pr) * nq + i
    dec_prologue, dec_chunk, n_chunks = _decode_fns(pt_ref, qd_ref, kn_ref, vn_ref, cn_ref, ck_hbm,
                                                    cv_hbm, clf_hbm, od_ref, *dec_scratch, **dec)
    per_step = -(-n_chunks // n_steps)

    @pl.when(step == 0)
    def _():
        dec_prologue()

    for k in range(per_step):
        g = step * per_step + k

        @pl.when(g < n_chunks)
        def _(g=g):
            dec_chunk(g)

    q = qa_ref[...]
    m_ref[...] = jnp.full(m_ref.shape, NEG, F32)
    l_ref[...] = jnp.zeros(l_ref.shape, F32)
    acc_ref[...] = jnp.zeros(acc_ref.shape, F32)

    def block(kstart, width, diag_col):
        vblk = vb_ref[pl.ds(kstart, width), :]
        for e in range(2):
            kblk = ka_ref[pl.ds(kstart, width), 2 * LANES * e:2 * LANES * (e + 1)]
            s = lax.dot_general(q, kblk, (((1,), (1,)), ((), ())), preferred_element_type=F32)
            if diag_col is not None:
                r = lax.broadcasted_iota(jnp.int32, (t, width), 0)
                c = lax.broadcasted_iota(jnp.int32, (t, width), 1)
                s = jnp.where(c - diag_col <= r, s, NEG)
            m_prev = m_ref[e]
            m_next = jnp.maximum(m_prev, jnp.max(s, axis=1, keepdims=True))
            alpha = jnp.exp2(m_prev - m_next)
            p = jnp.exp2(s - _lane_tile(m_next, width // LANES))
            l_ref[e] = alpha * l_ref[e] + jnp.sum(p, axis=1, keepdims=True)
            m_ref[e] = m_next
            acc_ref[e] = alpha * acc_ref[e] + jnp.dot(p.astype(BF16), vblk,
                                                      preferred_element_type=F32)

    def body(j, carry):
        block(pl.multiple_of(j * t, t), t, None)
        return carry

    first = plan_ref[(b * N_PAIRS + pr) * nq + i]
    one_before = jnp.logical_and(i > 0, first == i - 1)

    @pl.when(one_before)
    def _():
        block(pl.multiple_of((i - 1) * t, t), 2 * t, t)

    @pl.when(jnp.logical_not(one_before))
    def _():
        lax.fori_loop(first, i, body, 0)
        block(pl.multiple_of(i * t, t), t, 0)

    lane = lax.broadcasted_iota(jnp.int32, (t, LANES), 1)
    o = jnp.where(lane < HEAD_DIM, acc_ref[0] / l_ref[0], acc_ref[1] / l_ref[1])
    o_ref[...] = o.astype(o_ref.dtype)


def _attention(plan, qa, ka, vb, page_table, q3, kn3, vn3, cn3, ck, cv, clf_t, *, n_batch, seq):
    t = min(ATTN_T, seq)
    assert seq % t == 0
    nq = seq // t
    nb, n_new, _ = q3.shape
    dec, dec_scratch = _decode_setup(nb, n_new, page_table.shape[1], ck.shape[3])
    whole = lambda n: pl.BlockSpec((nb, n_new, n), lambda b, p, i, plan, pt: (0, 0, 0))
    hbm = pl.BlockSpec(memory_space=pl.ANY)
    grid_spec = pltpu.PrefetchScalarGridSpec(
        num_scalar_prefetch=2,
        grid=(n_batch, N_PAIRS, nq),
        in_specs=[pl.BlockSpec((t, 2 * LANES), lambda b, p, i, plan, pt: (b * nq + i, p)),
                  pl.BlockSpec((seq, 4 * LANES), lambda b, p, i, plan, pt: (b, p)),
                  pl.BlockSpec((seq, LANES), lambda b, p, i, plan, pt: (b, p)),
                  whole(ATTN_W), whole(ATTN_W), whole(ATTN_W), whole(LANES), hbm, hbm, hbm],
        out_specs=(pl.BlockSpec((t, LANES), lambda b, p, i, plan, pt: (b * nq + i, p)),
                   whole(ATTN_W)),
        scratch_shapes=[pltpu.VMEM((2, t, LANES), F32), pltpu.VMEM((2, t, LANES), F32),
                        pltpu.VMEM((2, t, LANES), F32)] + dec_scratch)
    return pl.pallas_call(
        functools.partial(_attn_kernel, t=t, nq=nq, n_steps=n_batch * N_PAIRS * nq, dec=dec),
        grid_spec=grid_spec,
        out_shape=(jax.ShapeDtypeStruct((n_batch * seq, ATTN_W), BF16),
                   jax.ShapeDtypeStruct((nb, n_new, ATTN_W), F32)),
        compiler_params=pltpu.CompilerParams(
            dimension_semantics=("arbitrary", "arbitrary", "arbitrary"),
            vmem_limit_bytes=VMEM_LIMIT),
        name="attention",
    )(plan, page_table.reshape(-1), qa, ka, vb, q3, kn3, vn3, cn3, ck, cv, clf_t)


def _merge(x, pooled, sgp, attn, sga, pw_ref, ps, wo_ref, nf):
    mixed = [jnp.dot(pooled[:, POOL_GC * g:POOL_GC * (g + 1)].astype(BF16), pw_ref[g],
                     preferred_element_type=F32) for g in range(len(POOL_WINDOWS))]
    pool_out = jnp.concatenate(mixed, axis=1) * ps
    mix = jnp.concatenate([(pool_out * sgp.astype(F32)).astype(BF16),
                           (attn.astype(F32) * sga.astype(F32)).astype(BF16)], axis=1)
    xo = x + jnp.dot(mix, wo_ref[...], preferred_element_type=F32)
    return _rmsnorm(xo, nf)


def _merge_prompt_kernel(x_ref, u_ref, uh_ref, sgp_ref, attn_ref, sga_ref, pw_ref, ps_ref, wo_ref,
                         nf_ref, y_ref, pool_ref, *, tm, tiles_per_batch):
    i = pl.program_id(0)
    ti = i % tiles_per_batch
    u = u_ref[...]
    halo = jnp.where(ti == 0, 0.0, uh_ref[...])
    ext = jnp.concatenate([halo, u], axis=0)
    pos = ti * tm + lax.broadcasted_iota(jnp.int32, (tm, 1), 0)
    groups = []
    level = ext
    shift = 1
    for g, w in enumerate(POOL_WINDOWS):
        while shift < w:
            level = level + pltpu.roll(level, shift, axis=0)
            shift *= 2
        cnt = jnp.minimum(w, pos + 1).astype(F32)
        sl = slice(POOL_GC * g, POOL_GC * (g + 1))
        groups.append(level[16:, sl] / cnt - u[:, sl])
    pooled = jnp.concatenate(groups, axis=1)
    y_ref[...] = _merge(x_ref[...], pooled, sgp_ref[...], attn_ref[...], sga_ref[...], pw_ref,
                        ps_ref[...], wo_ref, nf_ref[...])

    @pl.when(ti == tiles_per_batch - 1)
    def _():
        pool_ref[0] = u_ref[pl.ds(tm - POOL_HIST, POOL_HIST), :]


def _merge_prompt(x2, u, sgp, attn, sga, pw, ps, wo, nf, *, n_batch, seq):
    rows = x2.shape[0]
    tm = min(PROJ_TM, seq)
    tpb = seq // tm
    halo_blocks = tm // 16
    row_blk = lambda n: pl.BlockSpec((tm, n), lambda i: (i, 0))
    const2 = lambda shape: pl.BlockSpec(shape, lambda i: (0, 0))
    return pl.pallas_call(
        functools.partial(_merge_prompt_kernel, tm=tm, tiles_per_batch=tpb),
        grid=(rows // tm,),
        in_specs=[row_blk(D_MODEL), row_blk(POOL_W),
                  pl.BlockSpec((16, POOL_W), lambda i: (jnp.maximum(i * halo_blocks - 1, 0), 0)),
                  row_blk(POOL_W), row_blk(ATTN_W), row_blk(ATTN_W),
                  pl.BlockSpec((len(POOL_WINDOWS), POOL_GC, POOL_GC), lambda i: (0, 0, 0)),
                  const2((1, POOL_W)), const2((D_MODEL, D_MODEL)), const2((1, D_MODEL))],
        out_specs=(row_blk(D_MODEL),
                   pl.BlockSpec((1, POOL_HIST, POOL_W), lambda i: (i // tpb, 0, 0))),
        out_shape=(jax.ShapeDtypeStruct((rows, D_MODEL), F32),
                   jax.ShapeDtypeStruct((n_batch, POOL_HIST, POOL_W), F32)),
        compiler_params=pltpu.CompilerParams(dimension_semantics=("arbitrary",),
                                             vmem_limit_bytes=VMEM_LIMIT),
        name="merge_prompt",
    )(x2, u, u, sgp, attn, sga, pw, ps, wo, nf)


def _proj_sample_kernel(x_ref, nw_ref, wt_ref, wf_ref, bf_ref,
                        u_ref, sgp_ref, q_ref, k_ref, v_ref, sga_ref, logf_ref, cn_ref, *, n_new):
    nb = x_ref.shape[0]
    x = jnp.concatenate([x_ref[:, D_MODEL * i:D_MODEL * (i + 1)] for i in range(n_new)], axis=0)
    h = _rmsnorm(x, nw_ref[...]).astype(BF16)
    u, gp, q, k, v, ga, fl = _project(h, wt_ref, wf_ref)
    logf = _log_sigmoid(fl + bf_ref[...])
    outs = ((u_ref, u), (sgp_ref, _silu(gp).astype(BF16)), (q_ref, q * Q_SCALE), (k_ref, k),
            (v_ref, v), (sga_ref, _silu(ga).astype(BF16)), (logf_ref, logf))
    cn = jnp.zeros((nb, LANES), F32)
    for i in range(n_new):
        for ref, val in outs:
            n = val.shape[1]
            ref[:, n * i:n * (i + 1)] = val[nb * i:nb * (i + 1), :]
        cn = cn + logf[nb * i:nb * (i + 1), :]
        cn_ref[:, LANES * i:LANES * (i + 1)] = cn


def _proj_sample(xs2, nw, wt, wf, bft, *, n_new):
    nb = xs2.shape[0]
    widths = (POOL_W, POOL_W, ATTN_W, ATTN_W, ATTN_W, ATTN_W, LANES, LANES)
    dtypes = (F32, BF16, F32, F32, F32, BF16, F32, F32)
    return pl.pallas_call(
        functools.partial(_proj_sample_kernel, n_new=n_new),
        out_shape=tuple(jax.ShapeDtypeStruct((nb, n_new * n), dt) for n, dt in zip(widths, dtypes)),
        compiler_params=pltpu.CompilerParams(vmem_limit_bytes=VMEM_LIMIT),
        name="proj_sample",
    )(xs2, nw, wt, wf, bft)


def _decode_fns(pt_ref, q_ref, kn_ref, vn_ref, cn_ref, ck_hbm, cv_hbm, clf_hbm, o_ref,
                kbuf, vpre, vdem, lfbuf, ksem, lfsem, vsem,
                qbd_ref, cn8_ref, cncol_ref, m_ref, l_ref, acc_ref, tail_ref,
                *, nb, n_new, n_pages, page, pps, vsub):
    n_chunks = n_pages // pps
    n_groups = pps // vsub
    total = nb * n_chunks
    n_keys = pps * page
    g_keys = vsub * page
    rows = n_new * N_HEADS
    row_w = lax.broadcasted_iota(jnp.int32, (rows, ATTN_W), 0)
    lane_w = lax.broadcasted_iota(jnp.int32, (rows, ATTN_W), 1)
    head_lanes = (row_w % N_HEADS) == (lane_w // HEAD_DIM)
    row1 = lax.broadcasted_iota(jnp.int32, (rows, 1), 0)

    def page_index(g, r):
        b, c = lax.div(g, n_chunks), lax.rem(g, n_chunks)
        return pt_ref[b * n_pages + n_pages - 1 - (c * pps + r)]

    def k_copies(g, slot):
        cps = []
        for r in range(pps):
            idx = page_index(g, r)
            cps.append(pltpu.make_async_copy(ck_hbm.at[idx], kbuf.at[slot, r], ksem.at[slot]))
            cps.append(pltpu.make_async_copy(clf_hbm.at[idx], lfbuf.at[slot, r], lfsem.at[slot]))
        return cps

    def v_copies(g, grp, dst, sem):
        return [pltpu.make_async_copy(cv_hbm.at[page_index(g, grp * vsub + r)], dst.at[r], sem)
                for r in range(vsub)]

    def init(b):
        q = q_ref[b]
        qrep = jnp.concatenate(
            [jnp.broadcast_to(q[i:i + 1, :], (N_HEADS, ATTN_W)) for i in range(n_new)], axis=0)
        qbd = jnp.where(head_lanes, qrep, 0.0)
        qbd_ref[...] = qbd.astype(BF16)
        cn8_ref[...] = jnp.zeros(cn8_ref.shape, F32)
        cn8_ref[0:n_new, :] = cn_ref[b]
        cnt = jnp.transpose(cn8_ref[...])[0:rows, :] * LOG2E
        cncol = jnp.zeros((rows, 1), F32)
        for i in range(n_new):
            cncol = jnp.where(row1 // N_HEADS == i, cnt[:, i:i + 1], cncol)
        cncol_ref[...] = jnp.broadcast_to(cncol, cncol_ref.shape)
        kn = kn_ref[b]
        vn = vn_ref[b]
        s_new = []
        for j in range(n_new):
            sj = jnp.sum(qbd * kn[j:j + 1, :], axis=1, keepdims=True) + cncol - cnt[:, j:j + 1]
            s_new.append(jnp.where(row1 // N_HEADS >= j, sj, NEG))
        m0 = functools.reduce(jnp.maximum, s_new)
        l0 = jnp.zeros((rows, 1), F32)
        a0 = jnp.zeros((rows, ATTN_W), F32)
        for j in range(n_new):
            pj = jnp.exp2(s_new[j] - m0)
            l0 = l0 + pj
            a0 = a0 + pj * vn[j:j + 1, :]
        m_ref[...] = jnp.broadcast_to(m0, m_ref.shape)
        l_ref[...] = jnp.broadcast_to(l0, l_ref.shape)
        acc_ref[...] = a0
        tail_ref[...] = jnp.zeros(tail_ref.shape, F32)

    def finish(b):
        o = acc_ref[...] / _lane_tile(l_ref[...], ATTN_W // LANES)
        o = jnp.where(head_lanes, o, 0.0)
        for i in range(n_new):
            o_ref[b, i:i + 1, :] = jnp.sum(o[N_HEADS * i:N_HEADS * (i + 1), :], axis=0,
                                           keepdims=True)

    lane8 = lax.broadcasted_iota(jnp.int32, (N_HEADS, page), 1)

    def chunk(g):
        slot = lax.rem(g, 2)
        b, c = lax.div(g, n_chunks), lax.rem(g, n_chunks)

        @pl.when(g + 1 < total)
        def _():
            for cp in k_copies(g + 1, 1 - slot):
                cp.start()

        @pl.when(c == 0)
        def _():
            init(b)
            for cp in v_copies(g, 0, vpre, vsem.at[0]):
                cp.wait()

        for cp in k_copies(g, slot):
            cp.wait()

        kcat = jnp.concatenate([kbuf[slot, r].reshape(ATTN_W, page).astype(BF16)
                                for r in range(pps)], axis=1)
        qk = jnp.dot(qbd_ref[...], kcat, preferred_element_type=F32)
        tail = tail_ref[...]
        lfs, tails = [], []
        for r in range(pps):
            lf = lfbuf[slot, r] * LOG2E
            lfs.append(lf)
            tails.append(tail)
            tail = tail + jnp.sum(lf, axis=1, keepdims=True)
        tail_ref[...] = tail
        base = qk + jnp.concatenate([jnp.concatenate(tails, axis=1)] * n_new, axis=0) + cncol_ref[:, 0:1]
        m_prev = m_ref[...]
        over = base - _lane_tile(m_prev, n_keys // LANES)
        gaps = [jnp.max(over[:, g_keys * h:g_keys * (h + 1)]) for h in range(n_groups)]
        live = [gp >= -SKIP_LOG2 for gp in gaps]

        @pl.when(functools.reduce(jnp.logical_or, live))
        def _():
            g_parts = []
            for r in range(pps):
                incl = lfs[r]
                d = 1
                while d < page:
                    incl = incl + jnp.where(lane8 + d < page, pltpu.roll(incl, page - d, axis=1), 0.0)
                    d *= 2
                g_parts.append(incl - lfs[r])
            s = base + jnp.concatenate([jnp.concatenate(g_parts, axis=1)] * n_new, axis=0)
            m_next = jnp.maximum(m_prev, jnp.max(s, axis=1, keepdims=True))
            alpha = jnp.exp2(m_prev - m_next)
            p = jnp.exp2(s - _lane_tile(m_next, n_keys // LANES))
            l_ref[...] = alpha * l_ref[...] + jnp.sum(p, axis=1, keepdims=True)
            m_ref[...] = m_next
            acc_ref[...] = _lane_tile(alpha, ATTN_W // LANES) * acc_ref[...]
            pb = p.astype(BF16)

            def add_pv(h, vsrc):
                vcat = jnp.concatenate([vsrc[r].reshape(ATTN_W, page).astype(BF16)
                                        for r in range(vsub)], axis=1)
                acc_ref[...] += lax.dot_general(pb[:, g_keys * h:g_keys * (h + 1)], vcat,
                                                (((1,), (1,)), ((), ())),
                                                preferred_element_type=F32)

            def fetch_and_add(h):
                cps = v_copies(g, h, vdem, vsem.at[1])
                for cp in cps:
                    cp.start()
                for cp in cps:
                    cp.wait()
                add_pv(h, vdem)

            for h in range(n_groups):
                if h == 0:
                    @pl.when(jnp.logical_and(live[0], c == 0))
                    def _():
                        add_pv(0, vpre)

                    @pl.when(jnp.logical_and(live[0], c > 0))
                    def _():
                        fetch_and_add(0)
                else:
                    @pl.when(live[h])
                    def _(h=h):
                        fetch_and_add(h)

        @pl.when(c == n_chunks - 1)
        def _():
            finish(b)

        @pl.when(jnp.logical_and(c == 0, b + 1 < nb))
        def _():
            for cp in v_copies(g + n_chunks, 0, vpre, vsem.at[0]):
                cp.start()

    def prologue():
        for cp in k_copies(0, 0):
            cp.start()
        for cp in v_copies(0, 0, vpre, vsem.at[0]):
            cp.start()

    return prologue, chunk, total


def _decode_setup(nb, n_new, n_pages, page):
    pps = DEC_PAGES
    while n_pages % pps:
        pps //= 2
    vsub = min(DEC_VSUB, pps)
    rows = n_new * N_HEADS
    assert page % LANES == 0 and rows % 8 == 0 and pps % vsub == 0
    params = dict(nb=nb, n_new=n_new, n_pages=n_pages, page=page, pps=pps, vsub=vsub)
    scratch = [pltpu.VMEM((2, pps, N_HEADS, HEAD_DIM, page), F32),
               pltpu.VMEM((vsub, N_HEADS, HEAD_DIM, page), F32),
               pltpu.VMEM((vsub, N_HEADS, HEAD_DIM, page), F32),
               pltpu.VMEM((2, pps, N_HEADS, page), F32),
               pltpu.SemaphoreType.DMA((2,)),
               pltpu.SemaphoreType.DMA((2,)),
               pltpu.SemaphoreType.DMA((2,)),
               pltpu.VMEM((rows, ATTN_W), BF16),
               pltpu.VMEM((8, LANES), F32),
               pltpu.VMEM((rows, LANES), F32),
               pltpu.VMEM((rows, LANES), F32),
               pltpu.VMEM((rows, LANES), F32),
               pltpu.VMEM((rows, ATTN_W), F32),
               pltpu.VMEM((N_HEADS, LANES), F32)]
    return params, scratch


def _merge_sample_kernel(x_ref, u_ref, sp_ref, sgp_ref, attn_ref, sga_ref, pw_ref, ps_ref, wo_ref,
                         nf_ref, y_ref, pool_ref, *, n_new, past):
    u = [u_ref[:, POOL_W * i:POOL_W * (i + 1)] for i in range(n_new)]
    hist = [sp_ref[r] for r in range(POOL_HIST)]
    ext = hist + u
    pooled_rows = []
    for i in range(n_new):
        groups = []
        for g, w in enumerate(POOL_WINDOWS):
            sl = slice(POOL_GC * g, POOL_GC * (g + 1))
            end = POOL_HIST + i
            total = ext[end][:, sl]
            for r in range(end - w + 1, end):
                total = total + ext[r][:, sl]
            groups.append(total / float(min(w, past + i + 1)) - u[i][:, sl])
        pooled_rows.append(jnp.concatenate(groups, axis=1))
    pooled = jnp.concatenate(pooled_rows, axis=0)
    cat = lambda ref, n: jnp.concatenate([ref[:, n * i:n * (i + 1)] for i in range(n_new)], axis=0)
    y = _merge(cat(x_ref, D_MODEL), pooled, cat(sgp_ref, POOL_W), cat(attn_ref, ATTN_W),
               cat(sga_ref, ATTN_W), pw_ref, ps_ref[...], wo_ref, nf_ref[...])
    nb = x_ref.shape[0]
    for i in range(n_new):
        y_ref[:, D_MODEL * i:D_MODEL * (i + 1)] = y[nb * i:nb * (i + 1), :]
    new_hist = ext[-POOL_HIST:]
    for r in range(POOL_HIST):
        pool_ref[:, POOL_W * r:POOL_W * (r + 1)] = new_hist[r]


def _merge_sample(xs2, u2, sp2, sgp2, attn2, sga2, pw, ps, wo, nf, *, n_new, past):
    nb = xs2.shape[0]
    return pl.pallas_call(
        functools.partial(_merge_sample_kernel, n_new=n_new, past=past),
        out_shape=(jax.ShapeDtypeStruct((nb, n_new * D_MODEL), F32),
                   jax.ShapeDtypeStruct((nb, POOL_HIST * POOL_W), F32)),
        compiler_params=pltpu.CompilerParams(vmem_limit_bytes=VMEM_LIMIT),
        name="merge_sample",
    )(xs2, u2, sp2, sgp2, attn2, sga2, pw, ps, wo, nf)


def kernel(x_prompt, x_sample, cache_k, cache_v, cache_logf, state_pool, page_table,
           norm_w, w_in, b_forget, pool_w, pool_scale, w_out, norm_f):
    depth = norm_w.shape[0]
    assert depth == 1, "a single layer is supported"
    b_p, seq, _ = x_prompt.shape
    b_s, t_s, _ = x_sample.shape
    n_phys, page = cache_k.shape[1], cache_k.shape[2]
    past = page_table.shape[1] * page
    ll = 0

    wt_all = jnp.swapaxes(w_in[ll], 0, 1)
    wt = wt_all[:W_MAIN].astype(BF16)
    wf = jnp.tile(wt_all[W_MAIN:], (LANES // N_HEADS, 1)).astype(BF16)
    bft = jnp.tile(b_forget[ll], LANES // N_HEADS).reshape(1, LANES).astype(F32)
    nw = norm_w[ll].reshape(1, D_MODEL)
    nf = norm_f.reshape(1, D_MODEL)
    pw = pool_w[ll].astype(BF16)
    ps = pool_scale[ll].reshape(1, POOL_W)
    wo = w_out[ll].astype(BF16)
    pp_np, cc_np = _placement()
    pp = jnp.asarray(pp_np, BF16)
    cc = jnp.asarray(cc_np, F32)

    xp2 = x_prompt.reshape(b_p * seq, D_MODEL)
    seg_np = (np.arange(ATTN_W)[:, None] // HEAD_DIM == np.arange(LANES)[None, :] % N_HEADS)
    seg = jnp.asarray(seg_np, BF16)
    u, sgp, qa, ka, kf, vf, vb, sga, logf, stats = _proj_prompt(xp2, nw, wt, wf, bft, pp, cc, seg,
                                                                seq=seq)
    n_blk = seq // min(ATTN_T, seq)
    plan = _plan(stats, n_batch=b_p, n_blk=n_blk)[:b_p * N_PAIRS, :n_blk].reshape(-1)

    xs2 = x_sample.reshape(b_s, t_s * D_MODEL)
    us, sgps, qs, ks, vs, sgas, lfs, cns = _proj_sample(xs2, nw, wt, wf, bft, n_new=t_s)
    ck = jnp.transpose(cache_k[ll], (0, 2, 3, 1))
    cv = jnp.transpose(cache_v[ll], (0, 2, 3, 1))
    clf_t = jnp.swapaxes(cache_logf[ll], 1, 2)

    attn, attn_s = _attention(plan, qa, ka, vb, page_table, qs.reshape(b_s, t_s, ATTN_W),
                              ks.reshape(b_s, t_s, ATTN_W), vs.reshape(b_s, t_s, ATTN_W),
                              cns.reshape(b_s, t_s, LANES), ck, cv, clf_t, n_batch=b_p, seq=seq)
    yp, pool_p = _merge_prompt(xp2, u, sgp, attn, sga, pw, ps, wo, nf, n_batch=b_p, seq=seq)
    ys, pool_s = _merge_sample(xs2, us, jnp.swapaxes(state_pool[ll], 0, 1), sgps,
                               attn_s.reshape(b_s, t_s * ATTN_W), sgas, pw, ps, wo, nf,
                               n_new=t_s, past=past)

    return (yp.reshape(b_p, seq, D_MODEL),
            ys.reshape(b_s, t_s, D_MODEL),
            jnp.transpose(kf.reshape(1, b_p, N_HEADS, HEAD_DIM, seq), (0, 1, 4, 2, 3)),
            jnp.transpose(vf.reshape(1, b_p, N_HEADS, HEAD_DIM, seq), (0, 1, 4, 2, 3)),
            jnp.transpose(logf, (0, 2, 1)).reshape(1, b_p, seq, N_HEADS),
            pool_p.reshape(1, b_p, POOL_HIST, POOL_W),
            ks.reshape(1, b_s, t_s, N_HEADS, HEAD_DIM),
            vs.reshape(1, b_s, t_s, N_HEADS, HEAD_DIM),
            lfs.reshape(b_s, t_s, LANES)[:, :, :N_HEADS].reshape(1, b_s, t_s, N_HEADS),
            pool_s.reshape(1, b_s, POOL_HIST, POOL_W))
```

```python
import functools
import math

import numpy as np
import jax
import jax.numpy as jnp
from jax import lax
from jax.experimental import pallas as pl
from jax.experimental.pallas import tpu as pltpu

D_MODEL = 1024
POOL_W = 512
ATTN_W = 512
N_HEADS = 8
HEAD_DIM = 64
POOL_WINDOWS = (2, 4, 8, 16)
POOL_GC = 128
POOL_HIST = 15
EPS = 1e-6

LANES = 128
LOG2E = 1.4426950408889634
Q_SCALE = LOG2E / math.sqrt(HEAD_DIM)
NEG = -1e30
N_PAIRS = N_HEADS // 2
W_MAIN = 2 * POOL_W + 4 * ATTN_W
BIAS_LANES = LANES // N_HEADS
BIAS_COLS = 2 * LANES
VMEM_LIMIT = 56 * 1024 * 1024

SKIP_LOG2 = 152.0
NORM_SLACK = 1.02

PROJ_TM = 512
ATTN_T = 512
DEC_PAGES = 16
DEC_VSUB = 8

BF16 = jnp.bfloat16
F32 = jnp.float32


def _placement():
    pp = np.zeros((LANES, BIAS_COLS), np.float32)
    cc = np.zeros((1, BIAS_COLS), np.float32)
    for h in range(N_HEADS):
        for i in range(3):
            pp[8 * i + h, BIAS_LANES * h + i] = 1.0
            pp[8 * i + h, LANES + BIAS_LANES * h + 3 + i] = -1.0
            cc[0, BIAS_LANES * h + 3 + i] = 1.0
            cc[0, LANES + BIAS_LANES * h + i] = 1.0
    return pp, cc


def _log_sigmoid(x):
    return jnp.minimum(x, 0.0) - jnp.log(1.0 + jnp.exp(-jnp.abs(x)))


def _silu(x):
    return x * jax.nn.sigmoid(x)


def _rmsnorm(xf, w):
    ms = jnp.mean(xf * xf, axis=-1, keepdims=True)
    return xf * lax.rsqrt(ms + EPS) * w


def _project(h, wt_ref, wf_ref):
    def cols(ref, c0, n):
        return lax.dot_general(h, ref[c0:c0 + n, :], (((1,), (1,)), ((), ())),
                               preferred_element_type=F32)
    fl = cols(wf_ref, 0, LANES)
    k = cols(wt_ref, 2 * POOL_W + ATTN_W, ATTN_W)
    v = cols(wt_ref, 2 * POOL_W + 2 * ATTN_W, ATTN_W)
    q = cols(wt_ref, 2 * POOL_W, ATTN_W)
    gp = cols(wt_ref, POOL_W, POOL_W)
    ga = cols(wt_ref, 2 * POOL_W + 3 * ATTN_W, ATTN_W)
    u = cols(wt_ref, 0, POOL_W)
    return u, gp, q, k, v, ga, fl


def _proj_prompt_kernel(x_ref, nw_ref, wt_ref, wf_ref, bf_ref, pp_ref, cc_ref, seg_ref,
                        u_ref, sgp_ref, qa_ref, ka_ref, k_ref, v_ref, vb_ref, sga_ref, logf_ref,
                        st_ref, carry_ref, *, tm, ta, tiles_per_batch):
    i = pl.program_id(0)
    h = _rmsnorm(x_ref[...], nw_ref[...]).astype(BF16)
    u, gp, q, k, v, ga, fl = _project(h, wt_ref, wf_ref)
    u_ref[...] = u
    sgp_ref[...] = _silu(gp).astype(BF16)
    k_ref[0] = jnp.transpose(k)
    v_ref[0] = jnp.transpose(v)
    vb_ref[...] = v.astype(BF16)
    sga_ref[...] = _silu(ga).astype(BF16)
    logf = _log_sigmoid(fl + bf_ref[...])
    logf_ref[0] = jnp.transpose(logf)[0:N_HEADS, :]

    row = lax.broadcasted_iota(jnp.int32, (tm, LANES), 0)
    acc = logf
    d = 1
    while d < tm:
        acc = acc + jnp.where(row >= d, pltpu.roll(acc, d, axis=0), 0.0)
        d *= 2
    @pl.when(i % tiles_per_batch == 0)
    def _():
        carry_ref[...] = jnp.zeros(carry_ref.shape, F32)

    f_run = acc + carry_ref[0:1, :]
    carry_ref[...] = jnp.broadcast_to(f_run[tm - 1:tm, :], carry_ref.shape)

    f2 = f_run * LOG2E
    p1 = f2.astype(BF16).astype(F32)
    r1 = f2 - p1
    p2 = r1.astype(BF16).astype(F32)
    p3 = (r1 - p2).astype(BF16).astype(F32)
    lane = lax.broadcasted_iota(jnp.int32, (tm, LANES), 1)
    pieces = jnp.where(lane < 8, p1, jnp.where(lane < 16, p2, jnp.where(lane < 24, p3, 0.0)))
    bias = jnp.dot(pieces.astype(BF16), pp_ref[...], preferred_element_type=F32) + cc_ref[...]

    qs = q * Q_SCALE

    qn2 = jnp.dot((qs * qs).astype(BF16), seg_ref[...], preferred_element_type=F32)
    kn2 = jnp.dot((k * k).astype(BF16), seg_ref[...], preferred_element_type=F32)
    srow = lax.broadcasted_iota(jnp.int32, (8, LANES), 0)
    for sb in range(tm // ta):
        r0, r1 = sb * ta, (sb + 1) * ta
        qmax = jnp.sqrt(jnp.max(qn2[r0:r1], axis=0, keepdims=True))
        kmax = jnp.sqrt(jnp.max(kn2[r0:r1], axis=0, keepdims=True))
        st_ref[8 * sb:8 * (sb + 1), :] = jnp.where(
            srow == 0, qmax, jnp.where(srow == 1, kmax, jnp.where(
                srow == 2, f2[r0:r0 + 1, :], jnp.where(srow == 3, f2[r1 - 1:r1, :], 0.0))))

    first_half = lane < HEAD_DIM
    q_bias = bias[:, 0:LANES].astype(BF16)
    k_bias = bias[:, LANES:2 * LANES]
    for p in range(N_PAIRS):
        qa_ref[:, 2 * LANES * p:2 * LANES * p + LANES] = qs[:, LANES * p:LANES * (p + 1)].astype(BF16)
        qa_ref[:, 2 * LANES * p + LANES:2 * LANES * (p + 1)] = q_bias
        kp = k[:, LANES * p:LANES * (p + 1)]
        for e in range(2):
            hh = 2 * p + e
            keep = first_half if e == 0 else jnp.logical_not(first_half)
            ka_ref[:, 2 * LANES * hh:2 * LANES * hh + LANES] = jnp.where(keep, kp, 0.0).astype(BF16)
            own = jnp.logical_and(lane >= BIAS_LANES * hh, lane < BIAS_LANES * (hh + 1))
            ka_ref[:, 2 * LANES * hh + LANES:2 * LANES * (hh + 1)] = (
                jnp.where(own, k_bias, 0.0).astype(BF16))


def _proj_prompt(x2, nw, wt, wf, bft, pp, cc, seg, *, seq):
    rows = x2.shape[0]
    tm = min(PROJ_TM, seq)
    ta = min(ATTN_T, seq)
    assert seq % tm == 0 and rows % seq == 0 and tm % ta == 0 and tm % LANES == 0
    n_batch = rows // seq
    row_blk = lambda n: pl.BlockSpec((tm, n), lambda i: (i, 0))
    const = lambda shape: pl.BlockSpec(shape, lambda i: (0, 0))
    out_shape = (
        jax.ShapeDtypeStruct((rows, POOL_W), F32),
        jax.ShapeDtypeStruct((rows, POOL_W), BF16),
        jax.ShapeDtypeStruct((rows, N_PAIRS * 2 * LANES), BF16),
        jax.ShapeDtypeStruct((rows, N_HEADS * 2 * LANES), BF16),
        jax.ShapeDtypeStruct((n_batch, ATTN_W, seq), F32),
        jax.ShapeDtypeStruct((n_batch, ATTN_W, seq), F32),
        jax.ShapeDtypeStruct((rows, ATTN_W), BF16),
        jax.ShapeDtypeStruct((rows, ATTN_W), BF16),
        jax.ShapeDtypeStruct((n_batch, N_HEADS, seq), F32),
        jax.ShapeDtypeStruct((rows // ta * 8, LANES), F32),
    )
    tpb = seq // tm
    t_minor = lambda n: pl.BlockSpec((1, n, tm), lambda i: (i // tpb, 0, i % tpb))
    out_specs = (row_blk(POOL_W), row_blk(POOL_W), row_blk(N_PAIRS * 2 * LANES),
                 row_blk(N_HEADS * 2 * LANES), t_minor(ATTN_W), t_minor(ATTN_W), row_blk(ATTN_W),
                 row_blk(ATTN_W), t_minor(N_HEADS),
                 pl.BlockSpec((tm // ta * 8, LANES), lambda i: (i, 0)))
    return pl.pallas_call(
        functools.partial(_proj_prompt_kernel, tm=tm, ta=ta, tiles_per_batch=seq // tm),
        grid=(rows // tm,),
        in_specs=[row_blk(D_MODEL), const((1, D_MODEL)), const((W_MAIN, D_MODEL)),
                  const((LANES, D_MODEL)), const((1, LANES)),
                  const((LANES, BIAS_COLS)), const((1, BIAS_COLS)), const((ATTN_W, LANES))],
        out_specs=out_specs,
        out_shape=out_shape,
        scratch_shapes=[pltpu.VMEM((8, LANES), F32)],
        compiler_params=pltpu.CompilerParams(dimension_semantics=("arbitrary",),
                                             vmem_limit_bytes=VMEM_LIMIT),
        name="proj_prompt",
    )(x2, nw, wt, wf, bft, pp, cc, seg)


def _lane_tile(x, reps):
    return jnp.concatenate([x] * reps, axis=1)


def _plan_kernel(st_ref, o_ref, *, n_batch, n_blk):
    row = lax.broadcasted_iota(jnp.int32, (LANES, LANES), 0).astype(F32)
    col = lax.broadcasted_iota(jnp.int32, (LANES, LANES), 1).astype(F32)
    out = jnp.zeros((LANES, LANES), F32)
    for b in range(n_batch):
        def stat(r):
            v = st_ref[pl.ds((b * n_blk) * 8 + r, n_blk, stride=8), :]
            return jnp.concatenate([v, jnp.zeros((LANES - n_blk, LANES), F32)], axis=0)
        qm, km, ft0, fs1 = stat(0), stat(1), stat(2), stat(3)
        km_t, fs1_t = jnp.transpose(km), jnp.transpose(fs1)
        for p in range(N_PAIRS):
            first = row
            for h in (2 * p, 2 * p + 1):
                qcol = qm[:, h:h + 1]
                ub = (NORM_SLACK * (qcol * km_t[h:h + 1, :] + qcol * km[:, h:h + 1])
                      + ft0[:, h:h + 1] - fs1_t[h:h + 1, :])
                needed = jnp.logical_and(ub >= -SKIP_LOG2, col < row)
                first = jnp.minimum(first, jnp.where(needed, col, row))
            start = jnp.min(first, axis=1, keepdims=True)
            out = jnp.where(col == b * N_PAIRS + p, start, out)
    o_ref[...] = jnp.transpose(out).astype(jnp.int32)


def _plan(stats, *, n_batch, n_blk):
    assert n_blk <= LANES and n_batch * N_PAIRS <= LANES
    return pl.pallas_call(
        functools.partial(_plan_kernel, n_batch=n_batch, n_blk=n_blk),
        out_shape=jax.ShapeDtypeStruct((LANES, LANES), jnp.int32),
        name="attn_plan",
    )(stats)


def _attn_kernel(plan_ref, pt_ref, qa_ref, ka_ref, vb_ref, qd_ref, kn_ref, vn_ref, cn_ref,
                 ck_hbm, cv_hbm, clf_hbm, o_ref, od_ref, m_ref, l_ref, acc_ref, *dec_scratch,
                 t, nq, n_steps, dec):
    b, pr, i = pl.program_id(0), pl.program_id(1), pl.program_id(2)
    step = (b * N_PAIRS + pr) * nq + i
    dec_prologue, (dec_head, dec_scores, dec_tail), n_chunks = _decode_fns(
        pt_ref, qd_ref, kn_ref, vn_ref, cn_ref, ck_hbm, cv_hbm, clf_hbm, od_ref, *dec_scratch, **dec)
    per_step = -(-n_chunks // n_steps)
    interleave = n_chunks == n_steps

    @pl.when(step == 0)
    def _():
        dec_prologue()

    if interleave:
        dec_head(step)
    else:
        for k in range(per_step):
            g = step * per_step + k

            @pl.when(g < n_chunks)
            def _(g=g):
                dec_head(g)
                dec_scores(g)
                dec_tail(g)

    q = qa_ref[...]
    lane = lax.broadcasted_iota(jnp.int32, (t, LANES), 1)

    def scores(kstart, width, diag_col, e):
        kblk = ka_ref[pl.ds(kstart, width), 2 * LANES * e:2 * LANES * (e + 1)]
        s = lax.dot_general(q, kblk, (((1,), (1,)), ((), ())), preferred_element_type=F32)
        if diag_col is not None:
            r = lax.broadcasted_iota(jnp.int32, (t, width), 0)
            c = lax.broadcasted_iota(jnp.int32, (t, width), 1)
            s = jnp.where(c - diag_col <= r, s, NEG)
        return s

    def block(kstart, width, diag_col):
        vblk = vb_ref[pl.ds(kstart, width), :]
        for e in range(2):
            s = scores(kstart, width, diag_col, e)
            m_prev = m_ref[e]
            m_next = jnp.maximum(m_prev, jnp.max(s, axis=1, keepdims=True))
            alpha = jnp.exp2(m_prev - m_next)
            p = jnp.exp2(s - _lane_tile(m_next, width // LANES))
            l_ref[e] = alpha * l_ref[e] + jnp.sum(p, axis=1, keepdims=True)
            m_ref[e] = m_next
            acc_ref[e] = alpha * acc_ref[e] + jnp.dot(p.astype(BF16), vblk,
                                                      preferred_element_type=F32)

    def body(j, carry):
        block(pl.multiple_of(j * t, t), t, None)
        return carry

    first = plan_ref[(b * N_PAIRS + pr) * nq + i]
    one_before = jnp.logical_and(i > 0, first == i - 1)

    m_ref[...] = jnp.full(m_ref.shape, NEG, F32)
    l_ref[...] = jnp.zeros(l_ref.shape, F32)
    acc_ref[...] = jnp.zeros(acc_ref.shape, F32)

    @pl.when(one_before)
    def _():
        if interleave:
            dec_scores(step)
        block(pl.multiple_of((i - 1) * t, t), 2 * t, t)

    @pl.when(jnp.logical_not(one_before))
    def _():
        if interleave:
            dec_scores(step)
        lax.fori_loop(first, i, body, 0)
        block(pl.multiple_of(i * t, t), t, 0)

    o = jnp.where(lane < HEAD_DIM, acc_ref[0] / l_ref[0], acc_ref[1] / l_ref[1])
    o_ref[...] = o.astype(o_ref.dtype)
    if interleave:
        dec_tail(step)


def _attention(plan, qa, ka, vb, page_table, q3, kn3, vn3, cn3, ck, cv, clf_t, *, n_batch, seq):
    t = min(ATTN_T, seq)
    assert seq % t == 0
    nq = seq // t
    nb, n_new, _ = q3.shape
    dec, dec_scratch = _decode_setup(nb, n_new, page_table.shape[1], ck.shape[3])
    whole = lambda n: pl.BlockSpec((nb, n_new, n), lambda b, p, i, plan, pt: (0, 0, 0))
    hbm = pl.BlockSpec(memory_space=pl.ANY)
    grid_spec = pltpu.PrefetchScalarGridSpec(
        num_scalar_prefetch=2,
        grid=(n_batch, N_PAIRS, nq),
        in_specs=[pl.BlockSpec((t, 2 * LANES), lambda b, p, i, plan, pt: (b * nq + i, p)),
                  pl.BlockSpec((seq, 4 * LANES), lambda b, p, i, plan, pt: (b, p)),
                  pl.BlockSpec((seq, LANES), lambda b, p, i, plan, pt: (b, p)),
                  whole(ATTN_W), whole(ATTN_W), whole(ATTN_W), whole(LANES), hbm, hbm, hbm],
        out_specs=(pl.BlockSpec((t, LANES), lambda b, p, i, plan, pt: (b * nq + i, p)),
                   whole(ATTN_W)),
        scratch_shapes=[pltpu.VMEM((2, t, LANES), F32), pltpu.VMEM((2, t, LANES), F32),
                        pltpu.VMEM((2, t, LANES), F32)] + dec_scratch)
    return pl.pallas_call(
        functools.partial(_attn_kernel, t=t, nq=nq, n_steps=n_batch * N_PAIRS * nq, dec=dec),
        grid_spec=grid_spec,
        out_shape=(jax.ShapeDtypeStruct((n_batch * seq, ATTN_W), BF16),
                   jax.ShapeDtypeStruct((nb, n_new, ATTN_W), F32)),
        compiler_params=pltpu.CompilerParams(
            dimension_semantics=("arbitrary", "arbitrary", "arbitrary"),
            vmem_limit_bytes=VMEM_LIMIT),
        name="attention",
    )(plan, page_table.reshape(-1), qa, ka, vb, q3, kn3, vn3, cn3, ck, cv, clf_t)


def _merge(x, pooled, sgp, attn, sga, pw_ref, ps, wo_ref, nf):
    mixed = [jnp.dot(pooled[:, POOL_GC * g:POOL_GC * (g + 1)].astype(BF16), pw_ref[g],
                     preferred_element_type=F32) for g in range(len(POOL_WINDOWS))]
    pool_out = jnp.concatenate(mixed, axis=1) * ps
    mix = jnp.concatenate([(pool_out * sgp.astype(F32)).astype(BF16),
                           (attn.astype(F32) * sga.astype(F32)).astype(BF16)], axis=1)
    xo = x + jnp.dot(mix, wo_ref[...], preferred_element_type=F32)
    return _rmsnorm(xo, nf)


def _merge_prompt_kernel(x_ref, u_ref, uh_ref, sgp_ref, attn_ref, sga_ref, pw_ref, ps_ref, wo_ref,
                         nf_ref, y_ref, pool_ref, *, tm, tiles_per_batch):
    i = pl.program_id(0)
    ti = i % tiles_per_batch
    u = u_ref[...]
    halo = jnp.where(ti == 0, 0.0, uh_ref[...])
    ext = jnp.concatenate([halo, u], axis=0)
    pos = ti * tm + lax.broadcasted_iota(jnp.int32, (tm, 1), 0)
    groups = []
    level = ext
    shift = 1
    for g, w in enumerate(POOL_WINDOWS):
        while shift < w:
            level = level + pltpu.roll(level, shift, axis=0)
            shift *= 2
        cnt = jnp.minimum(w, pos + 1).astype(F32)
        sl = slice(POOL_GC * g, POOL_GC * (g + 1))
        groups.append(level[16:, sl] / cnt - u[:, sl])
    pooled = jnp.concatenate(groups, axis=1)
    y_ref[...] = _merge(x_ref[...], pooled, sgp_ref[...], attn_ref[...], sga_ref[...], pw_ref,
                        ps_ref[...], wo_ref, nf_ref[...])

    @pl.when(ti == tiles_per_batch - 1)
    def _():
        pool_ref[0] = u_ref[pl.ds(tm - POOL_HIST, POOL_HIST), :]


def _merge_prompt(x2, u, sgp, attn, sga, pw, ps, wo, nf, *, n_batch, seq):
    rows = x2.shape[0]
    tm = min(PROJ_TM, seq)
    tpb = seq // tm
    halo_blocks = tm // 16
    row_blk = lambda n: pl.BlockSpec((tm, n), lambda i: (i, 0))
    const2 = lambda shape: pl.BlockSpec(shape, lambda i: (0, 0))
    return pl.pallas_call(
        functools.partial(_merge_prompt_kernel, tm=tm, tiles_per_batch=tpb),
        grid=(rows // tm,),
        in_specs=[row_blk(D_MODEL), row_blk(POOL_W),
                  pl.BlockSpec((16, POOL_W), lambda i: (jnp.maximum(i * halo_blocks - 1, 0), 0)),
                  row_blk(POOL_W), row_blk(ATTN_W), row_blk(ATTN_W),
                  pl.BlockSpec((len(POOL_WINDOWS), POOL_GC, POOL_GC), lambda i: (0, 0, 0)),
                  const2((1, POOL_W)), const2((D_MODEL, D_MODEL)), const2((1, D_MODEL))],
        out_specs=(row_blk(D_MODEL),
                   pl.BlockSpec((1, POOL_HIST, POOL_W), lambda i: (i // tpb, 0, 0))),
        out_shape=(jax.ShapeDtypeStruct((rows, D_MODEL), F32),
                   jax.ShapeDtypeStruct((n_batch, POOL_HIST, POOL_W), F32)),
        compiler_params=pltpu.CompilerParams(dimension_semantics=("arbitrary",),
                                             vmem_limit_bytes=VMEM_LIMIT),
        name="merge_prompt",
    )(x2, u, u, sgp, attn, sga, pw, ps, wo, nf)


def _proj_sample_kernel(x_ref, nw_ref, wt_ref, wf_ref, bf_ref,
                        u_ref, sgp_ref, q_ref, k_ref, v_ref, sga_ref, logf_ref, cn_ref, *, n_new):
    nb = x_ref.shape[0]
    x = jnp.concatenate([x_ref[:, D_MODEL * i:D_MODEL * (i + 1)] for i in range(n_new)], axis=0)
    h = _rmsnorm(x, nw_ref[...]).astype(BF16)
    u, gp, q, k, v, ga, fl = _project(h, wt_ref, wf_ref)
    logf = _log_sigmoid(fl + bf_ref[...])
    outs = ((u_ref, u), (sgp_ref, _silu(gp).astype(BF16)), (q_ref, q * Q_SCALE), (k_ref, k),
            (v_ref, v), (sga_ref, _silu(ga).astype(BF16)), (logf_ref, logf))
    cn = jnp.zeros((nb, LANES), F32)
    for i in range(n_new):
        for ref, val in outs:
            n = val.shape[1]
            ref[:, n * i:n * (i + 1)] = val[nb * i:nb * (i + 1), :]
        cn = cn + logf[nb * i:nb * (i + 1), :]
        cn_ref[:, LANES * i:LANES * (i + 1)] = cn


def _proj_sample(xs2, nw, wt, wf, bft, *, n_new):
    nb = xs2.shape[0]
    widths = (POOL_W, POOL_W, ATTN_W, ATTN_W, ATTN_W, ATTN_W, LANES, LANES)
    dtypes = (F32, BF16, F32, F32, F32, BF16, F32, F32)
    return pl.pallas_call(
        functools.partial(_proj_sample_kernel, n_new=n_new),
        out_shape=tuple(jax.ShapeDtypeStruct((nb, n_new * n), dt) for n, dt in zip(widths, dtypes)),
        compiler_params=pltpu.CompilerParams(vmem_limit_bytes=VMEM_LIMIT),
        name="proj_sample",
    )(xs2, nw, wt, wf, bft)


def _decode_fns(pt_ref, q_ref, kn_ref, vn_ref, cn_ref, ck_hbm, cv_hbm, clf_hbm, o_ref,
                kbuf, vpre, vdem, lfbuf, ksem, lfsem, vsem,
                qbd_ref, cn8_ref, cncol_ref, m_ref, l_ref, acc_ref, tail_ref, base_ref, gap_ref,
                *, nb, n_new, n_pages, page, pps, vsub):
    n_chunks = n_pages // pps
    n_groups = pps // vsub
    total = nb * n_chunks
    n_keys = pps * page
    g_keys = vsub * page
    rows = n_new * N_HEADS
    row_w = lax.broadcasted_iota(jnp.int32, (rows, ATTN_W), 0)
    lane_w = lax.broadcasted_iota(jnp.int32, (rows, ATTN_W), 1)
    head_lanes = (row_w % N_HEADS) == (lane_w // HEAD_DIM)
    row1 = lax.broadcasted_iota(jnp.int32, (rows, 1), 0)

    def page_index(g, r):
        b, c = lax.div(g, n_chunks), lax.rem(g, n_chunks)
        return pt_ref[b * n_pages + n_pages - 1 - (c * pps + r)]

    def k_copies(g, slot):
        cps = []
        for r in range(pps):
            idx = page_index(g, r)
            cps.append(pltpu.make_async_copy(ck_hbm.at[idx], kbuf.at[slot, r], ksem.at[slot]))
            cps.append(pltpu.make_async_copy(clf_hbm.at[idx], lfbuf.at[slot, r], lfsem.at[slot]))
        return cps

    def v_copies(g, grp, dst, sem):
        return [pltpu.make_async_copy(cv_hbm.at[page_index(g, grp * vsub + r)], dst.at[r], sem)
                for r in range(vsub)]

    def init(b):
        q = q_ref[b]
        qrep = jnp.concatenate(
            [jnp.broadcast_to(q[i:i + 1, :], (N_HEADS, ATTN_W)) for i in range(n_new)], axis=0)
        qbd = jnp.where(head_lanes, qrep, 0.0)
        qbd_ref[...] = qbd.astype(BF16)
        cn8_ref[...] = jnp.zeros(cn8_ref.shape, F32)
        cn8_ref[0:n_new, :] = cn_ref[b]
        cnt = jnp.transpose(cn8_ref[...])[0:rows, :] * LOG2E
        cncol = jnp.zeros((rows, 1), F32)
        for i in range(n_new):
            cncol = jnp.where(row1 // N_HEADS == i, cnt[:, i:i + 1], cncol)
        cncol_ref[...] = jnp.broadcast_to(cncol, cncol_ref.shape)
        kn = kn_ref[b]
        vn = vn_ref[b]
        s_new = []
        for j in range(n_new):
            sj = jnp.sum(qbd * kn[j:j + 1, :], axis=1, keepdims=True) + cncol - cnt[:, j:j + 1]
            s_new.append(jnp.where(row1 // N_HEADS >= j, sj, NEG))
        m0 = functools.reduce(jnp.maximum, s_new)
        l0 = jnp.zeros((rows, 1), F32)
        a0 = jnp.zeros((rows, ATTN_W), F32)
        for j in range(n_new):
            pj = jnp.exp2(s_new[j] - m0)
            l0 = l0 + pj
            a0 = a0 + pj * vn[j:j + 1, :]
        m_ref[...] = jnp.broadcast_to(m0, m_ref.shape)
        l_ref[...] = jnp.broadcast_to(l0, l_ref.shape)
        acc_ref[...] = a0
        tail_ref[...] = jnp.zeros(tail_ref.shape, F32)

    def finish(b):
        o = acc_ref[...] / _lane_tile(l_ref[...], ATTN_W // LANES)
        o = jnp.where(head_lanes, o, 0.0)
        for i in range(n_new):
            o_ref[b, i:i + 1, :] = jnp.sum(o[N_HEADS * i:N_HEADS * (i + 1), :], axis=0,
                                           keepdims=True)

    lane8 = lax.broadcasted_iota(jnp.int32, (N_HEADS, page), 1)

    def head(g):
        slot = lax.rem(g, 2)
        b, c = lax.div(g, n_chunks), lax.rem(g, n_chunks)

        @pl.when(g + 1 < total)
        def _():
            for cp in k_copies(g + 1, 1 - slot):
                cp.start()

        @pl.when(c == 0)
        def _():
            init(b)
            for cp in v_copies(g, 0, vpre, vsem.at[0]):
                cp.wait()

        for cp in k_copies(g, slot):
            cp.wait()

    def scores(g):
        slot = lax.rem(g, 2)
        kcat = jnp.concatenate([kbuf[slot, r].reshape(ATTN_W, page).astype(BF16)
                                for r in range(pps)], axis=1)
        qk = jnp.dot(qbd_ref[...], kcat, preferred_element_type=F32)
        tail = tail_ref[...]
        tails = []
        for r in range(pps):
            tails.append(tail)
            tail = tail + jnp.sum(lfbuf[slot, r] * LOG2E, axis=1, keepdims=True)
        tail_ref[...] = tail
        base = qk + jnp.concatenate([jnp.concatenate(tails, axis=1)] * n_new, axis=0) + cncol_ref[:, 0:1]
        base_ref[...] = base
        over = base - _lane_tile(m_ref[...], n_keys // LANES)
        for h in range(n_groups):
            gap_ref[h] = jnp.max(over[:, g_keys * h:g_keys * (h + 1)])

    def tail_phase(g):
        slot = lax.rem(g, 2)
        b, c = lax.div(g, n_chunks), lax.rem(g, n_chunks)
        live = [gap_ref[h] >= -SKIP_LOG2 for h in range(n_groups)]

        @pl.when(functools.reduce(jnp.logical_or, live))
        def _():
            m_prev = m_ref[...]
            lfs = [lfbuf[slot, r] * LOG2E for r in range(pps)]
            g_parts = []
            for r in range(pps):
                incl = lfs[r]
                d = 1
                while d < page:
                    incl = incl + jnp.where(lane8 + d < page, pltpu.roll(incl, page - d, axis=1), 0.0)
                    d *= 2
                g_parts.append(incl - lfs[r])
            s = base_ref[...] + jnp.concatenate([jnp.concatenate(g_parts, axis=1)] * n_new, axis=0)
            m_next = jnp.maximum(m_prev, jnp.max(s, axis=1, keepdims=True))
            alpha = jnp.exp2(m_prev - m_next)
            p = jnp.exp2(s - _lane_tile(m_next, n_keys // LANES))
            l_ref[...] = alpha * l_ref[...] + jnp.sum(p, axis=1, keepdims=True)
            m_ref[...] = m_next
            acc_ref[...] = _lane_tile(alpha, ATTN_W // LANES) * acc_ref[...]
            pb = p.astype(BF16)

            def add_pv(h, vsrc):
                vcat = jnp.concatenate([vsrc[r].reshape(ATTN_W, page).astype(BF16)
                                        for r in range(vsub)], axis=1)
                acc_ref[...] += lax.dot_general(pb[:, g_keys * h:g_keys * (h + 1)], vcat,
                                                (((1,), (1,)), ((), ())),
                                                preferred_element_type=F32)

            def fetch_and_add(h):
                cps = v_copies(g, h, vdem, vsem.at[1])
                for cp in cps:
                    cp.start()
                for cp in cps:
                    cp.wait()
                add_pv(h, vdem)

            for h in range(n_groups):
                if h == 0:
                    @pl.when(jnp.logical_and(live[0], c == 0))
                    def _():
                        add_pv(0, vpre)

                    @pl.when(jnp.logical_and(live[0], c > 0))
                    def _():
                        fetch_and_add(0)
                else:
                    @pl.when(live[h])
                    def _(h=h):
                        fetch_and_add(h)

        @pl.when(c == n_chunks - 1)
        def _():
            finish(b)

        @pl.when(jnp.logical_and(c == 0, b + 1 < nb))
        def _():
            for cp in v_copies(g + n_chunks, 0, vpre, vsem.at[0]):
                cp.start()

    def prologue():
        for cp in k_copies(0, 0):
            cp.start()
        for cp in v_copies(0, 0, vpre, vsem.at[0]):
            cp.start()

    return prologue, (head, scores, tail_phase), total


def _decode_setup(nb, n_new, n_pages, page):
    pps = DEC_PAGES
    while n_pages % pps:
        pps //= 2
    vsub = min(DEC_VSUB, pps)
    rows = n_new * N_HEADS
    assert page % LANES == 0 and rows % 8 == 0 and pps % vsub == 0
    params = dict(nb=nb, n_new=n_new, n_pages=n_pages, page=page, pps=pps, vsub=vsub)
    scratch = [pltpu.VMEM((2, pps, N_HEADS, HEAD_DIM, page), F32),
               pltpu.VMEM((vsub, N_HEADS, HEAD_DIM, page), F32),
               pltpu.VMEM((vsub, N_HEADS, HEAD_DIM, page), F32),
               pltpu.VMEM((2, pps, N_HEADS, page), F32),
               pltpu.SemaphoreType.DMA((2,)),
               pltpu.SemaphoreType.DMA((2,)),
               pltpu.SemaphoreType.DMA((2,)),
               pltpu.VMEM((rows, ATTN_W), BF16),
               pltpu.VMEM((8, LANES), F32),
               pltpu.VMEM((rows, LANES), F32),
               pltpu.VMEM((rows, LANES), F32),
               pltpu.VMEM((rows, LANES), F32),
               pltpu.VMEM((rows, ATTN_W), F32),
               pltpu.VMEM((N_HEADS, LANES), F32),
               pltpu.VMEM((rows, pps * page), F32),
               pltpu.SMEM((pps // vsub,), F32)]
    return params, scratch


def _merge_sample_kernel(x_ref, u_ref, sp_ref, sgp_ref, attn_ref, sga_ref, pw_ref, ps_ref, wo_ref,
                         nf_ref, y_ref, pool_ref, *, n_new, past):
    u = [u_ref[:, POOL_W * i:POOL_W * (i + 1)] for i in range(n_new)]
    hist = [sp_ref[r] for r in range(POOL_HIST)]
    ext = hist + u
    pooled_rows = []
    for i in range(n_new):
        groups = []
        for g, w in enumerate(POOL_WINDOWS):
            sl = slice(POOL_GC * g, POOL_GC * (g + 1))
            end = POOL_HIST + i
            total = ext[end][:, sl]
            for r in range(end - w + 1, end):
                total = total + ext[r][:, sl]
            groups.append(total / float(min(w, past + i + 1)) - u[i][:, sl])
        pooled_rows.append(jnp.concatenate(groups, axis=1))
    pooled = jnp.concatenate(pooled_rows, axis=0)
    cat = lambda ref, n: jnp.concatenate([ref[:, n * i:n * (i + 1)] for i in range(n_new)], axis=0)
    y = _merge(cat(x_ref, D_MODEL), pooled, cat(sgp_ref, POOL_W), cat(attn_ref, ATTN_W),
               cat(sga_ref, ATTN_W), pw_ref, ps_ref[...], wo_ref, nf_ref[...])
    nb = x_ref.shape[0]
    for i in range(n_new):
        y_ref[:, D_MODEL * i:D_MODEL * (i + 1)] = y[nb * i:nb * (i + 1), :]
    new_hist = ext[-POOL_HIST:]
    for r in range(POOL_HIST):
        pool_ref[:, POOL_W * r:POOL_W * (r + 1)] = new_hist[r]


def _merge_sample(xs2, u2, sp2, sgp2, attn2, sga2, pw, ps, wo, nf, *, n_new, past):
    nb = xs2.shape[0]
    return pl.pallas_call(
        functools.partial(_merge_sample_kernel, n_new=n_new, past=past),
        out_shape=(jax.ShapeDtypeStruct((nb, n_new * D_MODEL), F32),
                   jax.ShapeDtypeStruct((nb, POOL_HIST * POOL_W), F32)),
        compiler_params=pltpu.CompilerParams(vmem_limit_bytes=VMEM_LIMIT),
        name="merge_sample",
    )(xs2, u2, sp2, sgp2, attn2, sga2, pw, ps, wo, nf)


def kernel(x_prompt, x_sample, cache_k, cache_v, cache_logf, state_pool, page_table,
           norm_w, w_in, b_forget, pool_w, pool_scale, w_out, norm_f):
    depth = norm_w.shape[0]
    assert depth == 1, "a single layer is supported"
    b_p, seq, _ = x_prompt.shape
    b_s, t_s, _ = x_sample.shape
    n_phys, page = cache_k.shape[1], cache_k.shape[2]
    past = page_table.shape[1] * page
    ll = 0

    wt_all = jnp.swapaxes(w_in[ll], 0, 1)
    wt = wt_all[:W_MAIN].astype(BF16)
    wf = jnp.tile(wt_all[W_MAIN:], (LANES // N_HEADS, 1)).astype(BF16)
    bft = jnp.tile(b_forget[ll], LANES // N_HEADS).reshape(1, LANES).astype(F32)
    nw = norm_w[ll].reshape(1, D_MODEL)
    nf = norm_f.reshape(1, D_MODEL)
    pw = pool_w[ll].astype(BF16)
    ps = pool_scale[ll].reshape(1, POOL_W)
    wo = w_out[ll].astype(BF16)
    pp_np, cc_np = _placement()
    pp = jnp.asarray(pp_np, BF16)
    cc = jnp.asarray(cc_np, F32)

    xp2 = x_prompt.reshape(b_p * seq, D_MODEL)
    seg_np = (np.arange(ATTN_W)[:, None] // HEAD_DIM == np.arange(LANES)[None, :] % N_HEADS)
    seg = jnp.asarray(seg_np, BF16)
    u, sgp, qa, ka, kf, vf, vb, sga, logf, stats = _proj_prompt(xp2, nw, wt, wf, bft, pp, cc, seg,
                                                                seq=seq)
    n_blk = seq // min(ATTN_T, seq)
    plan = _plan(stats, n_batch=b_p, n_blk=n_blk)[:b_p * N_PAIRS, :n_blk].reshape(-1)

    xs2 = x_sample.reshape(b_s, t_s * D_MODEL)
    us, sgps, qs, ks, vs, sgas, lfs, cns = _proj_sample(xs2, nw, wt, wf, bft, n_new=t_s)
    ck = jnp.transpose(cache_k[ll], (0, 2, 3, 1))
    cv = jnp.transpose(cache_v[ll], (0, 2, 3, 1))
    clf_t = jnp.swapaxes(cache_logf[ll], 1, 2)

    attn, attn_s = _attention(plan, qa, ka, vb, page_table, qs.reshape(b_s, t_s, ATTN_W),
                              ks.reshape(b_s, t_s, ATTN_W), vs.reshape(b_s, t_s, ATTN_W),
                              cns.reshape(b_s, t_s, LANES), ck, cv, clf_t, n_batch=b_p, seq=seq)
    yp, pool_p = _merge_prompt(xp2, u, sgp, attn, sga, pw, ps, wo, nf, n_batch=b_p, seq=seq)
    ys, pool_s = _merge_sample(xs2, us, jnp.swapaxes(state_pool[ll], 0, 1), sgps,
                               attn_s.reshape(b_s, t_s * ATTN_W), sgas, pw, ps, wo, nf,
                               n_new=t_s, past=past)

    return (yp.reshape(b_p, seq, D_MODEL),
            ys.reshape(b_s, t_s, D_MODEL),
            jnp.transpose(kf.reshape(1, b_p, N_HEADS, HEAD_DIM, seq), (0, 1, 4, 2, 3)),
            jnp.transpose(vf.reshape(1, b_p, N_HEADS, HEAD_DIM, seq), (0, 1, 4, 2, 3)),
            jnp.transpose(logf, (0, 2, 1)).reshape(1, b_p, seq, N_HEADS),
            pool_p.reshape(1, b_p, POOL_HIST, POOL_W),
            ks.reshape(1, b_s, t_s, N_HEADS, HEAD_DIM),
            vs.reshape(1, b_s, t_s, N_HEADS, HEAD_DIM),
            lfs.reshape(b_s, t_s, LANES)[:, :, :N_HEADS].reshape(1, b_s, t_s, N_HEADS),
            pool_s.reshape(1, b_s, POOL_HIST, POOL_W))
```

```python
import functools
import math

import numpy as np
import jax
import jax.numpy as jnp
from jax import lax
from jax.experimental import pallas as pl
from jax.experimental.pallas import tpu as pltpu

D_MODEL = 1024
POOL_W = 512
ATTN_W = 512
N_HEADS = 8
HEAD_DIM = 64
POOL_WINDOWS = (2, 4, 8, 16)
POOL_GC = 128
POOL_HIST = 15
EPS = 1e-6

LANES = 128
LOG2E = 1.4426950408889634
Q_SCALE = LOG2E / math.sqrt(HEAD_DIM)
NEG = -1e30
N_PAIRS = N_HEADS // 2
W_MAIN = 2 * POOL_W + 4 * ATTN_W
BIAS_LANES = LANES // N_HEADS
BIAS_COLS = 2 * LANES
VMEM_LIMIT = 56 * 1024 * 1024

SKIP_LOG2 = 152.0
NORM_SLACK = 1.02

PROJ_TM = 512
ATTN_T = 512
DEC_PAGES = 16
DEC_VSUB = 8

BF16 = jnp.bfloat16
F32 = jnp.float32


def _placement():
    pp = np.zeros((LANES, BIAS_COLS), np.float32)
    cc = np.zeros((1, BIAS_COLS), np.float32)
    for h in range(N_HEADS):
        for i in range(3):
            pp[8 * i + h, BIAS_LANES * h + i] = 1.0
            pp[8 * i + h, LANES + BIAS_LANES * h + 3 + i] = -1.0
            cc[0, BIAS_LANES * h + 3 + i] = 1.0
            cc[0, LANES + BIAS_LANES * h + i] = 1.0
    return pp, cc


def _log_sigmoid(x):
    return jnp.minimum(x, 0.0) - jnp.log(1.0 + jnp.exp(-jnp.abs(x)))


def _silu(x):
    return x * jax.nn.sigmoid(x)


def _rmsnorm(xf, w):
    ms = jnp.mean(xf * xf, axis=-1, keepdims=True)
    return xf * lax.rsqrt(ms + EPS) * w


def _project(h, wt_ref, wf_ref):
    def cols(ref, c0, n):
        return lax.dot_general(h, ref[c0:c0 + n, :], (((1,), (1,)), ((), ())),
                               preferred_element_type=F32)
    fl = cols(wf_ref, 0, LANES)
    k = cols(wt_ref, 2 * POOL_W + ATTN_W, ATTN_W)
    v = cols(wt_ref, 2 * POOL_W + 2 * ATTN_W, ATTN_W)
    q = cols(wt_ref, 2 * POOL_W, ATTN_W)
    gp = cols(wt_ref, POOL_W, POOL_W)
    ga = cols(wt_ref, 2 * POOL_W + 3 * ATTN_W, ATTN_W)
    u = cols(wt_ref, 0, POOL_W)
    return u, gp, q, k, v, ga, fl


def _proj_prompt_kernel(x_ref, nw_ref, wt_ref, wf_ref, bf_ref, pp_ref, cc_ref, seg_ref,
                        u_ref, sgp_ref, qa_ref, ka_ref, k_ref, v_ref, vb_ref, sga_ref, logf_ref,
                        st_ref, carry_ref, *, tm, ta, tiles_per_batch):
    i = pl.program_id(0)
    h = _rmsnorm(x_ref[...], nw_ref[...]).astype(BF16)
    u, gp, q, k, v, ga, fl = _project(h, wt_ref, wf_ref)
    u_ref[...] = u
    sgp_ref[...] = _silu(gp).astype(BF16)
    k_ref[0] = jnp.transpose(k)
    v_ref[0] = jnp.transpose(v)
    vb_ref[...] = v.astype(BF16)
    sga_ref[...] = _silu(ga).astype(BF16)
    logf = _log_sigmoid(fl + bf_ref[...])
    logf_ref[0] = jnp.transpose(logf)[0:N_HEADS, :]

    row = lax.broadcasted_iota(jnp.int32, (tm, LANES), 0)
    acc = logf
    d = 1
    while d < tm:
        acc = acc + jnp.where(row >= d, pltpu.roll(acc, d, axis=0), 0.0)
        d *= 2
    @pl.when(i % tiles_per_batch == 0)
    def _():
        carry_ref[...] = jnp.zeros(carry_ref.shape, F32)

    f_run = acc + carry_ref[0:1, :]
    carry_ref[...] = jnp.broadcast_to(f_run[tm - 1:tm, :], carry_ref.shape)

    f2 = f_run * LOG2E
    p1 = f2.astype(BF16).astype(F32)
    r1 = f2 - p1
    p2 = r1.astype(BF16).astype(F32)
    p3 = (r1 - p2).astype(BF16).astype(F32)
    lane = lax.broadcasted_iota(jnp.int32, (tm, LANES), 1)
    pieces = jnp.where(lane < 8, p1, jnp.where(lane < 16, p2, jnp.where(lane < 24, p3, 0.0)))
    bias = jnp.dot(pieces.astype(BF16), pp_ref[...], preferred_element_type=F32) + cc_ref[...]

    qs = q * Q_SCALE

    qn2 = jnp.dot((qs * qs).astype(BF16), seg_ref[...], preferred_element_type=F32)
    kn2 = jnp.dot((k * k).astype(BF16), seg_ref[...], preferred_element_type=F32)
    srow = lax.broadcasted_iota(jnp.int32, (8, LANES), 0)
    for sb in range(tm // ta):
        r0, r1 = sb * ta, (sb + 1) * ta
        qmax = jnp.sqrt(jnp.max(qn2[r0:r1], axis=0, keepdims=True))
        kmax = jnp.sqrt(jnp.max(kn2[r0:r1], axis=0, keepdims=True))
        st_ref[8 * sb:8 * (sb + 1), :] = jnp.where(
            srow == 0, qmax, jnp.where(srow == 1, kmax, jnp.where(
                srow == 2, f2[r0:r0 + 1, :], jnp.where(srow == 3, f2[r1 - 1:r1, :], 0.0))))

    first_half = lane < HEAD_DIM
    q_bias = bias[:, 0:LANES].astype(BF16)
    k_bias = bias[:, LANES:2 * LANES]
    for p in range(N_PAIRS):
        qa_ref[:, 2 * LANES * p:2 * LANES * p + LANES] = qs[:, LANES * p:LANES * (p + 1)].astype(BF16)
        qa_ref[:, 2 * LANES * p + LANES:2 * LANES * (p + 1)] = q_bias
        kp = k[:, LANES * p:LANES * (p + 1)]
        for e in range(2):
            hh = 2 * p + e
            keep = first_half if e == 0 else jnp.logical_not(first_half)
            ka_ref[:, 2 * LANES * hh:2 * LANES * hh + LANES] = jnp.where(keep, kp, 0.0).astype(BF16)
            own = jnp.logical_and(lane >= BIAS_LANES * hh, lane < BIAS_LANES * (hh + 1))
            ka_ref[:, 2 * LANES * hh + LANES:2 * LANES * (hh + 1)] = (
                jnp.where(own, k_bias, 0.0).astype(BF16))


def _proj_prompt(x2, nw, wt, wf, bft, pp, cc, seg, *, seq):
    rows = x2.shape[0]
    tm = min(PROJ_TM, seq)
    ta = min(ATTN_T, seq)
    assert seq % tm == 0 and rows % seq == 0 and tm % ta == 0 and tm % LANES == 0
    n_batch = rows // seq
    row_blk = lambda n: pl.BlockSpec((tm, n), lambda i: (i, 0))
    const = lambda shape: pl.BlockSpec(shape, lambda i: (0, 0))
    out_shape = (
        jax.ShapeDtypeStruct((rows, POOL_W), F32),
        jax.ShapeDtypeStruct((rows, POOL_W), BF16),
        jax.ShapeDtypeStruct((rows, N_PAIRS * 2 * LANES), BF16),
        jax.ShapeDtypeStruct((rows, N_HEADS * 2 * LANES), BF16),
        jax.ShapeDtypeStruct((n_batch, ATTN_W, seq), F32),
        jax.ShapeDtypeStruct((n_batch, ATTN_W, seq), F32),
        jax.ShapeDtypeStruct((rows, ATTN_W), BF16),
        jax.ShapeDtypeStruct((rows, ATTN_W), BF16),
        jax.ShapeDtypeStruct((n_batch, N_HEADS, seq), F32),
        jax.ShapeDtypeStruct((rows // ta * 8, LANES), F32),
    )
    tpb = seq // tm
    t_minor = lambda n: pl.BlockSpec((1, n, tm), lambda i: (i // tpb, 0, i % tpb))
    out_specs = (row_blk(POOL_W), row_blk(POOL_W), row_blk(N_PAIRS * 2 * LANES),
                 row_blk(N_HEADS * 2 * LANES), t_minor(ATTN_W), t_minor(ATTN_W), row_blk(ATTN_W),
                 row_blk(ATTN_W), t_minor(N_HEADS),
                 pl.BlockSpec((tm // ta * 8, LANES), lambda i: (i, 0)))
    return pl.pallas_call(
        functools.partial(_proj_prompt_kernel, tm=tm, ta=ta, tiles_per_batch=seq // tm),
        grid=(rows // tm,),
        in_specs=[row_blk(D_MODEL), const((1, D_MODEL)), const((W_MAIN, D_MODEL)),
                  const((LANES, D_MODEL)), const((1, LANES)),
                  const((LANES, BIAS_COLS)), const((1, BIAS_COLS)), const((ATTN_W, LANES))],
        out_specs=out_specs,
        out_shape=out_shape,
        scratch_shapes=[pltpu.VMEM((8, LANES), F32)],
        compiler_params=pltpu.CompilerParams(dimension_semantics=("arbitrary",),
                                             vmem_limit_bytes=VMEM_LIMIT),
        name="proj_prompt",
    )(x2, nw, wt, wf, bft, pp, cc, seg)


def _lane_tile(x, reps):
    return jnp.concatenate([x] * reps, axis=1)


def _plan_kernel(st_ref, o_ref, *, n_batch, n_blk):
    row = lax.broadcasted_iota(jnp.int32, (LANES, LANES), 0).astype(F32)
    col = lax.broadcasted_iota(jnp.int32, (LANES, LANES), 1).astype(F32)
    out = jnp.zeros((LANES, LANES), F32)
    for b in range(n_batch):
        def stat(r):
            v = st_ref[pl.ds((b * n_blk) * 8 + r, n_blk, stride=8), :]
            return jnp.concatenate([v, jnp.zeros((LANES - n_blk, LANES), F32)], axis=0)
        qm, km, ft0, fs1 = stat(0), stat(1), stat(2), stat(3)
        km_t, fs1_t = jnp.transpose(km), jnp.transpose(fs1)
        for p in range(N_PAIRS):
            first = row
            for h in (2 * p, 2 * p + 1):
                qcol = qm[:, h:h + 1]
                ub = (NORM_SLACK * (qcol * km_t[h:h + 1, :] + qcol * km[:, h:h + 1])
                      + ft0[:, h:h + 1] - fs1_t[h:h + 1, :])
                needed = jnp.logical_and(ub >= -SKIP_LOG2, col < row)
                first = jnp.minimum(first, jnp.where(needed, col, row))
            start = jnp.min(first, axis=1, keepdims=True)
            out = jnp.where(col == b * N_PAIRS + p, start, out)
    o_ref[...] = jnp.transpose(out).astype(jnp.int32)


def _plan(stats, *, n_batch, n_blk):
    assert n_blk <= LANES and n_batch * N_PAIRS <= LANES
    return pl.pallas_call(
        functools.partial(_plan_kernel, n_batch=n_batch, n_blk=n_blk),
        out_shape=jax.ShapeDtypeStruct((LANES, LANES), jnp.int32),
        name="attn_plan",
    )(stats)


def _attn_kernel(plan_ref, pt_ref, qa_ref, ka_ref, vb_ref, qd_ref, kn_ref, vn_ref, cn_ref,
                 ck_hbm, cv_hbm, clf_hbm, o_ref, od_ref, m_ref, l_ref, acc_ref, *dec_scratch,
                 t, nq, n_steps, dec):
    b, pr, i = pl.program_id(0), pl.program_id(1), pl.program_id(2)
    step = (b * N_PAIRS + pr) * nq + i
    dec_prologue, (dec_head, dec_scores, dec_tail), n_chunks = _decode_fns(
        pt_ref, qd_ref, kn_ref, vn_ref, cn_ref, ck_hbm, cv_hbm, clf_hbm, od_ref, *dec_scratch, **dec)
    per_step = -(-n_chunks // n_steps)
    interleave = n_chunks == n_steps

    @pl.when(step == 0)
    def _():
        dec_prologue()

    if interleave:
        dec_head(step)
    else:
        for k in range(per_step):
            g = step * per_step + k

            @pl.when(g < n_chunks)
            def _(g=g):
                dec_head(g)
                dec_scores(g)
                dec_tail(g)

    q = qa_ref[...]
    lane = lax.broadcasted_iota(jnp.int32, (t, LANES), 1)

    def scores(kstart, width, diag_col, e):
        kblk = ka_ref[pl.ds(kstart, width), 2 * LANES * e:2 * LANES * (e + 1)]
        s = lax.dot_general(q, kblk, (((1,), (1,)), ((), ())), preferred_element_type=F32)
        if diag_col is not None:
            r = lax.broadcasted_iota(jnp.int32, (t, width), 0)
            c = lax.broadcasted_iota(jnp.int32, (t, width), 1)
            s = jnp.where(c - diag_col <= r, s, NEG)
        return s

    def block(kstart, width, diag_col):
        vblk = vb_ref[pl.ds(kstart, width), :]
        for e in range(2):
            s = scores(kstart, width, diag_col, e)
            m_prev = m_ref[e]
            m_next = jnp.maximum(m_prev, jnp.max(s, axis=1, keepdims=True))
            alpha = jnp.exp2(m_prev - m_next)
            p = jnp.exp2(s - _lane_tile(m_next, width // LANES))
            l_ref[e] = alpha * l_ref[e] + jnp.sum(p, axis=1, keepdims=True)
            m_ref[e] = m_next
            acc_ref[e] = alpha * acc_ref[e] + jnp.dot(p.astype(BF16), vblk,
                                                      preferred_element_type=F32)

    def body(j, carry):
        block(pl.multiple_of(j * t, t), t, None)
        return carry

    first = plan_ref[(b * N_PAIRS + pr) * nq + i]
    one_before = jnp.logical_and(i > 0, first == i - 1)

    m_ref[...] = jnp.full(m_ref.shape, NEG, F32)
    l_ref[...] = jnp.zeros(l_ref.shape, F32)
    acc_ref[...] = jnp.zeros(acc_ref.shape, F32)

    @pl.when(one_before)
    def _():
        if interleave:
            dec_scores(step)
        block(pl.multiple_of((i - 1) * t, t), 2 * t, t)

    @pl.when(jnp.logical_not(one_before))
    def _():
        if interleave:
            dec_scores(step)
        lax.fori_loop(first, i, body, 0)
        block(pl.multiple_of(i * t, t), t, 0)

    o = jnp.where(lane < HEAD_DIM, acc_ref[0] / l_ref[0], acc_ref[1] / l_ref[1])
    o_ref[...] = o.astype(o_ref.dtype)
    if interleave:
        dec_tail(step)


def _attention(plan, qa, ka, vb, page_table, qd, kn, vn, cn, ck, cv, clf_t, *, n_batch, seq):
    t = min(ATTN_T, seq)
    assert seq % t == 0
    nq = seq // t
    nb, n_new, _ = qd.shape
    dec, dec_scratch = _decode_setup(nb, n_new, page_table.shape[1], ck.shape[3])
    whole = lambda n: pl.BlockSpec((nb, n_new, n), lambda b, p, i, plan, pt: (0, 0, 0))
    hbm = pl.BlockSpec(memory_space=pl.ANY)
    grid_spec = pltpu.PrefetchScalarGridSpec(
        num_scalar_prefetch=2,
        grid=(n_batch, N_PAIRS, nq),
        in_specs=[pl.BlockSpec((t, 2 * LANES), lambda b, p, i, plan, pt: (b * nq + i, p)),
                  pl.BlockSpec((seq, 4 * LANES), lambda b, p, i, plan, pt: (b, p)),
                  pl.BlockSpec((seq, LANES), lambda b, p, i, plan, pt: (b, p)),
                  whole(ATTN_W), whole(ATTN_W), whole(ATTN_W), whole(LANES), hbm, hbm, hbm],
        out_specs=(pl.BlockSpec((t, LANES), lambda b, p, i, plan, pt: (b * nq + i, p)),
                   whole(ATTN_W)),
        scratch_shapes=[pltpu.VMEM((2, t, LANES), F32), pltpu.VMEM((2, t, LANES), F32),
                        pltpu.VMEM((2, t, LANES), F32)] + dec_scratch)
    return pl.pallas_call(
        functools.partial(_attn_kernel, t=t, nq=nq, n_steps=n_batch * N_PAIRS * nq, dec=dec),
        grid_spec=grid_spec,
        out_shape=(jax.ShapeDtypeStruct((n_batch * seq, ATTN_W), BF16),
                   jax.ShapeDtypeStruct((nb, n_new, ATTN_W), F32)),
        compiler_params=pltpu.CompilerParams(
            dimension_semantics=("arbitrary", "arbitrary", "arbitrary"),
            vmem_limit_bytes=VMEM_LIMIT),
        name="attention",
    )(plan, page_table.reshape(-1), qa, ka, vb, qd, kn, vn, cn, ck, cv, clf_t)


def _merge(x, pooled, sgp, attn, sga, pw_ref, ps, wo_ref, nf):
    mixed = [jnp.dot(pooled[:, POOL_GC * g:POOL_GC * (g + 1)].astype(BF16), pw_ref[g],
                     preferred_element_type=F32) for g in range(len(POOL_WINDOWS))]
    pool_out = jnp.concatenate(mixed, axis=1) * ps
    mix = jnp.concatenate([(pool_out * sgp.astype(F32)).astype(BF16),
                           (attn.astype(F32) * sga.astype(F32)).astype(BF16)], axis=1)
    xo = x + jnp.dot(mix, wo_ref[...], preferred_element_type=F32)
    return _rmsnorm(xo, nf)


def _merge_prompt_kernel(x_ref, u_ref, uh_ref, sgp_ref, attn_ref, sga_ref, pw_ref, ps_ref, wo_ref,
                         nf_ref, y_ref, pool_ref, *, tm, tiles_per_batch):
    i = pl.program_id(0)
    ti = i % tiles_per_batch
    u = u_ref[...]
    halo = jnp.where(ti == 0, 0.0, uh_ref[...])
    ext = jnp.concatenate([halo, u], axis=0)
    pos = ti * tm + lax.broadcasted_iota(jnp.int32, (tm, 1), 0)
    groups = []
    level = ext
    shift = 1
    for g, w in enumerate(POOL_WINDOWS):
        while shift < w:
            level = level + pltpu.roll(level, shift, axis=0)
            shift *= 2
        cnt = jnp.minimum(w, pos + 1).astype(F32)
        sl = slice(POOL_GC * g, POOL_GC * (g + 1))
        groups.append(level[16:, sl] / cnt - u[:, sl])
    pooled = jnp.concatenate(groups, axis=1)
    y_ref[...] = _merge(x_ref[...], pooled, sgp_ref[...], attn_ref[...], sga_ref[...], pw_ref,
                        ps_ref[...], wo_ref, nf_ref[...])

    @pl.when(ti == tiles_per_batch - 1)
    def _():
        pool_ref[0] = u_ref[pl.ds(tm - POOL_HIST, POOL_HIST), :]


def _merge_prompt(x2, u, sgp, attn, sga, pw, ps, wo, nf, *, n_batch, seq):
    rows = x2.shape[0]
    tm = min(PROJ_TM, seq)
    tpb = seq // tm
    halo_blocks = tm // 16
    row_blk = lambda n: pl.BlockSpec((tm, n), lambda i: (i, 0))
    const2 = lambda shape: pl.BlockSpec(shape, lambda i: (0, 0))
    return pl.pallas_call(
        functools.partial(_merge_prompt_kernel, tm=tm, tiles_per_batch=tpb),
        grid=(rows // tm,),
        in_specs=[row_blk(D_MODEL), row_blk(POOL_W),
                  pl.BlockSpec((16, POOL_W), lambda i: (jnp.maximum(i * halo_blocks - 1, 0), 0)),
                  row_blk(POOL_W), row_blk(ATTN_W), row_blk(ATTN_W),
                  pl.BlockSpec((len(POOL_WINDOWS), POOL_GC, POOL_GC), lambda i: (0, 0, 0)),
                  const2((1, POOL_W)), const2((D_MODEL, D_MODEL)), const2((1, D_MODEL))],
        out_specs=(row_blk(D_MODEL),
                   pl.BlockSpec((1, POOL_HIST, POOL_W), lambda i: (i // tpb, 0, 0))),
        out_shape=(jax.ShapeDtypeStruct((rows, D_MODEL), F32),
                   jax.ShapeDtypeStruct((n_batch, POOL_HIST, POOL_W), F32)),
        compiler_params=pltpu.CompilerParams(dimension_semantics=("arbitrary",),
                                             vmem_limit_bytes=VMEM_LIMIT),
        name="merge_prompt",
    )(x2, u, u, sgp, attn, sga, pw, ps, wo, nf)


def _proj_sample_kernel(x_ref, nw_ref, wt_ref, wf_ref, bf_ref,
                        u_ref, sgp_ref, q_ref, k_ref, v_ref, sga_ref, logf_ref, cn_ref, *, n_new):
    nb = x_ref.shape[0]
    x = jnp.concatenate([x_ref[:, D_MODEL * i:D_MODEL * (i + 1)] for i in range(n_new)], axis=0)
    h = _rmsnorm(x, nw_ref[...]).astype(BF16)
    u, gp, q, k, v, ga, fl = _project(h, wt_ref, wf_ref)
    logf = _log_sigmoid(fl + bf_ref[...])
    outs = ((u_ref, u), (sgp_ref, _silu(gp).astype(BF16)), (q_ref, q * Q_SCALE), (k_ref, k),
            (v_ref, v), (sga_ref, _silu(ga).astype(BF16)), (logf_ref, logf))
    cn = jnp.zeros((nb, LANES), F32)
    for i in range(n_new):
        for ref, val in outs:
            n = val.shape[1]
            ref[:, n * i:n * (i + 1)] = val[nb * i:nb * (i + 1), :]
        cn = cn + logf[nb * i:nb * (i + 1), :]
        cn_ref[:, LANES * i:LANES * (i + 1)] = cn


def _proj_sample(xs2, nw, wt, wf, bft, *, n_new):
    nb = xs2.shape[0]
    widths = (POOL_W, POOL_W, ATTN_W, ATTN_W, ATTN_W, ATTN_W, LANES, LANES)
    dtypes = (F32, BF16, F32, F32, F32, BF16, F32, F32)
    return pl.pallas_call(
        functools.partial(_proj_sample_kernel, n_new=n_new),
        out_shape=tuple(jax.ShapeDtypeStruct((nb, n_new * n), dt) for n, dt in zip(widths, dtypes)),
        compiler_params=pltpu.CompilerParams(vmem_limit_bytes=VMEM_LIMIT),
        name="proj_sample",
    )(xs2, nw, wt, wf, bft)


def _decode_fns(pt_ref, q_ref, kn_ref, vn_ref, cn_ref, ck_hbm, cv_hbm, clf_hbm, o_ref,
                kbuf, vpre, vdem, lfbuf, ksem, lfsem, vsem,
                qbd_ref, cn8_ref, cncol_ref, m_ref, l_ref, acc_ref, tail_ref, base_ref, gap_ref,
                *, nb, n_new, n_pages, page, pps, vsub):
    n_chunks = n_pages // pps
    n_groups = pps // vsub
    total = nb * n_chunks
    n_keys = pps * page
    g_keys = vsub * page
    rows = n_new * N_HEADS
    row_w = lax.broadcasted_iota(jnp.int32, (rows, ATTN_W), 0)
    lane_w = lax.broadcasted_iota(jnp.int32, (rows, ATTN_W), 1)
    head_lanes = (row_w % N_HEADS) == (lane_w // HEAD_DIM)
    row1 = lax.broadcasted_iota(jnp.int32, (rows, 1), 0)

    def page_index(g, r):
        b, c = lax.div(g, n_chunks), lax.rem(g, n_chunks)
        return pt_ref[b * n_pages + n_pages - 1 - (c * pps + r)]

    def k_copies(g, slot):
        cps = []
        for r in range(pps):
            idx = page_index(g, r)
            cps.append(pltpu.make_async_copy(ck_hbm.at[idx], kbuf.at[slot, r], ksem.at[slot]))
            cps.append(pltpu.make_async_copy(clf_hbm.at[idx], lfbuf.at[slot, r], lfsem.at[slot]))
        return cps

    def v_copies(g, grp, dst, sem):
        return [pltpu.make_async_copy(cv_hbm.at[page_index(g, grp * vsub + r)], dst.at[r], sem)
                for r in range(vsub)]

    def row(ref, b, i, n):
        return ref[b, i:i + 1, :]

    def init(b):
        qrep = jnp.concatenate(
            [jnp.broadcast_to(row(q_ref, b, i, ATTN_W), (N_HEADS, ATTN_W)) for i in range(n_new)],
            axis=0)
        qbd = jnp.where(head_lanes, qrep, 0.0)
        qbd_ref[...] = qbd.astype(BF16)
        cn8_ref[...] = jnp.zeros(cn8_ref.shape, F32)
        for i in range(n_new):
            cn8_ref[i:i + 1, :] = row(cn_ref, b, i, LANES)
        cnt = jnp.transpose(cn8_ref[...])[0:rows, :] * LOG2E
        cncol = jnp.zeros((rows, 1), F32)
        for i in range(n_new):
            cncol = jnp.where(row1 // N_HEADS == i, cnt[:, i:i + 1], cncol)
        cncol_ref[...] = jnp.broadcast_to(cncol, cncol_ref.shape)
        s_new = []
        for j in range(n_new):
            sj = (jnp.sum(qbd * row(kn_ref, b, j, ATTN_W), axis=1, keepdims=True)
                  + cncol - cnt[:, j:j + 1])
            s_new.append(jnp.where(row1 // N_HEADS >= j, sj, NEG))
        m0 = functools.reduce(jnp.maximum, s_new)
        l0 = jnp.zeros((rows, 1), F32)
        a0 = jnp.zeros((rows, ATTN_W), F32)
        for j in range(n_new):
            pj = jnp.exp2(s_new[j] - m0)
            l0 = l0 + pj
            a0 = a0 + pj * row(vn_ref, b, j, ATTN_W)
        m_ref[...] = jnp.broadcast_to(m0, m_ref.shape)
        l_ref[...] = jnp.broadcast_to(l0, l_ref.shape)
        acc_ref[...] = a0
        tail_ref[...] = jnp.zeros(tail_ref.shape, F32)

    def finish(b):
        o = acc_ref[...] / _lane_tile(l_ref[...], ATTN_W // LANES)
        o = jnp.where(head_lanes, o, 0.0)
        for i in range(n_new):
            o_ref[b, i:i + 1, :] = jnp.sum(o[N_HEADS * i:N_HEADS * (i + 1), :], axis=0,
                                           keepdims=True)

    lane8 = lax.broadcasted_iota(jnp.int32, (N_HEADS, page), 1)

    def head(g):
        slot = lax.rem(g, 2)
        b, c = lax.div(g, n_chunks), lax.rem(g, n_chunks)

        @pl.when(g + 1 < total)
        def _():
            for cp in k_copies(g + 1, 1 - slot):
                cp.start()

        @pl.when(c == 0)
        def _():
            init(b)
            for cp in v_copies(g, 0, vpre, vsem.at[0]):
                cp.wait()

        for cp in k_copies(g, slot):
            cp.wait()

    def scores(g):
        slot = lax.rem(g, 2)
        kcat = jnp.concatenate([kbuf[slot, r].reshape(ATTN_W, page).astype(BF16)
                                for r in range(pps)], axis=1)
        qk = jnp.dot(qbd_ref[...], kcat, preferred_element_type=F32)
        tail = tail_ref[...]
        tails = []
        for r in range(pps):
            tails.append(tail)
            tail = tail + jnp.sum(lfbuf[slot, r] * LOG2E, axis=1, keepdims=True)
        tail_ref[...] = tail
        base = qk + jnp.concatenate([jnp.concatenate(tails, axis=1)] * n_new, axis=0) + cncol_ref[:, 0:1]
        base_ref[...] = base
        over = base - _lane_tile(m_ref[...], n_keys // LANES)
        for h in range(n_groups):
            gap_ref[h] = jnp.max(over[:, g_keys * h:g_keys * (h + 1)])

    def tail_phase(g):
        slot = lax.rem(g, 2)
        b, c = lax.div(g, n_chunks), lax.rem(g, n_chunks)
        live = [gap_ref[h] >= -SKIP_LOG2 for h in range(n_groups)]

        def update(h, vsrc):
            g_parts = []
            for r in range(vsub * h, vsub * (h + 1)):
                lf = lfbuf[slot, r] * LOG2E
                incl = lf
                d = 1
                while d < page:
                    incl = incl + jnp.where(lane8 + d < page, pltpu.roll(incl, page - d, axis=1), 0.0)
                    d *= 2
                g_parts.append(incl - lf)
            s = (base_ref[:, g_keys * h:g_keys * (h + 1)]
                 + jnp.concatenate([jnp.concatenate(g_parts, axis=1)] * n_new, axis=0))
            m_prev = m_ref[...]
            m_next = jnp.maximum(m_prev, jnp.max(s, axis=1, keepdims=True))
            alpha = jnp.exp2(m_prev - m_next)
            p = jnp.exp2(s - _lane_tile(m_next, g_keys // LANES))
            l_ref[...] = alpha * l_ref[...] + jnp.sum(p, axis=1, keepdims=True)
            m_ref[...] = m_next
            vcat = jnp.concatenate([vsrc[r].reshape(ATTN_W, page).astype(BF16)
                                    for r in range(vsub)], axis=1)
            acc_ref[...] = (_lane_tile(alpha, ATTN_W // LANES) * acc_ref[...]
                            + lax.dot_general(p.astype(BF16), vcat, (((1,), (1,)), ((), ())),
                                              preferred_element_type=F32))

        def fetch_and_update(h):
            cps = v_copies(g, h, vdem, vsem.at[1])
            for cp in cps:
                cp.start()
            for cp in cps:
                cp.wait()
            update(h, vdem)

        for h in range(n_groups):
            if h == 0:
                @pl.when(jnp.logical_and(live[0], c == 0))
                def _():
                    update(0, vpre)

                @pl.when(jnp.logical_and(live[0], c > 0))
                def _():
                    fetch_and_update(0)
            else:
                @pl.when(live[h])
                def _(h=h):
                    fetch_and_update(h)

        @pl.when(c == n_chunks - 1)
        def _():
            finish(b)

        @pl.when(jnp.logical_and(c == 0, b + 1 < nb))
        def _():
            for cp in v_copies(g + n_chunks, 0, vpre, vsem.at[0]):
                cp.start()

    def prologue():
        for cp in k_copies(0, 0):
            cp.start()
        for cp in v_copies(0, 0, vpre, vsem.at[0]):
            cp.start()

    return prologue, (head, scores, tail_phase), total


def _decode_setup(nb, n_new, n_pages, page):
    pps = DEC_PAGES
    while n_pages % pps:
        pps //= 2
    vsub = min(DEC_VSUB, pps)
    rows = n_new * N_HEADS
    assert page % LANES == 0 and rows % 8 == 0 and pps % vsub == 0
    params = dict(nb=nb, n_new=n_new, n_pages=n_pages, page=page, pps=pps, vsub=vsub)
    scratch = [pltpu.VMEM((2, pps, N_HEADS, HEAD_DIM, page), F32),
               pltpu.VMEM((vsub, N_HEADS, HEAD_DIM, page), F32),
               pltpu.VMEM((vsub, N_HEADS, HEAD_DIM, page), F32),
               pltpu.VMEM((2, pps, N_HEADS, page), F32),
               pltpu.SemaphoreType.DMA((2,)),
               pltpu.SemaphoreType.DMA((2,)),
               pltpu.SemaphoreType.DMA((2,)),
               pltpu.VMEM((rows, ATTN_W), BF16),
               pltpu.VMEM((8, LANES), F32),
               pltpu.VMEM((rows, LANES), F32),
               pltpu.VMEM((rows, LANES), F32),
               pltpu.VMEM((rows, LANES), F32),
               pltpu.VMEM((rows, ATTN_W), F32),
               pltpu.VMEM((N_HEADS, LANES), F32),
               pltpu.VMEM((rows, pps * page), F32),
               pltpu.SMEM((pps // vsub,), F32)]
    return params, scratch


def _merge_sample_kernel(x_ref, u_ref, sp_ref, sgp_ref, attn_ref, sga_ref, pw_ref, ps_ref, wo_ref,
                         nf_ref, y_ref, pool_ref, *, n_new, past):
    u = [u_ref[:, POOL_W * i:POOL_W * (i + 1)] for i in range(n_new)]
    hist = [sp_ref[r] for r in range(POOL_HIST)]
    ext = hist + u
    pooled_rows = []
    for i in range(n_new):
        groups = []
        for g, w in enumerate(POOL_WINDOWS):
            sl = slice(POOL_GC * g, POOL_GC * (g + 1))
            end = POOL_HIST + i
            total = ext[end][:, sl]
            for r in range(end - w + 1, end):
                total = total + ext[r][:, sl]
            groups.append(total / float(min(w, past + i + 1)) - u[i][:, sl])
        pooled_rows.append(jnp.concatenate(groups, axis=1))
    pooled = jnp.concatenate(pooled_rows, axis=0)
    cat = lambda ref, n: jnp.concatenate([ref[:, n * i:n * (i + 1)] for i in range(n_new)], axis=0)
    y = _merge(cat(x_ref, D_MODEL), pooled, cat(sgp_ref, POOL_W), cat(attn_ref, ATTN_W),
               cat(sga_ref, ATTN_W), pw_ref, ps_ref[...], wo_ref, nf_ref[...])
    nb = x_ref.shape[0]
    for i in range(n_new):
        y_ref[:, D_MODEL * i:D_MODEL * (i + 1)] = y[nb * i:nb * (i + 1), :]
    new_hist = ext[-POOL_HIST:]
    for r in range(POOL_HIST):
        pool_ref[:, POOL_W * r:POOL_W * (r + 1)] = new_hist[r]


def _merge_sample(xs2, u2, sp2, sgp2, attn2, sga2, pw, ps, wo, nf, *, n_new, past):
    nb = xs2.shape[0]
    return pl.pallas_call(
        functools.partial(_merge_sample_kernel, n_new=n_new, past=past),
        out_shape=(jax.ShapeDtypeStruct((nb, n_new * D_MODEL), F32),
                   jax.ShapeDtypeStruct((nb, POOL_HIST * POOL_W), F32)),
        compiler_params=pltpu.CompilerParams(vmem_limit_bytes=VMEM_LIMIT),
        name="merge_sample",
    )(xs2, u2, sp2, sgp2, attn2, sga2, pw, ps, wo, nf)


def kernel(x_prompt, x_sample, cache_k, cache_v, cache_logf, state_pool, page_table,
           norm_w, w_in, b_forget, pool_w, pool_scale, w_out, norm_f):
    depth = norm_w.shape[0]
    assert depth == 1, "a single layer is supported"
    b_p, seq, _ = x_prompt.shape
    b_s, t_s, _ = x_sample.shape
    n_phys, page = cache_k.shape[1], cache_k.shape[2]
    past = page_table.shape[1] * page
    ll = 0

    wt_all = jnp.swapaxes(w_in[ll], 0, 1)
    wt = wt_all[:W_MAIN].astype(BF16)
    wf = jnp.tile(wt_all[W_MAIN:], (LANES // N_HEADS, 1)).astype(BF16)
    bft = jnp.tile(b_forget[ll], LANES // N_HEADS).reshape(1, LANES).astype(F32)
    nw = norm_w[ll].reshape(1, D_MODEL)
    nf = norm_f.reshape(1, D_MODEL)
    pw = pool_w[ll].astype(BF16)
    ps = pool_scale[ll].reshape(1, POOL_W)
    wo = w_out[ll].astype(BF16)
    pp_np, cc_np = _placement()
    pp = jnp.asarray(pp_np, BF16)
    cc = jnp.asarray(cc_np, F32)

    xp2 = x_prompt.reshape(b_p * seq, D_MODEL)
    seg_np = (np.arange(ATTN_W)[:, None] // HEAD_DIM == np.arange(LANES)[None, :] % N_HEADS)
    seg = jnp.asarray(seg_np, BF16)
    u, sgp, qa, ka, kf, vf, vb, sga, logf, stats = _proj_prompt(xp2, nw, wt, wf, bft, pp, cc, seg,
                                                                seq=seq)
    n_blk = seq // min(ATTN_T, seq)
    plan = _plan(stats, n_batch=b_p, n_blk=n_blk)[:b_p * N_PAIRS, :n_blk].reshape(-1)

    xs2 = x_sample.reshape(b_s, t_s * D_MODEL)
    us, sgps, qs, ks, vs, sgas, lfs, cns = _proj_sample(xs2, nw, wt, wf, bft, n_new=t_s)
    ck = jnp.transpose(cache_k[ll], (0, 2, 3, 1))
    cv = jnp.transpose(cache_v[ll], (0, 2, 3, 1))
    clf_t = jnp.swapaxes(cache_logf[ll], 1, 2)

    per_pos = lambda a: a.reshape(b_s, t_s, a.shape[1] // t_s)
    attn, attn_s = _attention(plan, qa, ka, vb, page_table, per_pos(qs), per_pos(ks), per_pos(vs),
                              per_pos(cns), ck, cv, clf_t, n_batch=b_p, seq=seq)
    yp, pool_p = _merge_prompt(xp2, u, sgp, attn, sga, pw, ps, wo, nf, n_batch=b_p, seq=seq)
    ys, pool_s = _merge_sample(xs2, us, jnp.swapaxes(state_pool[ll], 0, 1), sgps,
                               attn_s.reshape(b_s, t_s * ATTN_W), sgas, pw, ps, wo, nf,
                               n_new=t_s, past=past)

    return (yp.reshape(b_p, seq, D_MODEL),
            ys.reshape(b_s, t_s, D_MODEL),
            jnp.transpose(kf.reshape(1, b_p, N_HEADS, HEAD_DIM, seq), (0, 1, 4, 2, 3)),
            jnp.transpose(vf.reshape(1, b_p, N_HEADS, HEAD_DIM, seq), (0, 1, 4, 2, 3)),
            jnp.transpose(logf, (0, 2, 1)).reshape(1, b_p, seq, N_HEADS),
            pool_p.reshape(1, b_p, POOL_HIST, POOL_W),
            ks.reshape(1, b_s, t_s, N_HEADS, HEAD_DIM),
            vs.reshape(1, b_s, t_s, N_HEADS, HEAD_DIM),
            lfs.reshape(b_s, t_s, LANES)[:, :, :N_HEADS].reshape(1, b_s, t_s, N_HEADS),
            pool_s.reshape(1, b_s, POOL_HIST, POOL_W))
```

```python
import functools
import math

import numpy as np
import jax
import jax.numpy as jnp
from jax import lax
from jax.experimental import pallas as pl
from jax.experimental.pallas import tpu as pltpu

D_MODEL = 1024
POOL_W = 512
ATTN_W = 512
N_HEADS = 8
HEAD_DIM = 64
POOL_WINDOWS = (2, 4, 8, 16)
POOL_GC = 128
POOL_HIST = 15
EPS = 1e-6

LANES = 128
LOG2E = 1.4426950408889634
Q_SCALE = LOG2E / math.sqrt(HEAD_DIM)
NEG = -1e30
N_PAIRS = N_HEADS // 2
W_MAIN = 2 * POOL_W + 4 * ATTN_W
BIAS_LANES = LANES // N_HEADS
BIAS_COLS = 2 * LANES
VMEM_LIMIT = 56 * 1024 * 1024

SKIP_LOG2 = 152.0
NORM_SLACK = 1.02

PROJ_TM = 1024
ATTN_T = 512
DEC_PAGES = 16
DEC_VSUB = 8

BF16 = jnp.bfloat16
F32 = jnp.float32


def _placement():
    pp = np.zeros((LANES, BIAS_COLS), np.float32)
    cc = np.zeros((1, BIAS_COLS), np.float32)
    for h in range(N_HEADS):
        for i in range(3):
            pp[8 * i + h, BIAS_LANES * h + i] = 1.0
            pp[8 * i + h, LANES + BIAS_LANES * h + 3 + i] = -1.0
            cc[0, BIAS_LANES * h + 3 + i] = 1.0
            cc[0, LANES + BIAS_LANES * h + i] = 1.0
    return pp, cc


def _log_sigmoid(x):
    return jnp.minimum(x, 0.0) - jnp.log(1.0 + jnp.exp(-jnp.abs(x)))


def _silu(x):
    return x * jax.nn.sigmoid(x)


def _rmsnorm(xf, w):
    ms = jnp.mean(xf * xf, axis=-1, keepdims=True)
    return xf * lax.rsqrt(ms + EPS) * w


def _project(h, wt_ref, wf_ref):
    def cols(ref, c0, n):
        return lax.dot_general(h, ref[c0:c0 + n, :], (((1,), (1,)), ((), ())),
                               preferred_element_type=F32)
    fl = cols(wf_ref, 0, LANES)
    k = cols(wt_ref, 2 * POOL_W + ATTN_W, ATTN_W)
    v = cols(wt_ref, 2 * POOL_W + 2 * ATTN_W, ATTN_W)
    q = cols(wt_ref, 2 * POOL_W, ATTN_W)
    gp = cols(wt_ref, POOL_W, POOL_W)
    ga = cols(wt_ref, 2 * POOL_W + 3 * ATTN_W, ATTN_W)
    u = cols(wt_ref, 0, POOL_W)
    return u, gp, q, k, v, ga, fl


def _proj_prompt_kernel(x_ref, nw_ref, wt_ref, wf_ref, bf_ref, pp_ref, cc_ref, seg_ref,
                        u_ref, sgp_ref, qa_ref, ka_ref, k_ref, v_ref, vb_ref, sga_ref, logf_ref,
                        st_ref, carry_ref, *, tm, ta, tiles_per_batch):
    i = pl.program_id(0)
    h = _rmsnorm(x_ref[...], nw_ref[...]).astype(BF16)
    u, gp, q, k, v, ga, fl = _project(h, wt_ref, wf_ref)
    u_ref[...] = u
    sgp_ref[...] = _silu(gp).astype(BF16)
    k_ref[0] = jnp.transpose(k)
    v_ref[0] = jnp.transpose(v)
    vb_ref[...] = v.astype(BF16)
    sga_ref[...] = _silu(ga).astype(BF16)
    logf = _log_sigmoid(fl + bf_ref[...])
    logf_ref[0] = jnp.transpose(logf)[0:N_HEADS, :]

    row = lax.broadcasted_iota(jnp.int32, (tm, LANES), 0)
    acc = logf
    d = 1
    while d < tm:
        acc = acc + jnp.where(row >= d, pltpu.roll(acc, d, axis=0), 0.0)
        d *= 2
    @pl.when(i % tiles_per_batch == 0)
    def _():
        carry_ref[...] = jnp.zeros(carry_ref.shape, F32)

    f_run = acc + carry_ref[0:1, :]
    carry_ref[...] = jnp.broadcast_to(f_run[tm - 1:tm, :], carry_ref.shape)

    f2 = f_run * LOG2E
    p1 = f2.astype(BF16).astype(F32)
    r1 = f2 - p1
    p2 = r1.astype(BF16).astype(F32)
    p3 = (r1 - p2).astype(BF16).astype(F32)
    lane = lax.broadcasted_iota(jnp.int32, (tm, LANES), 1)
    pieces = jnp.where(lane < 8, p1, jnp.where(lane < 16, p2, jnp.where(lane < 24, p3, 0.0)))
    bias = jnp.dot(pieces.astype(BF16), pp_ref[...], preferred_element_type=F32) + cc_ref[...]

    qs = q * Q_SCALE

    qn2 = jnp.dot((qs * qs).astype(BF16), seg_ref[...], preferred_element_type=F32)
    kn2 = jnp.dot((k * k).astype(BF16), seg_ref[...], preferred_element_type=F32)
    srow = lax.broadcasted_iota(jnp.int32, (8, LANES), 0)
    for sb in range(tm // ta):
        r0, r1 = sb * ta, (sb + 1) * ta
        qmax = jnp.sqrt(jnp.max(qn2[r0:r1], axis=0, keepdims=True))
        kmax = jnp.sqrt(jnp.max(kn2[r0:r1], axis=0, keepdims=True))
        st_ref[8 * sb:8 * (sb + 1), :] = jnp.where(
            srow == 0, qmax, jnp.where(srow == 1, kmax, jnp.where(
                srow == 2, f2[r0:r0 + 1, :], jnp.where(srow == 3, f2[r1 - 1:r1, :], 0.0))))

    first_half = lane < HEAD_DIM
    q_bias = bias[:, 0:LANES].astype(BF16)
    k_bias = bias[:, LANES:2 * LANES]
    for p in range(N_PAIRS):
        qa_ref[:, 2 * LANES * p:2 * LANES * p + LANES] = qs[:, LANES * p:LANES * (p + 1)].astype(BF16)
        qa_ref[:, 2 * LANES * p + LANES:2 * LANES * (p + 1)] = q_bias
        kp = k[:, LANES * p:LANES * (p + 1)]
        for e in range(2):
            hh = 2 * p + e
            keep = first_half if e == 0 else jnp.logical_not(first_half)
            ka_ref[:, 2 * LANES * hh:2 * LANES * hh + LANES] = jnp.where(keep, kp, 0.0).astype(BF16)
            own = jnp.logical_and(lane >= BIAS_LANES * hh, lane < BIAS_LANES * (hh + 1))
            ka_ref[:, 2 * LANES * hh + LANES:2 * LANES * (hh + 1)] = (
                jnp.where(own, k_bias, 0.0).astype(BF16))


def _proj_prompt(x2, nw, wt, wf, bft, pp, cc, seg, *, seq):
    rows = x2.shape[0]
    tm = min(PROJ_TM, seq)
    ta = min(ATTN_T, seq)
    assert seq % tm == 0 and rows % seq == 0 and tm % ta == 0 and tm % LANES == 0
    n_batch = rows // seq
    row_blk = lambda n: pl.BlockSpec((tm, n), lambda i: (i, 0))
    const = lambda shape: pl.BlockSpec(shape, lambda i: (0, 0))
    out_shape = (
        jax.ShapeDtypeStruct((rows, POOL_W), F32),
        jax.ShapeDtypeStruct((rows, POOL_W), BF16),
        jax.ShapeDtypeStruct((rows, N_PAIRS * 2 * LANES), BF16),
        jax.ShapeDtypeStruct((rows, N_HEADS * 2 * LANES), BF16),
        jax.ShapeDtypeStruct((n_batch, ATTN_W, seq), F32),
        jax.ShapeDtypeStruct((n_batch, ATTN_W, seq), F32),
        jax.ShapeDtypeStruct((rows, ATTN_W), BF16),
        jax.ShapeDtypeStruct((rows, ATTN_W), BF16),
        jax.ShapeDtypeStruct((n_batch, N_HEADS, seq), F32),
        jax.ShapeDtypeStruct((rows // ta * 8, LANES), F32),
    )
    tpb = seq // tm
    t_minor = lambda n: pl.BlockSpec((1, n, tm), lambda i: (i // tpb, 0, i % tpb))
    out_specs = (row_blk(POOL_W), row_blk(POOL_W), row_blk(N_PAIRS * 2 * LANES),
                 row_blk(N_HEADS * 2 * LANES), t_minor(ATTN_W), t_minor(ATTN_W), row_blk(ATTN_W),
                 row_blk(ATTN_W), t_minor(N_HEADS),
                 pl.BlockSpec((tm // ta * 8, LANES), lambda i: (i, 0)))
    return pl.pallas_call(
        functools.partial(_proj_prompt_kernel, tm=tm, ta=ta, tiles_per_batch=seq // tm),
        grid=(rows // tm,),
        in_specs=[row_blk(D_MODEL), const((1, D_MODEL)), const((W_MAIN, D_MODEL)),
                  const((LANES, D_MODEL)), const((1, LANES)),
                  const((LANES, BIAS_COLS)), const((1, BIAS_COLS)), const((ATTN_W, LANES))],
        out_specs=out_specs,
        out_shape=out_shape,
        scratch_shapes=[pltpu.VMEM((8, LANES), F32)],
        compiler_params=pltpu.CompilerParams(dimension_semantics=("arbitrary",),
                                             vmem_limit_bytes=VMEM_LIMIT),
        name="proj_prompt",
    )(x2, nw, wt, wf, bft, pp, cc, seg)


def _lane_tile(x, reps):
    return jnp.concatenate([x] * reps, axis=1)


def _plan_kernel(st_ref, o_ref, *, n_batch, n_blk):
    row = lax.broadcasted_iota(jnp.int32, (LANES, LANES), 0).astype(F32)
    col = lax.broadcasted_iota(jnp.int32, (LANES, LANES), 1).astype(F32)
    out = jnp.zeros((LANES, LANES), F32)
    for b in range(n_batch):
        def stat(r):
            v = st_ref[pl.ds((b * n_blk) * 8 + r, n_blk, stride=8), :]
            return jnp.concatenate([v, jnp.zeros((LANES - n_blk, LANES), F32)], axis=0)
        qm, km, ft0, fs1 = stat(0), stat(1), stat(2), stat(3)
        km_t, fs1_t = jnp.transpose(km), jnp.transpose(fs1)
        for p in range(N_PAIRS):
            first = row
            for h in (2 * p, 2 * p + 1):
                qcol = qm[:, h:h + 1]
                ub = (NORM_SLACK * (qcol * km_t[h:h + 1, :] + qcol * km[:, h:h + 1])
                      + ft0[:, h:h + 1] - fs1_t[h:h + 1, :])
                needed = jnp.logical_and(ub >= -SKIP_LOG2, col < row)
                first = jnp.minimum(first, jnp.where(needed, col, row))
            start = jnp.min(first, axis=1, keepdims=True)
            out = jnp.where(col == b * N_PAIRS + p, start, out)
    o_ref[...] = jnp.transpose(out).astype(jnp.int32)


def _plan(stats, *, n_batch, n_blk):
    assert n_blk <= LANES and n_batch * N_PAIRS <= LANES
    return pl.pallas_call(
        functools.partial(_plan_kernel, n_batch=n_batch, n_blk=n_blk),
        out_shape=jax.ShapeDtypeStruct((LANES, LANES), jnp.int32),
        name="attn_plan",
    )(stats)


def _attn_kernel(plan_ref, pt_ref, qa_ref, ka_ref, vb_ref, qd_ref, kn_ref, vn_ref, cn_ref,
                 ck_hbm, cv_hbm, clf_hbm, o_ref, od_ref, m_ref, l_ref, acc_ref, *dec_scratch,
                 t, nq, n_steps, dec):
    b, pr, i = pl.program_id(0), pl.program_id(1), pl.program_id(2)
    step = (b * N_PAIRS + pr) * nq + i
    dec_prologue, (dec_head, dec_scores, dec_tail), n_chunks = _decode_fns(
        pt_ref, qd_ref, kn_ref, vn_ref, cn_ref, ck_hbm, cv_hbm, clf_hbm, od_ref, *dec_scratch, **dec)
    per_step = -(-n_chunks // n_steps)
    interleave = n_chunks == n_steps

    @pl.when(step == 0)
    def _():
        dec_prologue()

    if interleave:
        dec_head(step)
    else:
        for k in range(per_step):
            g = step * per_step + k

            @pl.when(g < n_chunks)
            def _(g=g):
                dec_head(g)
                dec_scores(g)
                dec_tail(g)

    q = qa_ref[...]
    lane = lax.broadcasted_iota(jnp.int32, (t, LANES), 1)

    def scores(kstart, width, diag_col, e):
        kblk = ka_ref[pl.ds(kstart, width), 2 * LANES * e:2 * LANES * (e + 1)]
        s = lax.dot_general(q, kblk, (((1,), (1,)), ((), ())), preferred_element_type=F32)
        if diag_col is not None:
            r = lax.broadcasted_iota(jnp.int32, (t, width - diag_col), 0)
            c = lax.broadcasted_iota(jnp.int32, (t, width - diag_col), 1)
            diag = jnp.where(c <= r, s[:, diag_col:], NEG)
            s = diag if diag_col == 0 else jnp.concatenate([s[:, :diag_col], diag], axis=1)
        return s

    def block(kstart, width, diag_col):
        vblk = vb_ref[pl.ds(kstart, width), :]
        for e in range(2):
            s = scores(kstart, width, diag_col, e)
            m_prev = m_ref[e]
            m_next = jnp.maximum(m_prev, jnp.max(s, axis=1, keepdims=True))
            alpha = jnp.exp2(m_prev - m_next)
            p = jnp.exp2(s - _lane_tile(m_next, width // LANES))
            l_ref[e] = alpha * l_ref[e] + jnp.sum(p, axis=1, keepdims=True)
            m_ref[e] = m_next
            acc_ref[e] = alpha * acc_ref[e] + jnp.dot(p.astype(BF16), vblk,
                                                      preferred_element_type=F32)

    def body(j, carry):
        block(pl.multiple_of(j * t, t), t, None)
        return carry

    first = plan_ref[(b * N_PAIRS + pr) * nq + i]
    one_before = jnp.logical_and(i > 0, first == i - 1)

    m_ref[...] = jnp.full(m_ref.shape, NEG, F32)
    l_ref[...] = jnp.zeros(l_ref.shape, F32)
    acc_ref[...] = jnp.zeros(acc_ref.shape, F32)

    @pl.when(one_before)
    def _():
        if interleave:
            dec_scores(step)
        block(pl.multiple_of((i - 1) * t, t), 2 * t, t)

    @pl.when(jnp.logical_not(one_before))
    def _():
        if interleave:
            dec_scores(step)
        lax.fori_loop(first, i, body, 0)
        block(pl.multiple_of(i * t, t), t, 0)

    o = jnp.where(lane < HEAD_DIM, acc_ref[0] / l_ref[0], acc_ref[1] / l_ref[1])
    o_ref[...] = o.astype(o_ref.dtype)
    if interleave:
        dec_tail(step)


def _attention(plan, qa, ka, vb, page_table, qd, kn, vn, cn, ck, cv, clf_t, *, n_batch, seq):
    t = min(ATTN_T, seq)
    assert seq % t == 0
    nq = seq // t
    nb, n_new, _ = qd.shape
    dec, dec_scratch = _decode_setup(nb, n_new, page_table.shape[1], ck.shape[3])
    whole = lambda n: pl.BlockSpec((nb, n_new, n), lambda b, p, i, plan, pt: (0, 0, 0))
    hbm = pl.BlockSpec(memory_space=pl.ANY)
    grid_spec = pltpu.PrefetchScalarGridSpec(
        num_scalar_prefetch=2,
        grid=(n_batch, N_PAIRS, nq),
        in_specs=[pl.BlockSpec((t, 2 * LANES), lambda b, p, i, plan, pt: (b * nq + i, p)),
                  pl.BlockSpec((seq, 4 * LANES), lambda b, p, i, plan, pt: (b, p)),
                  pl.BlockSpec((seq, LANES), lambda b, p, i, plan, pt: (b, p)),
                  whole(ATTN_W), whole(ATTN_W), whole(ATTN_W), whole(LANES), hbm, hbm, hbm],
        out_specs=(pl.BlockSpec((t, LANES), lambda b, p, i, plan, pt: (b * nq + i, p)),
                   whole(ATTN_W)),
        scratch_shapes=[pltpu.VMEM((2, t, LANES), F32), pltpu.VMEM((2, t, LANES), F32),
                        pltpu.VMEM((2, t, LANES), F32)] + dec_scratch)
    return pl.pallas_call(
        functools.partial(_attn_kernel, t=t, nq=nq, n_steps=n_batch * N_PAIRS * nq, dec=dec),
        grid_spec=grid_spec,
        out_shape=(jax.ShapeDtypeStruct((n_batch * seq, ATTN_W), BF16),
                   jax.ShapeDtypeStruct((nb, n_new, ATTN_W), F32)),
        compiler_params=pltpu.CompilerParams(
            dimension_semantics=("arbitrary", "arbitrary", "arbitrary"),
            vmem_limit_bytes=VMEM_LIMIT),
        name="attention",
    )(plan, page_table.reshape(-1), qa, ka, vb, qd, kn, vn, cn, ck, cv, clf_t)


def _merge(x, pooled, sgp, attn, sga, pw_ref, ps, wo_ref, nf):
    mixed = [jnp.dot(pooled[:, POOL_GC * g:POOL_GC * (g + 1)].astype(BF16), pw_ref[g],
                     preferred_element_type=F32) for g in range(len(POOL_WINDOWS))]
    pool_out = jnp.concatenate(mixed, axis=1) * ps
    mix = jnp.concatenate([(pool_out * sgp.astype(F32)).astype(BF16),
                           (attn.astype(F32) * sga.astype(F32)).astype(BF16)], axis=1)
    xo = x + jnp.dot(mix, wo_ref[...], preferred_element_type=F32)
    return _rmsnorm(xo, nf)


def _merge_prompt_kernel(x_ref, u_ref, uh_ref, sgp_ref, attn_ref, sga_ref, pw_ref, ps_ref, wo_ref,
                         nf_ref, y_ref, pool_ref, *, tm, tiles_per_batch):
    i = pl.program_id(0)
    ti = i % tiles_per_batch
    u = u_ref[...]
    halo = jnp.where(ti == 0, 0.0, uh_ref[...])
    ext = jnp.concatenate([halo, u], axis=0)
    pos = ti * tm + lax.broadcasted_iota(jnp.int32, (tm, 1), 0)
    groups = []
    level = ext
    shift = 1
    for g, w in enumerate(POOL_WINDOWS):
        while shift < w:
            level = level + pltpu.roll(level, shift, axis=0)
            shift *= 2
        cnt = jnp.minimum(w, pos + 1).astype(F32)
        groups.append(level[16:, 0:POOL_GC] / cnt - u[:, POOL_GC * g:POOL_GC * (g + 1)])
        level = level[:, POOL_GC:]
    pooled = jnp.concatenate(groups, axis=1)
    y_ref[...] = _merge(x_ref[...], pooled, sgp_ref[...], attn_ref[...], sga_ref[...], pw_ref,
                        ps_ref[...], wo_ref, nf_ref[...])

    @pl.when(ti == tiles_per_batch - 1)
    def _():
        pool_ref[0] = u_ref[pl.ds(tm - POOL_HIST, POOL_HIST), :]


def _merge_prompt(x2, u, sgp, attn, sga, pw, ps, wo, nf, *, n_batch, seq):
    rows = x2.shape[0]
    tm = min(PROJ_TM, seq)
    tpb = seq // tm
    halo_blocks = tm // 16
    row_blk = lambda n: pl.BlockSpec((tm, n), lambda i: (i, 0))
    const2 = lambda shape: pl.BlockSpec(shape, lambda i: (0, 0))
    return pl.pallas_call(
        functools.partial(_merge_prompt_kernel, tm=tm, tiles_per_batch=tpb),
        grid=(rows // tm,),
        in_specs=[row_blk(D_MODEL), row_blk(POOL_W),
                  pl.BlockSpec((16, POOL_W), lambda i: (jnp.maximum(i * halo_blocks - 1, 0), 0)),
                  row_blk(POOL_W), row_blk(ATTN_W), row_blk(ATTN_W),
                  pl.BlockSpec((len(POOL_WINDOWS), POOL_GC, POOL_GC), lambda i: (0, 0, 0)),
                  const2((1, POOL_W)), const2((D_MODEL, D_MODEL)), const2((1, D_MODEL))],
        out_specs=(row_blk(D_MODEL),
                   pl.BlockSpec((1, POOL_HIST, POOL_W), lambda i: (i // tpb, 0, 0))),
        out_shape=(jax.ShapeDtypeStruct((rows, D_MODEL), F32),
                   jax.ShapeDtypeStruct((n_batch, POOL_HIST, POOL_W), F32)),
        compiler_params=pltpu.CompilerParams(dimension_semantics=("arbitrary",),
                                             vmem_limit_bytes=VMEM_LIMIT),
        name="merge_prompt",
    )(x2, u, u, sgp, attn, sga, pw, ps, wo, nf)


def _proj_sample_kernel(x_ref, nw_ref, wt_ref, wf_ref, bf_ref,
                        u_ref, sgp_ref, q_ref, k_ref, v_ref, sga_ref, logf_ref, cn_ref, *, n_new):
    nb = x_ref.shape[0]
    x = jnp.concatenate([x_ref[:, D_MODEL * i:D_MODEL * (i + 1)] for i in range(n_new)], axis=0)
    h = _rmsnorm(x, nw_ref[...]).astype(BF16)
    u, gp, q, k, v, ga, fl = _project(h, wt_ref, wf_ref)
    logf = _log_sigmoid(fl + bf_ref[...])
    outs = ((u_ref, u), (sgp_ref, _silu(gp).astype(BF16)), (q_ref, q * Q_SCALE), (k_ref, k),
            (v_ref, v), (sga_ref, _silu(ga).astype(BF16)), (logf_ref, logf))
    cn = jnp.zeros((nb, LANES), F32)
    for i in range(n_new):
        for ref, val in outs:
            n = val.shape[1]
            ref[:, n * i:n * (i + 1)] = val[nb * i:nb * (i + 1), :]
        cn = cn + logf[nb * i:nb * (i + 1), :]
        cn_ref[:, LANES * i:LANES * (i + 1)] = cn


def _proj_sample(xs2, nw, wt, wf, bft, *, n_new):
    nb = xs2.shape[0]
    widths = (POOL_W, POOL_W, ATTN_W, ATTN_W, ATTN_W, ATTN_W, LANES, LANES)
    dtypes = (F32, BF16, F32, F32, F32, BF16, F32, F32)
    return pl.pallas_call(
        functools.partial(_proj_sample_kernel, n_new=n_new),
        out_shape=tuple(jax.ShapeDtypeStruct((nb, n_new * n), dt) for n, dt in zip(widths, dtypes)),
        compiler_params=pltpu.CompilerParams(vmem_limit_bytes=VMEM_LIMIT),
        name="proj_sample",
    )(xs2, nw, wt, wf, bft)


def _decode_fns(pt_ref, q_ref, kn_ref, vn_ref, cn_ref, ck_hbm, cv_hbm, clf_hbm, o_ref,
                kbuf, vpre, vdem, lfbuf, ksem, lfsem, vsem,
                qbd_ref, cn8_ref, cncol_ref, m_ref, l_ref, acc_ref, tail_ref, base_ref, gap_ref,
                *, nb, n_new, n_pages, page, pps, vsub):
    n_chunks = n_pages // pps
    n_groups = pps // vsub
    total = nb * n_chunks
    n_keys = pps * page
    g_keys = vsub * page
    rows = n_new * N_HEADS
    row_w = lax.broadcasted_iota(jnp.int32, (rows, ATTN_W), 0)
    lane_w = lax.broadcasted_iota(jnp.int32, (rows, ATTN_W), 1)
    head_lanes = (row_w % N_HEADS) == (lane_w // HEAD_DIM)
    row1 = lax.broadcasted_iota(jnp.int32, (rows, 1), 0)

    def page_index(g, r):
        b, c = lax.div(g, n_chunks), lax.rem(g, n_chunks)
        return pt_ref[b * n_pages + n_pages - 1 - (c * pps + r)]

    def k_copies(g, slot):
        cps = []
        for r in range(pps):
            idx = page_index(g, r)
            cps.append(pltpu.make_async_copy(ck_hbm.at[idx], kbuf.at[slot, r], ksem.at[slot]))
            cps.append(pltpu.make_async_copy(clf_hbm.at[idx], lfbuf.at[slot, r], lfsem.at[slot]))
        return cps

    def v_copies(g, grp, dst, sem):
        return [pltpu.make_async_copy(cv_hbm.at[page_index(g, grp * vsub + r)], dst.at[r], sem)
                for r in range(vsub)]

    def row(ref, b, i, n):
        return ref[b, i:i + 1, :]

    def init(b):
        qrep = jnp.concatenate(
            [jnp.broadcast_to(row(q_ref, b, i, ATTN_W), (N_HEADS, ATTN_W)) for i in range(n_new)],
            axis=0)
        qbd = jnp.where(head_lanes, qrep, 0.0)
        qbd_ref[...] = qbd.astype(BF16)
        cn8_ref[...] = jnp.zeros(cn8_ref.shape, F32)
        for i in range(n_new):
            cn8_ref[i:i + 1, :] = row(cn_ref, b, i, LANES)
        cnt = jnp.transpose(cn8_ref[...])[0:rows, :] * LOG2E
        cncol = jnp.zeros((rows, 1), F32)
        for i in range(n_new):
            cncol = jnp.where(row1 // N_HEADS == i, cnt[:, i:i + 1], cncol)
        cncol_ref[...] = jnp.broadcast_to(cncol, cncol_ref.shape)
        s_new = []
        for j in range(n_new):
            sj = (jnp.sum(qbd * row(kn_ref, b, j, ATTN_W), axis=1, keepdims=True)
                  + cncol - cnt[:, j:j + 1])
            s_new.append(jnp.where(row1 // N_HEADS >= j, sj, NEG))
        m0 = functools.reduce(jnp.maximum, s_new)
        l0 = jnp.zeros((rows, 1), F32)
        a0 = jnp.zeros((rows, ATTN_W), F32)
        for j in range(n_new):
            pj = jnp.exp2(s_new[j] - m0)
            l0 = l0 + pj
            a0 = a0 + pj * row(vn_ref, b, j, ATTN_W)
        m_ref[...] = jnp.broadcast_to(m0, m_ref.shape)
        l_ref[...] = jnp.broadcast_to(l0, l_ref.shape)
        acc_ref[...] = a0
        tail_ref[...] = jnp.zeros(tail_ref.shape, F32)

    def finish(b):
        o = acc_ref[...] / _lane_tile(l_ref[...], ATTN_W // LANES)
        o = jnp.where(head_lanes, o, 0.0)
        for i in range(n_new):
            o_ref[b, i:i + 1, :] = jnp.sum(o[N_HEADS * i:N_HEADS * (i + 1), :], axis=0,
                                           keepdims=True)

    lane8 = lax.broadcasted_iota(jnp.int32, (N_HEADS, page), 1)

    def head(g):
        slot = lax.rem(g, 2)
        b, c = lax.div(g, n_chunks), lax.rem(g, n_chunks)

        @pl.when(g + 1 < total)
        def _():
            for cp in k_copies(g + 1, 1 - slot):
                cp.start()

        @pl.when(c == 0)
        def _():
            init(b)
            for cp in v_copies(g, 0, vpre, vsem.at[0]):
                cp.wait()

        for cp in k_copies(g, slot):
            cp.wait()

    def scores(g):
        slot = lax.rem(g, 2)
        kcat = jnp.concatenate([kbuf[slot, r].reshape(ATTN_W, page).astype(BF16)
                                for r in range(pps)], axis=1)
        qk = jnp.dot(qbd_ref[...], kcat, preferred_element_type=F32)
        tail = tail_ref[...]
        tails = []
        for r in range(pps):
            tails.append(tail)
            tail = tail + jnp.sum(lfbuf[slot, r] * LOG2E, axis=1, keepdims=True)
        tail_ref[...] = tail
        base = qk + jnp.concatenate([jnp.concatenate(tails, axis=1)] * n_new, axis=0) + cncol_ref[:, 0:1]
        base_ref[...] = base
        over = base - _lane_tile(m_ref[...], n_keys // LANES)
        for h in range(n_groups):
            gap_ref[h] = jnp.max(over[:, g_keys * h:g_keys * (h + 1)])

    def tail_phase(g):
        slot = lax.rem(g, 2)
        b, c = lax.div(g, n_chunks), lax.rem(g, n_chunks)
        live = [gap_ref[h] >= -SKIP_LOG2 for h in range(n_groups)]

        def update(h, vsrc):
            g_parts = []
            for r in range(vsub * h, vsub * (h + 1)):
                lf = lfbuf[slot, r] * LOG2E
                incl = lf
                d = 1
                while d < page:
                    incl = incl + jnp.where(lane8 + d < page, pltpu.roll(incl, page - d, axis=1), 0.0)
                    d *= 2
                g_parts.append(incl - lf)
            s = (base_ref[:, g_keys * h:g_keys * (h + 1)]
                 + jnp.concatenate([jnp.concatenate(g_parts, axis=1)] * n_new, axis=0))
            m_prev = m_ref[...]
            m_next = jnp.maximum(m_prev, jnp.max(s, axis=1, keepdims=True))
            alpha = jnp.exp2(m_prev - m_next)
            p = jnp.exp2(s - _lane_tile(m_next, g_keys // LANES))
            l_ref[...] = alpha * l_ref[...] + jnp.sum(p, axis=1, keepdims=True)
            m_ref[...] = m_next
            vcat = jnp.concatenate([vsrc[r].reshape(ATTN_W, page).astype(BF16)
                                    for r in range(vsub)], axis=1)
            acc_ref[...] = (_lane_tile(alpha, ATTN_W // LANES) * acc_ref[...]
                            + lax.dot_general(p.astype(BF16), vcat, (((1,), (1,)), ((), ())),
                                              preferred_element_type=F32))

        def fetch_and_update(h):
            cps = v_copies(g, h, vdem, vsem.at[1])
            for cp in cps:
                cp.start()
            for cp in cps:
                cp.wait()
            update(h, vdem)

        for h in range(n_groups):
            if h == 0:
                @pl.when(jnp.logical_and(live[0], c == 0))
                def _():
                    update(0, vpre)

                @pl.when(jnp.logical_and(live[0], c > 0))
                def _():
                    fetch_and_update(0)
            else:
                @pl.when(live[h])
                def _(h=h):
                    fetch_and_update(h)

        @pl.when(c == n_chunks - 1)
        def _():
            finish(b)

        @pl.when(jnp.logical_and(c == 0, b + 1 < nb))
        def _():
            for cp in v_copies(g + n_chunks, 0, vpre, vsem.at[0]):
                cp.start()

    def prologue():
        for cp in k_copies(0, 0):
            cp.start()
        for cp in v_copies(0, 0, vpre, vsem.at[0]):
            cp.start()

    return prologue, (head, scores, tail_phase), total


def _decode_setup(nb, n_new, n_pages, page):
    pps = DEC_PAGES
    while n_pages % pps:
        pps //= 2
    vsub = min(DEC_VSUB, pps)
    rows = n_new * N_HEADS
    assert page % LANES == 0 and rows % 8 == 0 and pps % vsub == 0
    params = dict(nb=nb, n_new=n_new, n_pages=n_pages, page=page, pps=pps, vsub=vsub)
    scratch = [pltpu.VMEM((2, pps, N_HEADS, HEAD_DIM, page), F32),
               pltpu.VMEM((vsub, N_HEADS, HEAD_DIM, page), F32),
               pltpu.VMEM((vsub, N_HEADS, HEAD_DIM, page), F32),
               pltpu.VMEM((2, pps, N_HEADS, page), F32),
               pltpu.SemaphoreType.DMA((2,)),
               pltpu.SemaphoreType.DMA((2,)),
               pltpu.SemaphoreType.DMA((2,)),
               pltpu.VMEM((rows, ATTN_W), BF16),
               pltpu.VMEM((8, LANES), F32),
               pltpu.VMEM((rows, LANES), F32),
               pltpu.VMEM((rows, LANES), F32),
               pltpu.VMEM((rows, LANES), F32),
               pltpu.VMEM((rows, ATTN_W), F32),
               pltpu.VMEM((N_HEADS, LANES), F32),
               pltpu.VMEM((rows, pps * page), F32),
               pltpu.SMEM((pps // vsub,), F32)]
    return params, scratch


def _merge_sample_kernel(x_ref, u_ref, sp_ref, sgp_ref, attn_ref, sga_ref, pw_ref, ps_ref, wo_ref,
                         nf_ref, y_ref, pool_ref, *, n_new, past):
    u = [u_ref[:, POOL_W * i:POOL_W * (i + 1)] for i in range(n_new)]
    hist = [sp_ref[r] for r in range(POOL_HIST)]
    ext = hist + u
    pooled_rows = []
    for i in range(n_new):
        groups = []
        for g, w in enumerate(POOL_WINDOWS):
            sl = slice(POOL_GC * g, POOL_GC * (g + 1))
            end = POOL_HIST + i
            total = ext[end][:, sl]
            for r in range(end - w + 1, end):
                total = total + ext[r][:, sl]
            groups.append(total / float(min(w, past + i + 1)) - u[i][:, sl])
        pooled_rows.append(jnp.concatenate(groups, axis=1))
    pooled = jnp.concatenate(pooled_rows, axis=0)
    cat = lambda ref, n: jnp.concatenate([ref[:, n * i:n * (i + 1)] for i in range(n_new)], axis=0)
    y = _merge(cat(x_ref, D_MODEL), pooled, cat(sgp_ref, POOL_W), cat(attn_ref, ATTN_W),
               cat(sga_ref, ATTN_W), pw_ref, ps_ref[...], wo_ref, nf_ref[...])
    nb = x_ref.shape[0]
    for i in range(n_new):
        y_ref[:, D_MODEL * i:D_MODEL * (i + 1)] = y[nb * i:nb * (i + 1), :]
    new_hist = ext[-POOL_HIST:]
    for r in range(POOL_HIST):
        pool_ref[:, POOL_W * r:POOL_W * (r + 1)] = new_hist[r]


def _merge_sample(xs2, u2, sp2, sgp2, attn2, sga2, pw, ps, wo, nf, *, n_new, past):
    nb = xs2.shape[0]
    return pl.pallas_call(
        functools.partial(_merge_sample_kernel, n_new=n_new, past=past),
        out_shape=(jax.ShapeDtypeStruct((nb, n_new * D_MODEL), F32),
                   jax.ShapeDtypeStruct((nb, POOL_HIST * POOL_W), F32)),
        compiler_params=pltpu.CompilerParams(vmem_limit_bytes=VMEM_LIMIT),
        name="merge_sample",
    )(xs2, u2, sp2, sgp2, attn2, sga2, pw, ps, wo, nf)


def kernel(x_prompt, x_sample, cache_k, cache_v, cache_logf, state_pool, page_table,
           norm_w, w_in, b_forget, pool_w, pool_scale, w_out, norm_f):
    depth = norm_w.shape[0]
    assert depth == 1, "a single layer is supported"
    b_p, seq, _ = x_prompt.shape
    b_s, t_s, _ = x_sample.shape
    n_phys, page = cache_k.shape[1], cache_k.shape[2]
    past = page_table.shape[1] * page
    ll = 0

    wt_all = jnp.swapaxes(w_in[ll], 0, 1)
    wt = wt_all[:W_MAIN].astype(BF16)
    wf = jnp.tile(wt_all[W_MAIN:], (LANES // N_HEADS, 1)).astype(BF16)
    bft = jnp.tile(b_forget[ll], LANES // N_HEADS).reshape(1, LANES).astype(F32)
    nw = norm_w[ll].reshape(1, D_MODEL)
    nf = norm_f.reshape(1, D_MODEL)
    pw = pool_w[ll].astype(BF16)
    ps = pool_scale[ll].reshape(1, POOL_W)
    wo = w_out[ll].astype(BF16)
    pp_np, cc_np = _placement()
    pp = jnp.asarray(pp_np, BF16)
    cc = jnp.asarray(cc_np, F32)

    xp2 = x_prompt.reshape(b_p * seq, D_MODEL)
    seg_np = (np.arange(ATTN_W)[:, None] // HEAD_DIM == np.arange(LANES)[None, :] % N_HEADS)
    seg = jnp.asarray(seg_np, BF16)
    u, sgp, qa, ka, kf, vf, vb, sga, logf, stats = _proj_prompt(xp2, nw, wt, wf, bft, pp, cc, seg,
                                                                seq=seq)
    n_blk = seq // min(ATTN_T, seq)
    plan = _plan(stats, n_batch=b_p, n_blk=n_blk)[:b_p * N_PAIRS, :n_blk].reshape(-1)

    xs2 = x_sample.reshape(b_s, t_s * D_MODEL)
    us, sgps, qs, ks, vs, sgas, lfs, cns = _proj_sample(xs2, nw, wt, wf, bft, n_new=t_s)
    ck = jnp.transpose(cache_k[ll], (0, 2, 3, 1))
    cv = jnp.transpose(cache_v[ll], (0, 2, 3, 1))
    clf_t = jnp.swapaxes(cache_logf[ll], 1, 2)

    per_pos = lambda a: a.reshape(b_s, t_s, a.shape[1] // t_s)
    attn, attn_s = _attention(plan, qa, ka, vb, page_table, per_pos(qs), per_pos(ks), per_pos(vs),
                              per_pos(cns), ck, cv, clf_t, n_batch=b_p, seq=seq)
    yp, pool_p = _merge_prompt(xp2, u, sgp, attn, sga, pw, ps, wo, nf, n_batch=b_p, seq=seq)
    ys, pool_s = _merge_sample(xs2, us, jnp.swapaxes(state_pool[ll], 0, 1), sgps,
                               attn_s.reshape(b_s, t_s * ATTN_W), sgas, pw, ps, wo, nf,
                               n_new=t_s, past=past)

    return (yp.reshape(b_p, seq, D_MODEL),
            ys.reshape(b_s, t_s, D_MODEL),
            jnp.transpose(kf.reshape(1, b_p, N_HEADS, HEAD_DIM, seq), (0, 1, 4, 2, 3)),
            jnp.transpose(vf.reshape(1, b_p, N_HEADS, HEAD_DIM, seq), (0, 1, 4, 2, 3)),
            jnp.transpose(logf, (0, 2, 1)).reshape(1, b_p, seq, N_HEADS),
            pool_p.reshape(1, b_p, POOL_HIST, POOL_W),
            ks.reshape(1, b_s, t_s, N_HEADS, HEAD_DIM),
            vs.reshape(1, b_s, t_s, N_HEADS, HEAD_DIM),
            lfs.reshape(b_s, t_s, LANES)[:, :, :N_HEADS].reshape(1, b_s, t_s, N_HEADS),
            pool_s.reshape(1, b_s, POOL_HIST, POOL_W))
```

```python
import functools
import math

import numpy as np
import jax
import jax.numpy as jnp
from jax import lax
from jax.experimental import pallas as pl
from jax.experimental.pallas import tpu as pltpu

D_MODEL = 1024
POOL_W = 512
ATTN_W = 512
N_HEADS = 8
HEAD_DIM = 64
POOL_WINDOWS = (2, 4, 8, 16)
POOL_GC = 128
POOL_HIST = 15
EPS = 1e-6

LANES = 128
LOG2E = 1.4426950408889634
Q_SCALE = LOG2E / math.sqrt(HEAD_DIM)
NEG = -1e30
N_PAIRS = N_HEADS // 2
W_MAIN = 2 * POOL_W + 4 * ATTN_W
BIAS_LANES = LANES // N_HEADS
BIAS_COLS = 2 * LANES
VMEM_LIMIT = 56 * 1024 * 1024

SKIP_LOG2 = 152.0
NORM_SLACK = 1.02

PROJ_TM = 1024
ATTN_T = 512
DEC_PAGES = 16
DEC_VSUB = 8

BF16 = jnp.bfloat16
F32 = jnp.float32


def _placement():
    pp = np.zeros((LANES, BIAS_COLS), np.float32)
    cc = np.zeros((1, BIAS_COLS), np.float32)
    for h in range(N_HEADS):
        for i in range(3):
            pp[8 * i + h, BIAS_LANES * h + i] = 1.0
            pp[8 * i + h, LANES + BIAS_LANES * h + 3 + i] = -1.0
            cc[0, BIAS_LANES * h + 3 + i] = 1.0
            cc[0, LANES + BIAS_LANES * h + i] = 1.0
    return pp, cc


def _log_sigmoid(x):
    return jnp.minimum(x, 0.0) - jnp.log(1.0 + jnp.exp(-jnp.abs(x)))


def _silu(x):
    return x * jax.nn.sigmoid(x)


def _rmsnorm(xf, w):
    ms = jnp.mean(xf * xf, axis=-1, keepdims=True)
    return xf * lax.rsqrt(ms + EPS) * w


def _project(h, wt_ref, wf_ref):
    def cols(ref, c0, n):
        return lax.dot_general(h, ref[c0:c0 + n, :], (((1,), (1,)), ((), ())),
                               preferred_element_type=F32)
    fl = cols(wf_ref, 0, LANES)
    gp = cols(wt_ref, POOL_W, POOL_W)
    ga = cols(wt_ref, 2 * POOL_W + 3 * ATTN_W, ATTN_W)
    k = cols(wt_ref, 2 * POOL_W + ATTN_W, ATTN_W)
    v = cols(wt_ref, 2 * POOL_W + 2 * ATTN_W, ATTN_W)
    q = cols(wt_ref, 2 * POOL_W, ATTN_W)
    u = cols(wt_ref, 0, POOL_W)
    return u, gp, q, k, v, ga, fl


def _proj_prompt_kernel(x_ref, nw_ref, wt_ref, wf_ref, bf_ref, pp_ref, cc_ref, seg_ref,
                        u_ref, sgp_ref, qa_ref, ka_ref, k_ref, v_ref, vb_ref, sga_ref, logf_ref,
                        st_ref, carry_ref, *, tm, ta, tiles_per_batch):
    i = pl.program_id(0)

    @pl.when(i % tiles_per_batch == 0)
    def _():
        carry_ref[...] = jnp.zeros(carry_ref.shape, F32)

    row = lax.broadcasted_iota(jnp.int32, (ta, LANES), 0)
    lane = lax.broadcasted_iota(jnp.int32, (ta, LANES), 1)
    srow = lax.broadcasted_iota(jnp.int32, (8, LANES), 0)
    first_half = lane < HEAD_DIM
    carry = carry_ref[0:1, :]

    for sb in range(tm // ta):
        rs = slice(sb * ta, (sb + 1) * ta)
        h = _rmsnorm(x_ref[rs, :], nw_ref[...]).astype(BF16)
        u, gp, q, k, v, ga, fl = _project(h, wt_ref, wf_ref)
        u_ref[rs, :] = u
        sgp_ref[rs, :] = _silu(gp).astype(BF16)
        k_ref[0, :, rs] = jnp.transpose(k)
        v_ref[0, :, rs] = jnp.transpose(v)
        vb_ref[rs, :] = v.astype(BF16)
        sga_ref[rs, :] = _silu(ga).astype(BF16)
        logf = _log_sigmoid(fl + bf_ref[...])
        logf_ref[0, :, rs] = jnp.transpose(logf)[0:N_HEADS, :]

        acc = logf
        d = 1
        while d < ta:
            acc = acc + jnp.where(row >= d, pltpu.roll(acc, d, axis=0), 0.0)
            d *= 2
        f_run = acc + carry
        carry = f_run[ta - 1:ta, :]

        f2 = f_run * LOG2E
        p1 = f2.astype(BF16).astype(F32)
        r1 = f2 - p1
        p2 = r1.astype(BF16).astype(F32)
        p3 = (r1 - p2).astype(BF16).astype(F32)
        pieces = jnp.where(lane < 8, p1, jnp.where(lane < 16, p2, jnp.where(lane < 24, p3, 0.0)))
        bias = jnp.dot(pieces.astype(BF16), pp_ref[...], preferred_element_type=F32) + cc_ref[...]

        qs = q * Q_SCALE

        qn2 = jnp.dot((qs * qs).astype(BF16), seg_ref[...], preferred_element_type=F32)
        kn2 = jnp.dot((k * k).astype(BF16), seg_ref[...], preferred_element_type=F32)
        qmax = jnp.sqrt(jnp.max(qn2, axis=0, keepdims=True))
        kmax = jnp.sqrt(jnp.max(kn2, axis=0, keepdims=True))
        st_ref[8 * sb:8 * (sb + 1), :] = jnp.where(
            srow == 0, qmax, jnp.where(srow == 1, kmax, jnp.where(
                srow == 2, f2[0:1, :], jnp.where(srow == 3, f2[ta - 1:ta, :], 0.0))))

        q_bias = bias[:, 0:LANES].astype(BF16)
        k_bias = bias[:, LANES:2 * LANES]
        for p in range(N_PAIRS):
            qa_ref[rs, 2 * LANES * p:2 * LANES * p + LANES] = (
                qs[:, LANES * p:LANES * (p + 1)].astype(BF16))
            qa_ref[rs, 2 * LANES * p + LANES:2 * LANES * (p + 1)] = q_bias
            kp = k[:, LANES * p:LANES * (p + 1)]
            for e in range(2):
                hh = 2 * p + e
                keep = first_half if e == 0 else jnp.logical_not(first_half)
                ka_ref[rs, 2 * LANES * hh:2 * LANES * hh + LANES] = (
                    jnp.where(keep, kp, 0.0).astype(BF16))
                own = jnp.logical_and(lane >= BIAS_LANES * hh, lane < BIAS_LANES * (hh + 1))
                ka_ref[rs, 2 * LANES * hh + LANES:2 * LANES * (hh + 1)] = (
                    jnp.where(own, k_bias, 0.0).astype(BF16))

    carry_ref[...] = jnp.broadcast_to(carry, carry_ref.shape)


def _proj_prompt(x2, nw, wt, wf, bft, pp, cc, seg, *, seq):
    rows = x2.shape[0]
    tm = min(PROJ_TM, seq)
    ta = min(ATTN_T, seq)
    assert seq % tm == 0 and rows % seq == 0 and tm % ta == 0 and tm % LANES == 0
    n_batch = rows // seq
    row_blk = lambda n: pl.BlockSpec((tm, n), lambda i: (i, 0))
    const = lambda shape: pl.BlockSpec(shape, lambda i: (0, 0))
    out_shape = (
        jax.ShapeDtypeStruct((rows, POOL_W), F32),
        jax.ShapeDtypeStruct((rows, POOL_W), BF16),
        jax.ShapeDtypeStruct((rows, N_PAIRS * 2 * LANES), BF16),
        jax.ShapeDtypeStruct((rows, N_HEADS * 2 * LANES), BF16),
        jax.ShapeDtypeStruct((n_batch, ATTN_W, seq), F32),
        jax.ShapeDtypeStruct((n_batch, ATTN_W, seq), F32),
        jax.ShapeDtypeStruct((rows, ATTN_W), BF16),
        jax.ShapeDtypeStruct((rows, ATTN_W), BF16),
        jax.ShapeDtypeStruct((n_batch, N_HEADS, seq), F32),
        jax.ShapeDtypeStruct((rows // ta * 8, LANES), F32),
    )
    tpb = seq // tm
    t_minor = lambda n: pl.BlockSpec((1, n, tm), lambda i: (i // tpb, 0, i % tpb))
    out_specs = (row_blk(POOL_W), row_blk(POOL_W), row_blk(N_PAIRS * 2 * LANES),
                 row_blk(N_HEADS * 2 * LANES), t_minor(ATTN_W), t_minor(ATTN_W), row_blk(ATTN_W),
                 row_blk(ATTN_W), t_minor(N_HEADS),
                 pl.BlockSpec((tm // ta * 8, LANES), lambda i: (i, 0)))
    return pl.pallas_call(
        functools.partial(_proj_prompt_kernel, tm=tm, ta=ta, tiles_per_batch=seq // tm),
        grid=(rows // tm,),
        in_specs=[row_blk(D_MODEL), const((1, D_MODEL)), const((W_MAIN, D_MODEL)),
                  const((LANES, D_MODEL)), const((1, LANES)),
                  const((LANES, BIAS_COLS)), const((1, BIAS_COLS)), const((ATTN_W, LANES))],
        out_specs=out_specs,
        out_shape=out_shape,
        scratch_shapes=[pltpu.VMEM((8, LANES), F32)],
        compiler_params=pltpu.CompilerParams(dimension_semantics=("arbitrary",),
                                             vmem_limit_bytes=VMEM_LIMIT),
        name="proj_prompt",
    )(x2, nw, wt, wf, bft, pp, cc, seg)


def _lane_tile(x, reps):
    return jnp.concatenate([x] * reps, axis=1)


def _plan_kernel(st_ref, o_ref, *, n_batch, n_blk):
    row = lax.broadcasted_iota(jnp.int32, (LANES, LANES), 0).astype(F32)
    col = lax.broadcasted_iota(jnp.int32, (LANES, LANES), 1).astype(F32)
    out = jnp.zeros((LANES, LANES), F32)
    for b in range(n_batch):
        def stat(r):
            v = st_ref[pl.ds((b * n_blk) * 8 + r, n_blk, stride=8), :]
            return jnp.concatenate([v, jnp.zeros((LANES - n_blk, LANES), F32)], axis=0)
        qm, km, ft0, fs1 = stat(0), stat(1), stat(2), stat(3)
        km_t, fs1_t = jnp.transpose(km), jnp.transpose(fs1)
        for p in range(N_PAIRS):
            first = row
            for h in (2 * p, 2 * p + 1):
                qcol = qm[:, h:h + 1]
                ub = (NORM_SLACK * (qcol * km_t[h:h + 1, :] + qcol * km[:, h:h + 1])
                      + ft0[:, h:h + 1] - fs1_t[h:h + 1, :])
                needed = jnp.logical_and(ub >= -SKIP_LOG2, col < row)
                first = jnp.minimum(first, jnp.where(needed, col, row))
            start = jnp.min(first, axis=1, keepdims=True)
            out = jnp.where(col == b * N_PAIRS + p, start, out)
    o_ref[...] = jnp.transpose(out).astype(jnp.int32)


def _plan(stats, *, n_batch, n_blk):
    assert n_blk <= LANES and n_batch * N_PAIRS <= LANES
    return pl.pallas_call(
        functools.partial(_plan_kernel, n_batch=n_batch, n_blk=n_blk),
        out_shape=jax.ShapeDtypeStruct((LANES, LANES), jnp.int32),
        name="attn_plan",
    )(stats)


def _attn_kernel(plan_ref, pt_ref, qa_ref, ka_ref, vb_ref, qd_ref, kn_ref, vn_ref, cn_ref,
                 ck_hbm, cv_hbm, clf_hbm, o_ref, od_ref, m_ref, l_ref, acc_ref, *dec_scratch,
                 t, nq, n_steps, dec):
    b, pr, i = pl.program_id(0), pl.program_id(1), pl.program_id(2)
    step = (b * N_PAIRS + pr) * nq + i
    dec_prologue, (dec_head, dec_scores, dec_tail), n_chunks = _decode_fns(
        pt_ref, qd_ref, kn_ref, vn_ref, cn_ref, ck_hbm, cv_hbm, clf_hbm, od_ref, *dec_scratch, **dec)
    per_step = -(-n_chunks // n_steps)
    interleave = n_chunks == n_steps

    @pl.when(step == 0)
    def _():
        dec_prologue()

    if interleave:
        dec_head(step)
    else:
        for k in range(per_step):
            g = step * per_step + k

            @pl.when(g < n_chunks)
            def _(g=g):
                dec_head(g)
                dec_scores(g)
                dec_tail(g)

    q = qa_ref[...]
    lane = lax.broadcasted_iota(jnp.int32, (t, LANES), 1)

    def scores(kstart, width, diag_col, e):
        kblk = ka_ref[pl.ds(kstart, width), 2 * LANES * e:2 * LANES * (e + 1)]
        s = lax.dot_general(q, kblk, (((1,), (1,)), ((), ())), preferred_element_type=F32)
        if diag_col is not None:
            r = lax.broadcasted_iota(jnp.int32, (t, width - diag_col), 0)
            c = lax.broadcasted_iota(jnp.int32, (t, width - diag_col), 1)
            diag = jnp.where(c <= r, s[:, diag_col:], NEG)
            s = diag if diag_col == 0 else jnp.concatenate([s[:, :diag_col], diag], axis=1)
        return s

    def block(kstart, width, diag_col):
        vblk = vb_ref[pl.ds(kstart, width), :]
        for e in range(2):
            s = scores(kstart, width, diag_col, e)
            m_prev = m_ref[e]
            m_next = jnp.maximum(m_prev, jnp.max(s, axis=1, keepdims=True))
            alpha = jnp.exp2(m_prev - m_next)
            p = jnp.exp2(s - _lane_tile(m_next, width // LANES))
            l_ref[e] = alpha * l_ref[e] + jnp.sum(p, axis=1, keepdims=True)
            m_ref[e] = m_next
            acc_ref[e] = alpha * acc_ref[e] + jnp.dot(p.astype(BF16), vblk,
                                                      preferred_element_type=F32)

    def body(j, carry):
        block(pl.multiple_of(j * t, t), t, None)
        return carry

    first = plan_ref[(b * N_PAIRS + pr) * nq + i]
    one_before = jnp.logical_and(i > 0, first == i - 1)

    m_ref[...] = jnp.full(m_ref.shape, NEG, F32)
    l_ref[...] = jnp.zeros(l_ref.shape, F32)
    acc_ref[...] = jnp.zeros(acc_ref.shape, F32)

    @pl.when(one_before)
    def _():
        if interleave:
            dec_scores(step)
        block(pl.multiple_of((i - 1) * t, t), 2 * t, t)

    @pl.when(jnp.logical_not(one_before))
    def _():
        if interleave:
            dec_scores(step)
        lax.fori_loop(first, i, body, 0)
        block(pl.multiple_of(i * t, t), t, 0)

    o = jnp.where(lane < HEAD_DIM, acc_ref[0] / l_ref[0], acc_ref[1] / l_ref[1])
    o_ref[...] = o.astype(o_ref.dtype)
    if interleave:
        dec_tail(step)


def _attention(plan, qa, ka, vb, page_table, qd, kn, vn, cn, ck, cv, clf_t, *, n_batch, seq):
    t = min(ATTN_T, seq)
    assert seq % t == 0
    nq = seq // t
    nb, n_new, _ = qd.shape
    dec, dec_scratch = _decode_setup(nb, n_new, page_table.shape[1], ck.shape[3])
    whole = lambda n: pl.BlockSpec((nb, n_new, n), lambda b, p, i, plan, pt: (0, 0, 0))
    hbm = pl.BlockSpec(memory_space=pl.ANY)
    grid_spec = pltpu.PrefetchScalarGridSpec(
        num_scalar_prefetch=2,
        grid=(n_batch, N_PAIRS, nq),
        in_specs=[pl.BlockSpec((t, 2 * LANES), lambda b, p, i, plan, pt: (b * nq + i, p)),
                  pl.BlockSpec((seq, 4 * LANES), lambda b, p, i, plan, pt: (b, p)),
                  pl.BlockSpec((seq, LANES), lambda b, p, i, plan, pt: (b, p)),
                  whole(ATTN_W), whole(ATTN_W), whole(ATTN_W), whole(LANES), hbm, hbm, hbm],
        out_specs=(pl.BlockSpec((t, LANES), lambda b, p, i, plan, pt: (b * nq + i, p)),
                   whole(ATTN_W)),
        scratch_shapes=[pltpu.VMEM((2, t, LANES), F32), pltpu.VMEM((2, t, LANES), F32),
                        pltpu.VMEM((2, t, LANES), F32)] + dec_scratch)
    return pl.pallas_call(
        functools.partial(_attn_kernel, t=t, nq=nq, n_steps=n_batch * N_PAIRS * nq, dec=dec),
        grid_spec=grid_spec,
        out_shape=(jax.ShapeDtypeStruct((n_batch * seq, ATTN_W), BF16),
                   jax.ShapeDtypeStruct((nb, n_new, ATTN_W), F32)),
        compiler_params=pltpu.CompilerParams(
            dimension_semantics=("arbitrary", "arbitrary", "arbitrary"),
            vmem_limit_bytes=VMEM_LIMIT),
        name="attention",
    )(plan, page_table.reshape(-1), qa, ka, vb, qd, kn, vn, cn, ck, cv, clf_t)


def _merge(x, pooled, sgp, attn, sga, pw_ref, ps, wo_ref, nf):
    mixed = [jnp.dot(pooled[:, POOL_GC * g:POOL_GC * (g + 1)].astype(BF16), pw_ref[g],
                     preferred_element_type=F32) for g in range(len(POOL_WINDOWS))]
    pool_out = jnp.concatenate(mixed, axis=1) * ps
    mix = jnp.concatenate([(pool_out * sgp.astype(F32)).astype(BF16),
                           (attn.astype(F32) * sga.astype(F32)).astype(BF16)], axis=1)
    xo = x + jnp.dot(mix, wo_ref[...], preferred_element_type=F32)
    return _rmsnorm(xo, nf)


def _merge_prompt_kernel(x_ref, u_ref, uh_ref, sgp_ref, attn_ref, sga_ref, pw_ref, ps_ref, wo_ref,
                         nf_ref, y_ref, pool_ref, *, tm, sub, tiles_per_batch):
    i = pl.program_id(0)
    ti = i % tiles_per_batch
    u = u_ref[...]
    halo = jnp.where(ti == 0, 0.0, uh_ref[...])
    ext = jnp.concatenate([halo, u], axis=0)
    pos = ti * tm + lax.broadcasted_iota(jnp.int32, (tm, 1), 0)
    groups = []
    level = ext
    shift = 1
    for g, w in enumerate(POOL_WINDOWS):
        while shift < w:
            level = level + pltpu.roll(level, shift, axis=0)
            shift *= 2
        cnt = jnp.minimum(w, pos + 1).astype(F32)
        groups.append(level[16:, 0:POOL_GC] / cnt - u[:, POOL_GC * g:POOL_GC * (g + 1)])
        level = level[:, POOL_GC:]
    pooled = jnp.concatenate(groups, axis=1)
    for sb in range(tm // sub):
        rs = slice(sb * sub, (sb + 1) * sub)
        y_ref[rs, :] = _merge(x_ref[rs, :], pooled[rs, :], sgp_ref[rs, :], attn_ref[rs, :],
                              sga_ref[rs, :], pw_ref, ps_ref[...], wo_ref, nf_ref[...])

    @pl.when(ti == tiles_per_batch - 1)
    def _():
        pool_ref[0] = u_ref[pl.ds(tm - POOL_HIST, POOL_HIST), :]


def _merge_prompt(x2, u, sgp, attn, sga, pw, ps, wo, nf, *, n_batch, seq):
    rows = x2.shape[0]
    tm = min(PROJ_TM, seq)
    tpb = seq // tm
    halo_blocks = tm // 16
    row_blk = lambda n: pl.BlockSpec((tm, n), lambda i: (i, 0))
    const2 = lambda shape: pl.BlockSpec(shape, lambda i: (0, 0))
    return pl.pallas_call(
        functools.partial(_merge_prompt_kernel, tm=tm, sub=min(ATTN_T, tm), tiles_per_batch=tpb),
        grid=(rows // tm,),
        in_specs=[row_blk(D_MODEL), row_blk(POOL_W),
                  pl.BlockSpec((16, POOL_W), lambda i: (jnp.maximum(i * halo_blocks - 1, 0), 0)),
                  row_blk(POOL_W), row_blk(ATTN_W), row_blk(ATTN_W),
                  pl.BlockSpec((len(POOL_WINDOWS), POOL_GC, POOL_GC), lambda i: (0, 0, 0)),
                  const2((1, POOL_W)), const2((D_MODEL, D_MODEL)), const2((1, D_MODEL))],
        out_specs=(row_blk(D_MODEL),
                   pl.BlockSpec((1, POOL_HIST, POOL_W), lambda i: (i // tpb, 0, 0))),
        out_shape=(jax.ShapeDtypeStruct((rows, D_MODEL), F32),
                   jax.ShapeDtypeStruct((n_batch, POOL_HIST, POOL_W), F32)),
        compiler_params=pltpu.CompilerParams(dimension_semantics=("arbitrary",),
                                             vmem_limit_bytes=VMEM_LIMIT),
        name="merge_prompt",
    )(x2, u, u, sgp, attn, sga, pw, ps, wo, nf)


def _proj_sample_kernel(x_ref, nw_ref, wt_ref, wf_ref, bf_ref,
                        u_ref, sgp_ref, q_ref, k_ref, v_ref, sga_ref, logf_ref, cn_ref, *, n_new):
    nb = x_ref.shape[0]
    x = jnp.concatenate([x_ref[:, D_MODEL * i:D_MODEL * (i + 1)] for i in range(n_new)], axis=0)
    h = _rmsnorm(x, nw_ref[...]).astype(BF16)
    u, gp, q, k, v, ga, fl = _project(h, wt_ref, wf_ref)
    logf = _log_sigmoid(fl + bf_ref[...])
    outs = ((u_ref, u), (sgp_ref, _silu(gp).astype(BF16)), (q_ref, q * Q_SCALE), (k_ref, k),
            (v_ref, v), (sga_ref, _silu(ga).astype(BF16)), (logf_ref, logf))
    cn = jnp.zeros((nb, LANES), F32)
    for i in range(n_new):
        for ref, val in outs:
            n = val.shape[1]
            ref[:, n * i:n * (i + 1)] = val[nb * i:nb * (i + 1), :]
        cn = cn + logf[nb * i:nb * (i + 1), :]
        cn_ref[:, LANES * i:LANES * (i + 1)] = cn


def _proj_sample(xs2, nw, wt, wf, bft, *, n_new):
    nb = xs2.shape[0]
    widths = (POOL_W, POOL_W, ATTN_W, ATTN_W, ATTN_W, ATTN_W, LANES, LANES)
    dtypes = (F32, BF16, F32, F32, F32, BF16, F32, F32)
    return pl.pallas_call(
        functools.partial(_proj_sample_kernel, n_new=n_new),
        out_shape=tuple(jax.ShapeDtypeStruct((nb, n_new * n), dt) for n, dt in zip(widths, dtypes)),
        compiler_params=pltpu.CompilerParams(vmem_limit_bytes=VMEM_LIMIT),
        name="proj_sample",
    )(xs2, nw, wt, wf, bft)


def _decode_fns(pt_ref, q_ref, kn_ref, vn_ref, cn_ref, ck_hbm, cv_hbm, clf_hbm, o_ref,
                kbuf, vpre, vdem, lfbuf, ksem, lfsem, vsem,
                qbd_ref, cn8_ref, cncol_ref, m_ref, l_ref, acc_ref, tail_ref, base_ref, gap_ref,
                *, nb, n_new, n_pages, page, pps, vsub):
    n_chunks = n_pages // pps
    n_groups = pps // vsub
    total = nb * n_chunks
    n_keys = pps * page
    g_keys = vsub * page
    rows = n_new * N_HEADS
    row_w = lax.broadcasted_iota(jnp.int32, (rows, ATTN_W), 0)
    lane_w = lax.broadcasted_iota(jnp.int32, (rows, ATTN_W), 1)
    head_lanes = (row_w % N_HEADS) == (lane_w // HEAD_DIM)
    row1 = lax.broadcasted_iota(jnp.int32, (rows, 1), 0)

    def page_index(g, r):
        b, c = lax.div(g, n_chunks), lax.rem(g, n_chunks)
        return pt_ref[b * n_pages + n_pages - 1 - (c * pps + r)]

    def k_copies(g, slot):
        cps = []
        for r in range(pps):
            idx = page_index(g, r)
            cps.append(pltpu.make_async_copy(ck_hbm.at[idx], kbuf.at[slot, r], ksem.at[slot]))
            cps.append(pltpu.make_async_copy(clf_hbm.at[idx], lfbuf.at[slot, r], lfsem.at[slot]))
        return cps

    def v_copies(g, grp, dst, sem):
        return [pltpu.make_async_copy(cv_hbm.at[page_index(g, grp * vsub + r)], dst.at[r], sem)
                for r in range(vsub)]

    def row(ref, b, i, n):
        return ref[b, i:i + 1, :]

    def init(b):
        qrep = jnp.concatenate(
            [jnp.broadcast_to(row(q_ref, b, i, ATTN_W), (N_HEADS, ATTN_W)) for i in range(n_new)],
            axis=0)
        qbd = jnp.where(head_lanes, qrep, 0.0)
        qbd_ref[...] = qbd.astype(BF16)
        cn8_ref[...] = jnp.zeros(cn8_ref.shape, F32)
        for i in range(n_new):
            cn8_ref[i:i + 1, :] = row(cn_ref, b, i, LANES)
        cnt = jnp.transpose(cn8_ref[...])[0:rows, :] * LOG2E
        cncol = jnp.zeros((rows, 1), F32)
        for i in range(n_new):
            cncol = jnp.where(row1 // N_HEADS == i, cnt[:, i:i + 1], cncol)
        cncol_ref[...] = jnp.broadcast_to(cncol, cncol_ref.shape)
        s_new = []
        for j in range(n_new):
            sj = (jnp.sum(qbd * row(kn_ref, b, j, ATTN_W), axis=1, keepdims=True)
                  + cncol - cnt[:, j:j + 1])
            s_new.append(jnp.where(row1 // N_HEADS >= j, sj, NEG))
        m0 = functools.reduce(jnp.maximum, s_new)
        l0 = jnp.zeros((rows, 1), F32)
        a0 = jnp.zeros((rows, ATTN_W), F32)
        for j in range(n_new):
            pj = jnp.exp2(s_new[j] - m0)
            l0 = l0 + pj
            a0 = a0 + pj * row(vn_ref, b, j, ATTN_W)
        m_ref[...] = jnp.broadcast_to(m0, m_ref.shape)
        l_ref[...] = jnp.broadcast_to(l0, l_ref.shape)
        acc_ref[...] = a0
        tail_ref[...] = jnp.zeros(tail_ref.shape, F32)

    def finish(b):
        o = acc_ref[...] / _lane_tile(l_ref[...], ATTN_W // LANES)
        o = jnp.where(head_lanes, o, 0.0)
        for i in range(n_new):
            o_ref[b, i:i + 1, :] = jnp.sum(o[N_HEADS * i:N_HEADS * (i + 1), :], axis=0,
                                           keepdims=True)

    lane8 = lax.broadcasted_iota(jnp.int32, (N_HEADS, page), 1)

    def head(g):
        slot = lax.rem(g, 2)
        b, c = lax.div(g, n_chunks), lax.rem(g, n_chunks)

        @pl.when(g + 1 < total)
        def _():
            for cp in k_copies(g + 1, 1 - slot):
                cp.start()

        @pl.when(c == 0)
        def _():
            init(b)
            for cp in v_copies(g, 0, vpre, vsem.at[0]):
                cp.wait()

        for cp in k_copies(g, slot):
            cp.wait()

    def scores(g):
        slot = lax.rem(g, 2)
        kcat = jnp.concatenate([kbuf[slot, r].reshape(ATTN_W, page).astype(BF16)
                                for r in range(pps)], axis=1)
        qk = jnp.dot(qbd_ref[...], kcat, preferred_element_type=F32)
        tail = tail_ref[...]
        tails = []
        for r in range(pps):
            tails.append(tail)
            tail = tail + jnp.sum(lfbuf[slot, r] * LOG2E, axis=1, keepdims=True)
        tail_ref[...] = tail
        base = qk + jnp.concatenate([jnp.concatenate(tails, axis=1)] * n_new, axis=0) + cncol_ref[:, 0:1]
        base_ref[...] = base
        over = base - _lane_tile(m_ref[...], n_keys // LANES)
        for h in range(n_groups):
            gap_ref[h] = jnp.max(over[:, g_keys * h:g_keys * (h + 1)])

    def tail_phase(g):
        slot = lax.rem(g, 2)
        b, c = lax.div(g, n_chunks), lax.rem(g, n_chunks)
        live = [gap_ref[h] >= -SKIP_LOG2 for h in range(n_groups)]

        def update(h, vsrc):
            g_parts = []
            for r in range(vsub * h, vsub * (h + 1)):
                lf = lfbuf[slot, r] * LOG2E
                incl = lf
                d = 1
                while d < page:
                    incl = incl + jnp.where(lane8 + d < page, pltpu.roll(incl, page - d, axis=1), 0.0)
                    d *= 2
                g_parts.append(incl - lf)
            s = (base_ref[:, g_keys * h:g_keys * (h + 1)]
                 + jnp.concatenate([jnp.concatenate(g_parts, axis=1)] * n_new, axis=0))
            m_prev = m_ref[...]
            m_next = jnp.maximum(m_prev, jnp.max(s, axis=1, keepdims=True))
            alpha = jnp.exp2(m_prev - m_next)
            p = jnp.exp2(s - _lane_tile(m_next, g_keys // LANES))
            l_ref[...] = alpha * l_ref[...] + jnp.sum(p, axis=1, keepdims=True)
            m_ref[...] = m_next
            vcat = jnp.concatenate([vsrc[r].reshape(ATTN_W, page).astype(BF16)
                                    for r in range(vsub)], axis=1)
            acc_ref[...] = (_lane_tile(alpha, ATTN_W // LANES) * acc_ref[...]
                            + lax.dot_general(p.astype(BF16), vcat, (((1,), (1,)), ((), ())),
                                              preferred_element_type=F32))

        def fetch_and_update(h):
            cps = v_copies(g, h, vdem, vsem.at[1])
            for cp in cps:
                cp.start()
            for cp in cps:
                cp.wait()
            update(h, vdem)

        for h in range(n_groups):
            if h == 0:
                @pl.when(jnp.logical_and(live[0], c == 0))
                def _():
                    update(0, vpre)

                @pl.when(jnp.logical_and(live[0], c > 0))
                def _():
                    fetch_and_update(0)
            else:
                @pl.when(live[h])
                def _(h=h):
                    fetch_and_update(h)

        @pl.when(c == n_chunks - 1)
        def _():
            finish(b)

        @pl.when(jnp.logical_and(c == 0, b + 1 < nb))
        def _():
            for cp in v_copies(g + n_chunks, 0, vpre, vsem.at[0]):
                cp.start()

    def prologue():
        for cp in k_copies(0, 0):
            cp.start()
        for cp in v_copies(0, 0, vpre, vsem.at[0]):
            cp.start()

    return prologue, (head, scores, tail_phase), total


def _decode_setup(nb, n_new, n_pages, page):
    pps = DEC_PAGES
    while n_pages % pps:
        pps //= 2
    vsub = min(DEC_VSUB, pps)
    rows = n_new * N_HEADS
    assert page % LANES == 0 and rows % 8 == 0 and pps % vsub == 0
    params = dict(nb=nb, n_new=n_new, n_pages=n_pages, page=page, pps=pps, vsub=vsub)
    scratch = [pltpu.VMEM((2, pps, N_HEADS, HEAD_DIM, page), F32),
               pltpu.VMEM((vsub, N_HEADS, HEAD_DIM, page), F32),
               pltpu.VMEM((vsub, N_HEADS, HEAD_DIM, page), F32),
               pltpu.VMEM((2, pps, N_HEADS, page), F32),
               pltpu.SemaphoreType.DMA((2,)),
               pltpu.SemaphoreType.DMA((2,)),
               pltpu.SemaphoreType.DMA((2,)),
               pltpu.VMEM((rows, ATTN_W), BF16),
               pltpu.VMEM((8, LANES), F32),
               pltpu.VMEM((rows, LANES), F32),
               pltpu.VMEM((rows, LANES), F32),
               pltpu.VMEM((rows, LANES), F32),
               pltpu.VMEM((rows, ATTN_W), F32),
               pltpu.VMEM((N_HEADS, LANES), F32),
               pltpu.VMEM((rows, pps * page), F32),
               pltpu.SMEM((pps // vsub,), F32)]
    return params, scratch


def _merge_sample_kernel(x_ref, u_ref, sp_ref, sgp_ref, attn_ref, sga_ref, pw_ref, ps_ref, wo_ref,
                         nf_ref, y_ref, pool_ref, *, n_new, past):
    u = [u_ref[:, POOL_W * i:POOL_W * (i + 1)] for i in range(n_new)]
    hist = [sp_ref[r] for r in range(POOL_HIST)]
    ext = hist + u
    pooled_rows = []
    for i in range(n_new):
        groups = []
        for g, w in enumerate(POOL_WINDOWS):
            sl = slice(POOL_GC * g, POOL_GC * (g + 1))
            end = POOL_HIST + i
            total = ext[end][:, sl]
            for r in range(end - w + 1, end):
                total = total + ext[r][:, sl]
            groups.append(total / float(min(w, past + i + 1)) - u[i][:, sl])
        pooled_rows.append(jnp.concatenate(groups, axis=1))
    pooled = jnp.concatenate(pooled_rows, axis=0)
    cat = lambda ref, n: jnp.concatenate([ref[:, n * i:n * (i + 1)] for i in range(n_new)], axis=0)
    y = _merge(cat(x_ref, D_MODEL), pooled, cat(sgp_ref, POOL_W), cat(attn_ref, ATTN_W),
               cat(sga_ref, ATTN_W), pw_ref, ps_ref[...], wo_ref, nf_ref[...])
    nb = x_ref.shape[0]
    for i in range(n_new):
        y_ref[:, D_MODEL * i:D_MODEL * (i + 1)] = y[nb * i:nb * (i + 1), :]
    new_hist = ext[-POOL_HIST:]
    for r in range(POOL_HIST):
        pool_ref[:, POOL_W * r:POOL_W * (r + 1)] = new_hist[r]


def _merge_sample(xs2, u2, sp2, sgp2, attn2, sga2, pw, ps, wo, nf, *, n_new, past):
    nb = xs2.shape[0]
    return pl.pallas_call(
        functools.partial(_merge_sample_kernel, n_new=n_new, past=past),
        out_shape=(jax.ShapeDtypeStruct((nb, n_new * D_MODEL), F32),
                   jax.ShapeDtypeStruct((nb, POOL_HIST * POOL_W), F32)),
        compiler_params=pltpu.CompilerParams(vmem_limit_bytes=VMEM_LIMIT),
        name="merge_sample",
    )(xs2, u2, sp2, sgp2, attn2, sga2, pw, ps, wo, nf)


def kernel(x_prompt, x_sample, cache_k, cache_v, cache_logf, state_pool, page_table,
           norm_w, w_in, b_forget, pool_w, pool_scale, w_out, norm_f):
    depth = norm_w.shape[0]
    assert depth == 1, "a single layer is supported"
    b_p, seq, _ = x_prompt.shape
    b_s, t_s, _ = x_sample.shape
    n_phys, page = cache_k.shape[1], cache_k.shape[2]
    past = page_table.shape[1] * page
    ll = 0

    wt_all = jnp.swapaxes(w_in[ll], 0, 1)
    wt = wt_all[:W_MAIN].astype(BF16)
    wf = jnp.tile(wt_all[W_MAIN:], (LANES // N_HEADS, 1)).astype(BF16)
    bft = jnp.tile(b_forget[ll], LANES // N_HEADS).reshape(1, LANES).astype(F32)
    nw = norm_w[ll].reshape(1, D_MODEL)
    nf = norm_f.reshape(1, D_MODEL)
    pw = pool_w[ll].astype(BF16)
    ps = pool_scale[ll].reshape(1, POOL_W)
    wo = w_out[ll].astype(BF16)
    pp_np, cc_np = _placement()
    pp = jnp.asarray(pp_np, BF16)
    cc = jnp.asarray(cc_np, F32)

    xp2 = x_prompt.reshape(b_p * seq, D_MODEL)
    seg_np = (np.arange(ATTN_W)[:, None] // HEAD_DIM == np.arange(LANES)[None, :] % N_HEADS)
    seg = jnp.asarray(seg_np, BF16)
    u, sgp, qa, ka, kf, vf, vb, sga, logf, stats = _proj_prompt(xp2, nw, wt, wf, bft, pp, cc, seg,
                                                                seq=seq)
    n_blk = seq // min(ATTN_T, seq)
    plan = _plan(stats, n_batch=b_p, n_blk=n_blk)[:b_p * N_PAIRS, :n_blk].reshape(-1)

    xs2 = x_sample.reshape(b_s, t_s * D_MODEL)
    us, sgps, qs, ks, vs, sgas, lfs, cns = _proj_sample(xs2, nw, wt, wf, bft, n_new=t_s)
    ck = jnp.transpose(cache_k[ll], (0, 2, 3, 1))
    cv = jnp.transpose(cache_v[ll], (0, 2, 3, 1))
    clf_t = jnp.swapaxes(cache_logf[ll], 1, 2)

    per_pos = lambda a: a.reshape(b_s, t_s, a.shape[1] // t_s)
    attn, attn_s = _attention(plan, qa, ka, vb, page_table, per_pos(qs), per_pos(ks), per_pos(vs),
                              per_pos(cns), ck, cv, clf_t, n_batch=b_p, seq=seq)
    yp, pool_p = _merge_prompt(xp2, u, sgp, attn, sga, pw, ps, wo, nf, n_batch=b_p, seq=seq)
    ys, pool_s = _merge_sample(xs2, us, jnp.swapaxes(state_pool[ll], 0, 1), sgps,
                               attn_s.reshape(b_s, t_s * ATTN_W), sgas, pw, ps, wo, nf,
                               n_new=t_s, past=past)

    return (yp.reshape(b_p, seq, D_MODEL),
            ys.reshape(b_s, t_s, D_MODEL),
            jnp.transpose(kf.reshape(1, b_p, N_HEADS, HEAD_DIM, seq), (0, 1, 4, 2, 3)),
            jnp.transpose(vf.reshape(1, b_p, N_HEADS, HEAD_DIM, seq), (0, 1, 4, 2, 3)),
            jnp.transpose(logf, (0, 2, 1)).reshape(1, b_p, seq, N_HEADS),
            pool_p.reshape(1, b_p, POOL_HIST, POOL_W),
            ks.reshape(1, b_s, t_s, N_HEADS, HEAD_DIM),
            vs.reshape(1, b_s, t_s, N_HEADS, HEAD_DIM),
            lfs.reshape(b_s, t_s, LANES)[:, :, :N_HEADS].reshape(1, b_s, t_s, N_HEADS),
            pool_s.reshape(1, b_s, POOL_HIST, POOL_W))
```

```python
import functools
import math

import numpy as np
import jax
import jax.numpy as jnp
from jax import lax
from jax.experimental import pallas as pl
from jax.experimental.pallas import tpu as pltpu

D_MODEL = 1024
POOL_W = 512
ATTN_W = 512
N_HEADS = 8
HEAD_DIM = 64
POOL_WINDOWS = (2, 4, 8, 16)
POOL_GC = 128
POOL_HIST = 15
EPS = 1e-6

LANES = 128
LOG2E = 1.4426950408889634
Q_SCALE = LOG2E / math.sqrt(HEAD_DIM)
NEG = -1e30
N_PAIRS = N_HEADS // 2
W_MAIN = 2 * POOL_W + 4 * ATTN_W
BIAS_LANES = LANES // N_HEADS
BIAS_COLS = 2 * LANES
VMEM_LIMIT = 56 * 1024 * 1024

SKIP_LOG2 = 152.0
NORM_SLACK = 1.02

PROJ_TM = 1024
ATTN_T = 512
DEC_PAGES = 16
DEC_VSUB = 4

BF16 = jnp.bfloat16
F32 = jnp.float32


def _placement():
    pp = np.zeros((LANES, BIAS_COLS), np.float32)
    cc = np.zeros((1, BIAS_COLS), np.float32)
    for h in range(N_HEADS):
        for i in range(3):
            pp[8 * i + h, BIAS_LANES * h + i] = 1.0
            pp[8 * i + h, LANES + BIAS_LANES * h + 3 + i] = -1.0
            cc[0, BIAS_LANES * h + 3 + i] = 1.0
            cc[0, LANES + BIAS_LANES * h + i] = 1.0
    return pp, cc


def _log_sigmoid(x):
    return jnp.minimum(x, 0.0) - jnp.log(1.0 + jnp.exp(-jnp.abs(x)))


def _silu(x):
    return x * jax.nn.sigmoid(x)


def _rmsnorm(xf, w):
    ms = jnp.mean(xf * xf, axis=-1, keepdims=True)
    return xf * lax.rsqrt(ms + EPS) * w


def _project(h, wt_ref, wf_ref):
    def cols(ref, c0, n):
        return lax.dot_general(h, ref[c0:c0 + n, :], (((1,), (1,)), ((), ())),
                               preferred_element_type=F32)
    fl = cols(wf_ref, 0, LANES)
    gp = cols(wt_ref, POOL_W, POOL_W)
    ga = cols(wt_ref, 2 * POOL_W + 3 * ATTN_W, ATTN_W)
    k = cols(wt_ref, 2 * POOL_W + ATTN_W, ATTN_W)
    v = cols(wt_ref, 2 * POOL_W + 2 * ATTN_W, ATTN_W)
    q = cols(wt_ref, 2 * POOL_W, ATTN_W)
    u = cols(wt_ref, 0, POOL_W)
    return u, gp, q, k, v, ga, fl


def _proj_prompt_kernel(x_ref, nw_ref, wt_ref, wf_ref, bf_ref, pp_ref, cc_ref, seg_ref,
                        u_ref, sgp_ref, qa_ref, ka_ref, k_ref, v_ref, vb_ref, sga_ref, logf_ref,
                        st_ref, carry_ref, *, tm, ta, tiles_per_batch):
    i = pl.program_id(0)

    @pl.when(i % tiles_per_batch == 0)
    def _():
        carry_ref[...] = jnp.zeros(carry_ref.shape, F32)

    row = lax.broadcasted_iota(jnp.int32, (ta, LANES), 0)
    lane = lax.broadcasted_iota(jnp.int32, (ta, LANES), 1)
    srow = lax.broadcasted_iota(jnp.int32, (8, LANES), 0)
    first_half = lane < HEAD_DIM
    carry = carry_ref[0:1, :]

    for sb in range(tm // ta):
        rs = slice(sb * ta, (sb + 1) * ta)
        h = _rmsnorm(x_ref[rs, :], nw_ref[...]).astype(BF16)
        u, gp, q, k, v, ga, fl = _project(h, wt_ref, wf_ref)
        u_ref[rs, :] = u
        sgp_ref[rs, :] = _silu(gp).astype(BF16)
        k_ref[0, :, rs] = jnp.transpose(k)
        v_ref[0, :, rs] = jnp.transpose(v)
        vb_ref[rs, :] = v.astype(BF16)
        sga_ref[rs, :] = _silu(ga).astype(BF16)
        logf = _log_sigmoid(fl + bf_ref[...])
        logf_ref[0, :, rs] = jnp.transpose(logf)[0:N_HEADS, :]

        acc = logf
        d = 1
        while d < ta:
            acc = acc + jnp.where(row >= d, pltpu.roll(acc, d, axis=0), 0.0)
            d *= 2
        f_run = acc + carry
        carry = f_run[ta - 1:ta, :]

        f2 = f_run * LOG2E
        p1 = f2.astype(BF16).astype(F32)
        r1 = f2 - p1
        p2 = r1.astype(BF16).astype(F32)
        p3 = (r1 - p2).astype(BF16).astype(F32)
        pieces = jnp.where(lane < 8, p1, jnp.where(lane < 16, p2, jnp.where(lane < 24, p3, 0.0)))
        bias = jnp.dot(pieces.astype(BF16), pp_ref[...], preferred_element_type=F32) + cc_ref[...]

        qs = q * Q_SCALE

        qn2 = jnp.dot((qs * qs).astype(BF16), seg_ref[...], preferred_element_type=F32)
        kn2 = jnp.dot((k * k).astype(BF16), seg_ref[...], preferred_element_type=F32)
        qmax = jnp.sqrt(jnp.max(qn2, axis=0, keepdims=True))
        kmax = jnp.sqrt(jnp.max(kn2, axis=0, keepdims=True))
        st_ref[8 * sb:8 * (sb + 1), :] = jnp.where(
            srow == 0, qmax, jnp.where(srow == 1, kmax, jnp.where(
                srow == 2, f2[0:1, :], jnp.where(srow == 3, f2[ta - 1:ta, :], 0.0))))

        q_bias = bias[:, 0:LANES].astype(BF16)
        k_bias = bias[:, LANES:2 * LANES]
        for p in range(N_PAIRS):
            qa_ref[rs, 2 * LANES * p:2 * LANES * p + LANES] = (
                qs[:, LANES * p:LANES * (p + 1)].astype(BF16))
            qa_ref[rs, 2 * LANES * p + LANES:2 * LANES * (p + 1)] = q_bias
            kp = k[:, LANES * p:LANES * (p + 1)]
            for e in range(2):
                hh = 2 * p + e
                keep = first_half if e == 0 else jnp.logical_not(first_half)
                ka_ref[rs, 2 * LANES * hh:2 * LANES * hh + LANES] = (
                    jnp.where(keep, kp, 0.0).astype(BF16))
                own = jnp.logical_and(lane >= BIAS_LANES * hh, lane < BIAS_LANES * (hh + 1))
                ka_ref[rs, 2 * LANES * hh + LANES:2 * LANES * (hh + 1)] = (
                    jnp.where(own, k_bias, 0.0).astype(BF16))

    carry_ref[...] = jnp.broadcast_to(carry, carry_ref.shape)


def _proj_prompt(x2, nw, wt, wf, bft, pp, cc, seg, *, seq):
    rows = x2.shape[0]
    tm = min(PROJ_TM, seq)
    ta = min(ATTN_T, seq)
    assert seq % tm == 0 and rows % seq == 0 and tm % ta == 0 and tm % LANES == 0
    n_batch = rows // seq
    row_blk = lambda n: pl.BlockSpec((tm, n), lambda i: (i, 0))
    const = lambda shape: pl.BlockSpec(shape, lambda i: (0, 0))
    out_shape = (
        jax.ShapeDtypeStruct((rows, POOL_W), F32),
        jax.ShapeDtypeStruct((rows, POOL_W), BF16),
        jax.ShapeDtypeStruct((rows, N_PAIRS * 2 * LANES), BF16),
        jax.ShapeDtypeStruct((rows, N_HEADS * 2 * LANES), BF16),
        jax.ShapeDtypeStruct((n_batch, ATTN_W, seq), F32),
        jax.ShapeDtypeStruct((n_batch, ATTN_W, seq), F32),
        jax.ShapeDtypeStruct((rows, ATTN_W), BF16),
        jax.ShapeDtypeStruct((rows, ATTN_W), BF16),
        jax.ShapeDtypeStruct((n_batch, N_HEADS, seq), F32),
        jax.ShapeDtypeStruct((rows // ta * 8, LANES), F32),
    )
    tpb = seq // tm
    t_minor = lambda n: pl.BlockSpec((1, n, tm), lambda i: (i // tpb, 0, i % tpb))
    out_specs = (row_blk(POOL_W), row_blk(POOL_W), row_blk(N_PAIRS * 2 * LANES),
                 row_blk(N_HEADS * 2 * LANES), t_minor(ATTN_W), t_minor(ATTN_W), row_blk(ATTN_W),
                 row_blk(ATTN_W), t_minor(N_HEADS),
                 pl.BlockSpec((tm // ta * 8, LANES), lambda i: (i, 0)))
    return pl.pallas_call(
        functools.partial(_proj_prompt_kernel, tm=tm, ta=ta, tiles_per_batch=seq // tm),
        grid=(rows // tm,),
        in_specs=[row_blk(D_MODEL), const((1, D_MODEL)), const((W_MAIN, D_MODEL)),
                  const((LANES, D_MODEL)), const((1, LANES)),
                  const((LANES, BIAS_COLS)), const((1, BIAS_COLS)), const((ATTN_W, LANES))],
        out_specs=out_specs,
        out_shape=out_shape,
        scratch_shapes=[pltpu.VMEM((8, LANES), F32)],
        compiler_params=pltpu.CompilerParams(dimension_semantics=("arbitrary",),
                                             vmem_limit_bytes=VMEM_LIMIT),
        name="proj_prompt",
    )(x2, nw, wt, wf, bft, pp, cc, seg)


def _lane_tile(x, reps):
    return jnp.concatenate([x] * reps, axis=1)


def _plan_kernel(st_ref, o_ref, *, n_batch, n_blk):
    row = lax.broadcasted_iota(jnp.int32, (LANES, LANES), 0).astype(F32)
    col = lax.broadcasted_iota(jnp.int32, (LANES, LANES), 1).astype(F32)
    out = jnp.zeros((LANES, LANES), F32)
    for b in range(n_batch):
        def stat(r):
            v = st_ref[pl.ds((b * n_blk) * 8 + r, n_blk, stride=8), :]
            return jnp.concatenate([v, jnp.zeros((LANES - n_blk, LANES), F32)], axis=0)
        qm, km, ft0, fs1 = stat(0), stat(1), stat(2), stat(3)
        km_t, fs1_t = jnp.transpose(km), jnp.transpose(fs1)
        for p in range(N_PAIRS):
            first = row
            for h in (2 * p, 2 * p + 1):
                qcol = qm[:, h:h + 1]
                ub = (NORM_SLACK * (qcol * km_t[h:h + 1, :] + qcol * km[:, h:h + 1])
                      + ft0[:, h:h + 1] - fs1_t[h:h + 1, :])
                needed = jnp.logical_and(ub >= -SKIP_LOG2, col < row)
                first = jnp.minimum(first, jnp.where(needed, col, row))
            start = jnp.min(first, axis=1, keepdims=True)
            out = jnp.where(col == b * N_PAIRS + p, start, out)
    o_ref[...] = jnp.transpose(out).astype(jnp.int32)


def _plan(stats, *, n_batch, n_blk):
    assert n_blk <= LANES and n_batch * N_PAIRS <= LANES
    return pl.pallas_call(
        functools.partial(_plan_kernel, n_batch=n_batch, n_blk=n_blk),
        out_shape=jax.ShapeDtypeStruct((LANES, LANES), jnp.int32),
        name="attn_plan",
    )(stats)


def _attn_kernel(plan_ref, pt_ref, qa_ref, ka_ref, vb_ref, qd_ref, kn_ref, vn_ref, cn_ref,
                 ck_hbm, cv_hbm, clf_hbm, o_ref, od_ref, m_ref, l_ref, acc_ref, *dec_scratch,
                 t, nq, n_steps, dec):
    b, pr, i = pl.program_id(0), pl.program_id(1), pl.program_id(2)
    step = (b * N_PAIRS + pr) * nq + i
    dec_prologue, (dec_head, dec_scores, dec_tail), n_chunks = _decode_fns(
        pt_ref, qd_ref, kn_ref, vn_ref, cn_ref, ck_hbm, cv_hbm, clf_hbm, od_ref, *dec_scratch, **dec)
    per_step = -(-n_chunks // n_steps)
    interleave = n_chunks == n_steps

    @pl.when(step == 0)
    def _():
        dec_prologue()

    if interleave:
        dec_head(step)
    else:
        for k in range(per_step):
            g = step * per_step + k

            @pl.when(g < n_chunks)
            def _(g=g):
                dec_head(g)
                dec_scores(g)
                dec_tail(g)

    q = qa_ref[...]
    lane = lax.broadcasted_iota(jnp.int32, (t, LANES), 1)

    def scores(kstart, width, diag_col, e):
        kblk = ka_ref[pl.ds(kstart, width), 2 * LANES * e:2 * LANES * (e + 1)]
        s = lax.dot_general(q, kblk, (((1,), (1,)), ((), ())), preferred_element_type=F32)
        if diag_col is not None:
            r = lax.broadcasted_iota(jnp.int32, (t, width - diag_col), 0)
            c = lax.broadcasted_iota(jnp.int32, (t, width - diag_col), 1)
            diag = jnp.where(c <= r, s[:, diag_col:], NEG)
            s = diag if diag_col == 0 else jnp.concatenate([s[:, :diag_col], diag], axis=1)
        return s

    def block(kstart, width, diag_col):
        vblk = vb_ref[pl.ds(kstart, width), :]
        for e in range(2):
            s = scores(kstart, width, diag_col, e)
            m_prev = m_ref[e]
            m_next = jnp.maximum(m_prev, jnp.max(s, axis=1, keepdims=True))
            alpha = jnp.exp2(m_prev - m_next)
            p = jnp.exp2(s - _lane_tile(m_next, width // LANES))
            l_ref[e] = alpha * l_ref[e] + jnp.sum(p, axis=1, keepdims=True)
            m_ref[e] = m_next
            acc_ref[e] = alpha * acc_ref[e] + jnp.dot(p.astype(BF16), vblk,
                                                      preferred_element_type=F32)

    def body(j, carry):
        block(pl.multiple_of(j * t, t), t, None)
        return carry

    first = plan_ref[(b * N_PAIRS + pr) * nq + i]
    one_before = jnp.logical_and(i > 0, first == i - 1)

    m_ref[...] = jnp.full(m_ref.shape, NEG, F32)
    l_ref[...] = jnp.zeros(l_ref.shape, F32)
    acc_ref[...] = jnp.zeros(acc_ref.shape, F32)

    @pl.when(one_before)
    def _():
        if interleave:
            dec_scores(step)
        block(pl.multiple_of((i - 1) * t, t), 2 * t, t)

    @pl.when(jnp.logical_not(one_before))
    def _():
        if interleave:
            dec_scores(step)
        lax.fori_loop(first, i, body, 0)
        block(pl.multiple_of(i * t, t), t, 0)

    o = jnp.where(lane < HEAD_DIM, acc_ref[0] / l_ref[0], acc_ref[1] / l_ref[1])
    o_ref[...] = o.astype(o_ref.dtype)
    if interleave:
        dec_tail(step)


def _attention(plan, qa, ka, vb, page_table, qd, kn, vn, cn, ck, cv, clf_t, *, n_batch, seq):
    t = min(ATTN_T, seq)
    assert seq % t == 0
    nq = seq // t
    nb, n_new, _ = qd.shape
    dec, dec_scratch = _decode_setup(nb, n_new, page_table.shape[1], ck.shape[3])
    whole = lambda n: pl.BlockSpec((nb, n_new, n), lambda b, p, i, plan, pt: (0, 0, 0))
    hbm = pl.BlockSpec(memory_space=pl.ANY)
    grid_spec = pltpu.PrefetchScalarGridSpec(
        num_scalar_prefetch=2,
        grid=(n_batch, N_PAIRS, nq),
        in_specs=[pl.BlockSpec((t, 2 * LANES), lambda b, p, i, plan, pt: (b * nq + i, p)),
                  pl.BlockSpec((seq, 4 * LANES), lambda b, p, i, plan, pt: (b, p)),
                  pl.BlockSpec((seq, LANES), lambda b, p, i, plan, pt: (b, p)),
                  whole(ATTN_W), whole(ATTN_W), whole(ATTN_W), whole(LANES), hbm, hbm, hbm],
        out_specs=(pl.BlockSpec((t, LANES), lambda b, p, i, plan, pt: (b * nq + i, p)),
                   whole(ATTN_W)),
        scratch_shapes=[pltpu.VMEM((2, t, LANES), F32), pltpu.VMEM((2, t, LANES), F32),
                        pltpu.VMEM((2, t, LANES), F32)] + dec_scratch)
    return pl.pallas_call(
        functools.partial(_attn_kernel, t=t, nq=nq, n_steps=n_batch * N_PAIRS * nq, dec=dec),
        grid_spec=grid_spec,
        out_shape=(jax.ShapeDtypeStruct((n_batch * seq, ATTN_W), BF16),
                   jax.ShapeDtypeStruct((nb, n_new, ATTN_W), F32)),
        compiler_params=pltpu.CompilerParams(
            dimension_semantics=("arbitrary", "arbitrary", "arbitrary"),
            vmem_limit_bytes=VMEM_LIMIT),
        name="attention",
    )(plan, page_table.reshape(-1), qa, ka, vb, qd, kn, vn, cn, ck, cv, clf_t)


def _merge(x, pooled, sgp, attn, sga, pw_ref, ps, wo_ref, nf):
    mixed = [jnp.dot(pooled[:, POOL_GC * g:POOL_GC * (g + 1)].astype(BF16), pw_ref[g],
                     preferred_element_type=F32) for g in range(len(POOL_WINDOWS))]
    pool_out = jnp.concatenate(mixed, axis=1) * ps
    mix = jnp.concatenate([(pool_out * sgp.astype(F32)).astype(BF16),
                           (attn.astype(F32) * sga.astype(F32)).astype(BF16)], axis=1)
    xo = x + jnp.dot(mix, wo_ref[...], preferred_element_type=F32)
    return _rmsnorm(xo, nf)


def _merge_prompt_kernel(x_ref, u_ref, uh_ref, sgp_ref, attn_ref, sga_ref, pw_ref, ps_ref, wo_ref,
                         nf_ref, y_ref, pool_ref, *, tm, sub, tiles_per_batch):
    i = pl.program_id(0)
    ti = i % tiles_per_batch
    u = u_ref[...]
    halo = jnp.where(ti == 0, 0.0, uh_ref[...])
    ext = jnp.concatenate([halo, u], axis=0)
    pos = ti * tm + lax.broadcasted_iota(jnp.int32, (tm, 1), 0)
    groups = []
    level = ext
    shift = 1
    for g, w in enumerate(POOL_WINDOWS):
        while shift < w:
            level = level + pltpu.roll(level, shift, axis=0)
            shift *= 2
        cnt = jnp.minimum(w, pos + 1).astype(F32)
        groups.append(level[16:, 0:POOL_GC] / cnt - u[:, POOL_GC * g:POOL_GC * (g + 1)])
        level = level[:, POOL_GC:]
    pooled = jnp.concatenate(groups, axis=1)
    for sb in range(tm // sub):
        rs = slice(sb * sub, (sb + 1) * sub)
        y_ref[rs, :] = _merge(x_ref[rs, :], pooled[rs, :], sgp_ref[rs, :], attn_ref[rs, :],
                              sga_ref[rs, :], pw_ref, ps_ref[...], wo_ref, nf_ref[...])

    @pl.when(ti == tiles_per_batch - 1)
    def _():
        pool_ref[0] = u_ref[pl.ds(tm - POOL_HIST, POOL_HIST), :]


def _merge_prompt(x2, u, sgp, attn, sga, pw, ps, wo, nf, *, n_batch, seq):
    rows = x2.shape[0]
    tm = min(PROJ_TM, seq)
    tpb = seq // tm
    halo_blocks = tm // 16
    row_blk = lambda n: pl.BlockSpec((tm, n), lambda i: (i, 0))
    const2 = lambda shape: pl.BlockSpec(shape, lambda i: (0, 0))
    return pl.pallas_call(
        functools.partial(_merge_prompt_kernel, tm=tm, sub=min(ATTN_T, tm), tiles_per_batch=tpb),
        grid=(rows // tm,),
        in_specs=[row_blk(D_MODEL), row_blk(POOL_W),
                  pl.BlockSpec((16, POOL_W), lambda i: (jnp.maximum(i * halo_blocks - 1, 0), 0)),
                  row_blk(POOL_W), row_blk(ATTN_W), row_blk(ATTN_W),
                  pl.BlockSpec((len(POOL_WINDOWS), POOL_GC, POOL_GC), lambda i: (0, 0, 0)),
                  const2((1, POOL_W)), const2((D_MODEL, D_MODEL)), const2((1, D_MODEL))],
        out_specs=(row_blk(D_MODEL),
                   pl.BlockSpec((1, POOL_HIST, POOL_W), lambda i: (i // tpb, 0, 0))),
        out_shape=(jax.ShapeDtypeStruct((rows, D_MODEL), F32),
                   jax.ShapeDtypeStruct((n_batch, POOL_HIST, POOL_W), F32)),
        compiler_params=pltpu.CompilerParams(dimension_semantics=("arbitrary",),
                                             vmem_limit_bytes=VMEM_LIMIT),
        name="merge_prompt",
    )(x2, u, u, sgp, attn, sga, pw, ps, wo, nf)


def _proj_sample_kernel(x_ref, nw_ref, wt_ref, wf_ref, bf_ref,
                        u_ref, sgp_ref, q_ref, k_ref, v_ref, sga_ref, logf_ref, cn_ref, *, n_new):
    nb = x_ref.shape[0]
    x = jnp.concatenate([x_ref[:, D_MODEL * i:D_MODEL * (i + 1)] for i in range(n_new)], axis=0)
    h = _rmsnorm(x, nw_ref[...]).astype(BF16)
    u, gp, q, k, v, ga, fl = _project(h, wt_ref, wf_ref)
    logf = _log_sigmoid(fl + bf_ref[...])
    outs = ((u_ref, u), (sgp_ref, _silu(gp).astype(BF16)), (q_ref, q * Q_SCALE), (k_ref, k),
            (v_ref, v), (sga_ref, _silu(ga).astype(BF16)), (logf_ref, logf))
    cn = jnp.zeros((nb, LANES), F32)
    for i in range(n_new):
        for ref, val in outs:
            n = val.shape[1]
            ref[:, n * i:n * (i + 1)] = val[nb * i:nb * (i + 1), :]
        cn = cn + logf[nb * i:nb * (i + 1), :]
        cn_ref[:, LANES * i:LANES * (i + 1)] = cn


def _proj_sample(xs2, nw, wt, wf, bft, *, n_new):
    nb = xs2.shape[0]
    widths = (POOL_W, POOL_W, ATTN_W, ATTN_W, ATTN_W, ATTN_W, LANES, LANES)
    dtypes = (F32, BF16, F32, F32, F32, BF16, F32, F32)
    return pl.pallas_call(
        functools.partial(_proj_sample_kernel, n_new=n_new),
        out_shape=tuple(jax.ShapeDtypeStruct((nb, n_new * n), dt) for n, dt in zip(widths, dtypes)),
        compiler_params=pltpu.CompilerParams(vmem_limit_bytes=VMEM_LIMIT),
        name="proj_sample",
    )(xs2, nw, wt, wf, bft)


def _decode_fns(pt_ref, q_ref, kn_ref, vn_ref, cn_ref, ck_hbm, cv_hbm, clf_hbm, o_ref,
                kbuf, vpre, vdem, lfbuf, ksem, lfsem, vsem,
                qbd_ref, cn8_ref, cncol_ref, m_ref, l_ref, acc_ref, tail_ref, base_ref, gap_ref,
                *, nb, n_new, n_pages, page, pps, vsub):
    n_chunks = n_pages // pps
    n_groups = pps // vsub
    total = nb * n_chunks
    n_keys = pps * page
    g_keys = vsub * page
    rows = n_new * N_HEADS
    row_w = lax.broadcasted_iota(jnp.int32, (rows, ATTN_W), 0)
    lane_w = lax.broadcasted_iota(jnp.int32, (rows, ATTN_W), 1)
    head_lanes = (row_w % N_HEADS) == (lane_w // HEAD_DIM)
    row1 = lax.broadcasted_iota(jnp.int32, (rows, 1), 0)

    def page_index(g, r):
        b, c = lax.div(g, n_chunks), lax.rem(g, n_chunks)
        return pt_ref[b * n_pages + n_pages - 1 - (c * pps + r)]

    def k_copies(g, slot):
        cps = []
        for r in range(pps):
            idx = page_index(g, r)
            cps.append(pltpu.make_async_copy(ck_hbm.at[idx], kbuf.at[slot, r], ksem.at[slot]))
            cps.append(pltpu.make_async_copy(clf_hbm.at[idx], lfbuf.at[slot, r], lfsem.at[slot]))
        return cps

    def v_copies(g, grp, dst, sem):
        return [pltpu.make_async_copy(cv_hbm.at[page_index(g, grp * vsub + r)], dst.at[r], sem)
                for r in range(vsub)]

    def row(ref, b, i, n):
        return ref[b, i:i + 1, :]

    def init(b):
        qrep = jnp.concatenate(
            [jnp.broadcast_to(row(q_ref, b, i, ATTN_W), (N_HEADS, ATTN_W)) for i in range(n_new)],
            axis=0)
        qbd = jnp.where(head_lanes, qrep, 0.0)
        qbd_ref[...] = qbd.astype(BF16)
        cn8_ref[...] = jnp.zeros(cn8_ref.shape, F32)
        for i in range(n_new):
            cn8_ref[i:i + 1, :] = row(cn_ref, b, i, LANES)
        cnt = jnp.transpose(cn8_ref[...])[0:rows, :] * LOG2E
        cncol = jnp.zeros((rows, 1), F32)
        for i in range(n_new):
            cncol = jnp.where(row1 // N_HEADS == i, cnt[:, i:i + 1], cncol)
        cncol_ref[...] = jnp.broadcast_to(cncol, cncol_ref.shape)
        s_new = []
        for j in range(n_new):
            sj = (jnp.sum(qbd * row(kn_ref, b, j, ATTN_W), axis=1, keepdims=True)
                  + cncol - cnt[:, j:j + 1])
            s_new.append(jnp.where(row1 // N_HEADS >= j, sj, NEG))
        m0 = functools.reduce(jnp.maximum, s_new)
        l0 = jnp.zeros((rows, 1), F32)
        a0 = jnp.zeros((rows, ATTN_W), F32)
        for j in range(n_new):
            pj = jnp.exp2(s_new[j] - m0)
            l0 = l0 + pj
            a0 = a0 + pj * row(vn_ref, b, j, ATTN_W)
        m_ref[...] = jnp.broadcast_to(m0, m_ref.shape)
        l_ref[...] = jnp.broadcast_to(l0, l_ref.shape)
        acc_ref[...] = a0
        tail_ref[...] = jnp.zeros(tail_ref.shape, F32)

    def finish(b):
        o = acc_ref[...] / _lane_tile(l_ref[...], ATTN_W // LANES)
        o = jnp.where(head_lanes, o, 0.0)
        for i in range(n_new):
            o_ref[b, i:i + 1, :] = jnp.sum(o[N_HEADS * i:N_HEADS * (i + 1), :], axis=0,
                                           keepdims=True)

    lane8 = lax.broadcasted_iota(jnp.int32, (N_HEADS, page), 1)

    def head(g):
        slot = lax.rem(g, 2)
        b, c = lax.div(g, n_chunks), lax.rem(g, n_chunks)

        @pl.when(g + 1 < total)
        def _():
            for cp in k_copies(g + 1, 1 - slot):
                cp.start()

        @pl.when(c == 0)
        def _():
            init(b)
            for cp in v_copies(g, 0, vpre, vsem.at[0]):
                cp.wait()

        for cp in k_copies(g, slot):
            cp.wait()

    def scores(g):
        slot = lax.rem(g, 2)
        kcat = jnp.concatenate([kbuf[slot, r].reshape(ATTN_W, page).astype(BF16)
                                for r in range(pps)], axis=1)
        qk = jnp.dot(qbd_ref[...], kcat, preferred_element_type=F32)
        tail = tail_ref[...]
        tails = []
        for r in range(pps):
            tails.append(tail)
            tail = tail + jnp.sum(lfbuf[slot, r] * LOG2E, axis=1, keepdims=True)
        tail_ref[...] = tail
        base = qk + jnp.concatenate([jnp.concatenate(tails, axis=1)] * n_new, axis=0) + cncol_ref[:, 0:1]
        base_ref[...] = base
        over = base - _lane_tile(m_ref[...], n_keys // LANES)
        for h in range(n_groups):
            gap_ref[h] = jnp.max(over[:, g_keys * h:g_keys * (h + 1)])

    def tail_phase(g):
        slot = lax.rem(g, 2)
        b, c = lax.div(g, n_chunks), lax.rem(g, n_chunks)
        live = [gap_ref[h] >= -SKIP_LOG2 for h in range(n_groups)]

        def update(h, vsrc):
            g_parts = []
            for r in range(vsub * h, vsub * (h + 1)):
                lf = lfbuf[slot, r] * LOG2E
                incl = lf
                d = 1
                while d < page:
                    incl = incl + jnp.where(lane8 + d < page, pltpu.roll(incl, page - d, axis=1), 0.0)
                    d *= 2
                g_parts.append(incl - lf)
            s = (base_ref[:, g_keys * h:g_keys * (h + 1)]
                 + jnp.concatenate([jnp.concatenate(g_parts, axis=1)] * n_new, axis=0))
            m_prev = m_ref[...]
            m_next = jnp.maximum(m_prev, jnp.max(s, axis=1, keepdims=True))
            alpha = jnp.exp2(m_prev - m_next)
            p = jnp.exp2(s - _lane_tile(m_next, g_keys // LANES))
            l_ref[...] = alpha * l_ref[...] + jnp.sum(p, axis=1, keepdims=True)
            m_ref[...] = m_next
            vcat = jnp.concatenate([vsrc[r].reshape(ATTN_W, page).astype(BF16)
                                    for r in range(vsub)], axis=1)
            acc_ref[...] = (_lane_tile(alpha, ATTN_W // LANES) * acc_ref[...]
                            + lax.dot_general(p.astype(BF16), vcat, (((1,), (1,)), ((), ())),
                                              preferred_element_type=F32))

        def fetch_and_update(h):
            cps = v_copies(g, h, vdem, vsem.at[1])
            for cp in cps:
                cp.start()
            for cp in cps:
                cp.wait()
            update(h, vdem)

        for h in range(n_groups):
            if h == 0:
                @pl.when(jnp.logical_and(live[0], c == 0))
                def _():
                    update(0, vpre)

                @pl.when(jnp.logical_and(live[0], c > 0))
                def _():
                    fetch_and_update(0)
            else:
                @pl.when(live[h])
                def _(h=h):
                    fetch_and_update(h)

        @pl.when(c == n_chunks - 1)
        def _():
            finish(b)

        @pl.when(jnp.logical_and(c == 0, b + 1 < nb))
        def _():
            for cp in v_copies(g + n_chunks, 0, vpre, vsem.at[0]):
                cp.start()

    def prologue():
        for cp in k_copies(0, 0):
            cp.start()
        for cp in v_copies(0, 0, vpre, vsem.at[0]):
            cp.start()

    return prologue, (head, scores, tail_phase), total


def _decode_setup(nb, n_new, n_pages, page):
    pps = DEC_PAGES
    while n_pages % pps:
        pps //= 2
    vsub = min(DEC_VSUB, pps)
    rows = n_new * N_HEADS
    assert page % LANES == 0 and rows % 8 == 0 and pps % vsub == 0
    params = dict(nb=nb, n_new=n_new, n_pages=n_pages, page=page, pps=pps, vsub=vsub)
    scratch = [pltpu.VMEM((2, pps, N_HEADS, HEAD_DIM, page), F32),
               pltpu.VMEM((vsub, N_HEADS, HEAD_DIM, page), F32),
               pltpu.VMEM((vsub, N_HEADS, HEAD_DIM, page), F32),
               pltpu.VMEM((2, pps, N_HEADS, page), F32),
               pltpu.SemaphoreType.DMA((2,)),
               pltpu.SemaphoreType.DMA((2,)),
               pltpu.SemaphoreType.DMA((2,)),
               pltpu.VMEM((rows, ATTN_W), BF16),
               pltpu.VMEM((8, LANES), F32),
               pltpu.VMEM((rows, LANES), F32),
               pltpu.VMEM((rows, LANES), F32),
               pltpu.VMEM((rows, LANES), F32),
               pltpu.VMEM((rows, ATTN_W), F32),
               pltpu.VMEM((N_HEADS, LANES), F32),
               pltpu.VMEM((rows, pps * page), F32),
               pltpu.SMEM((pps // vsub,), F32)]
    return params, scratch


def _merge_sample_kernel(x_ref, u_ref, sp_ref, sgp_ref, attn_ref, sga_ref, pw_ref, ps_ref, wo_ref,
                         nf_ref, y_ref, pool_ref, *, n_new, past):
    u = [u_ref[:, POOL_W * i:POOL_W * (i + 1)] for i in range(n_new)]
    hist = [sp_ref[r] for r in range(POOL_HIST)]
    ext = hist + u
    pooled_rows = []
    for i in range(n_new):
        groups = []
        for g, w in enumerate(POOL_WINDOWS):
            sl = slice(POOL_GC * g, POOL_GC * (g + 1))
            end = POOL_HIST + i
            total = ext[end][:, sl]
            for r in range(end - w + 1, end):
                total = total + ext[r][:, sl]
            groups.append(total / float(min(w, past + i + 1)) - u[i][:, sl])
        pooled_rows.append(jnp.concatenate(groups, axis=1))
    pooled = jnp.concatenate(pooled_rows, axis=0)
    cat = lambda ref, n: jnp.concatenate([ref[:, n * i:n * (i + 1)] for i in range(n_new)], axis=0)
    y = _merge(cat(x_ref, D_MODEL), pooled, cat(sgp_ref, POOL_W), cat(attn_ref, ATTN_W),
               cat(sga_ref, ATTN_W), pw_ref, ps_ref[...], wo_ref, nf_ref[...])
    nb = x_ref.shape[0]
    for i in range(n_new):
        y_ref[:, D_MODEL * i:D_MODEL * (i + 1)] = y[nb * i:nb * (i + 1), :]
    new_hist = ext[-POOL_HIST:]
    for r in range(POOL_HIST):
        pool_ref[:, POOL_W * r:POOL_W * (r + 1)] = new_hist[r]


def _merge_sample(xs2, u2, sp2, sgp2, attn2, sga2, pw, ps, wo, nf, *, n_new, past):
    nb = xs2.shape[0]
    return pl.pallas_call(
        functools.partial(_merge_sample_kernel, n_new=n_new, past=past),
        out_shape=(jax.ShapeDtypeStruct((nb, n_new * D_MODEL), F32),
                   jax.ShapeDtypeStruct((nb, POOL_HIST * POOL_W), F32)),
        compiler_params=pltpu.CompilerParams(vmem_limit_bytes=VMEM_LIMIT),
        name="merge_sample",
    )(xs2, u2, sp2, sgp2, attn2, sga2, pw, ps, wo, nf)


def kernel(x_prompt, x_sample, cache_k, cache_v, cache_logf, state_pool, page_table,
           norm_w, w_in, b_forget, pool_w, pool_scale, w_out, norm_f):
    depth = norm_w.shape[0]
    assert depth == 1, "a single layer is supported"
    b_p, seq, _ = x_prompt.shape
    b_s, t_s, _ = x_sample.shape
    n_phys, page = cache_k.shape[1], cache_k.shape[2]
    past = page_table.shape[1] * page
    ll = 0

    wt_all = jnp.swapaxes(w_in[ll], 0, 1)
    wt = wt_all[:W_MAIN].astype(BF16)
    wf = jnp.tile(wt_all[W_MAIN:], (LANES // N_HEADS, 1)).astype(BF16)
    bft = jnp.tile(b_forget[ll], LANES // N_HEADS).reshape(1, LANES).astype(F32)
    nw = norm_w[ll].reshape(1, D_MODEL)
    nf = norm_f.reshape(1, D_MODEL)
    pw = pool_w[ll].astype(BF16)
    ps = pool_scale[ll].reshape(1, POOL_W)
    wo = w_out[ll].astype(BF16)
    pp_np, cc_np = _placement()
    pp = jnp.asarray(pp_np, BF16)
    cc = jnp.asarray(cc_np, F32)

    xp2 = x_prompt.reshape(b_p * seq, D_MODEL)
    seg_np = (np.arange(ATTN_W)[:, None] // HEAD_DIM == np.arange(LANES)[None, :] % N_HEADS)
    seg = jnp.asarray(seg_np, BF16)
    u, sgp, qa, ka, kf, vf, vb, sga, logf, stats = _proj_prompt(xp2, nw, wt, wf, bft, pp, cc, seg,
                                                                seq=seq)
    n_blk = seq // min(ATTN_T, seq)
    plan = _plan(stats, n_batch=b_p, n_blk=n_blk)[:b_p * N_PAIRS, :n_blk].reshape(-1)

    xs2 = x_sample.reshape(b_s, t_s * D_MODEL)
    us, sgps, qs, ks, vs, sgas, lfs, cns = _proj_sample(xs2, nw, wt, wf, bft, n_new=t_s)
    ck = jnp.transpose(cache_k[ll], (0, 2, 3, 1))
    cv = jnp.transpose(cache_v[ll], (0, 2, 3, 1))
    clf_t = jnp.swapaxes(cache_logf[ll], 1, 2)

    per_pos = lambda a: a.reshape(b_s, t_s, a.shape[1] // t_s)
    attn, attn_s = _attention(plan, qa, ka, vb, page_table, per_pos(qs), per_pos(ks), per_pos(vs),
                              per_pos(cns), ck, cv, clf_t, n_batch=b_p, seq=seq)
    yp, pool_p = _merge_prompt(xp2, u, sgp, attn, sga, pw, ps, wo, nf, n_batch=b_p, seq=seq)
    ys, pool_s = _merge_sample(xs2, us, jnp.swapaxes(state_pool[ll], 0, 1), sgps,
                               attn_s.reshape(b_s, t_s * ATTN_W), sgas, pw, ps, wo, nf,
                               n_new=t_s, past=past)

    return (yp.reshape(b_p, seq, D_MODEL),
            ys.reshape(b_s, t_s, D_MODEL),
            jnp.transpose(kf.reshape(1, b_p, N_HEADS, HEAD_DIM, seq), (0, 1, 4, 2, 3)),
            jnp.transpose(vf.reshape(1, b_p, N_HEADS, HEAD_DIM, seq), (0, 1, 4, 2, 3)),
            jnp.transpose(logf, (0, 2, 1)).reshape(1, b_p, seq, N_HEADS),
            pool_p.reshape(1, b_p, POOL_HIST, POOL_W),
            ks.reshape(1, b_s, t_s, N_HEADS, HEAD_DIM),
            vs.reshape(1, b_s, t_s, N_HEADS, HEAD_DIM),
            lfs.reshape(b_s, t_s, LANES)[:, :, :N_HEADS].reshape(1, b_s, t_s, N_HEADS),
            pool_s.reshape(1, b_s, POOL_HIST, POOL_W))
```

```python
import functools
import math

import numpy as np
import jax
import jax.numpy as jnp
from jax import lax
from jax.experimental import pallas as pl
from jax.experimental.pallas import tpu as pltpu

D_MODEL = 1024
POOL_W = 512
ATTN_W = 512
N_HEADS = 8
HEAD_DIM = 64
POOL_WINDOWS = (2, 4, 8, 16)
POOL_GC = 128
POOL_HIST = 15
EPS = 1e-6

LANES = 128
LOG2E = 1.4426950408889634
Q_SCALE = LOG2E / math.sqrt(HEAD_DIM)
NEG = -1e30
N_PAIRS = N_HEADS // 2
W_MAIN = 2 * POOL_W + 4 * ATTN_W
BIAS_LANES = LANES // N_HEADS
BIAS_COLS = 2 * LANES
VMEM_LIMIT = 56 * 1024 * 1024

SKIP_LOG2 = 152.0
NORM_SLACK = 1.02

PROJ_TM = 1024
MERGE_SUB = 256
ATTN_T = 512
DEC_PAGES = 16
DEC_VSUB = 4

BF16 = jnp.bfloat16
F32 = jnp.float32


def _placement():
    pp = np.zeros((LANES, BIAS_COLS), np.float32)
    cc = np.zeros((1, BIAS_COLS), np.float32)
    for h in range(N_HEADS):
        for i in range(3):
            pp[8 * i + h, BIAS_LANES * h + i] = 1.0
            pp[8 * i + h, LANES + BIAS_LANES * h + 3 + i] = -1.0
            cc[0, BIAS_LANES * h + 3 + i] = 1.0
            cc[0, LANES + BIAS_LANES * h + i] = 1.0
    return pp, cc


def _log_sigmoid(x):
    return jnp.minimum(x, 0.0) - jnp.log(1.0 + jnp.exp(-jnp.abs(x)))


def _silu(x):
    return x * jax.nn.sigmoid(x)


def _rmsnorm(xf, w):
    ms = jnp.mean(xf * xf, axis=-1, keepdims=True)
    return xf * lax.rsqrt(ms + EPS) * w


def _project(h, wt_ref, wf_ref):
    def cols(ref, c0, n):
        return lax.dot_general(h, ref[c0:c0 + n, :], (((1,), (1,)), ((), ())),
                               preferred_element_type=F32)
    fl = cols(wf_ref, 0, LANES)
    gp = cols(wt_ref, POOL_W, POOL_W)
    ga = cols(wt_ref, 2 * POOL_W + 3 * ATTN_W, ATTN_W)
    k = cols(wt_ref, 2 * POOL_W + ATTN_W, ATTN_W)
    v = cols(wt_ref, 2 * POOL_W + 2 * ATTN_W, ATTN_W)
    q = cols(wt_ref, 2 * POOL_W, ATTN_W)
    u = cols(wt_ref, 0, POOL_W)
    return u, gp, q, k, v, ga, fl


def _proj_prompt_kernel(x_ref, nw_ref, wt_ref, wf_ref, bf_ref, pp_ref, cc_ref, seg_ref,
                        u_ref, sgp_ref, qa_ref, ka_ref, k_ref, v_ref, vb_ref, sga_ref, logf_ref,
                        st_ref, carry_ref, *, tm, ta, tiles_per_batch):
    i = pl.program_id(0)

    @pl.when(i % tiles_per_batch == 0)
    def _():
        carry_ref[...] = jnp.zeros(carry_ref.shape, F32)

    row = lax.broadcasted_iota(jnp.int32, (ta, LANES), 0)
    lane = lax.broadcasted_iota(jnp.int32, (ta, LANES), 1)
    srow = lax.broadcasted_iota(jnp.int32, (8, LANES), 0)
    first_half = lane < HEAD_DIM
    carry = carry_ref[0:1, :]

    for sb in range(tm // ta):
        rs = slice(sb * ta, (sb + 1) * ta)
        h = _rmsnorm(x_ref[rs, :], nw_ref[...]).astype(BF16)
        u, gp, q, k, v, ga, fl = _project(h, wt_ref, wf_ref)
        u_ref[rs, :] = u
        sgp_ref[rs, :] = _silu(gp).astype(BF16)
        k_ref[0, :, rs] = jnp.transpose(k)
        v_ref[0, :, rs] = jnp.transpose(v)
        vb_ref[rs, :] = v.astype(BF16)
        sga_ref[rs, :] = _silu(ga).astype(BF16)
        logf = _log_sigmoid(fl + bf_ref[...])
        logf_ref[0, :, rs] = jnp.transpose(logf)[0:N_HEADS, :]

        acc = logf
        d = 1
        while d < ta:
            acc = acc + jnp.where(row >= d, pltpu.roll(acc, d, axis=0), 0.0)
            d *= 2
        f_run = acc + carry
        carry = f_run[ta - 1:ta, :]

        f2 = f_run * LOG2E
        p1 = f2.astype(BF16).astype(F32)
        r1 = f2 - p1
        p2 = r1.astype(BF16).astype(F32)
        p3 = (r1 - p2).astype(BF16).astype(F32)
        pieces = jnp.where(lane < 8, p1, jnp.where(lane < 16, p2, jnp.where(lane < 24, p3, 0.0)))
        bias = jnp.dot(pieces.astype(BF16), pp_ref[...], preferred_element_type=F32) + cc_ref[...]

        qs = q * Q_SCALE

        qn2 = jnp.dot((qs * qs).astype(BF16), seg_ref[...], preferred_element_type=F32)
        kn2 = jnp.dot((k * k).astype(BF16), seg_ref[...], preferred_element_type=F32)
        qmax = jnp.sqrt(jnp.max(qn2, axis=0, keepdims=True))
        kmax = jnp.sqrt(jnp.max(kn2, axis=0, keepdims=True))
        st_ref[8 * sb:8 * (sb + 1), :] = jnp.where(
            srow == 0, qmax, jnp.where(srow == 1, kmax, jnp.where(
                srow == 2, f2[0:1, :], jnp.where(srow == 3, f2[ta - 1:ta, :], 0.0))))

        q_bias = bias[:, 0:LANES].astype(BF16)
        k_bias = bias[:, LANES:2 * LANES]
        for p in range(N_PAIRS):
            qa_ref[rs, 2 * LANES * p:2 * LANES * p + LANES] = (
                qs[:, LANES * p:LANES * (p + 1)].astype(BF16))
            qa_ref[rs, 2 * LANES * p + LANES:2 * LANES * (p + 1)] = q_bias
            kp = k[:, LANES * p:LANES * (p + 1)]
            for e in range(2):
                hh = 2 * p + e
                keep = first_half if e == 0 else jnp.logical_not(first_half)
                ka_ref[rs, 2 * LANES * hh:2 * LANES * hh + LANES] = (
                    jnp.where(keep, kp, 0.0).astype(BF16))
                own = jnp.logical_and(lane >= BIAS_LANES * hh, lane < BIAS_LANES * (hh + 1))
                ka_ref[rs, 2 * LANES * hh + LANES:2 * LANES * (hh + 1)] = (
                    jnp.where(own, k_bias, 0.0).astype(BF16))

    carry_ref[...] = jnp.broadcast_to(carry, carry_ref.shape)


def _proj_prompt(x2, nw, wt, wf, bft, pp, cc, seg, *, seq):
    rows = x2.shape[0]
    tm = min(PROJ_TM, seq)
    ta = min(ATTN_T, seq)
    assert seq % tm == 0 and rows % seq == 0 and tm % ta == 0 and tm % LANES == 0
    n_batch = rows // seq
    row_blk = lambda n: pl.BlockSpec((tm, n), lambda i: (i, 0))
    const = lambda shape: pl.BlockSpec(shape, lambda i: (0, 0))
    out_shape = (
        jax.ShapeDtypeStruct((rows, POOL_W), F32),
        jax.ShapeDtypeStruct((rows, POOL_W), BF16),
        jax.ShapeDtypeStruct((rows, N_PAIRS * 2 * LANES), BF16),
        jax.ShapeDtypeStruct((rows, N_HEADS * 2 * LANES), BF16),
        jax.ShapeDtypeStruct((n_batch, ATTN_W, seq), F32),
        jax.ShapeDtypeStruct((n_batch, ATTN_W, seq), F32),
        jax.ShapeDtypeStruct((rows, ATTN_W), BF16),
        jax.ShapeDtypeStruct((rows, ATTN_W), BF16),
        jax.ShapeDtypeStruct((n_batch, N_HEADS, seq), F32),
        jax.ShapeDtypeStruct((rows // ta * 8, LANES), F32),
    )
    tpb = seq // tm
    t_minor = lambda n: pl.BlockSpec((1, n, tm), lambda i: (i // tpb, 0, i % tpb))
    out_specs = (row_blk(POOL_W), row_blk(POOL_W), row_blk(N_PAIRS * 2 * LANES),
                 row_blk(N_HEADS * 2 * LANES), t_minor(ATTN_W), t_minor(ATTN_W), row_blk(ATTN_W),
                 row_blk(ATTN_W), t_minor(N_HEADS),
                 pl.BlockSpec((tm // ta * 8, LANES), lambda i: (i, 0)))
    return pl.pallas_call(
        functools.partial(_proj_prompt_kernel, tm=tm, ta=ta, tiles_per_batch=seq // tm),
        grid=(rows // tm,),
        in_specs=[row_blk(D_MODEL), const((1, D_MODEL)), const((W_MAIN, D_MODEL)),
                  const((LANES, D_MODEL)), const((1, LANES)),
                  const((LANES, BIAS_COLS)), const((1, BIAS_COLS)), const((ATTN_W, LANES))],
        out_specs=out_specs,
        out_shape=out_shape,
        scratch_shapes=[pltpu.VMEM((8, LANES), F32)],
        compiler_params=pltpu.CompilerParams(dimension_semantics=("arbitrary",),
                                             vmem_limit_bytes=VMEM_LIMIT),
        name="proj_prompt",
    )(x2, nw, wt, wf, bft, pp, cc, seg)


def _lane_tile(x, reps):
    return jnp.concatenate([x] * reps, axis=1)


def _plan_kernel(st_ref, o_ref, *, n_batch, n_blk):
    row = lax.broadcasted_iota(jnp.int32, (LANES, LANES), 0).astype(F32)
    col = lax.broadcasted_iota(jnp.int32, (LANES, LANES), 1).astype(F32)
    out = jnp.zeros((LANES, LANES), F32)
    for b in range(n_batch):
        def stat(r):
            v = st_ref[pl.ds((b * n_blk) * 8 + r, n_blk, stride=8), :]
            return jnp.concatenate([v, jnp.zeros((LANES - n_blk, LANES), F32)], axis=0)
        qm, km, ft0, fs1 = stat(0), stat(1), stat(2), stat(3)
        km_t, fs1_t = jnp.transpose(km), jnp.transpose(fs1)
        for p in range(N_PAIRS):
            first = row
            for h in (2 * p, 2 * p + 1):
                qcol = qm[:, h:h + 1]
                ub = (NORM_SLACK * (qcol * km_t[h:h + 1, :] + qcol * km[:, h:h + 1])
                      + ft0[:, h:h + 1] - fs1_t[h:h + 1, :])
                needed = jnp.logical_and(ub >= -SKIP_LOG2, col < row)
                first = jnp.minimum(first, jnp.where(needed, col, row))
            start = jnp.min(first, axis=1, keepdims=True)
            out = jnp.where(col == b * N_PAIRS + p, start, out)
    o_ref[...] = jnp.transpose(out).astype(jnp.int32)


def _plan(stats, *, n_batch, n_blk):
    assert n_blk <= LANES and n_batch * N_PAIRS <= LANES
    return pl.pallas_call(
        functools.partial(_plan_kernel, n_batch=n_batch, n_blk=n_blk),
        out_shape=jax.ShapeDtypeStruct((LANES, LANES), jnp.int32),
        name="attn_plan",
    )(stats)


def _attn_kernel(plan_ref, pt_ref, qa_ref, ka_ref, vb_ref, qd_ref, kn_ref, vn_ref, cn_ref,
                 ck_hbm, cv_hbm, clf_hbm, o_ref, od_ref, m_ref, l_ref, acc_ref, *dec_scratch,
                 t, nq, n_steps, dec):
    b, pr, i = pl.program_id(0), pl.program_id(1), pl.program_id(2)
    step = (b * N_PAIRS + pr) * nq + i
    dec_prologue, (dec_head, dec_scores, dec_tail), n_chunks = _decode_fns(
        pt_ref, qd_ref, kn_ref, vn_ref, cn_ref, ck_hbm, cv_hbm, clf_hbm, od_ref, *dec_scratch, **dec)
    per_step = -(-n_chunks // n_steps)
    interleave = n_chunks == n_steps

    @pl.when(step == 0)
    def _():
        dec_prologue()

    if interleave:
        dec_head(step)
    else:
        for k in range(per_step):
            g = step * per_step + k

            @pl.when(g < n_chunks)
            def _(g=g):
                dec_head(g)
                dec_scores(g)
                dec_tail(g)

    q = qa_ref[...]
    lane = lax.broadcasted_iota(jnp.int32, (t, LANES), 1)

    def scores(kstart, width, diag_col, e):
        kblk = ka_ref[pl.ds(kstart, width), 2 * LANES * e:2 * LANES * (e + 1)]
        s = lax.dot_general(q, kblk, (((1,), (1,)), ((), ())), preferred_element_type=F32)
        if diag_col is not None:
            r = lax.broadcasted_iota(jnp.int32, (t, width - diag_col), 0)
            c = lax.broadcasted_iota(jnp.int32, (t, width - diag_col), 1)
            diag = jnp.where(c <= r, s[:, diag_col:], NEG)
            s = diag if diag_col == 0 else jnp.concatenate([s[:, :diag_col], diag], axis=1)
        return s

    def block(kstart, width, diag_col):
        vblk = vb_ref[pl.ds(kstart, width), :]
        for e in range(2):
            s = scores(kstart, width, diag_col, e)
            m_prev = m_ref[e]
            m_next = jnp.maximum(m_prev, jnp.max(s, axis=1, keepdims=True))
            alpha = jnp.exp2(m_prev - m_next)
            p = jnp.exp2(s - _lane_tile(m_next, width // LANES))
            l_ref[e] = alpha * l_ref[e] + jnp.sum(p, axis=1, keepdims=True)
            m_ref[e] = m_next
            acc_ref[e] = alpha * acc_ref[e] + jnp.dot(p.astype(BF16), vblk,
                                                      preferred_element_type=F32)

    def body(j, carry):
        block(pl.multiple_of(j * t, t), t, None)
        return carry

    first = plan_ref[(b * N_PAIRS + pr) * nq + i]
    one_before = jnp.logical_and(i > 0, first == i - 1)

    m_ref[...] = jnp.full(m_ref.shape, NEG, F32)
    l_ref[...] = jnp.zeros(l_ref.shape, F32)
    acc_ref[...] = jnp.zeros(acc_ref.shape, F32)

    @pl.when(one_before)
    def _():
        if interleave:
            dec_scores(step)
        block(pl.multiple_of((i - 1) * t, t), 2 * t, t)

    @pl.when(jnp.logical_not(one_before))
    def _():
        if interleave:
            dec_scores(step)
        lax.fori_loop(first, i, body, 0)
        block(pl.multiple_of(i * t, t), t, 0)

    o = jnp.where(lane < HEAD_DIM, acc_ref[0] / l_ref[0], acc_ref[1] / l_ref[1])
    o_ref[...] = o.astype(o_ref.dtype)
    if interleave:
        dec_tail(step)


def _attention(plan, qa, ka, vb, page_table, qd, kn, vn, cn, ck, cv, clf_t, *, n_batch, seq):
    t = min(ATTN_T, seq)
    assert seq % t == 0
    nq = seq // t
    nb, n_new, _ = qd.shape
    dec, dec_scratch = _decode_setup(nb, n_new, page_table.shape[1], ck.shape[3])
    whole = lambda n: pl.BlockSpec((nb, n_new, n), lambda b, p, i, plan, pt: (0, 0, 0))
    hbm = pl.BlockSpec(memory_space=pl.ANY)
    grid_spec = pltpu.PrefetchScalarGridSpec(
        num_scalar_prefetch=2,
        grid=(n_batch, N_PAIRS, nq),
        in_specs=[pl.BlockSpec((t, 2 * LANES), lambda b, p, i, plan, pt: (b * nq + i, p)),
                  pl.BlockSpec((seq, 4 * LANES), lambda b, p, i, plan, pt: (b, p)),
                  pl.BlockSpec((seq, LANES), lambda b, p, i, plan, pt: (b, p)),
                  whole(ATTN_W), whole(ATTN_W), whole(ATTN_W), whole(LANES), hbm, hbm, hbm],
        out_specs=(pl.BlockSpec((t, LANES), lambda b, p, i, plan, pt: (b * nq + i, p)),
                   whole(ATTN_W)),
        scratch_shapes=[pltpu.VMEM((2, t, LANES), F32), pltpu.VMEM((2, t, LANES), F32),
                        pltpu.VMEM((2, t, LANES), F32)] + dec_scratch)
    return pl.pallas_call(
        functools.partial(_attn_kernel, t=t, nq=nq, n_steps=n_batch * N_PAIRS * nq, dec=dec),
        grid_spec=grid_spec,
        out_shape=(jax.ShapeDtypeStruct((n_batch * seq, ATTN_W), BF16),
                   jax.ShapeDtypeStruct((nb, n_new, ATTN_W), F32)),
        compiler_params=pltpu.CompilerParams(
            dimension_semantics=("arbitrary", "arbitrary", "arbitrary"),
            vmem_limit_bytes=VMEM_LIMIT),
        name="attention",
    )(plan, page_table.reshape(-1), qa, ka, vb, qd, kn, vn, cn, ck, cv, clf_t)


def _merge(x, pooled, sgp, attn, sga, pw_ref, ps, wo_ref, nf):
    mixed = [jnp.dot(pooled[:, POOL_GC * g:POOL_GC * (g + 1)].astype(BF16), pw_ref[g],
                     preferred_element_type=F32) for g in range(len(POOL_WINDOWS))]
    pool_out = jnp.concatenate(mixed, axis=1) * ps
    mix = jnp.concatenate([(pool_out * sgp.astype(F32)).astype(BF16),
                           (attn.astype(F32) * sga.astype(F32)).astype(BF16)], axis=1)
    xo = x + jnp.dot(mix, wo_ref[...], preferred_element_type=F32)
    return _rmsnorm(xo, nf)


def _merge_prompt_kernel(x_ref, u_ref, uh_ref, sgp_ref, attn_ref, sga_ref, pw_ref, ps_ref, wo_ref,
                         nf_ref, y_ref, pool_ref, *, tm, sub, tiles_per_batch):
    i = pl.program_id(0)
    ti = i % tiles_per_batch
    halo = jnp.where(ti == 0, 0.0, uh_ref[...])
    for sb in range(tm // sub):
        rs = slice(sb * sub, (sb + 1) * sub)
        u = u_ref[rs, :]
        before = halo if sb == 0 else u_ref[sb * sub - 16:sb * sub, :]
        level = jnp.concatenate([before, u], axis=0)
        pos = ti * tm + sb * sub + lax.broadcasted_iota(jnp.int32, (sub, 1), 0)
        groups = []
        shift = 1
        for g, w in enumerate(POOL_WINDOWS):
            while shift < w:
                level = level + pltpu.roll(level, shift, axis=0)
                shift *= 2
            cnt = jnp.minimum(w, pos + 1).astype(F32)
            groups.append(level[16:, 0:POOL_GC] / cnt - u[:, POOL_GC * g:POOL_GC * (g + 1)])
            level = level[:, POOL_GC:]
        pooled = jnp.concatenate(groups, axis=1)
        y_ref[rs, :] = _merge(x_ref[rs, :], pooled, sgp_ref[rs, :], attn_ref[rs, :],
                              sga_ref[rs, :], pw_ref, ps_ref[...], wo_ref, nf_ref[...])

    @pl.when(ti == tiles_per_batch - 1)
    def _():
        pool_ref[0] = u_ref[pl.ds(tm - POOL_HIST, POOL_HIST), :]


def _merge_prompt(x2, u, sgp, attn, sga, pw, ps, wo, nf, *, n_batch, seq):
    rows = x2.shape[0]
    tm = min(PROJ_TM, seq)
    tpb = seq // tm
    halo_blocks = tm // 16
    row_blk = lambda n: pl.BlockSpec((tm, n), lambda i: (i, 0))
    const2 = lambda shape: pl.BlockSpec(shape, lambda i: (0, 0))
    return pl.pallas_call(
        functools.partial(_merge_prompt_kernel, tm=tm, sub=min(MERGE_SUB, tm), tiles_per_batch=tpb),
        grid=(rows // tm,),
        in_specs=[row_blk(D_MODEL), row_blk(POOL_W),
                  pl.BlockSpec((16, POOL_W), lambda i: (jnp.maximum(i * halo_blocks - 1, 0), 0)),
                  row_blk(POOL_W), row_blk(ATTN_W), row_blk(ATTN_W),
                  pl.BlockSpec((len(POOL_WINDOWS), POOL_GC, POOL_GC), lambda i: (0, 0, 0)),
                  const2((1, POOL_W)), const2((D_MODEL, D_MODEL)), const2((1, D_MODEL))],
        out_specs=(row_blk(D_MODEL),
                   pl.BlockSpec((1, POOL_HIST, POOL_W), lambda i: (i // tpb, 0, 0))),
        out_shape=(jax.ShapeDtypeStruct((rows, D_MODEL), F32),
                   jax.ShapeDtypeStruct((n_batch, POOL_HIST, POOL_W), F32)),
        compiler_params=pltpu.CompilerParams(dimension_semantics=("arbitrary",),
                                             vmem_limit_bytes=VMEM_LIMIT),
        name="merge_prompt",
    )(x2, u, u, sgp, attn, sga, pw, ps, wo, nf)


def _proj_sample_kernel(x_ref, nw_ref, wt_ref, wf_ref, bf_ref,
                        u_ref, sgp_ref, q_ref, k_ref, v_ref, sga_ref, logf_ref, cn_ref, *, n_new):
    nb = x_ref.shape[0]
    x = jnp.concatenate([x_ref[:, D_MODEL * i:D_MODEL * (i + 1)] for i in range(n_new)], axis=0)
    h = _rmsnorm(x, nw_ref[...]).astype(BF16)
    u, gp, q, k, v, ga, fl = _project(h, wt_ref, wf_ref)
    logf = _log_sigmoid(fl + bf_ref[...])
    outs = ((u_ref, u), (sgp_ref, _silu(gp).astype(BF16)), (q_ref, q * Q_SCALE), (k_ref, k),
            (v_ref, v), (sga_ref, _silu(ga).astype(BF16)), (logf_ref, logf))
    cn = jnp.zeros((nb, LANES), F32)
    for i in range(n_new):
        for ref, val in outs:
            n = val.shape[1]
            ref[:, n * i:n * (i + 1)] = val[nb * i:nb * (i + 1), :]
        cn = cn + logf[nb * i:nb * (i + 1), :]
        cn_ref[:, LANES * i:LANES * (i + 1)] = cn


def _proj_sample(xs2, nw, wt, wf, bft, *, n_new):
    nb = xs2.shape[0]
    widths = (POOL_W, POOL_W, ATTN_W, ATTN_W, ATTN_W, ATTN_W, LANES, LANES)
    dtypes = (F32, BF16, F32, F32, F32, BF16, F32, F32)
    return pl.pallas_call(
        functools.partial(_proj_sample_kernel, n_new=n_new),
        out_shape=tuple(jax.ShapeDtypeStruct((nb, n_new * n), dt) for n, dt in zip(widths, dtypes)),
        compiler_params=pltpu.CompilerParams(vmem_limit_bytes=VMEM_LIMIT),
        name="proj_sample",
    )(xs2, nw, wt, wf, bft)


def _decode_fns(pt_ref, q_ref, kn_ref, vn_ref, cn_ref, ck_hbm, cv_hbm, clf_hbm, o_ref,
                kbuf, vpre, vdem, lfbuf, ksem, lfsem, vsem,
                qbd_ref, cn8_ref, cncol_ref, m_ref, l_ref, acc_ref, tail_ref, base_ref, gap_ref,
                *, nb, n_new, n_pages, page, pps, vsub):
    n_chunks = n_pages // pps
    n_groups = pps // vsub
    total = nb * n_chunks
    n_keys = pps * page
    g_keys = vsub * page
    rows = n_new * N_HEADS
    row_w = lax.broadcasted_iota(jnp.int32, (rows, ATTN_W), 0)
    lane_w = lax.broadcasted_iota(jnp.int32, (rows, ATTN_W), 1)
    head_lanes = (row_w % N_HEADS) == (lane_w // HEAD_DIM)
    row1 = lax.broadcasted_iota(jnp.int32, (rows, 1), 0)

    def page_index(g, r):
        b, c = lax.div(g, n_chunks), lax.rem(g, n_chunks)
        return pt_ref[b * n_pages + n_pages - 1 - (c * pps + r)]

    def k_copies(g, slot):
        cps = []
        for r in range(pps):
            idx = page_index(g, r)
            cps.append(pltpu.make_async_copy(ck_hbm.at[idx], kbuf.at[slot, r], ksem.at[slot]))
            cps.append(pltpu.make_async_copy(clf_hbm.at[idx], lfbuf.at[slot, r], lfsem.at[slot]))
        return cps

    def v_copies(g, grp, dst, sem):
        return [pltpu.make_async_copy(cv_hbm.at[page_index(g, grp * vsub + r)], dst.at[r], sem)
                for r in range(vsub)]

    def row(ref, b, i, n):
        return ref[b, i:i + 1, :]

    def init(b):
        qrep = jnp.concatenate(
            [jnp.broadcast_to(row(q_ref, b, i, ATTN_W), (N_HEADS, ATTN_W)) for i in range(n_new)],
            axis=0)
        qbd = jnp.where(head_lanes, qrep, 0.0)
        qbd_ref[...] = qbd.astype(BF16)
        cn8_ref[...] = jnp.zeros(cn8_ref.shape, F32)
        for i in range(n_new):
            cn8_ref[i:i + 1, :] = row(cn_ref, b, i, LANES)
        cnt = jnp.transpose(cn8_ref[...])[0:rows, :] * LOG2E
        cncol = jnp.zeros((rows, 1), F32)
        for i in range(n_new):
            cncol = jnp.where(row1 // N_HEADS == i, cnt[:, i:i + 1], cncol)
        cncol_ref[...] = jnp.broadcast_to(cncol, cncol_ref.shape)
        s_new = []
        for j in range(n_new):
            sj = (jnp.sum(qbd * row(kn_ref, b, j, ATTN_W), axis=1, keepdims=True)
                  + cncol - cnt[:, j:j + 1])
            s_new.append(jnp.where(row1 // N_HEADS >= j, sj, NEG))
        m0 = functools.reduce(jnp.maximum, s_new)
        l0 = jnp.zeros((rows, 1), F32)
        a0 = jnp.zeros((rows, ATTN_W), F32)
        for j in range(n_new):
            pj = jnp.exp2(s_new[j] - m0)
            l0 = l0 + pj
            a0 = a0 + pj * row(vn_ref, b, j, ATTN_W)
        m_ref[...] = jnp.broadcast_to(m0, m_ref.shape)
        l_ref[...] = jnp.broadcast_to(l0, l_ref.shape)
        acc_ref[...] = a0
        tail_ref[...] = jnp.zeros(tail_ref.shape, F32)

    def finish(b):
        o = acc_ref[...] / _lane_tile(l_ref[...], ATTN_W // LANES)
        o = jnp.where(head_lanes, o, 0.0)
        for i in range(n_new):
            o_ref[b, i:i + 1, :] = jnp.sum(o[N_HEADS * i:N_HEADS * (i + 1), :], axis=0,
                                           keepdims=True)

    lane8 = lax.broadcasted_iota(jnp.int32, (N_HEADS, page), 1)

    def head(g):
        slot = lax.rem(g, 2)
        b, c = lax.div(g, n_chunks), lax.rem(g, n_chunks)

        @pl.when(g + 1 < total)
        def _():
            for cp in k_copies(g + 1, 1 - slot):
                cp.start()

        @pl.when(c == 0)
        def _():
            init(b)
            for cp in v_copies(g, 0, vpre, vsem.at[0]):
                cp.wait()

        for cp in k_copies(g, slot):
            cp.wait()

    def scores(g):
        slot = lax.rem(g, 2)
        kcat = jnp.concatenate([kbuf[slot, r].reshape(ATTN_W, page).astype(BF16)
                                for r in range(pps)], axis=1)
        qk = jnp.dot(qbd_ref[...], kcat, preferred_element_type=F32)
        tail = tail_ref[...]
        tails = []
        for r in range(pps):
            tails.append(tail)
            tail = tail + jnp.sum(lfbuf[slot, r] * LOG2E, axis=1, keepdims=True)
        tail_ref[...] = tail
        base = qk + jnp.concatenate([jnp.concatenate(tails, axis=1)] * n_new, axis=0) + cncol_ref[:, 0:1]
        base_ref[...] = base
        over = base - _lane_tile(m_ref[...], n_keys // LANES)
        for h in range(n_groups):
            gap_ref[h] = jnp.max(over[:, g_keys * h:g_keys * (h + 1)])

    def tail_phase(g):
        slot = lax.rem(g, 2)
        b, c = lax.div(g, n_chunks), lax.rem(g, n_chunks)
        live = [gap_ref[h] >= -SKIP_LOG2 for h in range(n_groups)]

        def update(h, vsrc):
            g_parts = []
            for r in range(vsub * h, vsub * (h + 1)):
                lf = lfbuf[slot, r] * LOG2E
                incl = lf
                d = 1
                while d < page:
                    incl = incl + jnp.where(lane8 + d < page, pltpu.roll(incl, page - d, axis=1), 0.0)
                    d *= 2
                g_parts.append(incl - lf)
            s = (base_ref[:, g_keys * h:g_keys * (h + 1)]
                 + jnp.concatenate([jnp.concatenate(g_parts, axis=1)] * n_new, axis=0))
            m_prev = m_ref[...]
            m_next = jnp.maximum(m_prev, jnp.max(s, axis=1, keepdims=True))
            alpha = jnp.exp2(m_prev - m_next)
            p = jnp.exp2(s - _lane_tile(m_next, g_keys // LANES))
            l_ref[...] = alpha * l_ref[...] + jnp.sum(p, axis=1, keepdims=True)
            m_ref[...] = m_next
            vcat = jnp.concatenate([vsrc[r].reshape(ATTN_W, page).astype(BF16)
                                    for r in range(vsub)], axis=1)
            acc_ref[...] = (_lane_tile(alpha, ATTN_W // LANES) * acc_ref[...]
                            + lax.dot_general(p.astype(BF16), vcat, (((1,), (1,)), ((), ())),
                                              preferred_element_type=F32))

        def fetch_and_update(h):
            cps = v_copies(g, h, vdem, vsem.at[1])
            for cp in cps:
                cp.start()
            for cp in cps:
                cp.wait()
            update(h, vdem)

        for h in range(n_groups):
            if h == 0:
                @pl.when(jnp.logical_and(live[0], c == 0))
                def _():
                    update(0, vpre)

                @pl.when(jnp.logical_and(live[0], c > 0))
                def _():
                    fetch_and_update(0)
            else:
                @pl.when(live[h])
                def _(h=h):
                    fetch_and_update(h)

        @pl.when(c == n_chunks - 1)
        def _():
            finish(b)

        @pl.when(jnp.logical_and(c == 0, b + 1 < nb))
        def _():
            for cp in v_copies(g + n_chunks, 0, vpre, vsem.at[0]):
                cp.start()

    def prologue():
        for cp in k_copies(0, 0):
            cp.start()
        for cp in v_copies(0, 0, vpre, vsem.at[0]):
            cp.start()

    return prologue, (head, scores, tail_phase), total


def _decode_setup(nb, n_new, n_pages, page):
    pps = DEC_PAGES
    while n_pages % pps:
        pps //= 2
    vsub = min(DEC_VSUB, pps)
    rows = n_new * N_HEADS
    assert page % LANES == 0 and rows % 8 == 0 and pps % vsub == 0
    params = dict(nb=nb, n_new=n_new, n_pages=n_pages, page=page, pps=pps, vsub=vsub)
    scratch = [pltpu.VMEM((2, pps, N_HEADS, HEAD_DIM, page), F32),
               pltpu.VMEM((vsub, N_HEADS, HEAD_DIM, page), F32),
               pltpu.VMEM((vsub, N_HEADS, HEAD_DIM, page), F32),
               pltpu.VMEM((2, pps, N_HEADS, page), F32),
               pltpu.SemaphoreType.DMA((2,)),
               pltpu.SemaphoreType.DMA((2,)),
               pltpu.SemaphoreType.DMA((2,)),
               pltpu.VMEM((rows, ATTN_W), BF16),
               pltpu.VMEM((8, LANES), F32),
               pltpu.VMEM((rows, LANES), F32),
               pltpu.VMEM((rows, LANES), F32),
               pltpu.VMEM((rows, LANES), F32),
               pltpu.VMEM((rows, ATTN_W), F32),
               pltpu.VMEM((N_HEADS, LANES), F32),
               pltpu.VMEM((rows, pps * page), F32),
               pltpu.SMEM((pps // vsub,), F32)]
    return params, scratch


def _merge_sample_kernel(x_ref, u_ref, sp_ref, sgp_ref, attn_ref, sga_ref, pw_ref, ps_ref, wo_ref,
                         nf_ref, y_ref, pool_ref, *, n_new, past):
    u = [u_ref[:, POOL_W * i:POOL_W * (i + 1)] for i in range(n_new)]
    hist = [sp_ref[r] for r in range(POOL_HIST)]
    ext = hist + u
    pooled_rows = []
    for i in range(n_new):
        groups = []
        for g, w in enumerate(POOL_WINDOWS):
            sl = slice(POOL_GC * g, POOL_GC * (g + 1))
            end = POOL_HIST + i
            total = ext[end][:, sl]
            for r in range(end - w + 1, end):
                total = total + ext[r][:, sl]
            groups.append(total / float(min(w, past + i + 1)) - u[i][:, sl])
        pooled_rows.append(jnp.concatenate(groups, axis=1))
    pooled = jnp.concatenate(pooled_rows, axis=0)
    cat = lambda ref, n: jnp.concatenate([ref[:, n * i:n * (i + 1)] for i in range(n_new)], axis=0)
    y = _merge(cat(x_ref, D_MODEL), pooled, cat(sgp_ref, POOL_W), cat(attn_ref, ATTN_W),
               cat(sga_ref, ATTN_W), pw_ref, ps_ref[...], wo_ref, nf_ref[...])
    nb = x_ref.shape[0]
    for i in range(n_new):
        y_ref[:, D_MODEL * i:D_MODEL * (i + 1)] = y[nb * i:nb * (i + 1), :]
    new_hist = ext[-POOL_HIST:]
    for r in range(POOL_HIST):
        pool_ref[:, POOL_W * r:POOL_W * (r + 1)] = new_hist[r]


def _merge_sample(xs2, u2, sp2, sgp2, attn2, sga2, pw, ps, wo, nf, *, n_new, past):
    nb = xs2.shape[0]
    return pl.pallas_call(
        functools.partial(_merge_sample_kernel, n_new=n_new, past=past),
        out_shape=(jax.ShapeDtypeStruct((nb, n_new * D_MODEL), F32),
                   jax.ShapeDtypeStruct((nb, POOL_HIST * POOL_W), F32)),
        compiler_params=pltpu.CompilerParams(vmem_limit_bytes=VMEM_LIMIT),
        name="merge_sample",
    )(xs2, u2, sp2, sgp2, attn2, sga2, pw, ps, wo, nf)


def kernel(x_prompt, x_sample, cache_k, cache_v, cache_logf, state_pool, page_table,
           norm_w, w_in, b_forget, pool_w, pool_scale, w_out, norm_f):
    depth = norm_w.shape[0]
    assert depth == 1, "a single layer is supported"
    b_p, seq, _ = x_prompt.shape
    b_s, t_s, _ = x_sample.shape
    n_phys, page = cache_k.shape[1], cache_k.shape[2]
    past = page_table.shape[1] * page
    ll = 0

    wt_all = jnp.swapaxes(w_in[ll], 0, 1)
    wt = wt_all[:W_MAIN].astype(BF16)
    wf = jnp.tile(wt_all[W_MAIN:], (LANES // N_HEADS, 1)).astype(BF16)
    bft = jnp.tile(b_forget[ll], LANES // N_HEADS).reshape(1, LANES).astype(F32)
    nw = norm_w[ll].reshape(1, D_MODEL)
    nf = norm_f.reshape(1, D_MODEL)
    pw = pool_w[ll].astype(BF16)
    ps = pool_scale[ll].reshape(1, POOL_W)
    wo = w_out[ll].astype(BF16)
    pp_np, cc_np = _placement()
    pp = jnp.asarray(pp_np, BF16)
    cc = jnp.asarray(cc_np, F32)

    xp2 = x_prompt.reshape(b_p * seq, D_MODEL)
    seg_np = (np.arange(ATTN_W)[:, None] // HEAD_DIM == np.arange(LANES)[None, :] % N_HEADS)
    seg = jnp.asarray(seg_np, BF16)
    u, sgp, qa, ka, kf, vf, vb, sga, logf, stats = _proj_prompt(xp2, nw, wt, wf, bft, pp, cc, seg,
                                                                seq=seq)
    n_blk = seq // min(ATTN_T, seq)
    plan = _plan(stats, n_batch=b_p, n_blk=n_blk)[:b_p * N_PAIRS, :n_blk].reshape(-1)

    xs2 = x_sample.reshape(b_s, t_s * D_MODEL)
    us, sgps, qs, ks, vs, sgas, lfs, cns = _proj_sample(xs2, nw, wt, wf, bft, n_new=t_s)
    ck = jnp.transpose(cache_k[ll], (0, 2, 3, 1))
    cv = jnp.transpose(cache_v[ll], (0, 2, 3, 1))
    clf_t = jnp.swapaxes(cache_logf[ll], 1, 2)

    per_pos = lambda a: a.reshape(b_s, t_s, a.shape[1] // t_s)
    attn, attn_s = _attention(plan, qa, ka, vb, page_table, per_pos(qs), per_pos(ks), per_pos(vs),
                              per_pos(cns), ck, cv, clf_t, n_batch=b_p, seq=seq)
    yp, pool_p = _merge_prompt(xp2, u, sgp, attn, sga, pw, ps, wo, nf, n_batch=b_p, seq=seq)
    ys, pool_s = _merge_sample(xs2, us, jnp.swapaxes(state_pool[ll], 0, 1), sgps,
                               attn_s.reshape(b_s, t_s * ATTN_W), sgas, pw, ps, wo, nf,
                               n_new=t_s, past=past)

    return (yp.reshape(b_p, seq, D_MODEL),
            ys.reshape(b_s, t_s, D_MODEL),
            jnp.transpose(kf.reshape(1, b_p, N_HEADS, HEAD_DIM, seq), (0, 1, 4, 2, 3)),
            jnp.transpose(vf.reshape(1, b_p, N_HEADS, HEAD_DIM, seq), (0, 1, 4, 2, 3)),
            jnp.transpose(logf, (0, 2, 1)).reshape(1, b_p, seq, N_HEADS),
            pool_p.reshape(1, b_p, POOL_HIST, POOL_W),
            ks.reshape(1, b_s, t_s, N_HEADS, HEAD_DIM),
            vs.reshape(1, b_s, t_s, N_HEADS, HEAD_DIM),
            lfs.reshape(b_s, t_s, LANES)[:, :, :N_HEADS].reshape(1, b_s, t_s, N_HEADS),
            pool_s.reshape(1, b_s, POOL_HIST, POOL_W))
```

```python
import functools
import math

import numpy as np
import jax
import jax.numpy as jnp
from jax import lax
from jax.experimental import pallas as pl
from jax.experimental.pallas import tpu as pltpu

D_MODEL = 1024
POOL_W = 512
ATTN_W = 512
N_HEADS = 8
HEAD_DIM = 64
POOL_WINDOWS = (2, 4, 8, 16)
POOL_GC = 128
POOL_HIST = 15
EPS = 1e-6

LANES = 128
LOG2E = 1.4426950408889634
Q_SCALE = LOG2E / math.sqrt(HEAD_DIM)
NEG = -1e30
N_PAIRS = N_HEADS // 2
W_MAIN = 2 * POOL_W + 4 * ATTN_W
BIAS_LANES = LANES // N_HEADS
BIAS_COLS = 2 * LANES
VMEM_LIMIT = 56 * 1024 * 1024

SKIP_LOG2 = 152.0
NORM_SLACK = 1.02

PROJ_TM = 1024
MERGE_SUB = 256
ATTN_T = 512
DEC_PAGES = 16
DEC_VSUB = 4

BF16 = jnp.bfloat16
F32 = jnp.float32


def _placement():
    pp = np.zeros((LANES, BIAS_COLS), np.float32)
    cc = np.zeros((1, BIAS_COLS), np.float32)
    for h in range(N_HEADS):
        for i in range(3):
            pp[8 * i + h, BIAS_LANES * h + i] = 1.0
            pp[8 * i + h, LANES + BIAS_LANES * h + 3 + i] = -1.0
            cc[0, BIAS_LANES * h + 3 + i] = 1.0
            cc[0, LANES + BIAS_LANES * h + i] = 1.0
    return pp, cc


def _log_sigmoid(x):
    return jnp.minimum(x, 0.0) - jnp.log(1.0 + jnp.exp(-jnp.abs(x)))


def _silu(x):
    return x * jax.nn.sigmoid(x)


def _rmsnorm(xf, w):
    ms = jnp.mean(xf * xf, axis=-1, keepdims=True)
    return xf * lax.rsqrt(ms + EPS) * w


def _project(h, wt_ref, wf_ref):
    def cols(ref, c0, n):
        return lax.dot_general(h, ref[c0:c0 + n, :], (((1,), (1,)), ((), ())),
                               preferred_element_type=F32)
    fl = cols(wf_ref, 0, LANES)
    gp = cols(wt_ref, POOL_W, POOL_W)
    ga = cols(wt_ref, 2 * POOL_W + 3 * ATTN_W, ATTN_W)
    k = cols(wt_ref, 2 * POOL_W + ATTN_W, ATTN_W)
    v = cols(wt_ref, 2 * POOL_W + 2 * ATTN_W, ATTN_W)
    q = cols(wt_ref, 2 * POOL_W, ATTN_W)
    u = cols(wt_ref, 0, POOL_W)
    return u, gp, q, k, v, ga, fl


def _proj_prompt_kernel(x_ref, nw_ref, wt_ref, wf_ref, bf_ref, pp_ref, cc_ref, seg_ref,
                        u_ref, sgp_ref, qa_ref, ka_ref, k_ref, v_ref, vb_ref, sga_ref, logf_ref,
                        st_ref, carry_ref, *, tm, ta, tiles_per_batch):
    i = pl.program_id(0)

    @pl.when(i % tiles_per_batch == 0)
    def _():
        carry_ref[...] = jnp.zeros(carry_ref.shape, F32)

    row = lax.broadcasted_iota(jnp.int32, (ta, LANES), 0)
    lane = lax.broadcasted_iota(jnp.int32, (ta, LANES), 1)
    srow = lax.broadcasted_iota(jnp.int32, (8, LANES), 0)
    first_half = lane < HEAD_DIM
    carry = carry_ref[0:1, :]

    for sb in range(tm // ta):
        rs = slice(sb * ta, (sb + 1) * ta)
        h = _rmsnorm(x_ref[rs, :], nw_ref[...]).astype(BF16)
        u, gp, q, k, v, ga, fl = _project(h, wt_ref, wf_ref)
        u_ref[rs, :] = u
        sgp_ref[rs, :] = _silu(gp).astype(BF16)
        k_ref[0, :, rs] = jnp.transpose(k)
        v_ref[0, :, rs] = jnp.transpose(v)
        vb_ref[rs, :] = v.astype(BF16)
        sga_ref[rs, :] = _silu(ga).astype(BF16)
        logf = _log_sigmoid(fl + bf_ref[...])
        logf_ref[0, :, rs] = jnp.transpose(logf)[0:N_HEADS, :]

        acc = logf
        d = 1
        while d < ta:
            acc = acc + jnp.where(row >= d, pltpu.roll(acc, d, axis=0), 0.0)
            d *= 2
        f_run = acc + carry
        carry = f_run[ta - 1:ta, :]

        f2 = f_run * LOG2E
        p1 = f2.astype(BF16).astype(F32)
        r1 = f2 - p1
        p2 = r1.astype(BF16).astype(F32)
        p3 = (r1 - p2).astype(BF16).astype(F32)
        pieces = jnp.where(lane < 8, p1, jnp.where(lane < 16, p2, jnp.where(lane < 24, p3, 0.0)))
        bias = jnp.dot(pieces.astype(BF16), pp_ref[...], preferred_element_type=F32) + cc_ref[...]

        qs = q * Q_SCALE

        qn2 = jnp.dot((qs * qs).astype(BF16), seg_ref[...], preferred_element_type=F32)
        kn2 = jnp.dot((k * k).astype(BF16), seg_ref[...], preferred_element_type=F32)
        qmax = jnp.sqrt(jnp.max(qn2, axis=0, keepdims=True))
        kmax = jnp.sqrt(jnp.max(kn2, axis=0, keepdims=True))
        st_ref[8 * sb:8 * (sb + 1), :] = jnp.where(
            srow == 0, qmax, jnp.where(srow == 1, kmax, jnp.where(
                srow == 2, f2[0:1, :], jnp.where(srow == 3, f2[ta - 1:ta, :], 0.0))))

        q_bias = bias[:, 0:LANES].astype(BF16)
        k_bias = bias[:, LANES:2 * LANES]
        for p in range(N_PAIRS):
            qa_ref[rs, 2 * LANES * p:2 * LANES * p + LANES] = (
                qs[:, LANES * p:LANES * (p + 1)].astype(BF16))
            qa_ref[rs, 2 * LANES * p + LANES:2 * LANES * (p + 1)] = q_bias
            kp = k[:, LANES * p:LANES * (p + 1)]
            for e in range(2):
                hh = 2 * p + e
                keep = first_half if e == 0 else jnp.logical_not(first_half)
                ka_ref[rs, 2 * LANES * hh:2 * LANES * hh + LANES] = (
                    jnp.where(keep, kp, 0.0).astype(BF16))
                own = jnp.logical_and(lane >= BIAS_LANES * hh, lane < BIAS_LANES * (hh + 1))
                ka_ref[rs, 2 * LANES * hh + LANES:2 * LANES * (hh + 1)] = (
                    jnp.where(own, k_bias, 0.0).astype(BF16))

    carry_ref[...] = jnp.broadcast_to(carry, carry_ref.shape)


def _proj_prompt(x2, nw, wt, wf, bft, pp, cc, seg, *, seq):
    rows = x2.shape[0]
    tm = min(PROJ_TM, seq)
    ta = min(ATTN_T, seq)
    assert seq % tm == 0 and rows % seq == 0 and tm % ta == 0 and tm % LANES == 0
    n_batch = rows // seq
    row_blk = lambda n: pl.BlockSpec((tm, n), lambda i: (i, 0))
    const = lambda shape: pl.BlockSpec(shape, lambda i: (0, 0))
    out_shape = (
        jax.ShapeDtypeStruct((rows, POOL_W), F32),
        jax.ShapeDtypeStruct((rows, POOL_W), BF16),
        jax.ShapeDtypeStruct((rows, N_PAIRS * 2 * LANES), BF16),
        jax.ShapeDtypeStruct((rows, N_HEADS * 2 * LANES), BF16),
        jax.ShapeDtypeStruct((n_batch, ATTN_W, seq), F32),
        jax.ShapeDtypeStruct((n_batch, ATTN_W, seq), F32),
        jax.ShapeDtypeStruct((rows, ATTN_W), BF16),
        jax.ShapeDtypeStruct((rows, ATTN_W), BF16),
        jax.ShapeDtypeStruct((n_batch, N_HEADS, seq), F32),
        jax.ShapeDtypeStruct((rows // ta * 8, LANES), F32),
    )
    tpb = seq // tm
    t_minor = lambda n: pl.BlockSpec((1, n, tm), lambda i: (i // tpb, 0, i % tpb))
    out_specs = (row_blk(POOL_W), row_blk(POOL_W), row_blk(N_PAIRS * 2 * LANES),
                 row_blk(N_HEADS * 2 * LANES), t_minor(ATTN_W), t_minor(ATTN_W), row_blk(ATTN_W),
                 row_blk(ATTN_W), t_minor(N_HEADS),
                 pl.BlockSpec((tm // ta * 8, LANES), lambda i: (i, 0)))
    return pl.pallas_call(
        functools.partial(_proj_prompt_kernel, tm=tm, ta=ta, tiles_per_batch=seq // tm),
        grid=(rows // tm,),
        in_specs=[row_blk(D_MODEL), const((1, D_MODEL)), const((W_MAIN, D_MODEL)),
                  const((LANES, D_MODEL)), const((1, LANES)),
                  const((LANES, BIAS_COLS)), const((1, BIAS_COLS)), const((ATTN_W, LANES))],
        out_specs=out_specs,
        out_shape=out_shape,
        scratch_shapes=[pltpu.VMEM((8, LANES), F32)],
        compiler_params=pltpu.CompilerParams(dimension_semantics=("arbitrary",),
                                             vmem_limit_bytes=VMEM_LIMIT),
        name="proj_prompt",
    )(x2, nw, wt, wf, bft, pp, cc, seg)


def _lane_tile(x, reps):
    return jnp.concatenate([x] * reps, axis=1)


def _plan_kernel(st_ref, o_ref, *, n_batch, n_blk):
    row = lax.broadcasted_iota(jnp.int32, (LANES, LANES), 0).astype(F32)
    col = lax.broadcasted_iota(jnp.int32, (LANES, LANES), 1).astype(F32)
    out = jnp.zeros((LANES, LANES), F32)
    for b in range(n_batch):
        def stat(r):
            v = st_ref[pl.ds((b * n_blk) * 8 + r, n_blk, stride=8), :]
            return jnp.concatenate([v, jnp.zeros((LANES - n_blk, LANES), F32)], axis=0)
        qm, km, ft0, fs1 = stat(0), stat(1), stat(2), stat(3)
        km_t, fs1_t = jnp.transpose(km), jnp.transpose(fs1)
        for p in range(N_PAIRS):
            first = row
            for h in (2 * p, 2 * p + 1):
                qcol = qm[:, h:h + 1]
                ub = (NORM_SLACK * (qcol * km_t[h:h + 1, :] + qcol * km[:, h:h + 1])
                      + ft0[:, h:h + 1] - fs1_t[h:h + 1, :])
                needed = jnp.logical_and(ub >= -SKIP_LOG2, col < row)
                first = jnp.minimum(first, jnp.where(needed, col, row))
            start = jnp.min(first, axis=1, keepdims=True)
            out = jnp.where(col == b * N_PAIRS + p, start, out)
    o_ref[...] = jnp.transpose(out).astype(jnp.int32)


def _plan(stats, *, n_batch, n_blk):
    assert n_blk <= LANES and n_batch * N_PAIRS <= LANES
    return pl.pallas_call(
        functools.partial(_plan_kernel, n_batch=n_batch, n_blk=n_blk),
        out_shape=jax.ShapeDtypeStruct((LANES, LANES), jnp.int32),
        name="attn_plan",
    )(stats)


def _attn_kernel(plan_ref, pt_ref, qa_ref, ka_ref, vb_ref, sg_ref, qd_ref, kn_ref, vn_ref, cn_ref,
                 ck_hbm, cv_hbm, clf_hbm, o_ref, od_ref, m_ref, l_ref, acc_ref, *dec_scratch,
                 t, nq, n_steps, dec):
    b, pr, i = pl.program_id(0), pl.program_id(1), pl.program_id(2)
    step = (b * N_PAIRS + pr) * nq + i
    dec_prologue, (dec_head, dec_scores, dec_tail), n_chunks = _decode_fns(
        pt_ref, qd_ref, kn_ref, vn_ref, cn_ref, ck_hbm, cv_hbm, clf_hbm, od_ref, *dec_scratch, **dec)
    per_step = -(-n_chunks // n_steps)
    interleave = n_chunks == n_steps

    @pl.when(step == 0)
    def _():
        dec_prologue()

    if interleave:
        dec_head(step)
    else:
        for k in range(per_step):
            g = step * per_step + k

            @pl.when(g < n_chunks)
            def _(g=g):
                dec_head(g)
                dec_scores(g)
                dec_tail(g)

    q = qa_ref[...]
    lane = lax.broadcasted_iota(jnp.int32, (t, LANES), 1)

    def scores(kstart, width, diag_col, e):
        kblk = ka_ref[pl.ds(kstart, width), 2 * LANES * e:2 * LANES * (e + 1)]
        s = lax.dot_general(q, kblk, (((1,), (1,)), ((), ())), preferred_element_type=F32)
        if diag_col is not None:
            r = lax.broadcasted_iota(jnp.int32, (t, width - diag_col), 0)
            c = lax.broadcasted_iota(jnp.int32, (t, width - diag_col), 1)
            diag = jnp.where(c <= r, s[:, diag_col:], NEG)
            s = diag if diag_col == 0 else jnp.concatenate([s[:, :diag_col], diag], axis=1)
        return s

    def block(kstart, width, diag_col):
        vblk = vb_ref[pl.ds(kstart, width), :]
        for e in range(2):
            s = scores(kstart, width, diag_col, e)
            m_prev = m_ref[e]
            m_next = jnp.maximum(m_prev, jnp.max(s, axis=1, keepdims=True))
            alpha = jnp.exp2(m_prev - m_next)
            p = jnp.exp2(s - _lane_tile(m_next, width // LANES))
            l_ref[e] = alpha * l_ref[e] + jnp.sum(p, axis=1, keepdims=True)
            m_ref[e] = m_next
            acc_ref[e] = alpha * acc_ref[e] + jnp.dot(p.astype(BF16), vblk,
                                                      preferred_element_type=F32)

    def body(j, carry):
        block(pl.multiple_of(j * t, t), t, None)
        return carry

    first = plan_ref[(b * N_PAIRS + pr) * nq + i]
    one_before = jnp.logical_and(i > 0, first == i - 1)

    m_ref[...] = jnp.full(m_ref.shape, NEG, F32)
    l_ref[...] = jnp.zeros(l_ref.shape, F32)
    acc_ref[...] = jnp.zeros(acc_ref.shape, F32)

    @pl.when(one_before)
    def _():
        if interleave:
            dec_scores(step)
        block(pl.multiple_of((i - 1) * t, t), 2 * t, t)

    @pl.when(jnp.logical_not(one_before))
    def _():
        if interleave:
            dec_scores(step)
        lax.fori_loop(first, i, body, 0)
        block(pl.multiple_of(i * t, t), t, 0)

    o = jnp.where(lane < HEAD_DIM, acc_ref[0] / l_ref[0], acc_ref[1] / l_ref[1])
    o_ref[...] = (o * sg_ref[...].astype(F32)).astype(o_ref.dtype)
    if interleave:
        dec_tail(step)


def _attention(plan, qa, ka, vb, sga, page_table, qd, kn, vn, cn, ck, cv, clf_t, *, n_batch, seq):
    t = min(ATTN_T, seq)
    assert seq % t == 0
    nq = seq // t
    nb, n_new, _ = qd.shape
    dec, dec_scratch = _decode_setup(nb, n_new, page_table.shape[1], ck.shape[3])
    whole = lambda n: pl.BlockSpec((nb, n_new, n), lambda b, p, i, plan, pt: (0, 0, 0))
    hbm = pl.BlockSpec(memory_space=pl.ANY)
    grid_spec = pltpu.PrefetchScalarGridSpec(
        num_scalar_prefetch=2,
        grid=(n_batch, N_PAIRS, nq),
        in_specs=[pl.BlockSpec((t, 2 * LANES), lambda b, p, i, plan, pt: (b * nq + i, p)),
                  pl.BlockSpec((seq, 4 * LANES), lambda b, p, i, plan, pt: (b, p)),
                  pl.BlockSpec((seq, LANES), lambda b, p, i, plan, pt: (b, p)),
                  pl.BlockSpec((t, LANES), lambda b, p, i, plan, pt: (b * nq + i, p)),
                  whole(ATTN_W), whole(ATTN_W), whole(ATTN_W), whole(LANES), hbm, hbm, hbm],
        out_specs=(pl.BlockSpec((t, LANES), lambda b, p, i, plan, pt: (b * nq + i, p)),
                   whole(ATTN_W)),
        scratch_shapes=[pltpu.VMEM((2, t, LANES), F32), pltpu.VMEM((2, t, LANES), F32),
                        pltpu.VMEM((2, t, LANES), F32)] + dec_scratch)
    return pl.pallas_call(
        functools.partial(_attn_kernel, t=t, nq=nq, n_steps=n_batch * N_PAIRS * nq, dec=dec),
        grid_spec=grid_spec,
        out_shape=(jax.ShapeDtypeStruct((n_batch * seq, ATTN_W), BF16),
                   jax.ShapeDtypeStruct((nb, n_new, ATTN_W), F32)),
        compiler_params=pltpu.CompilerParams(
            dimension_semantics=("arbitrary", "arbitrary", "arbitrary"),
            vmem_limit_bytes=VMEM_LIMIT),
        name="attention",
    )(plan, page_table.reshape(-1), qa, ka, vb, sga, qd, kn, vn, cn, ck, cv, clf_t)


def _merge(x, pooled, sgp, attn, sga, pw_ref, ps, wo_ref, nf):
    mixed = [jnp.dot(pooled[:, POOL_GC * g:POOL_GC * (g + 1)].astype(BF16), pw_ref[g],
                     preferred_element_type=F32) for g in range(len(POOL_WINDOWS))]
    pool_out = jnp.concatenate(mixed, axis=1) * ps
    mix = jnp.concatenate([(pool_out * sgp.astype(F32)).astype(BF16),
                           attn if sga is None else
                           (attn.astype(F32) * sga.astype(F32)).astype(BF16)], axis=1)
    xo = x + jnp.dot(mix, wo_ref[...], preferred_element_type=F32)
    return _rmsnorm(xo, nf)


def _merge_prompt_kernel(x_ref, u_ref, uh_ref, sgp_ref, attn_ref, pw_ref, ps_ref, wo_ref,
                         nf_ref, y_ref, pool_ref, *, tm, sub, tiles_per_batch):
    i = pl.program_id(0)
    ti = i % tiles_per_batch
    halo = jnp.where(ti == 0, 0.0, uh_ref[...])
    for sb in range(tm // sub):
        rs = slice(sb * sub, (sb + 1) * sub)
        u = u_ref[rs, :]
        before = halo if sb == 0 else u_ref[sb * sub - 16:sb * sub, :]
        level = jnp.concatenate([before, u], axis=0)
        pos = ti * tm + sb * sub + lax.broadcasted_iota(jnp.int32, (sub, 1), 0)
        groups = []
        shift = 1
        for g, w in enumerate(POOL_WINDOWS):
            while shift < w:
                level = level + pltpu.roll(level, shift, axis=0)
                shift *= 2
            cnt = jnp.minimum(w, pos + 1).astype(F32)
            groups.append(level[16:, 0:POOL_GC] / cnt - u[:, POOL_GC * g:POOL_GC * (g + 1)])
            level = level[:, POOL_GC:]
        pooled = jnp.concatenate(groups, axis=1)
        y_ref[rs, :] = _merge(x_ref[rs, :], pooled, sgp_ref[rs, :], attn_ref[rs, :], None,
                              pw_ref, ps_ref[...], wo_ref, nf_ref[...])

    @pl.when(ti == tiles_per_batch - 1)
    def _():
        pool_ref[0] = u_ref[pl.ds(tm - POOL_HIST, POOL_HIST), :]


def _merge_prompt(x2, u, sgp, attn, pw, ps, wo, nf, *, n_batch, seq):
    rows = x2.shape[0]
    tm = min(PROJ_TM, seq)
    tpb = seq // tm
    halo_blocks = tm // 16
    row_blk = lambda n: pl.BlockSpec((tm, n), lambda i: (i, 0))
    const2 = lambda shape: pl.BlockSpec(shape, lambda i: (0, 0))
    return pl.pallas_call(
        functools.partial(_merge_prompt_kernel, tm=tm, sub=min(MERGE_SUB, tm), tiles_per_batch=tpb),
        grid=(rows // tm,),
        in_specs=[row_blk(D_MODEL), row_blk(POOL_W),
                  pl.BlockSpec((16, POOL_W), lambda i: (jnp.maximum(i * halo_blocks - 1, 0), 0)),
                  row_blk(POOL_W), row_blk(ATTN_W),
                  pl.BlockSpec((len(POOL_WINDOWS), POOL_GC, POOL_GC), lambda i: (0, 0, 0)),
                  const2((1, POOL_W)), const2((D_MODEL, D_MODEL)), const2((1, D_MODEL))],
        out_specs=(row_blk(D_MODEL),
                   pl.BlockSpec((1, POOL_HIST, POOL_W), lambda i: (i // tpb, 0, 0))),
        out_shape=(jax.ShapeDtypeStruct((rows, D_MODEL), F32),
                   jax.ShapeDtypeStruct((n_batch, POOL_HIST, POOL_W), F32)),
        compiler_params=pltpu.CompilerParams(dimension_semantics=("arbitrary",),
                                             vmem_limit_bytes=VMEM_LIMIT),
        name="merge_prompt",
    )(x2, u, u, sgp, attn, pw, ps, wo, nf)


def _proj_sample_kernel(x_ref, nw_ref, wt_ref, wf_ref, bf_ref,
                        u_ref, sgp_ref, q_ref, k_ref, v_ref, sga_ref, logf_ref, cn_ref, *, n_new):
    nb = x_ref.shape[0]
    x = jnp.concatenate([x_ref[:, D_MODEL * i:D_MODEL * (i + 1)] for i in range(n_new)], axis=0)
    h = _rmsnorm(x, nw_ref[...]).astype(BF16)
    u, gp, q, k, v, ga, fl = _project(h, wt_ref, wf_ref)
    logf = _log_sigmoid(fl + bf_ref[...])
    outs = ((u_ref, u), (sgp_ref, _silu(gp).astype(BF16)), (q_ref, q * Q_SCALE), (k_ref, k),
            (v_ref, v), (sga_ref, _silu(ga).astype(BF16)), (logf_ref, logf))
    cn = jnp.zeros((nb, LANES), F32)
    for i in range(n_new):
        for ref, val in outs:
            n = val.shape[1]
            ref[:, n * i:n * (i + 1)] = val[nb * i:nb * (i + 1), :]
        cn = cn + logf[nb * i:nb * (i + 1), :]
        cn_ref[:, LANES * i:LANES * (i + 1)] = cn


def _proj_sample(xs2, nw, wt, wf, bft, *, n_new):
    nb = xs2.shape[0]
    widths = (POOL_W, POOL_W, ATTN_W, ATTN_W, ATTN_W, ATTN_W, LANES, LANES)
    dtypes = (F32, BF16, F32, F32, F32, BF16, F32, F32)
    return pl.pallas_call(
        functools.partial(_proj_sample_kernel, n_new=n_new),
        out_shape=tuple(jax.ShapeDtypeStruct((nb, n_new * n), dt) for n, dt in zip(widths, dtypes)),
        compiler_params=pltpu.CompilerParams(vmem_limit_bytes=VMEM_LIMIT),
        name="proj_sample",
    )(xs2, nw, wt, wf, bft)


def _decode_fns(pt_ref, q_ref, kn_ref, vn_ref, cn_ref, ck_hbm, cv_hbm, clf_hbm, o_ref,
                kbuf, vpre, vdem, lfbuf, ksem, lfsem, vsem,
                qbd_ref, cn8_ref, cncol_ref, m_ref, l_ref, acc_ref, tail_ref, base_ref, gap_ref,
                *, nb, n_new, n_pages, page, pps, vsub):
    n_chunks = n_pages // pps
    n_groups = pps // vsub
    total = nb * n_chunks
    n_keys = pps * page
    g_keys = vsub * page
    rows = n_new * N_HEADS
    row_w = lax.broadcasted_iota(jnp.int32, (rows, ATTN_W), 0)
    lane_w = lax.broadcasted_iota(jnp.int32, (rows, ATTN_W), 1)
    head_lanes = (row_w % N_HEADS) == (lane_w // HEAD_DIM)
    row1 = lax.broadcasted_iota(jnp.int32, (rows, 1), 0)

    def page_index(g, r):
        b, c = lax.div(g, n_chunks), lax.rem(g, n_chunks)
        return pt_ref[b * n_pages + n_pages - 1 - (c * pps + r)]

    def k_copies(g, slot):
        cps = []
        for r in range(pps):
            idx = page_index(g, r)
            cps.append(pltpu.make_async_copy(ck_hbm.at[idx], kbuf.at[slot, r], ksem.at[slot]))
            cps.append(pltpu.make_async_copy(clf_hbm.at[idx], lfbuf.at[slot, r], lfsem.at[slot]))
        return cps

    def v_copies(g, grp, dst, sem):
        return [pltpu.make_async_copy(cv_hbm.at[page_index(g, grp * vsub + r)], dst.at[r], sem)
                for r in range(vsub)]

    def row(ref, b, i, n):
        return ref[b, i:i + 1, :]

    def init(b):
        qrep = jnp.concatenate(
            [jnp.broadcast_to(row(q_ref, b, i, ATTN_W), (N_HEADS, ATTN_W)) for i in range(n_new)],
            axis=0)
        qbd = jnp.where(head_lanes, qrep, 0.0)
        qbd_ref[...] = qbd.astype(BF16)
        cn8_ref[...] = jnp.zeros(cn8_ref.shape, F32)
        for i in range(n_new):
            cn8_ref[i:i + 1, :] = row(cn_ref, b, i, LANES)
        cnt = jnp.transpose(cn8_ref[...])[0:rows, :] * LOG2E
        cncol = jnp.zeros((rows, 1), F32)
        for i in range(n_new):
            cncol = jnp.where(row1 // N_HEADS == i, cnt[:, i:i + 1], cncol)
        cncol_ref[...] = jnp.broadcast_to(cncol, cncol_ref.shape)
        s_new = []
        for j in range(n_new):
            sj = (jnp.sum(qbd * row(kn_ref, b, j, ATTN_W), axis=1, keepdims=True)
                  + cncol - cnt[:, j:j + 1])
            s_new.append(jnp.where(row1 // N_HEADS >= j, sj, NEG))
        m0 = functools.reduce(jnp.maximum, s_new)
        l0 = jnp.zeros((rows, 1), F32)
        a0 = jnp.zeros((rows, ATTN_W), F32)
        for j in range(n_new):
            pj = jnp.exp2(s_new[j] - m0)
            l0 = l0 + pj
            a0 = a0 + pj * row(vn_ref, b, j, ATTN_W)
        m_ref[...] = jnp.broadcast_to(m0, m_ref.shape)
        l_ref[...] = jnp.broadcast_to(l0, l_ref.shape)
        acc_ref[...] = a0
        tail_ref[...] = jnp.zeros(tail_ref.shape, F32)

    def finish(b):
        o = acc_ref[...] / _lane_tile(l_ref[...], ATTN_W // LANES)
        o = jnp.where(head_lanes, o, 0.0)
        for i in range(n_new):
            o_ref[b, i:i + 1, :] = jnp.sum(o[N_HEADS * i:N_HEADS * (i + 1), :], axis=0,
                                           keepdims=True)

    lane8 = lax.broadcasted_iota(jnp.int32, (N_HEADS, page), 1)

    def head(g):
        slot = lax.rem(g, 2)
        b, c = lax.div(g, n_chunks), lax.rem(g, n_chunks)

        @pl.when(g + 1 < total)
        def _():
            for cp in k_copies(g + 1, 1 - slot):
                cp.start()

        @pl.when(c == 0)
        def _():
            init(b)
            for cp in v_copies(g, 0, vpre, vsem.at[0]):
                cp.wait()

        for cp in k_copies(g, slot):
            cp.wait()

    def scores(g):
        slot = lax.rem(g, 2)
        kcat = jnp.concatenate([kbuf[slot, r].reshape(ATTN_W, page).astype(BF16)
                                for r in range(pps)], axis=1)
        qk = jnp.dot(qbd_ref[...], kcat, preferred_element_type=F32)
        tail = tail_ref[...]
        tails = []
        for r in range(pps):
            tails.append(tail)
            tail = tail + jnp.sum(lfbuf[slot, r] * LOG2E, axis=1, keepdims=True)
        tail_ref[...] = tail
        base = qk + jnp.concatenate([jnp.concatenate(tails, axis=1)] * n_new, axis=0) + cncol_ref[:, 0:1]
        base_ref[...] = base
        over = base - _lane_tile(m_ref[...], n_keys // LANES)
        for h in range(n_groups):
            gap_ref[h] = jnp.max(over[:, g_keys * h:g_keys * (h + 1)])

    def tail_phase(g):
        slot = lax.rem(g, 2)
        b, c = lax.div(g, n_chunks), lax.rem(g, n_chunks)
        live = [gap_ref[h] >= -SKIP_LOG2 for h in range(n_groups)]

        def update(h, vsrc):
            g_parts = []
            for r in range(vsub * h, vsub * (h + 1)):
                lf = lfbuf[slot, r] * LOG2E
                incl = lf
                d = 1
                while d < page:
                    incl = incl + jnp.where(lane8 + d < page, pltpu.roll(incl, page - d, axis=1), 0.0)
                    d *= 2
                g_parts.append(incl - lf)
            s = (base_ref[:, g_keys * h:g_keys * (h + 1)]
                 + jnp.concatenate([jnp.concatenate(g_parts, axis=1)] * n_new, axis=0))
            m_prev = m_ref[...]
            m_next = jnp.maximum(m_prev, jnp.max(s, axis=1, keepdims=True))
            alpha = jnp.exp2(m_prev - m_next)
            p = jnp.exp2(s - _lane_tile(m_next, g_keys // LANES))
            l_ref[...] = alpha * l_ref[...] + jnp.sum(p, axis=1, keepdims=True)
            m_ref[...] = m_next
            vcat = jnp.concatenate([vsrc[r].reshape(ATTN_W, page).astype(BF16)
                                    for r in range(vsub)], axis=1)
            acc_ref[...] = (_lane_tile(alpha, ATTN_W // LANES) * acc_ref[...]
                            + lax.dot_general(p.astype(BF16), vcat, (((1,), (1,)), ((), ())),
                                              preferred_element_type=F32))

        def fetch_and_update(h):
            cps = v_copies(g, h, vdem, vsem.at[1])
            for cp in cps:
                cp.start()
            for cp in cps:
                cp.wait()
            update(h, vdem)

        for h in range(n_groups):
            if h == 0:
                @pl.when(jnp.logical_and(live[0], c == 0))
                def _():
                    update(0, vpre)

                @pl.when(jnp.logical_and(live[0], c > 0))
                def _():
                    fetch_and_update(0)
            else:
                @pl.when(live[h])
                def _(h=h):
                    fetch_and_update(h)

        @pl.when(c == n_chunks - 1)
        def _():
            finish(b)

        @pl.when(jnp.logical_and(c == 0, b + 1 < nb))
        def _():
            for cp in v_copies(g + n_chunks, 0, vpre, vsem.at[0]):
                cp.start()

    def prologue():
        for cp in k_copies(0, 0):
            cp.start()
        for cp in v_copies(0, 0, vpre, vsem.at[0]):
            cp.start()

    return prologue, (head, scores, tail_phase), total


def _decode_setup(nb, n_new, n_pages, page):
    pps = DEC_PAGES
    while n_pages % pps:
        pps //= 2
    vsub = min(DEC_VSUB, pps)
    rows = n_new * N_HEADS
    assert page % LANES == 0 and rows % 8 == 0 and pps % vsub == 0
    params = dict(nb=nb, n_new=n_new, n_pages=n_pages, page=page, pps=pps, vsub=vsub)
    scratch = [pltpu.VMEM((2, pps, N_HEADS, HEAD_DIM, page), F32),
               pltpu.VMEM((vsub, N_HEADS, HEAD_DIM, page), F32),
               pltpu.VMEM((vsub, N_HEADS, HEAD_DIM, page), F32),
               pltpu.VMEM((2, pps, N_HEADS, page), F32),
               pltpu.SemaphoreType.DMA((2,)),
               pltpu.SemaphoreType.DMA((2,)),
               pltpu.SemaphoreType.DMA((2,)),
               pltpu.VMEM((rows, ATTN_W), BF16),
               pltpu.VMEM((8, LANES), F32),
               pltpu.VMEM((rows, LANES), F32),
               pltpu.VMEM((rows, LANES), F32),
               pltpu.VMEM((rows, LANES), F32),
               pltpu.VMEM((rows, ATTN_W), F32),
               pltpu.VMEM((N_HEADS, LANES), F32),
               pltpu.VMEM((rows, pps * page), F32),
               pltpu.SMEM((pps // vsub,), F32)]
    return params, scratch


def _merge_sample_kernel(x_ref, u_ref, sp_ref, sgp_ref, attn_ref, sga_ref, pw_ref, ps_ref, wo_ref,
                         nf_ref, y_ref, pool_ref, *, n_new, past):
    u = [u_ref[:, POOL_W * i:POOL_W * (i + 1)] for i in range(n_new)]
    hist = [sp_ref[r] for r in range(POOL_HIST)]
    ext = hist + u
    pooled_rows = []
    for i in range(n_new):
        groups = []
        for g, w in enumerate(POOL_WINDOWS):
            sl = slice(POOL_GC * g, POOL_GC * (g + 1))
            end = POOL_HIST + i
            total = ext[end][:, sl]
            for r in range(end - w + 1, end):
                total = total + ext[r][:, sl]
            groups.append(total / float(min(w, past + i + 1)) - u[i][:, sl])
        pooled_rows.append(jnp.concatenate(groups, axis=1))
    pooled = jnp.concatenate(pooled_rows, axis=0)
    cat = lambda ref, n: jnp.concatenate([ref[:, n * i:n * (i + 1)] for i in range(n_new)], axis=0)
    y = _merge(cat(x_ref, D_MODEL), pooled, cat(sgp_ref, POOL_W), cat(attn_ref, ATTN_W),
               cat(sga_ref, ATTN_W), pw_ref, ps_ref[...], wo_ref, nf_ref[...])
    nb = x_ref.shape[0]
    for i in range(n_new):
        y_ref[:, D_MODEL * i:D_MODEL * (i + 1)] = y[nb * i:nb * (i + 1), :]
    new_hist = ext[-POOL_HIST:]
    for r in range(POOL_HIST):
        pool_ref[:, POOL_W * r:POOL_W * (r + 1)] = new_hist[r]


def _merge_sample(xs2, u2, sp2, sgp2, attn2, sga2, pw, ps, wo, nf, *, n_new, past):
    nb = xs2.shape[0]
    return pl.pallas_call(
        functools.partial(_merge_sample_kernel, n_new=n_new, past=past),
        out_shape=(jax.ShapeDtypeStruct((nb, n_new * D_MODEL), F32),
                   jax.ShapeDtypeStruct((nb, POOL_HIST * POOL_W), F32)),
        compiler_params=pltpu.CompilerParams(vmem_limit_bytes=VMEM_LIMIT),
        name="merge_sample",
    )(xs2, u2, sp2, sgp2, attn2, sga2, pw, ps, wo, nf)


def kernel(x_prompt, x_sample, cache_k, cache_v, cache_logf, state_pool, page_table,
           norm_w, w_in, b_forget, pool_w, pool_scale, w_out, norm_f):
    depth = norm_w.shape[0]
    assert depth == 1, "a single layer is supported"
    b_p, seq, _ = x_prompt.shape
    b_s, t_s, _ = x_sample.shape
    n_phys, page = cache_k.shape[1], cache_k.shape[2]
    past = page_table.shape[1] * page
    ll = 0

    wt_all = jnp.swapaxes(w_in[ll], 0, 1)
    wt = wt_all[:W_MAIN].astype(BF16)
    wf = jnp.tile(wt_all[W_MAIN:], (LANES // N_HEADS, 1)).astype(BF16)
    bft = jnp.tile(b_forget[ll], LANES // N_HEADS).reshape(1, LANES).astype(F32)
    nw = norm_w[ll].reshape(1, D_MODEL)
    nf = norm_f.reshape(1, D_MODEL)
    pw = pool_w[ll].astype(BF16)
    ps = pool_scale[ll].reshape(1, POOL_W)
    wo = w_out[ll].astype(BF16)
    pp_np, cc_np = _placement()
    pp = jnp.asarray(pp_np, BF16)
    cc = jnp.asarray(cc_np, F32)

    xp2 = x_prompt.reshape(b_p * seq, D_MODEL)
    seg_np = (np.arange(ATTN_W)[:, None] // HEAD_DIM == np.arange(LANES)[None, :] % N_HEADS)
    seg = jnp.asarray(seg_np, BF16)
    u, sgp, qa, ka, kf, vf, vb, sga, logf, stats = _proj_prompt(xp2, nw, wt, wf, bft, pp, cc, seg,
                                                                seq=seq)
    n_blk = seq // min(ATTN_T, seq)
    plan = _plan(stats, n_batch=b_p, n_blk=n_blk)[:b_p * N_PAIRS, :n_blk].reshape(-1)

    xs2 = x_sample.reshape(b_s, t_s * D_MODEL)
    us, sgps, qs, ks, vs, sgas, lfs, cns = _proj_sample(xs2, nw, wt, wf, bft, n_new=t_s)
    ck = jnp.transpose(cache_k[ll], (0, 2, 3, 1))
    cv = jnp.transpose(cache_v[ll], (0, 2, 3, 1))
    clf_t = jnp.swapaxes(cache_logf[ll], 1, 2)

    per_pos = lambda a: a.reshape(b_s, t_s, a.shape[1] // t_s)
    attn, attn_s = _attention(plan, qa, ka, vb, sga, page_table, per_pos(qs), per_pos(ks),
                              per_pos(vs), per_pos(cns), ck, cv, clf_t, n_batch=b_p, seq=seq)
    yp, pool_p = _merge_prompt(xp2, u, sgp, attn, pw, ps, wo, nf, n_batch=b_p, seq=seq)
    ys, pool_s = _merge_sample(xs2, us, jnp.swapaxes(state_pool[ll], 0, 1), sgps,
                               attn_s.reshape(b_s, t_s * ATTN_W), sgas, pw, ps, wo, nf,
                               n_new=t_s, past=past)

    return (yp.reshape(b_p, seq, D_MODEL),
            ys.reshape(b_s, t_s, D_MODEL),
            jnp.transpose(kf.reshape(1, b_p, N_HEADS, HEAD_DIM, seq), (0, 1, 4, 2, 3)),
            jnp.transpose(vf.reshape(1, b_p, N_HEADS, HEAD_DIM, seq), (0, 1, 4, 2, 3)),
            jnp.transpose(logf, (0, 2, 1)).reshape(1, b_p, seq, N_HEADS),
            pool_p.reshape(1, b_p, POOL_HIST, POOL_W),
            ks.reshape(1, b_s, t_s, N_HEADS, HEAD_DIM),
            vs.reshape(1, b_s, t_s, N_HEADS, HEAD_DIM),
            lfs.reshape(b_s, t_s, LANES)[:, :, :N_HEADS].reshape(1, b_s, t_s, N_HEADS),
            pool_s.reshape(1, b_s, POOL_HIST, POOL_W))
```
